```python
import math
import jax, jax.numpy as jnp
from jax import lax
import numpy as np

D_MODEL = 1024
BATCH = 4
SEQ = 4096
DEPTH = 2

HEAD_DIM = D_MODEL // 16
A_HEADS = 4
B_HEADS = 4
C_HEADS = 8
C_KV_HEADS = 2
DIFF_DK = HEAD_DIM // 2
DIFF_DV = HEAD_DIM
CMP_LEN = 32
CMP_STRIDE = 16
CMP_HIDDEN = 2 * HEAD_DIM
SLC_BLOCK = 64
SLC_TOPN = 16
NSA_WINDOW = 512
SWA_WINDOW = 128
QBLK = 128
N_EXPERTS = 8
TOP_K = 2
D_FF_EXPERT = 7 * D_MODEL // 2
D_FF_DENSE = 2816
MOE_BLK = 256
PLE_DIM = 256
LN_EPS = 1e-5
NEG_INF = -1e30
FORCE_SCORE = 1e9
DN_ALPHA = (2 * DEPTH) ** 0.25
DN_BETA = (8 * DEPTH) ** -0.25
N_DENSE = (DEPTH + 1) // 2
N_MOE = DEPTH // 2
PROJ_SIZES = (A_HEADS * HEAD_DIM, 6 * HEAD_DIM, 3 * A_HEADS,
              B_HEADS * 2 * DIFF_DK, B_HEADS * 2 * DIFF_DK, B_HEADS * DIFF_DV,
              C_HEADS * HEAD_DIM, C_KV_HEADS * HEAD_DIM, C_KV_HEADS * HEAD_DIM)
N_PROJ = sum(PROJ_SIZES)

kernel_name = 'hybrid_nsa_diff_swa_moe_deepnorm'


def layer_norm(x, g, b):
    xf = x.astype(jnp.float32)
    mu = jnp.mean(xf, axis=-1, keepdims=True)
    var = jnp.mean(jnp.square(xf - mu), axis=-1, keepdims=True)
    return ((xf - mu) * lax.rsqrt(var + LN_EPS) * g + b).astype(x.dtype)


def rms_norm(x, g):
    xf = x.astype(jnp.float32)
    y = xf * lax.rsqrt(jnp.mean(jnp.square(xf), axis=-1, keepdims=True) + LN_EPS) * g
    return y.astype(x.dtype)


def masked_softmax(s, mask):
    s = jnp.where(mask, s.astype(jnp.float32), NEG_INF)
    m = jnp.max(s, axis=-1, keepdims=True)
    e = jnp.where(mask, jnp.exp(s - m), 0.0)
    return e / jnp.maximum(jnp.sum(e, axis=-1, keepdims=True), 1e-30)


def sink_softmax(s, mask, sink):
    s = jnp.where(mask, s.astype(jnp.float32), NEG_INF)
    sink = sink.astype(jnp.float32)
    m = jnp.maximum(jnp.max(s, axis=-1, keepdims=True), sink)
    e = jnp.where(mask, jnp.exp(s - m), 0.0)
    return e / (jnp.sum(e, axis=-1, keepdims=True) + jnp.exp(sink - m))


def alibi_slopes():
    n = A_HEADS + B_HEADS + C_HEADS
    s = 2.0 ** (-8.0 * jnp.arange(1, n + 1, dtype=jnp.float32) / n)
    rest = s[C_HEADS:]
    return rest[0::2], rest[1::2], s[:C_HEADS]


def band_blocks(a, n_prev):
    B, T = a.shape[:2]
    nb = T // QBLK
    pad = [(0, 0), (n_prev * QBLK, 0)] + [(0, 0)] * (a.ndim - 2)
    c = jnp.pad(a, pad).reshape(B, nb + n_prev, QBLK, *a.shape[2:])
    return jnp.concatenate([c[:, i:i + nb] for i in range(n_prev + 1)], axis=2)


def banded_attention(q, k, v, window, slopes, sinks):
    B, T, G, R, dh = q.shape
    nb = T // QBLK
    n_prev = window // QBLK
    kw = (n_prev + 1) * QBLK
    kb = band_blocks(k, n_prev)
    vb = band_blocks(v, n_prev)
    qb = q.reshape(B, nb, QBLK, G, R, dh)
    qq = jnp.arange(QBLK)[:, None]
    kk = jnp.arange(kw)[None, :]
    dist = n_prev * QBLK + qq - kk
    s_pos = (jnp.arange(nb)[:, None, None] - n_prev) * QBLK + kk[None]
    mask = (((dist >= 0) & (dist < window))[None] & (s_pos >= 0))[None, :, None, None]
    s = jnp.einsum('bnqgrd,bnkgd->bngrqk', qb, kb).astype(jnp.float32) * (dh ** -0.5)
    s = s - slopes[:, :, None, None] * dist.astype(jnp.float32)
    if sinks is None:
        p = masked_softmax(s, mask)
    else:
        p = sink_softmax(s, mask, sinks[:, :, None, None])
    o = jnp.einsum('bngrqk,bnkgd->bnqgrd', p.astype(v.dtype), vb)
    return o.reshape(B, T, G, R, dh)


def compress_blocks(kv, pos, w1, w2):
    B, T, dh = kv.shape
    n_chunk = T // CMP_STRIDE
    r = CMP_LEN // CMP_STRIDE
    n_cmp = n_chunk - r + 1
    c = kv.reshape(B, n_chunk, CMP_STRIDE, dh)
    blocks = jnp.concatenate([c[:, i:i + n_cmp] for i in range(r)], axis=2) + pos
    h = jax.nn.gelu(blocks.reshape(B, n_cmp, CMP_LEN * dh) @ w1)
    return h @ w2


def selected_attention(q, k, v, idx, slopes):
    B, T, H, dh = q.shape
    n = idx.shape[-1]
    nsl = T // SLC_BLOCK
    nq = T // QBLK
    kb = k.reshape(B, nsl, SLC_BLOCK, dh)
    vb = v.reshape(B, nsl, SLC_BLOCK, dh)
    qc = q.reshape(B, nq, QBLK, H, dh).transpose(1, 0, 2, 3, 4)
    ic = idx.reshape(B, nq, QBLK, n).transpose(1, 0, 2, 3)
    bidx = jnp.arange(B)[:, None, None]
    off = jnp.arange(SLC_BLOCK)

    def one(args):
        qj, ij, j = args
        kg = kb[bidx, ij].reshape(B, QBLK, n * SLC_BLOCK, dh)
        vg = vb[bidx, ij].reshape(B, QBLK, n * SLC_BLOCK, dh)
        s_pos = (ij[..., None] * SLC_BLOCK + off).reshape(B, QBLK, n * SLC_BLOCK)
        dist = (j * QBLK + jnp.arange(QBLK))[None, :, None] - s_pos
        s = jnp.einsum('bqhd,bqkd->bqhk', qj, kg).astype(jnp.float32) * (dh ** -0.5)
        s = s - slopes[None, None, :, None] * dist.astype(jnp.float32)[:, :, None, :]
        p = masked_softmax(s, (dist >= 0)[:, :, None, :])
        return jnp.einsum('bqhk,bqkd->bqhd', p.astype(vg.dtype), vg)

    o = lax.map(one, (qc, ic, jnp.arange(nq)))
    return o.transpose(1, 0, 2, 3, 4).reshape(B, T, H, dh)


def nsa_mixer(q, k_cmp, v_cmp, k_slc, v_slc, k_win, v_win, gates, cmp_pos, cmp_w1, cmp_w2, slopes):
    B, T, H, dh = q.shape
    kc = compress_blocks(k_cmp, cmp_pos[0], cmp_w1[0], cmp_w2[0])
    vc = compress_blocks(v_cmp, cmp_pos[1], cmp_w1[1], cmp_w2[1])
    n_cmp = kc.shape[1]
    t = jnp.arange(T)
    c_start = jnp.arange(n_cmp) * CMP_STRIDE
    dist_c = (t[:, None] - (c_start + CMP_LEN - 1)[None, :]).astype(jnp.float32)
    s_c = jnp.einsum('bthd,bnd->bthn', q, kc).astype(jnp.float32) * (dh ** -0.5)
    s_c = s_c - slopes[None, :, None] * dist_c[:, None, :]
    p_c = masked_softmax(s_c, (dist_c >= 0)[:, None, :])
    o_c = jnp.einsum('bthn,bnd->bthd', p_c.astype(vc.dtype), vc)
    nsl = T // SLC_BLOCK
    s_start = jnp.arange(nsl) * SLC_BLOCK
    overlap = jnp.clip(jnp.minimum(c_start[:, None] + CMP_LEN, s_start[None, :] + SLC_BLOCK)
                       - jnp.maximum(c_start[:, None], s_start[None, :]), 0, None).astype(jnp.float32) / CMP_LEN
    imp = jnp.einsum('bthn,nj->btj', p_c, overlap)
    t_blk = t // SLC_BLOCK
    j = jnp.arange(nsl)
    forced = (j[None] == 0) | (j[None] == t_blk[:, None]) | (j[None] == t_blk[:, None] - 1)
    future = j[None] > t_blk[:, None]
    imp = jnp.where(forced[None], FORCE_SCORE, jnp.where(future[None], -1.0, imp))
    _, idx = lax.top_k(imp, min(SLC_TOPN, nsl))
    o_s = selected_attention(q, k_slc, v_slc, idx, slopes)
    o_w = banded_attention(q[:, :, None], k_win[:, :, None], v_win[:, :, None],
                           NSA_WINDOW, slopes[None], None)[:, :, 0]
    o = gates[..., 0:1] * o_c + gates[..., 1:2] * o_s + gates[..., 2:3] * o_w
    return o.reshape(B, T, H * dh)


def diff_attention(q, k, v, lam, slopes):
    B, T, H, _, dk = q.shape
    nq = T // QBLK
    qc = q.reshape(B, nq, QBLK, H, 2, dk).transpose(1, 0, 2, 3, 4, 5)
    s_pos = jnp.arange(T)

    def one(args):
        qj, j = args
        dist = (j * QBLK + jnp.arange(QBLK))[:, None] - s_pos[None, :]
        bias = -slopes[:, None, None] * dist.astype(jnp.float32)
        s = jnp.einsum('bqhcd,bkhcd->bchqk', qj, k).astype(jnp.float32) * (dk ** -0.5) + bias
        p = masked_softmax(s, dist >= 0)
        w = p[:, 0] - lam * p[:, 1]
        return jnp.einsum('bhqk,bkhd->bqhd', w.astype(v.dtype), v)

    o = lax.map(one, (qc, jnp.arange(nq)))
    return o.transpose(1, 0, 2, 3, 4).reshape(B, T, H, -1)


def swiglu(h, wg, wu, wd):
    return (jax.nn.silu(h @ wg) * (h @ wu)) @ wd


def moe_swiglu(x, w_router, w_gate, w_up, w_down):
    B, T, D = x.shape
    n_tok = B * T
    xf = x.reshape(n_tok, D)
    logits = (xf @ w_router).astype(jnp.float32)
    top_logit, top_e = lax.top_k(logits, TOP_K)
    gate = jax.nn.softmax(top_logit, axis=-1)
    n_asg = n_tok * TOP_K
    e_flat = top_e.reshape(n_asg)
    tok_flat = jnp.arange(n_asg) // TOP_K
    g_flat = gate.reshape(n_asg)
    order = jnp.argsort(e_flat)
    e_sorted = e_flat[order]
    counts = jnp.bincount(e_flat, length=N_EXPERTS)
    padded = (counts + MOE_BLK - 1) // MOE_BLK * MOE_BLK
    pad_end = jnp.cumsum(padded)
    start = jnp.cumsum(counts) - counts
    dest = pad_end[e_sorted] - padded[e_sorted] + jnp.arange(n_asg) - start[e_sorted]
    n_blk = -(-n_asg // MOE_BLK) + N_EXPERTS
    cap = n_blk * MOE_BLK
    tok_buf = jnp.zeros((cap,), jnp.int32).at[dest].set(tok_flat[order])
    g_buf = jnp.zeros((cap,), jnp.float32).at[dest].set(g_flat[order])
    blk_e = jnp.minimum(jnp.sum(pad_end[None, :] <= (jnp.arange(n_blk) * MOE_BLK)[:, None], axis=-1),
                        N_EXPERTS - 1)
    x_buf = xf[tok_buf].reshape(n_blk, MOE_BLK, D)

    def expert_block(args):
        xb, e = args
        return swiglu(xb, w_gate[e], w_up[e], w_down[e])

    y = lax.map(expert_block, (x_buf, blk_e)).reshape(cap, D)
    out = jnp.zeros((n_tok, D), jnp.float32).at[tok_buf].add(y.astype(jnp.float32) * g_buf[:, None])
    return out.astype(x.dtype).reshape(B, T, D)


def setup_inputs(seed: int = 0) -> dict:
    key = jax.random.key(seed)
    keys = iter(jax.random.split(key, 32))

    def nrm(shape, scale):
        return jax.random.normal(next(keys), shape, jnp.float32) * scale

    D = D_MODEL
    return {
        'x': nrm((BATCH, SEQ, D), 1.0),
        'p': nrm((DEPTH, BATCH, SEQ, PLE_DIM), 1.0),
        'w_in': nrm((DEPTH, D, N_PROJ), D ** -0.5),
        'cmp_pos': nrm((DEPTH, 2, CMP_LEN, HEAD_DIM), 0.1),
        'cmp_w1': nrm((DEPTH, 2, CMP_LEN * HEAD_DIM, CMP_HIDDEN), (CMP_LEN * HEAD_DIM) ** -0.5),
        'cmp_w2': nrm((DEPTH, 2, CMP_HIDDEN, HEAD_DIM), CMP_HIDDEN ** -0.5),
        'diff_lambda': nrm((DEPTH, 4, DIFF_DK), 0.1),
        'diff_subln': 1.0 + nrm((DEPTH, DIFF_DV), 0.1),
        'sinks': nrm((DEPTH, C_HEADS), 1.0),
        'w_out': nrm((DEPTH, D, D), D ** -0.5 * DN_BETA),
        'ln1_g': 1.0 + nrm((DEPTH, D), 0.1),
        'ln1_b': nrm((DEPTH, D), 0.02),
        'ffn_w_gate': nrm((N_DENSE, D, D_FF_DENSE), D ** -0.5),
        'ffn_w_up': nrm((N_DENSE, D, D_FF_DENSE), D ** -0.5),
        'ffn_w_down': nrm((N_DENSE, D_FF_DENSE, D), D_FF_DENSE ** -0.5 * DN_BETA),
        'moe_router': nrm((N_MOE, D, N_EXPERTS), D ** -0.5),
        'moe_w_gate': nrm((N_MOE, N_EXPERTS, D, D_FF_EXPERT), D ** -0.5),
        'moe_w_up': nrm((N_MOE, N_EXPERTS, D, D_FF_EXPERT), D ** -0.5),
        'moe_w_down': nrm((N_MOE, N_EXPERTS, D_FF_EXPERT, D), D_FF_EXPERT ** -0.5 * DN_BETA),
        'ln2_g': 1.0 + nrm((DEPTH, D), 0.1),
        'ln2_b': nrm((DEPTH, D), 0.02),
        'ple_gate': nrm((DEPTH, D, D), D ** -0.5),
        'ple_proj': nrm((DEPTH, PLE_DIM, D), PLE_DIM ** -0.5 * DN_BETA),
        'ln3_g': 1.0 + nrm((DEPTH, D), 0.1),
        'ln3_b': nrm((DEPTH, D), 0.02),
    }


def reference(x, p, w_in, cmp_pos, cmp_w1, cmp_w2, diff_lambda, diff_subln, sinks, w_out,
              ln1_g, ln1_b, ffn_w_gate, ffn_w_up, ffn_w_down, moe_router, moe_w_gate, moe_w_up,
              moe_w_down, ln2_g, ln2_b, ple_gate, ple_proj, ln3_g, ln3_b):
    B, T, _ = x.shape
    slopes_a, slopes_b, slopes_c = alibi_slopes()
    offs = np.cumsum(PROJ_SIZES)[:-1].tolist()
    for i in range(DEPTH):
        proj = jnp.einsum('btd,dn->btn', x, w_in[i])
        q_a, kv_a, g_a, q_b, k_b, v_b, q_c, k_c, v_c = jnp.split(proj, offs, axis=-1)
        kv_a = kv_a.reshape(B, T, 6, HEAD_DIM)
        o_a = nsa_mixer(q_a.reshape(B, T, A_HEADS, HEAD_DIM),
                        kv_a[:, :, 0], kv_a[:, :, 1], kv_a[:, :, 2], kv_a[:, :, 3], kv_a[:, :, 4], kv_a[:, :, 5],
                        jax.nn.sigmoid(g_a).reshape(B, T, A_HEADS, 3),
                        cmp_pos[i], cmp_w1[i], cmp_w2[i], slopes_a)
        lam_init = 0.8 - 0.6 * math.exp(-0.3 * i)
        dl = diff_lambda[i].astype(jnp.float32)
        lam = jnp.exp(jnp.sum(dl[0] * dl[1])) - jnp.exp(jnp.sum(dl[2] * dl[3])) + lam_init
        o_b = diff_attention(q_b.reshape(B, T, B_HEADS, 2, DIFF_DK), k_b.reshape(B, T, B_HEADS, 2, DIFF_DK),
                             v_b.reshape(B, T, B_HEADS, DIFF_DV), lam, slopes_b)
        o_b = rms_norm(o_b, diff_subln[i]) * (1.0 - lam_init)
        o_c = banded_attention(q_c.reshape(B, T, C_KV_HEADS, C_HEADS // C_KV_HEADS, HEAD_DIM),
                               k_c.reshape(B, T, C_KV_HEADS, HEAD_DIM), v_c.reshape(B, T, C_KV_HEADS, HEAD_DIM),
                               SWA_WINDOW, slopes_c.reshape(C_KV_HEADS, -1), sinks[i].reshape(C_KV_HEADS, -1))
        mix = jnp.concatenate([o_a, o_b.reshape(B, T, -1), o_c.reshape(B, T, -1)], axis=-1) @ w_out[i]
        x = layer_norm(DN_ALPHA * x + mix, ln1_g[i], ln1_b[i])
        if i % 2 == 0:
            f = swiglu(x, ffn_w_gate[i // 2], ffn_w_up[i // 2], ffn_w_down[i // 2])
        else:
            f = moe_swiglu(x, moe_router[i // 2], moe_w_gate[i // 2], moe_w_up[i // 2], moe_w_down[i // 2])
        x = layer_norm(DN_ALPHA * x + f, ln2_g[i], ln2_b[i])
        e = jax.nn.sigmoid(x @ ple_gate[i]) * (p[i] @ ple_proj[i])
        x = layer_norm(DN_ALPHA * x + e, ln3_g[i], ln3_b[i])
    return x
```

```python
import functools
import math

import jax
import jax.numpy as jnp
import numpy as np
from jax import lax
from jax.experimental import pallas as pl
from jax.experimental.pallas import tpu as pltpu

F32 = jnp.float32
BF16 = jnp.bfloat16

D_MODEL = 1024
HEAD_DIM = 64
A_HEADS = 4
B_HEADS = 4
C_HEADS = 8
C_KV_HEADS = 2
DIFF_DK = 32
CMP_LEN = 32
CMP_STRIDE = 16
CMP_HIDDEN = 128
SLC_BLOCK = 64
SLC_SHIFT = 6
SLC_TOPN = 16
NSA_WINDOW = 512
SWA_WINDOW = 128
N_EXPERTS = 8
D_FF_EXPERT = 3584
D_FF_DENSE = 2816
PLE_DIM = 256
LN_EPS = 1e-5
NEG_INF = -1e30
FORCE_SCORE = 1e9
DEPTH = 2
DN_ALPHA = (2 * DEPTH) ** 0.25

LANES = 128
QA_BLK, QB_BLK, QC_BLK, KVB_BLK, KVC_BLK, KVA_BLK, N_MAIN_BLK = 0, 4, 8, 16, 20, 22, 25
N_MAIN = N_MAIN_BLK * LANES

TQ = 128
TM = 512
FFN_CHUNK = 1408
MOE_TILE = 2048
MOE_ROWS = 256
MOE_TB = 256
MOE_CHUNK = 512
VMEM_LIMIT = 56 * 1024 * 1024


def _cparams(sem):
    return pltpu.CompilerParams(dimension_semantics=sem, vmem_limit_bytes=VMEM_LIMIT)


def _alibi_slopes():
    n = A_HEADS + B_HEADS + C_HEADS
    s = [2.0 ** (-8.0 * i / n) for i in range(1, n + 1)]
    rest = s[C_HEADS:]
    return rest[0::2], rest[1::2], s[:C_HEADS]


def _dot(a, b):
    return jnp.dot(a, b, preferred_element_type=F32)


def _dot_nt(a, b):
    return lax.dot_general(a, b, (((1,), (1,)), ((), ())), preferred_element_type=F32)


def _split2(x):
    hi = x.astype(BF16)
    lo = (x - hi.astype(F32)).astype(BF16)
    return hi, lo


def _layer_norm(y, g, b):
    mu = jnp.mean(y, axis=-1, keepdims=True)
    d = y - mu
    var = jnp.mean(d * d, axis=-1, keepdims=True)
    return d * lax.rsqrt(var + LN_EPS) * g + b


def _sigmoid(x):
    return 1.0 / (1.0 + jnp.exp(-x))


def _lane(shape):
    return lax.broadcasted_iota(jnp.int32, shape, 1)


def _pack_upper(parts):
    out = []
    for a, b in zip(parts[0::2], parts[1::2]):
        lane = _lane(a.shape)
        out.append(jnp.where(lane < HEAD_DIM, pltpu.roll(a, HEAD_DIM, 1), b))
    return out[0] if len(out) == 1 else jnp.concatenate(out, axis=1)


def _proj_kernel(x_ref, w_ref, o_ref, g_ref):
    acc = _dot(x_ref[...].astype(BF16), w_ref[...])
    o_ref[...] = acc[:, :N_MAIN].astype(BF16)
    g_ref[...] = acc[:, N_MAIN:]


def _project(x2d, w_ext):
    n = x2d.shape[0]
    return pl.pallas_call(
        _proj_kernel,
        grid=(n // TM,),
        in_specs=[pl.BlockSpec((TM, D_MODEL), lambda i: (i, 0)),
                  pl.BlockSpec((D_MODEL, N_MAIN + LANES), lambda i: (0, 0))],
        out_specs=[pl.BlockSpec((TM, N_MAIN), lambda i: (i, 0)),
                   pl.BlockSpec((TM, LANES), lambda i: (i, 0))],
        out_shape=[jax.ShapeDtypeStruct((n, N_MAIN), BF16),
                   jax.ShapeDtypeStruct((n, LANES), F32)],
        compiler_params=_cparams(("parallel",)),
        name="in_proj",
    )(x2d, w_ext)


def _build_w_in(w):
    d = w.shape[0]
    o = np.cumsum([0, 256, 384, 12, 256, 256, 256, 512, 128, 128])
    q_a, kv_a, g_a, q_b, k_b, v_b, q_c, k_c, v_c = [w[:, o[i]:o[i + 1]] for i in range(9)]
    z = jnp.zeros((d, HEAD_DIM), F32)
    hd = HEAD_DIM
    cols = []
    for h in range(A_HEADS):
        cols += [q_a[:, h * hd:(h + 1) * hd] * (hd ** -0.5), z]
    for h in range(B_HEADS):
        cols += [q_b[:, h * hd:(h + 1) * hd] * (DIFF_DK ** -0.5), z]
    for h in range(C_HEADS):
        cols += [q_c[:, h * hd:(h + 1) * hd] * (hd ** -0.5), z]
    for h in range(B_HEADS):
        cols += [k_b[:, h * hd:(h + 1) * hd], v_b[:, h * hd:(h + 1) * hd]]
    for g in range(C_KV_HEADS):
        cols += [k_c[:, g * hd:(g + 1) * hd], v_c[:, g * hd:(g + 1) * hd]]
    cols.append(kv_a)
    cols += [g_a, jnp.zeros((d, LANES - 12), F32)]
    return jnp.concatenate(cols, axis=1).astype(BF16)


def _compress_kernel(ch_ref, wc_ref, pos_ref, w1_ref, w2_ref, o_ref):
    nck = ch_ref.shape[1]
    u = _dot(ch_ref[0], wc_ref[...])
    hs = []
    for j in range(2):
        pw = _dot(pos_ref[j].astype(BF16), w1_ref[j])[0:1]
        ua = u[:, (2 * j) * LANES:(2 * j + 1) * LANES]
        ub = u[:, (2 * j + 1) * LANES:(2 * j + 2) * LANES]
        pre = ua + pltpu.roll(ub, nck - 1, 0) + pw
        hs.append(0.5 * pre * (1.0 + jnp.tanh(0.7978845608028654 * (pre + 0.044715 * pre * pre * pre))))
    h = jnp.concatenate(hs, axis=1).astype(BF16)
    o_ref[0] = _dot(h, w2_ref[...]).astype(BF16)


def _compress(proj3, cmp_pos, cmp_w1, cmp_w2):
    b, t, _ = proj3.shape
    nck = t // CMP_STRIDE
    kv = proj3[:, :, KVA_BLK * LANES:(KVA_BLK + 1) * LANES]
    chunks = kv.reshape(b, nck, CMP_STRIDE * LANES)
    w1 = cmp_w1.reshape(2, 2, CMP_STRIDE, HEAD_DIM, CMP_HIDDEN)
    z = jnp.zeros((CMP_STRIDE, HEAD_DIM, CMP_HIDDEN), F32)
    blocks = []
    for j in range(2):
        for half in range(2):
            pair = [w1[j, half], z] if j == 0 else [z, w1[j, half]]
            blocks.append(jnp.concatenate(pair, axis=1).reshape(CMP_STRIDE * LANES, CMP_HIDDEN))
    wc = jnp.concatenate(blocks, axis=1).astype(BF16)
    pos = jnp.zeros((2, 8, CMP_LEN * HEAD_DIM), F32).at[:, 0].set(cmp_pos.reshape(2, -1))
    z2 = jnp.zeros((CMP_HIDDEN, HEAD_DIM), F32)
    w2 = jnp.concatenate([jnp.concatenate([cmp_w2[0], z2], axis=1),
                          jnp.concatenate([z2, cmp_w2[1]], axis=1)], axis=0).astype(BF16)
    return pl.pallas_call(
        _compress_kernel,
        grid=(b,),
        in_specs=[pl.BlockSpec((1, nck, CMP_STRIDE * LANES), lambda i: (i, 0, 0)),
                  pl.BlockSpec(wc.shape, lambda i: (0, 0)),
                  pl.BlockSpec(pos.shape, lambda i: (0, 0, 0)),
                  pl.BlockSpec((2, CMP_LEN * HEAD_DIM, CMP_HIDDEN), lambda i: (0, 0, 0)),
                  pl.BlockSpec(w2.shape, lambda i: (0, 0))],
        out_specs=pl.BlockSpec((1, nck, LANES), lambda i: (i, 0, 0)),
        out_shape=jax.ShapeDtypeStruct((b, nck, LANES), BF16),
        compiler_params=_cparams(("parallel",)),
        name="nsa_compress",
    )(chunks, wc, pos, cmp_w1.astype(BF16), w2)


def _cmp_attn_kernel(q_ref, kvc_ref, ov_ref, o_ref, sel_ref, *, slopes):
    tq = q_ref.shape[1]
    ncp = kvc_ref.shape[1]
    t0 = pl.program_id(1) * tq
    kvc = kvc_ref[0]
    q = q_ref[0]
    qs = jnp.concatenate([q[:, h * LANES:(h + 1) * LANES] for h in range(A_HEADS)], axis=0)
    s = _dot_nt(qs, kvc)
    row = lax.broadcasted_iota(jnp.int32, (tq, ncp), 0) + t0
    col = _lane((tq, ncp))
    dist_i = row - (col * CMP_STRIDE + (CMP_LEN - 1))
    keep = (dist_i >= 0) & (col < ncp - 1)
    dist = dist_i.astype(F32)
    ps = []
    for h in range(A_HEADS):
        sh = jnp.where(keep, s[h * tq:(h + 1) * tq] - slopes[h] * dist, NEG_INF)
        m = jnp.max(sh, axis=-1, keepdims=True)
        e = jnp.where(keep, jnp.exp(sh - m), 0.0)
        ps.append(e / jnp.maximum(jnp.sum(e, axis=-1, keepdims=True), 1e-30))
    p = jnp.concatenate(ps, axis=0)
    o = _dot(p.astype(BF16), kvc)
    o_ref[0] = _pack_upper([o[h * tq:(h + 1) * tq] for h in range(A_HEADS)]).astype(BF16)

    psum = ps[0] + ps[1] + ps[2] + ps[3]
    hi = psum.astype(BF16)
    r1 = psum - hi.astype(F32)
    mid = r1.astype(BF16)
    lo = (r1 - mid.astype(F32)).astype(BF16)
    ov = ov_ref[...]
    imp = _dot(hi, ov) + _dot(mid, ov) + _dot(lo, ov)
    nsl = ncp * CMP_STRIDE // SLC_BLOCK
    j = _lane((tq, LANES))
    t_blk = jnp.right_shift(lax.broadcasted_iota(jnp.int32, (tq, LANES), 0) + t0, SLC_SHIFT)
    forced = (j == 0) | (j == t_blk) | (j == t_blk - 1)
    imp = jnp.where(forced, FORCE_SCORE, jnp.where(j > t_blk, -1.0, imp))
    work = jnp.where(j < nsl, imp, -3.0)
    jf = j.astype(F32)
    sel = jnp.zeros((tq, LANES), F32)
    for _ in range(SLC_TOPN):
        mx = jnp.max(work, axis=-1, keepdims=True)
        first = jnp.min(jnp.where(work == mx, jf, float(LANES)), axis=-1, keepdims=True)
        pick = jf == first
        sel = jnp.where(pick, 1.0, sel)
        work = jnp.where(pick, -4.0, work)
    sel_ref[0] = sel.astype(BF16)


def _cmp_attn(proj3, kvc, slopes):
    b, t, _ = proj3.shape
    ncp = kvc.shape[1]
    nsl = t // SLC_BLOCK
    c0 = np.arange(ncp)[:, None] * CMP_STRIDE
    s0 = np.arange(LANES)[None, :] * SLC_BLOCK
    ov = np.clip(np.minimum(c0 + CMP_LEN, s0 + SLC_BLOCK) - np.maximum(c0, s0), 0, None) / CMP_LEN
    ov[ncp - 1, :] = 0.0
    ov[:, nsl:] = 0.0
    ov = jnp.asarray(ov, BF16)
    return pl.pallas_call(
        functools.partial(_cmp_attn_kernel, slopes=slopes),
        grid=(b, t // TQ),
        in_specs=[pl.BlockSpec((1, TQ, A_HEADS * LANES), lambda bi, i: (bi, i, QA_BLK // A_HEADS)),
                  pl.BlockSpec((1, ncp, LANES), lambda bi, i: (bi, 0, 0)),
                  pl.BlockSpec((ncp, LANES), lambda bi, i: (0, 0))],
        out_specs=[pl.BlockSpec((1, TQ, A_HEADS * HEAD_DIM), lambda bi, i: (bi, i, 0)),
                   pl.BlockSpec((1, TQ, LANES), lambda bi, i: (bi, i, 0))],
        out_shape=[jax.ShapeDtypeStruct((b, t, A_HEADS * HEAD_DIM), BF16),
                   jax.ShapeDtypeStruct((b, t, LANES), BF16)],
        compiler_params=_cparams(("parallel", "parallel")),
        name="nsa_cmp_attn",
    )(proj3, kvc, ov)


def _flash(qs, kv_ref, col, i, lo, slopes, *, tq, window=None, sel=None, m0=None, l0=None):
    r_heads = len(slopes)
    rows = r_heads * tq
    rc = (lax.broadcasted_iota(jnp.int32, (tq, tq), 0) - _lane((tq, tq))).astype(F32)

    def tile(kt, carry, diag):
        m, l, acc = carry
        k0 = pl.multiple_of(kt * tq, tq)
        kvt = kv_ref[0, pl.ds(k0, tq), col:col + LANES]
        s = _dot_nt(qs, kvt)
        dist = rc + ((i - kt) * tq).astype(F32)
        keep = None
        if diag:
            keep = rc >= 0.0
        elif window is not None:
            keep = dist < float(window)
        if sel is not None:
            blk = jnp.right_shift(lax.broadcasted_iota(jnp.int32, (LANES, tq), 1) + kt * tq, SLC_SHIFT)
            expand = jnp.where(lax.broadcasted_iota(jnp.int32, (LANES, tq), 0) == blk, 1.0, 0.0).astype(BF16)
            chosen = _dot(sel, expand) > 0.5
            keep = chosen if keep is None else keep & chosen
        parts = []
        for r in range(r_heads):
            sr = s[r * tq:(r + 1) * tq] - slopes[r] * dist
            if keep is not None:
                sr = jnp.where(keep, sr, NEG_INF)
            parts.append(sr)
        s = jnp.concatenate(parts, axis=0)
        m_new = jnp.maximum(m, jnp.max(s, axis=-1, keepdims=True))
        alpha = jnp.exp(m - m_new)
        p = jnp.exp(s - m_new)
        l = alpha * l + jnp.sum(p, axis=-1, keepdims=True)
        acc = alpha * acc + _dot(p.astype(BF16), kvt)
        return m_new, l, acc

    if m0 is None:
        m0 = jnp.full((rows, 1), NEG_INF, F32)
        l0 = jnp.zeros((rows, 1), F32)
    carry = (m0, l0, jnp.zeros((rows, LANES), F32))
    carry = lax.fori_loop(lo, i, lambda kt, c: tile(kt, c, False), carry)
    _, l, acc = tile(i, carry, True)
    return l, acc


def _stack_heads(q, n):
    return jnp.concatenate([q[:, h * LANES:(h + 1) * LANES] for h in range(n)], axis=0)


def _nsa_attn_kernel(q_ref, kv_ref, *rest, slopes, window):
    if window is None:
        sel_ref, o_ref = rest
        sel = sel_ref[0]
    else:
        (o_ref,) = rest
        sel = None
    tq = q_ref.shape[1]
    i = pl.program_id(1)
    lo = jnp.int32(0) if window is None else jnp.maximum(i - window // tq, 0)
    qs = _stack_heads(q_ref[0], A_HEADS)
    l, acc = _flash(qs, kv_ref, 0, i, lo, slopes, tq=tq, window=window, sel=sel)
    o = acc / l
    o_ref[0] = _pack_upper([o[h * tq:(h + 1) * tq] for h in range(A_HEADS)]).astype(BF16)


def _nsa_attn(proj3, kv_blk, slopes, *, window=None, sel=None):
    b, t, _ = proj3.shape
    in_specs = [pl.BlockSpec((1, TQ, A_HEADS * LANES), lambda bi, i: (bi, i, QA_BLK // A_HEADS)),
                pl.BlockSpec((1, t, LANES), lambda bi, i: (bi, 0, kv_blk))]
    args = [proj3, proj3]
    if sel is not None:
        in_specs.append(pl.BlockSpec((1, TQ, LANES), lambda bi, i: (bi, i, 0)))
        args.append(sel)
    return pl.pallas_call(
        functools.partial(_nsa_attn_kernel, slopes=slopes, window=window),
        grid=(b, t // TQ),
        in_specs=in_specs,
        out_specs=pl.BlockSpec((1, TQ, A_HEADS * HEAD_DIM), lambda bi, i: (bi, i, 0)),
        out_shape=jax.ShapeDtypeStruct((b, t, A_HEADS * HEAD_DIM), BF16),
        compiler_params=_cparams(("parallel", "parallel")),
        name="nsa_window" if window is not None else "nsa_selected",
    )(*args)


def _swa_kernel(slope_ref, sink_ref, q_ref, kv_ref, o_ref):
    tq = q_ref.shape[1]
    g = pl.program_id(1)
    i = pl.program_id(2)
    r_heads = C_HEADS // C_KV_HEADS
    slopes = [slope_ref[g * r_heads + r] for r in range(r_heads)]
    m0 = jnp.concatenate([jnp.full((tq, 1), sink_ref[g * r_heads + r], F32) for r in range(r_heads)], axis=0)
    l0 = jnp.ones((r_heads * tq, 1), F32)
    lo = jnp.maximum(i - SWA_WINDOW // tq, 0)
    qs = _stack_heads(q_ref[0], r_heads)
    l, acc = _flash(qs, kv_ref, 0, i, lo, slopes, tq=tq, window=SWA_WINDOW, m0=m0, l0=l0)
    o = acc / l
    o_ref[0] = _pack_upper([o[h * tq:(h + 1) * tq] for h in range(r_heads)]).astype(BF16)


def _swa_attn(proj3, slopes, sinks):
    b, t, _ = proj3.shape
    r_heads = C_HEADS // C_KV_HEADS
    smem = pl.BlockSpec(memory_space=pltpu.SMEM)
    return pl.pallas_call(
        _swa_kernel,
        grid=(b, C_KV_HEADS, t // TQ),
        in_specs=[smem, smem,
                  pl.BlockSpec((1, TQ, r_heads * LANES), lambda bi, g, i: (bi, i, QC_BLK // r_heads + g)),
                  pl.BlockSpec((1, t, LANES), lambda bi, g, i: (bi, 0, KVC_BLK + g))],
        out_specs=pl.BlockSpec((1, TQ, r_heads * HEAD_DIM), lambda bi, g, i: (bi, i, g)),
        out_shape=jax.ShapeDtypeStruct((b, t, C_HEADS * HEAD_DIM), BF16),
        compiler_params=_cparams(("parallel", "parallel", "parallel")),
        name="swa_gqa",
    )(jnp.asarray(slopes, F32), sinks.astype(F32), proj3, proj3)


def _diff_kernel(slope_ref, dl_ref, g_ref, q_ref, kv_ref, o_ref, *, lam_init):
    tq = q_ref.shape[1]
    hp = pl.program_id(1)
    i = pl.program_id(2)
    dl = dl_ref[...]
    lam = (jnp.exp(jnp.sum(dl[0:1] * dl[1:2], axis=-1, keepdims=True))
           - jnp.exp(jnp.sum(dl[2:3] * dl[3:4], axis=-1, keepdims=True)) + lam_init)
    lane = _lane((tq, LANES))
    outs = []
    for hh in range(2):
        q = q_ref[0, :, hh * LANES:(hh + 1) * LANES]
        zero = jnp.zeros_like(q)
        qs = jnp.concatenate([jnp.where(lane < DIFF_DK, q, zero),
                              jnp.where(lane >= DIFF_DK, q, zero)], axis=0)
        slope = slope_ref[hp * 2 + hh]
        l, acc = _flash(qs, kv_ref, hh * LANES, i, jnp.int32(0), [slope, slope], tq=tq)
        o = acc / l
        w = jnp.where(lane >= HEAD_DIM, o[:tq] - lam * o[tq:], 0.0)
        ms = jnp.sum(w * w, axis=-1, keepdims=True) * (1.0 / HEAD_DIM)
        outs.append(w * lax.rsqrt(ms + LN_EPS) * g_ref[...] * (1.0 - lam_init))
    o_ref[0] = _pack_upper(outs).astype(BF16)


def _diff_attn(proj3, slopes, diff_lambda, subln, lam_init):
    b, t, _ = proj3.shape
    g_ext = jnp.concatenate([jnp.zeros((1, HEAD_DIM), F32), subln.reshape(1, HEAD_DIM).astype(F32)], axis=1)
    return pl.pallas_call(
        functools.partial(_diff_kernel, lam_init=lam_init),
        grid=(b, B_HEADS // 2, t // TQ),
        in_specs=[pl.BlockSpec(memory_space=pltpu.SMEM),
                  pl.BlockSpec((4, DIFF_DK), lambda bi, h, i: (0, 0)),
                  pl.BlockSpec((1, LANES), lambda bi, h, i: (0, 0)),
                  pl.BlockSpec((1, TQ, 2 * LANES), lambda bi, h, i: (bi, i, QB_BLK // 2 + h)),
                  pl.BlockSpec((1, t, 2 * LANES), lambda bi, h, i: (bi, 0, KVB_BLK // 2 + h))],
        out_specs=pl.BlockSpec((1, TQ, 2 * HEAD_DIM), lambda bi, h, i: (bi, i, h)),
        out_shape=jax.ShapeDtypeStruct((b, t, B_HEADS * HEAD_DIM), BF16),
        compiler_params=_cparams(("parallel", "parallel", "parallel")),
        name="diff_attn",
    )(jnp.asarray(slopes, F32), diff_lambda.astype(F32), g_ext, proj3, proj3)


def _outproj_kernel(oc_ref, os_ref, ow_ref, gl_ref, ob_ref, osw_ref, x_ref, w_ref, ex_ref, g_ref, b_ref, o_ref):
    hi, lo = _split2(_sigmoid(gl_ref[...]))
    oa = None
    for j, ref in enumerate((oc_ref, os_ref, ow_ref)):
        gate = _dot(hi, ex_ref[j]) + _dot(lo, ex_ref[j])
        term = gate * ref[...].astype(F32)
        oa = term if oa is None else oa + term
    cat = jnp.concatenate([oa.astype(BF16), ob_ref[...], osw_ref[...]], axis=1)
    y = DN_ALPHA * x_ref[...] + _dot(cat, w_ref[...])
    o_ref[...] = _layer_norm(y, g_ref[...], b_ref[...])


def _outproj(o_c, o_s, o_w, gl, o_b, o_sw, x2d, w_out, ln_g, ln_b):
    n = x2d.shape[0]
    ex = np.zeros((3, LANES, A_HEADS * HEAD_DIM), np.float32)
    for j in range(3):
        for h in range(A_HEADS):
            ex[j, h * 3 + j, h * HEAD_DIM:(h + 1) * HEAD_DIM] = 1.0
    row = lambda w: pl.BlockSpec((TM, w), lambda i: (i, 0))
    full = lambda s: pl.BlockSpec(s, lambda i: (0,) * len(s))
    return pl.pallas_call(
        _outproj_kernel,
        grid=(n // TM,),
        in_specs=[row(256), row(256), row(256), row(LANES), row(256), row(512), row(D_MODEL),
                  full((D_MODEL, D_MODEL)), full(ex.shape), full((1, D_MODEL)), full((1, D_MODEL))],
        out_specs=row(D_MODEL),
        out_shape=jax.ShapeDtypeStruct((n, D_MODEL), F32),
        compiler_params=_cparams(("parallel",)),
        name="out_proj_ln",
    )(o_c, o_s, o_w, gl, o_b, o_sw, x2d, w_out.astype(BF16), jnp.asarray(ex, BF16),
      ln_g.reshape(1, -1), ln_b.reshape(1, -1))


def _ffn_kernel(x_ref, wg_ref, wu_ref, wd_ref, o_ref, xb_ref):
    c = pl.program_id(1)

    @pl.when(c == 0)
    def _():
        xb_ref[...] = x_ref[...].astype(BF16)

    xb = xb_ref[...]
    g = _dot(xb, wg_ref[...])
    u = _dot(xb, wu_ref[...])
    y = _dot((g * _sigmoid(g) * u).astype(BF16), wd_ref[...])

    @pl.when(c == 0)
    def _():
        o_ref[...] = y

    @pl.when(c > 0)
    def _():
        o_ref[...] += y


def _ffn(x2d, wg, wu, wd):
    n = x2d.shape[0]
    nc = D_FF_DENSE // FFN_CHUNK
    return pl.pallas_call(
        _ffn_kernel,
        grid=(n // TM, nc),
        in_specs=[pl.BlockSpec((TM, D_MODEL), lambda i, c: (i, 0)),
                  pl.BlockSpec((D_MODEL, FFN_CHUNK), lambda i, c: (0, c)),
                  pl.BlockSpec((D_MODEL, FFN_CHUNK), lambda i, c: (0, c)),
                  pl.BlockSpec((FFN_CHUNK, D_MODEL), lambda i, c: (c, 0))],
        out_specs=pl.BlockSpec((TM, D_MODEL), lambda i, c: (i, 0)),
        out_shape=jax.ShapeDtypeStruct((n, D_MODEL), F32),
        scratch_shapes=[pltpu.VMEM((TM, D_MODEL), BF16)],
        compiler_params=_cparams(("parallel", "arbitrary")),
        name="dense_swiglu",
    )(x2d, wg.astype(BF16), wu.astype(BF16), wd.astype(BF16))


def _router_kernel(x_ref, w_ref, tri_ref, xb_ref, gd_ref, scol_ref, srow_ref, base_ref, cnt_ref):
    tt = x_ref.shape[0]
    x = x_ref[...]
    xb_ref[...] = x.astype(BF16)
    xh, xl = _split2(x)
    wh, wl = _split2(w_ref[...])
    logits = _dot(xh, wh) + _dot(xh, wl) + _dot(xl, wh)
    lane = _lane((tt, LANES))
    lf = lane.astype(F32)
    logits = jnp.where(lane < N_EXPERTS, logits, -jnp.inf)
    m1 = jnp.max(logits, axis=-1, keepdims=True)
    i1 = jnp.min(jnp.where(logits == m1, lf, float(LANES)), axis=-1, keepdims=True)
    rest = jnp.where(lf == i1, -jnp.inf, logits)
    m2 = jnp.max(rest, axis=-1, keepdims=True)
    i2 = jnp.min(jnp.where(rest == m2, lf, float(LANES)), axis=-1, keepdims=True)
    e2 = jnp.exp(m2 - m1)
    g1 = 1.0 / (1.0 + e2)
    g2 = e2 / (1.0 + e2)
    first, second = lf == i1, lf == i2
    gd_ref[...] = jnp.where(first, g1, 0.0) + jnp.where(second, g2, 0.0)
    ind = jnp.where(first, 1.0, jnp.where(second, 1.0, 0.0))
    tri = tri_ref[...]
    run = jnp.zeros((1, LANES), F32)
    slots, bases = [], []
    for tb in range(tt // MOE_TB):
        blk = ind[tb * MOE_TB:(tb + 1) * MOE_TB]
        rank = _dot(tri, blk.astype(BF16)) + run
        slots.append(jnp.where(blk > 0.5, rank, -1.0))
        bases.append(run)
        run = run + jnp.sum(blk, axis=0, keepdims=True)
    slot = jnp.concatenate(slots, axis=0)
    scol_ref[...] = slot
    srow_ref[...] = jnp.transpose(slot)[0:N_EXPERTS]
    base_ref[0] = jnp.concatenate(bases, axis=0)
    cnt_ref[0] = jnp.broadcast_to(run, (8, LANES))


def _route(x2d, w_router):
    n = x2d.shape[0]
    nt = n // MOE_TILE
    ntb = MOE_TILE // MOE_TB
    w = jnp.zeros((D_MODEL, LANES), F32).at[:, :N_EXPERTS].set(w_router)
    tri = jnp.asarray(np.tril(np.ones((MOE_TB, MOE_TB), np.float32), -1), BF16)
    return pl.pallas_call(
        _router_kernel,
        grid=(nt,),
        in_specs=[pl.BlockSpec((MOE_TILE, D_MODEL), lambda i: (i, 0)),
                  pl.BlockSpec((D_MODEL, LANES), lambda i: (0, 0)),
                  pl.BlockSpec((MOE_TB, MOE_TB), lambda i: (0, 0))],
        out_specs=[pl.BlockSpec((MOE_TILE, D_MODEL), lambda i: (i, 0)),
                   pl.BlockSpec((MOE_TILE, LANES), lambda i: (i, 0)),
                   pl.BlockSpec((MOE_TILE, LANES), lambda i: (i, 0)),
                   pl.BlockSpec((N_EXPERTS, MOE_TILE), lambda i: (0, i)),
                   pl.BlockSpec((1, ntb, LANES), lambda i: (i, 0, 0)),
                   pl.BlockSpec((1, 8, LANES), lambda i: (i, 0, 0))],
        out_shape=[jax.ShapeDtypeStruct((n, D_MODEL), BF16),
                   jax.ShapeDtypeStruct((n, LANES), F32),
                   jax.ShapeDtypeStruct((n, LANES), F32),
                   jax.ShapeDtypeStruct((N_EXPERTS, n), F32),
                   jax.ShapeDtypeStruct((nt, ntb, LANES), F32),
                   jax.ShapeDtypeStruct((nt, 8, LANES), F32)],
        compiler_params=_cparams(("parallel",)),
        name="moe_router",
    )(x2d, w, tri)


MOE_GATHER_W = MOE_TB + 16
MOE_SCATTER_W = MOE_TB + 8
MOE_CAP = MOE_TILE + 512


def _moe_kernel(cnt_ref, base_ref, xb_ref, srow_ref, scol_ref, gd_ref, wg_ref, wu_ref, wd_ref,
                o_ref, xc_ref, yc_ref):
    i, e, c = pl.program_id(0), pl.program_id(1), pl.program_id(2)
    nc = pl.num_programs(2)
    ntb = MOE_TILE // MOE_TB
    nb = jnp.right_shift(cnt_ref[i * N_EXPERTS + e] + (MOE_ROWS - 1), MOE_ROWS.bit_length() - 1)

    @pl.when((i == 0) & (e == 0) & (c == 0))
    def _():
        yc_ref[...] = jnp.zeros_like(yc_ref)

    @pl.when((e == 0) & (c == 0))
    def _():
        o_ref[...] = jnp.zeros_like(o_ref)

    @pl.when(c == 0)
    def _():
        xc_ref[...] = jnp.zeros_like(xc_ref)
        srow = srow_ref[pl.ds(e, 1), :]
        rid = lax.broadcasted_iota(jnp.int32, (MOE_GATHER_W, MOE_TB), 0).astype(F32)
        for tb in range(ntb):
            b0 = base_ref[(i * ntb + tb) * N_EXPERTS + e]
            b_al = pl.multiple_of(jnp.left_shift(jnp.right_shift(b0, 4), 4), 16)
            rel = srow[:, tb * MOE_TB:(tb + 1) * MOE_TB] - b_al.astype(F32)
            onehot = jnp.where(rel == rid, 1.0, 0.0).astype(BF16)
            rows = _dot(onehot, xb_ref[tb * MOE_TB:(tb + 1) * MOE_TB, :])
            cur = xc_ref[pl.ds(b_al, MOE_GATHER_W), :].astype(F32)
            xc_ref[pl.ds(b_al, MOE_GATHER_W), :] = (cur + rows).astype(BF16)

    def rows_body(rb, carry):
        r0 = pl.multiple_of(rb * MOE_ROWS, MOE_ROWS)
        xr = xc_ref[pl.ds(r0, MOE_ROWS), :]
        g = _dot(xr, wg_ref[0])
        u = _dot(xr, wu_ref[0])
        y = _dot((g * _sigmoid(g) * u).astype(BF16), wd_ref[0])
        prev = yc_ref[pl.ds(r0, MOE_ROWS), :]
        yc_ref[pl.ds(r0, MOE_ROWS), :] = jnp.where(c == 0, y, prev + y)
        return carry

    lax.fori_loop(0, nb, rows_body, 0)

    @pl.when(c == nc - 1)
    def _():
        lane = _lane((MOE_TB, LANES))
        cid = lax.broadcasted_iota(jnp.int32, (MOE_TB, MOE_SCATTER_W), 1).astype(F32)
        for tb in range(ntb):
            b0 = base_ref[(i * ntb + tb) * N_EXPERTS + e]
            b_al = pl.multiple_of(jnp.left_shift(jnp.right_shift(b0, 3), 3), 8)
            rows = slice(tb * MOE_TB, (tb + 1) * MOE_TB)
            slot = jnp.sum(jnp.where(lane == e, scol_ref[rows, :], 0.0), axis=-1, keepdims=True)
            gate = jnp.sum(jnp.where(lane == e, gd_ref[rows, :], 0.0), axis=-1, keepdims=True)
            onehot = jnp.where(slot - b_al.astype(F32) == cid, 1.0, 0.0).astype(BF16)
            yh, yl = _split2(yc_ref[pl.ds(b_al, MOE_SCATTER_W), :])
            o_ref[rows, :] += gate * (_dot(onehot, yh) + _dot(onehot, yl))


def _moe(xb, srow, scol, gd, cnt, base, wg, wu, wd):
    n = xb.shape[0]
    nt = n // MOE_TILE
    nc = D_FF_EXPERT // MOE_CHUNK
    grid_spec = pltpu.PrefetchScalarGridSpec(
        num_scalar_prefetch=2,
        grid=(nt, N_EXPERTS, nc),
        in_specs=[pl.BlockSpec((MOE_TILE, D_MODEL), lambda i, e, c, *_: (i, 0)),
                  pl.BlockSpec((N_EXPERTS, MOE_TILE), lambda i, e, c, *_: (0, i)),
                  pl.BlockSpec((MOE_TILE, LANES), lambda i, e, c, *_: (i, 0)),
                  pl.BlockSpec((MOE_TILE, LANES), lambda i, e, c, *_: (i, 0)),
                  pl.BlockSpec((1, D_MODEL, MOE_CHUNK), lambda i, e, c, *_: (e, 0, c)),
                  pl.BlockSpec((1, D_MODEL, MOE_CHUNK), lambda i, e, c, *_: (e, 0, c)),
                  pl.BlockSpec((1, MOE_CHUNK, D_MODEL), lambda i, e, c, *_: (e, c, 0))],
        out_specs=pl.BlockSpec((MOE_TILE, D_MODEL), lambda i, e, c, *_: (i, 0)),
        scratch_shapes=[pltpu.VMEM((MOE_CAP, D_MODEL), BF16), pltpu.VMEM((MOE_CAP, D_MODEL), F32)],
    )
    return pl.pallas_call(
        _moe_kernel,
        grid_spec=grid_spec,
        out_shape=jax.ShapeDtypeStruct((n, D_MODEL), F32),
        compiler_params=_cparams(("arbitrary", "arbitrary", "arbitrary")),
        name="moe_experts",
    )(cnt, base, xb, srow, scol, gd, wg.astype(BF16), wu.astype(BF16), wd.astype(BF16))


def _ple_kernel(x_ref, f_ref, p_ref, wg_ref, wp_ref, g2_ref, b2_ref, g3_ref, b3_ref, o_ref):
    x2 = _layer_norm(DN_ALPHA * x_ref[...] + f_ref[...], g2_ref[...], b2_ref[...])
    e = _sigmoid(_dot(x2.astype(BF16), wg_ref[...])) * _dot(p_ref[...].astype(BF16), wp_ref[...])
    o_ref[...] = _layer_norm(DN_ALPHA * x2 + e, g3_ref[...], b3_ref[...])


def _ple(x2d, f2d, p2d, w_gate, w_proj, g2, b2, g3, b3):
    n = x2d.shape[0]
    row = lambda w: pl.BlockSpec((TM, w), lambda i: (i, 0))
    full = lambda s: pl.BlockSpec(s, lambda i: (0,) * len(s))
    vec = full((1, D_MODEL))
    return pl.pallas_call(
        _ple_kernel,
        grid=(n // TM,),
        in_specs=[row(D_MODEL), row(D_MODEL), row(PLE_DIM), full((D_MODEL, D_MODEL)), full((PLE_DIM, D_MODEL)),
                  vec, vec, vec, vec],
        out_specs=row(D_MODEL),
        out_shape=jax.ShapeDtypeStruct((n, D_MODEL), F32),
        compiler_params=_cparams(("parallel",)),
        name="ln_ple_ln",
    )(x2d, f2d, p2d, w_gate.astype(BF16), w_proj.astype(BF16),
      g2.reshape(1, -1), b2.reshape(1, -1), g3.reshape(1, -1), b3.reshape(1, -1))


def _mixer(x2d, b, t, i, w_in, cmp_pos, cmp_w1, cmp_w2, diff_lambda, diff_subln, sinks):
    slopes_a, slopes_b, slopes_c = _alibi_slopes()
    proj, gl = _project(x2d, _build_w_in(w_in))
    proj3 = proj.reshape(b, t, N_MAIN)
    kvc = _compress(proj3, cmp_pos, cmp_w1, cmp_w2)
    o_c, sel = _cmp_attn(proj3, kvc, slopes_a)
    o_s = _nsa_attn(proj3, KVA_BLK + 1, slopes_a, sel=sel)
    o_w = _nsa_attn(proj3, KVA_BLK + 2, slopes_a, window=NSA_WINDOW)
    lam_init = 0.8 - 0.6 * math.exp(-0.3 * i)
    o_b = _diff_attn(proj3, slopes_b, diff_lambda, diff_subln, lam_init)
    o_sw = _swa_attn(proj3, slopes_c, sinks)
    flat = lambda a: a.reshape(b * t, a.shape[-1])
    return flat(o_c), flat(o_s), flat(o_w), gl, flat(o_b), flat(o_sw)


def kernel(x, p, w_in, cmp_pos, cmp_w1, cmp_w2, diff_lambda, diff_subln, sinks, w_out, ln1_g, ln1_b,
           ffn_w_gate, ffn_w_up, ffn_w_down, moe_router, moe_w_gate, moe_w_up, moe_w_down, ln2_g, ln2_b,
           ple_gate, ple_proj, ln3_g, ln3_b):
    b, t, d = x.shape
    n = b * t
    x2d = x.reshape(n, d)
    for i in range(DEPTH):
        heads = _mixer(x2d, b, t, i, w_in[i], cmp_pos[i], cmp_w1[i], cmp_w2[i],
                       diff_lambda[i], diff_subln[i], sinks[i])
        x1 = _outproj(*heads, x2d, w_out[i], ln1_g[i], ln1_b[i])
        if i % 2 == 0:
            f = _ffn(x1, ffn_w_gate[i // 2], ffn_w_up[i // 2], ffn_w_down[i // 2])
        else:
            xb, gd, scol, srow, base, cnt = _route(x1, moe_router[i // 2])
            cnt_i = cnt[:, 0, :N_EXPERTS].astype(jnp.int32).reshape(-1)
            base_i = base[:, :, :N_EXPERTS].astype(jnp.int32).reshape(-1)
            f = _moe(xb, srow, scol, gd, cnt_i, base_i,
                     moe_w_gate[i // 2], moe_w_up[i // 2], moe_w_down[i // 2])
        x2d = _ple(x1, f, p[i].reshape(n, PLE_DIM), ple_gate[i], ple_proj[i],
                   ln2_g[i], ln2_b[i], ln3_g[i], ln3_b[i])
    return x2d.reshape(b, t, d)
```

```python
import functools
import math

import jax
import jax.numpy as jnp
import numpy as np
from jax import lax
from jax.experimental import pallas as pl
from jax.experimental.pallas import tpu as pltpu

F32 = jnp.float32
BF16 = jnp.bfloat16

D_MODEL = 1024
HEAD_DIM = 64
A_HEADS = 4
B_HEADS = 4
C_HEADS = 8
C_KV_HEADS = 2
DIFF_DK = 32
CMP_LEN = 32
CMP_STRIDE = 16
CMP_HIDDEN = 128
SLC_BLOCK = 64
SLC_SHIFT = 6
SLC_TOPN = 16
NSA_WINDOW = 512
SWA_WINDOW = 128
N_EXPERTS = 8
D_FF_EXPERT = 3584
D_FF_DENSE = 2816
PLE_DIM = 256
LN_EPS = 1e-5
NEG_INF = -1e30
FORCE_SCORE = 1e9
DEPTH = 2
DN_ALPHA = (2 * DEPTH) ** 0.25

LANES = 128
QA_BLK, QB_BLK, QC_BLK, KVB_BLK, KVC_BLK, KVA_BLK, N_MAIN_BLK = 0, 4, 8, 16, 20, 22, 25
N_MAIN = N_MAIN_BLK * LANES

LOG2E = 1.4426950408889634
TQ = 256
TK_DENSE = 512
TM = 512
FFN_CHUNK = 1408
MOE_TILE = 2048
MOE_ROWS = 256
MOE_TB = 256
MOE_CHUNK = 512
VMEM_LIMIT = 56 * 1024 * 1024


def _cparams(sem):
    return pltpu.CompilerParams(dimension_semantics=sem, vmem_limit_bytes=VMEM_LIMIT)


def _alibi_slopes():
    n = A_HEADS + B_HEADS + C_HEADS
    s = [LOG2E * 2.0 ** (-8.0 * i / n) for i in range(1, n + 1)]
    rest = s[C_HEADS:]
    return rest[0::2], rest[1::2], s[:C_HEADS]


def _dot(a, b):
    return jnp.dot(a, b, preferred_element_type=F32)


def _dot_nt(a, b):
    return lax.dot_general(a, b, (((1,), (1,)), ((), ())), preferred_element_type=F32)


def _split2(x):
    hi = x.astype(BF16)
    lo = (x - hi.astype(F32)).astype(BF16)
    return hi, lo


def _layer_norm(y, g, b):
    mu = jnp.mean(y, axis=-1, keepdims=True)
    d = y - mu
    var = jnp.mean(d * d, axis=-1, keepdims=True)
    return d * lax.rsqrt(var + LN_EPS) * g + b


def _sigmoid(x):
    return 1.0 / (1.0 + jnp.exp(-x))


def _lane(shape):
    return lax.broadcasted_iota(jnp.int32, shape, 1)


def _pack_upper(parts):
    out = []
    for a, b in zip(parts[0::2], parts[1::2]):
        lane = _lane(a.shape)
        out.append(jnp.where(lane < HEAD_DIM, pltpu.roll(a, HEAD_DIM, 1), b))
    return out[0] if len(out) == 1 else jnp.concatenate(out, axis=1)


def _proj_kernel(x_ref, w_ref, o_ref, g_ref):
    acc = _dot(x_ref[...].astype(BF16), w_ref[...])
    o_ref[...] = acc[:, :N_MAIN].astype(BF16)
    g_ref[...] = acc[:, N_MAIN:]


def _project(x2d, w_ext):
    n = x2d.shape[0]
    return pl.pallas_call(
        _proj_kernel,
        grid=(n // TM,),
        in_specs=[pl.BlockSpec((TM, D_MODEL), lambda i: (i, 0)),
                  pl.BlockSpec((D_MODEL, N_MAIN + LANES), lambda i: (0, 0))],
        out_specs=[pl.BlockSpec((TM, N_MAIN), lambda i: (i, 0)),
                   pl.BlockSpec((TM, LANES), lambda i: (i, 0))],
        out_shape=[jax.ShapeDtypeStruct((n, N_MAIN), BF16),
                   jax.ShapeDtypeStruct((n, LANES), F32)],
        compiler_params=_cparams(("parallel",)),
        name="in_proj",
    )(x2d, w_ext)


def _build_w_in(w):
    d = w.shape[0]
    o = np.cumsum([0, 256, 384, 12, 256, 256, 256, 512, 128, 128])
    q_a, kv_a, g_a, q_b, k_b, v_b, q_c, k_c, v_c = [w[:, o[i]:o[i + 1]] for i in range(9)]
    z = jnp.zeros((d, HEAD_DIM), F32)
    hd = HEAD_DIM
    cols = []
    for h in range(A_HEADS):
        cols += [q_a[:, h * hd:(h + 1) * hd] * (LOG2E * hd ** -0.5), z]
    for h in range(B_HEADS):
        cols += [q_b[:, h * hd:(h + 1) * hd] * (LOG2E * DIFF_DK ** -0.5), z]
    for h in range(C_HEADS):
        cols += [q_c[:, h * hd:(h + 1) * hd] * (LOG2E * hd ** -0.5), z]
    for h in range(B_HEADS):
        cols += [k_b[:, h * hd:(h + 1) * hd], v_b[:, h * hd:(h + 1) * hd]]
    for g in range(C_KV_HEADS):
        cols += [k_c[:, g * hd:(g + 1) * hd], v_c[:, g * hd:(g + 1) * hd]]
    cols.append(kv_a)
    cols += [g_a, jnp.zeros((d, LANES - 12), F32)]
    return jnp.concatenate(cols, axis=1).astype(BF16)


def _compress_kernel(ch_ref, wc_ref, pos_ref, w1_ref, w2_ref, o_ref):
    nck = ch_ref.shape[1]
    u = _dot(ch_ref[0], wc_ref[...])
    hs = []
    for j in range(2):
        pw = _dot(pos_ref[j].astype(BF16), w1_ref[j])[0:1]
        ua = u[:, (2 * j) * LANES:(2 * j + 1) * LANES]
        ub = u[:, (2 * j + 1) * LANES:(2 * j + 2) * LANES]
        pre = ua + pltpu.roll(ub, nck - 1, 0) + pw
        hs.append(0.5 * pre * (1.0 + jnp.tanh(0.7978845608028654 * (pre + 0.044715 * pre * pre * pre))))
    h = jnp.concatenate(hs, axis=1).astype(BF16)
    o_ref[0] = _dot(h, w2_ref[...]).astype(BF16)


def _compress(proj3, cmp_pos, cmp_w1, cmp_w2):
    b, t, _ = proj3.shape
    nck = t // CMP_STRIDE
    kv = proj3[:, :, KVA_BLK * LANES:(KVA_BLK + 1) * LANES]
    chunks = kv.reshape(b, nck, CMP_STRIDE * LANES)
    w1 = cmp_w1.reshape(2, 2, CMP_STRIDE, HEAD_DIM, CMP_HIDDEN)
    z = jnp.zeros((CMP_STRIDE, HEAD_DIM, CMP_HIDDEN), F32)
    blocks = []
    for j in range(2):
        for half in range(2):
            pair = [w1[j, half], z] if j == 0 else [z, w1[j, half]]
            blocks.append(jnp.concatenate(pair, axis=1).reshape(CMP_STRIDE * LANES, CMP_HIDDEN))
    wc = jnp.concatenate(blocks, axis=1).astype(BF16)
    pos = jnp.zeros((2, 8, CMP_LEN * HEAD_DIM), F32).at[:, 0].set(cmp_pos.reshape(2, -1))
    z2 = jnp.zeros((CMP_HIDDEN, HEAD_DIM), F32)
    w2 = jnp.concatenate([jnp.concatenate([cmp_w2[0], z2], axis=1),
                          jnp.concatenate([z2, cmp_w2[1]], axis=1)], axis=0).astype(BF16)
    return pl.pallas_call(
        _compress_kernel,
        grid=(b,),
        in_specs=[pl.BlockSpec((1, nck, CMP_STRIDE * LANES), lambda i: (i, 0, 0)),
                  pl.BlockSpec(wc.shape, lambda i: (0, 0)),
                  pl.BlockSpec(pos.shape, lambda i: (0, 0, 0)),
                  pl.BlockSpec((2, CMP_LEN * HEAD_DIM, CMP_HIDDEN), lambda i: (0, 0, 0)),
                  pl.BlockSpec(w2.shape, lambda i: (0, 0))],
        out_specs=pl.BlockSpec((1, nck, LANES), lambda i: (i, 0, 0)),
        out_shape=jax.ShapeDtypeStruct((b, nck, LANES), BF16),
        compiler_params=_cparams(("parallel",)),
        name="nsa_compress",
    )(chunks, wc, pos, cmp_w1.astype(BF16), w2)


def _cmp_attn_kernel(q_ref, kvc_ref, ov_ref, o_ref, sel_ref, *, slopes):
    tq = q_ref.shape[1]
    ncp = kvc_ref.shape[1]
    t0 = pl.program_id(1) * tq
    kvc = kvc_ref[0]
    q = q_ref[0]
    qs = jnp.concatenate([q[:, h * LANES:(h + 1) * LANES] for h in range(A_HEADS)], axis=0)
    s = _dot_nt(qs, kvc)
    row = lax.broadcasted_iota(jnp.int32, (tq, ncp), 0) + t0
    col = _lane((tq, ncp))
    dist_i = row - (col * CMP_STRIDE + (CMP_LEN - 1))
    keep = (dist_i >= 0) & (col < ncp - 1)
    dist = dist_i.astype(F32)
    ps = []
    for h in range(A_HEADS):
        sh = jnp.where(keep, s[h * tq:(h + 1) * tq] - slopes[h] * dist, NEG_INF)
        m = jnp.max(sh, axis=-1, keepdims=True)
        e = jnp.where(keep, jnp.exp2(sh - m), 0.0)
        ps.append(e / jnp.maximum(jnp.sum(e, axis=-1, keepdims=True), 1e-30))
    p = jnp.concatenate(ps, axis=0)
    o = _dot(p.astype(BF16), kvc)
    o_ref[0] = _pack_upper([o[h * tq:(h + 1) * tq] for h in range(A_HEADS)]).astype(BF16)

    psum = ps[0] + ps[1] + ps[2] + ps[3]
    hi = psum.astype(BF16)
    r1 = psum - hi.astype(F32)
    mid = r1.astype(BF16)
    lo = (r1 - mid.astype(F32)).astype(BF16)
    ov = ov_ref[...]
    imp = _dot(hi, ov) + _dot(mid, ov) + _dot(lo, ov)
    nsl = ncp * CMP_STRIDE // SLC_BLOCK
    j = _lane((tq, LANES))
    t_blk = jnp.right_shift(lax.broadcasted_iota(jnp.int32, (tq, LANES), 0) + t0, SLC_SHIFT)
    forced = (j == 0) | (j == t_blk) | (j == t_blk - 1)
    imp = jnp.where(forced, FORCE_SCORE, jnp.where(j > t_blk, -1.0, imp))
    work = jnp.where(j < nsl, imp, -3.0)
    jf = j.astype(F32)
    sel = jnp.zeros((tq, LANES), F32)
    for _ in range(SLC_TOPN):
        mx = jnp.max(work, axis=-1, keepdims=True)
        first = jnp.min(jnp.where(work == mx, jf, float(LANES)), axis=-1, keepdims=True)
        pick = jf == first
        sel = jnp.where(pick, 1.0, sel)
        work = jnp.where(pick, -4.0, work)
    sel_ref[0] = sel.astype(BF16)


def _cmp_attn(proj3, kvc, slopes):
    b, t, _ = proj3.shape
    ncp = kvc.shape[1]
    nsl = t // SLC_BLOCK
    c0 = np.arange(ncp)[:, None] * CMP_STRIDE
    s0 = np.arange(LANES)[None, :] * SLC_BLOCK
    ov = np.clip(np.minimum(c0 + CMP_LEN, s0 + SLC_BLOCK) - np.maximum(c0, s0), 0, None) / CMP_LEN
    ov[ncp - 1, :] = 0.0
    ov[:, nsl:] = 0.0
    ov = jnp.asarray(ov, BF16)
    return pl.pallas_call(
        functools.partial(_cmp_attn_kernel, slopes=slopes),
        grid=(b, t // TQ),
        in_specs=[pl.BlockSpec((1, TQ, A_HEADS * LANES), lambda bi, i: (bi, i, QA_BLK // A_HEADS)),
                  pl.BlockSpec((1, ncp, LANES), lambda bi, i: (bi, 0, 0)),
                  pl.BlockSpec((ncp, LANES), lambda bi, i: (0, 0))],
        out_specs=[pl.BlockSpec((1, TQ, A_HEADS * HEAD_DIM), lambda bi, i: (bi, i, 0)),
                   pl.BlockSpec((1, TQ, LANES), lambda bi, i: (bi, i, 0))],
        out_shape=[jax.ShapeDtypeStruct((b, t, A_HEADS * HEAD_DIM), BF16),
                   jax.ShapeDtypeStruct((b, t, LANES), BF16)],
        compiler_params=_cparams(("parallel", "parallel")),
        name="nsa_cmp_attn",
    )(proj3, kvc, ov)


def _flash(streams, kv_ref, i, *, tq, tk, sel=None):
    ratio = tk // tq
    n_full = lax.div(i, jnp.int32(ratio))
    rc = lax.broadcasted_iota(jnp.int32, (tq, tk), 0) - _lane((tq, tk))
    cidx = _lane((1, tk)).astype(F32)

    def tile(kt, carry, edge):
        k0 = pl.multiple_of(kt * tk, tk)
        cpos = cidx + (kt * tk).astype(F32)
        keep = None
        if edge:
            keep = rc >= kt * tk - i * tq
        if sel is not None:
            blk = jnp.right_shift(lax.broadcasted_iota(jnp.int32, (LANES, tk), 1) + kt * tk, SLC_SHIFT)
            expand = jnp.where(lax.broadcasted_iota(jnp.int32, (LANES, tk), 0) == blk, 1.0, 0.0).astype(BF16)
            chosen = _dot(sel, expand)
            if keep is not None:
                chosen = jnp.where(keep, chosen, 0.0)
            keep = chosen > 0.5
        out = []
        for (qs, col, slopes), (m, l, acc) in zip(streams, carry):
            kvt = kv_ref[0, pl.ds(k0, tk), col:col + LANES]
            s = _dot_nt(qs, kvt)
            parts = []
            for r, slope in enumerate(slopes):
                sr = s[r * tq:(r + 1) * tq] + slope * cpos
                if keep is not None:
                    sr = jnp.where(keep, sr, NEG_INF)
                parts.append(sr)
            s = jnp.concatenate(parts, axis=0)
            m_new = jnp.maximum(m, jnp.max(s, axis=-1, keepdims=True))
            alpha = jnp.exp2(m - m_new)
            p = jnp.exp2(s - m_new)
            l = alpha * l + jnp.sum(p, axis=-1, keepdims=True)
            acc = alpha * acc + _dot(p.astype(BF16), kvt)
            out.append((m_new, l, acc))
        return tuple(out)

    init = []
    for qs, _, _ in streams:
        rows = qs.shape[0]
        init.append((jnp.full((rows, 1), NEG_INF, F32), jnp.zeros((rows, 1), F32), jnp.zeros((rows, LANES), F32)))
    carry = lax.fori_loop(0, n_full, lambda kt, c: tile(kt, c, False), tuple(init))
    carry = tile(n_full, carry, True)
    return [(l, acc) for _, l, acc in carry]


def _window_attn(qs, kv_ref, i, slopes, *, tq, window, sinks=None):
    wk = window + tq
    start = pl.multiple_of(jnp.maximum(i * tq - window, 0), LANES)
    kvw = kv_ref[0, pl.ds(start, wk), :]
    s = _dot_nt(qs, kvw)
    t_row = lax.broadcasted_iota(jnp.int32, (tq, wk), 0) + i * tq
    c_abs = _lane((tq, wk)) + start
    keep = lax.bitcast_convert_type(t_row - c_abs, jnp.uint32) < jnp.uint32(window)
    cpos = (_lane((1, wk)) + start).astype(F32)
    t_col = (lax.broadcasted_iota(jnp.int32, (tq, 1), 0) + i * tq).astype(F32)
    ps, dens = [], []
    for r, slope in enumerate(slopes):
        sr = jnp.where(keep, s[r * tq:(r + 1) * tq] + slope * cpos, NEG_INF)
        m = jnp.max(sr, axis=-1, keepdims=True)
        if sinks is not None:
            sink = sinks[r] + slope * t_col
            m = jnp.maximum(m, sink)
        p = jnp.exp2(sr - m)
        den = jnp.sum(p, axis=-1, keepdims=True)
        if sinks is not None:
            den = den + jnp.exp2(sink - m)
        ps.append(p.astype(BF16))
        dens.append(den)
    return _dot(jnp.concatenate(ps, axis=0), kvw) / jnp.concatenate(dens, axis=0)


def _stack_heads(q, n):
    return jnp.concatenate([q[:, h * LANES:(h + 1) * LANES] for h in range(n)], axis=0)


def _nsa_attn_kernel(q_ref, kv_ref, *rest, slopes, window):
    tq = q_ref.shape[1]
    i = pl.program_id(1)
    qs = _stack_heads(q_ref[0], A_HEADS)
    if window is None:
        sel_ref, o_ref = rest
        ((l, acc),) = _flash([(qs, 0, slopes)], kv_ref, i, tq=tq, tk=TK_DENSE, sel=sel_ref[0])
        o = acc / l
    else:
        (o_ref,) = rest
        o = _window_attn(qs, kv_ref, i, slopes, tq=tq, window=window)
    o_ref[0] = _pack_upper([o[h * tq:(h + 1) * tq] for h in range(A_HEADS)]).astype(BF16)


def _nsa_attn(proj3, kv_blk, slopes, *, window=None, sel=None):
    b, t, _ = proj3.shape
    in_specs = [pl.BlockSpec((1, TQ, A_HEADS * LANES), lambda bi, i: (bi, i, QA_BLK // A_HEADS)),
                pl.BlockSpec((1, t, LANES), lambda bi, i: (bi, 0, kv_blk))]
    args = [proj3, proj3]
    if sel is not None:
        in_specs.append(pl.BlockSpec((1, TQ, LANES), lambda bi, i: (bi, i, 0)))
        args.append(sel)
    return pl.pallas_call(
        functools.partial(_nsa_attn_kernel, slopes=slopes, window=window),
        grid=(b, t // TQ),
        in_specs=in_specs,
        out_specs=pl.BlockSpec((1, TQ, A_HEADS * HEAD_DIM), lambda bi, i: (bi, i, 0)),
        out_shape=jax.ShapeDtypeStruct((b, t, A_HEADS * HEAD_DIM), BF16),
        compiler_params=_cparams(("parallel", "parallel")),
        name="nsa_window" if window is not None else "nsa_selected",
    )(*args)


def _swa_kernel(slope_ref, sink_ref, q_ref, kv_ref, o_ref):
    tq = q_ref.shape[1]
    g = pl.program_id(1)
    i = pl.program_id(2)
    r_heads = C_HEADS // C_KV_HEADS
    slopes = [slope_ref[g * r_heads + r] for r in range(r_heads)]
    sinks = [sink_ref[g * r_heads + r] for r in range(r_heads)]
    qs = _stack_heads(q_ref[0], r_heads)
    o = _window_attn(qs, kv_ref, i, slopes, tq=tq, window=SWA_WINDOW, sinks=sinks)
    o_ref[0] = _pack_upper([o[h * tq:(h + 1) * tq] for h in range(r_heads)]).astype(BF16)


def _swa_attn(proj3, slopes, sinks):
    b, t, _ = proj3.shape
    r_heads = C_HEADS // C_KV_HEADS
    smem = pl.BlockSpec(memory_space=pltpu.SMEM)
    return pl.pallas_call(
        _swa_kernel,
        grid=(b, C_KV_HEADS, t // TQ),
        in_specs=[smem, smem,
                  pl.BlockSpec((1, TQ, r_heads * LANES), lambda bi, g, i: (bi, i, QC_BLK // r_heads + g)),
                  pl.BlockSpec((1, t, LANES), lambda bi, g, i: (bi, 0, KVC_BLK + g))],
        out_specs=pl.BlockSpec((1, TQ, r_heads * HEAD_DIM), lambda bi, g, i: (bi, i, g)),
        out_shape=jax.ShapeDtypeStruct((b, t, C_HEADS * HEAD_DIM), BF16),
        compiler_params=_cparams(("parallel", "parallel", "parallel")),
        name="swa_gqa",
    )(jnp.asarray(slopes, F32), sinks.astype(F32) * LOG2E, proj3, proj3)


def _diff_kernel(slope_ref, dl_ref, g_ref, q_ref, kv_ref, o_ref, *, lam_init):
    tq = q_ref.shape[1]
    hp = pl.program_id(1)
    i = pl.program_id(2)
    dl = dl_ref[...]
    lam = (jnp.exp(jnp.sum(dl[0:1] * dl[1:2], axis=-1, keepdims=True))
           - jnp.exp(jnp.sum(dl[2:3] * dl[3:4], axis=-1, keepdims=True)) + lam_init)
    lane = _lane((tq, LANES))
    streams = []
    for hh in range(2):
        q = q_ref[0, :, hh * LANES:(hh + 1) * LANES]
        zero = jnp.zeros_like(q)
        qs = jnp.concatenate([jnp.where(lane < DIFF_DK, q, zero),
                              jnp.where(lane >= DIFF_DK, q, zero)], axis=0)
        slope = slope_ref[hp * 2 + hh]
        streams.append((qs, hh * LANES, [slope, slope]))
    outs = []
    for l, acc in _flash(streams, kv_ref, i, tq=tq, tk=TK_DENSE):
        o = acc / l
        w = jnp.where(lane >= HEAD_DIM, o[:tq] - lam * o[tq:], 0.0)
        ms = jnp.sum(w * w, axis=-1, keepdims=True) * (1.0 / HEAD_DIM)
        outs.append(w * lax.rsqrt(ms + LN_EPS) * g_ref[...] * (1.0 - lam_init))
    o_ref[0] = _pack_upper(outs).astype(BF16)


def _diff_attn(proj3, slopes, diff_lambda, subln, lam_init):
    b, t, _ = proj3.shape
    g_ext = jnp.concatenate([jnp.zeros((1, HEAD_DIM), F32), subln.reshape(1, HEAD_DIM).astype(F32)], axis=1)
    return pl.pallas_call(
        functools.partial(_diff_kernel, lam_init=lam_init),
        grid=(b, B_HEADS // 2, t // TQ),
        in_specs=[pl.BlockSpec(memory_space=pltpu.SMEM),
                  pl.BlockSpec((4, DIFF_DK), lambda bi, h, i: (0, 0)),
                  pl.BlockSpec((1, LANES), lambda bi, h, i: (0, 0)),
                  pl.BlockSpec((1, TQ, 2 * LANES), lambda bi, h, i: (bi, i, QB_BLK // 2 + h)),
                  pl.BlockSpec((1, t, 2 * LANES), lambda bi, h, i: (bi, 0, KVB_BLK // 2 + h))],
        out_specs=pl.BlockSpec((1, TQ, 2 * HEAD_DIM), lambda bi, h, i: (bi, i, h)),
        out_shape=jax.ShapeDtypeStruct((b, t, B_HEADS * HEAD_DIM), BF16),
        compiler_params=_cparams(("parallel", "parallel", "parallel")),
        name="diff_attn",
    )(jnp.asarray(slopes, F32), diff_lambda.astype(F32), g_ext, proj3, proj3)


def _outproj_kernel(oc_ref, os_ref, ow_ref, gl_ref, ob_ref, osw_ref, x_ref, w_ref, ex_ref, g_ref, b_ref, o_ref):
    hi, lo = _split2(_sigmoid(gl_ref[...]))
    oa = None
    for j, ref in enumerate((oc_ref, os_ref, ow_ref)):
        gate = _dot(hi, ex_ref[j]) + _dot(lo, ex_ref[j])
        term = gate * ref[...].astype(F32)
        oa = term if oa is None else oa + term
    cat = jnp.concatenate([oa.astype(BF16), ob_ref[...], osw_ref[...]], axis=1)
    y = DN_ALPHA * x_ref[...] + _dot(cat, w_ref[...])
    o_ref[...] = _layer_norm(y, g_ref[...], b_ref[...])


def _outproj(o_c, o_s, o_w, gl, o_b, o_sw, x2d, w_out, ln_g, ln_b):
    n = x2d.shape[0]
    ex = np.zeros((3, LANES, A_HEADS * HEAD_DIM), np.float32)
    for j in range(3):
        for h in range(A_HEADS):
            ex[j, h * 3 + j, h * HEAD_DIM:(h + 1) * HEAD_DIM] = 1.0
    row = lambda w: pl.BlockSpec((TM, w), lambda i: (i, 0))
    full = lambda s: pl.BlockSpec(s, lambda i: (0,) * len(s))
    return pl.pallas_call(
        _outproj_kernel,
        grid=(n // TM,),
        in_specs=[row(256), row(256), row(256), row(LANES), row(256), row(512), row(D_MODEL),
                  full((D_MODEL, D_MODEL)), full(ex.shape), full((1, D_MODEL)), full((1, D_MODEL))],
        out_specs=row(D_MODEL),
        out_shape=jax.ShapeDtypeStruct((n, D_MODEL), F32),
        compiler_params=_cparams(("parallel",)),
        name="out_proj_ln",
    )(o_c, o_s, o_w, gl, o_b, o_sw, x2d, w_out.astype(BF16), jnp.asarray(ex, BF16),
      ln_g.reshape(1, -1), ln_b.reshape(1, -1))


def _ffn_kernel(x_ref, wg_ref, wu_ref, wd_ref, o_ref, xb_ref):
    c = pl.program_id(1)

    @pl.when(c == 0)
    def _():
        xb_ref[...] = x_ref[...].astype(BF16)

    xb = xb_ref[...]
    g = _dot(xb, wg_ref[...])
    u = _dot(xb, wu_ref[...])
    y = _dot((g * _sigmoid(g) * u).astype(BF16), wd_ref[...])

    @pl.when(c == 0)
    def _():
        o_ref[...] = y

    @pl.when(c > 0)
    def _():
        o_ref[...] += y


def _ffn(x2d, wg, wu, wd):
    n = x2d.shape[0]
    nc = D_FF_DENSE // FFN_CHUNK
    return pl.pallas_call(
        _ffn_kernel,
        grid=(n // TM, nc),
        in_specs=[pl.BlockSpec((TM, D_MODEL), lambda i, c: (i, 0)),
                  pl.BlockSpec((D_MODEL, FFN_CHUNK), lambda i, c: (0, c)),
                  pl.BlockSpec((D_MODEL, FFN_CHUNK), lambda i, c: (0, c)),
                  pl.BlockSpec((FFN_CHUNK, D_MODEL), lambda i, c: (c, 0))],
        out_specs=pl.BlockSpec((TM, D_MODEL), lambda i, c: (i, 0)),
        out_shape=jax.ShapeDtypeStruct((n, D_MODEL), F32),
        scratch_shapes=[pltpu.VMEM((TM, D_MODEL), BF16)],
        compiler_params=_cparams(("parallel", "arbitrary")),
        name="dense_swiglu",
    )(x2d, wg.astype(BF16), wu.astype(BF16), wd.astype(BF16))


def _router_kernel(x_ref, w_ref, tri_ref, xb_ref, gd_ref, scol_ref, srow_ref, base_ref, cnt_ref):
    tt = x_ref.shape[0]
    x = x_ref[...]
    xb_ref[...] = x.astype(BF16)
    xh, xl = _split2(x)
    wh, wl = _split2(w_ref[...])
    logits = _dot(xh, wh) + _dot(xh, wl) + _dot(xl, wh)
    lane = _lane((tt, LANES))
    lf = lane.astype(F32)
    logits = jnp.where(lane < N_EXPERTS, logits, -jnp.inf)
    m1 = jnp.max(logits, axis=-1, keepdims=True)
    i1 = jnp.min(jnp.where(logits == m1, lf, float(LANES)), axis=-1, keepdims=True)
    rest = jnp.where(lf == i1, -jnp.inf, logits)
    m2 = jnp.max(rest, axis=-1, keepdims=True)
    i2 = jnp.min(jnp.where(rest == m2, lf, float(LANES)), axis=-1, keepdims=True)
    e2 = jnp.exp(m2 - m1)
    g1 = 1.0 / (1.0 + e2)
    g2 = e2 / (1.0 + e2)
    first, second = lf == i1, lf == i2
    gd_ref[...] = jnp.where(first, g1, 0.0) + jnp.where(second, g2, 0.0)
    ind = jnp.where(first, 1.0, jnp.where(second, 1.0, 0.0))
    tri = tri_ref[...]
    run = jnp.zeros((1, LANES), F32)
    slots, bases = [], []
    for tb in range(tt // MOE_TB):
        blk = ind[tb * MOE_TB:(tb + 1) * MOE_TB]
        rank = _dot(tri, blk.astype(BF16)) + run
        slots.append(jnp.where(blk > 0.5, rank, -1.0))
        bases.append(run)
        run = run + jnp.sum(blk, axis=0, keepdims=True)
    slot = jnp.concatenate(slots, axis=0)
    scol_ref[...] = slot
    srow_ref[...] = jnp.transpose(slot)[0:N_EXPERTS]
    base_ref[0] = jnp.concatenate(bases, axis=0)
    cnt_ref[0] = jnp.broadcast_to(run, (8, LANES))


def _route(x2d, w_router):
    n = x2d.shape[0]
    nt = n // MOE_TILE
    ntb = MOE_TILE // MOE_TB
    w = jnp.zeros((D_MODEL, LANES), F32).at[:, :N_EXPERTS].set(w_router)
    tri = jnp.asarray(np.tril(np.ones((MOE_TB, MOE_TB), np.float32), -1), BF16)
    return pl.pallas_call(
        _router_kernel,
        grid=(nt,),
        in_specs=[pl.BlockSpec((MOE_TILE, D_MODEL), lambda i: (i, 0)),
                  pl.BlockSpec((D_MODEL, LANES), lambda i: (0, 0)),
                  pl.BlockSpec((MOE_TB, MOE_TB), lambda i: (0, 0))],
        out_specs=[pl.BlockSpec((MOE_TILE, D_MODEL), lambda i: (i, 0)),
                   pl.BlockSpec((MOE_TILE, LANES), lambda i: (i, 0)),
                   pl.BlockSpec((MOE_TILE, LANES), lambda i: (i, 0)),
                   pl.BlockSpec((N_EXPERTS, MOE_TILE), lambda i: (0, i)),
                   pl.BlockSpec((1, ntb, LANES), lambda i: (i, 0, 0)),
                   pl.BlockSpec((1, 8, LANES), lambda i: (i, 0, 0))],
        out_shape=[jax.ShapeDtypeStruct((n, D_MODEL), BF16),
                   jax.ShapeDtypeStruct((n, LANES), F32),
                   jax.ShapeDtypeStruct((n, LANES), F32),
                   jax.ShapeDtypeStruct((N_EXPERTS, n), F32),
                   jax.ShapeDtypeStruct((nt, ntb, LANES), F32),
                   jax.ShapeDtypeStruct((nt, 8, LANES), F32)],
        compiler_params=_cparams(("parallel",)),
        name="moe_router",
    )(x2d, w, tri)


MOE_GATHER_W = MOE_TB + 16
MOE_SCATTER_W = MOE_TB + 8
MOE_CAP = MOE_TILE + 512


def _moe_kernel(cnt_ref, base_ref, xb_ref, srow_ref, scol_ref, gd_ref, wg_ref, wu_ref, wd_ref,
                o_ref, xc_ref, yc_ref):
    i, e, c = pl.program_id(0), pl.program_id(1), pl.program_id(2)
    nc = pl.num_programs(2)
    ntb = MOE_TILE // MOE_TB
    nb = jnp.right_shift(cnt_ref[i * N_EXPERTS + e] + (MOE_ROWS - 1), MOE_ROWS.bit_length() - 1)

    @pl.when((i == 0) & (e == 0) & (c == 0))
    def _():
        yc_ref[...] = jnp.zeros_like(yc_ref)

    @pl.when((e == 0) & (c == 0))
    def _():
        o_ref[...] = jnp.zeros_like(o_ref)

    @pl.when(c == 0)
    def _():
        xc_ref[...] = jnp.zeros_like(xc_ref)
        srow = srow_ref[pl.ds(e, 1), :]
        rid = lax.broadcasted_iota(jnp.int32, (MOE_GATHER_W, MOE_TB), 0).astype(F32)
        for tb in range(ntb):
            b0 = base_ref[(i * ntb + tb) * N_EXPERTS + e]
            b_al = pl.multiple_of(jnp.left_shift(jnp.right_shift(b0, 4), 4), 16)
            rel = srow[:, tb * MOE_TB:(tb + 1) * MOE_TB] - b_al.astype(F32)
            onehot = jnp.where(rel == rid, 1.0, 0.0).astype(BF16)
            rows = _dot(onehot, xb_ref[tb * MOE_TB:(tb + 1) * MOE_TB, :])
            cur = xc_ref[pl.ds(b_al, MOE_GATHER_W), :].astype(F32)
            xc_ref[pl.ds(b_al, MOE_GATHER_W), :] = (cur + rows).astype(BF16)

    def rows_body(rb, carry):
        r0 = pl.multiple_of(rb * MOE_ROWS, MOE_ROWS)
        xr = xc_ref[pl.ds(r0, MOE_ROWS), :]
        g = _dot(xr, wg_ref[0])
        u = _dot(xr, wu_ref[0])
        y = _dot((g * _sigmoid(g) * u).astype(BF16), wd_ref[0])
        prev = yc_ref[pl.ds(r0, MOE_ROWS), :]
        yc_ref[pl.ds(r0, MOE_ROWS), :] = jnp.where(c == 0, y, prev + y)
        return carry

    lax.fori_loop(0, nb, rows_body, 0)

    @pl.when(c == nc - 1)
    def _():
        lane = _lane((MOE_TB, LANES))
        cid = lax.broadcasted_iota(jnp.int32, (MOE_TB, MOE_SCATTER_W), 1).astype(F32)
        for tb in range(ntb):
            b0 = base_ref[(i * ntb + tb) * N_EXPERTS + e]
            b_al = pl.multiple_of(jnp.left_shift(jnp.right_shift(b0, 3), 3), 8)
            rows = slice(tb * MOE_TB, (tb + 1) * MOE_TB)
            slot = jnp.sum(jnp.where(lane == e, scol_ref[rows, :], 0.0), axis=-1, keepdims=True)
            gate = jnp.sum(jnp.where(lane == e, gd_ref[rows, :], 0.0), axis=-1, keepdims=True)
            onehot = jnp.where(slot - b_al.astype(F32) == cid, 1.0, 0.0).astype(BF16)
            yh, yl = _split2(yc_ref[pl.ds(b_al, MOE_SCATTER_W), :])
            o_ref[rows, :] += gate * (_dot(onehot, yh) + _dot(onehot, yl))


def _moe(xb, srow, scol, gd, cnt, base, wg, wu, wd):
    n = xb.shape[0]
    nt = n // MOE_TILE
    nc = D_FF_EXPERT // MOE_CHUNK
    grid_spec = pltpu.PrefetchScalarGridSpec(
        num_scalar_prefetch=2,
        grid=(nt, N_EXPERTS, nc),
        in_specs=[pl.BlockSpec((MOE_TILE, D_MODEL), lambda i, e, c, *_: (i, 0)),
                  pl.BlockSpec((N_EXPERTS, MOE_TILE), lambda i, e, c, *_: (0, i)),
                  pl.BlockSpec((MOE_TILE, LANES), lambda i, e, c, *_: (i, 0)),
                  pl.BlockSpec((MOE_TILE, LANES), lambda i, e, c, *_: (i, 0)),
                  pl.BlockSpec((1, D_MODEL, MOE_CHUNK), lambda i, e, c, *_: (e, 0, c)),
                  pl.BlockSpec((1, D_MODEL, MOE_CHUNK), lambda i, e, c, *_: (e, 0, c)),
                  pl.BlockSpec((1, MOE_CHUNK, D_MODEL), lambda i, e, c, *_: (e, c, 0))],
        out_specs=pl.BlockSpec((MOE_TILE, D_MODEL), lambda i, e, c, *_: (i, 0)),
        scratch_shapes=[pltpu.VMEM((MOE_CAP, D_MODEL), BF16), pltpu.VMEM((MOE_CAP, D_MODEL), F32)],
    )
    return pl.pallas_call(
        _moe_kernel,
        grid_spec=grid_spec,
        out_shape=jax.ShapeDtypeStruct((n, D_MODEL), F32),
        compiler_params=_cparams(("arbitrary", "arbitrary", "arbitrary")),
        name="moe_experts",
    )(cnt, base, xb, srow, scol, gd, wg.astype(BF16), wu.astype(BF16), wd.astype(BF16))


def _ple_kernel(x_ref, f_ref, p_ref, wg_ref, wp_ref, g2_ref, b2_ref, g3_ref, b3_ref, o_ref):
    x2 = _layer_norm(DN_ALPHA * x_ref[...] + f_ref[...], g2_ref[...], b2_ref[...])
    e = _sigmoid(_dot(x2.astype(BF16), wg_ref[...])) * _dot(p_ref[...].astype(BF16), wp_ref[...])
    o_ref[...] = _layer_norm(DN_ALPHA * x2 + e, g3_ref[...], b3_ref[...])


def _ple(x2d, f2d, p2d, w_gate, w_proj, g2, b2, g3, b3):
    n = x2d.shape[0]
    row = lambda w: pl.BlockSpec((TM, w), lambda i: (i, 0))
    full = lambda s: pl.BlockSpec(s, lambda i: (0,) * len(s))
    vec = full((1, D_MODEL))
    return pl.pallas_call(
        _ple_kernel,
        grid=(n // TM,),
        in_specs=[row(D_MODEL), row(D_MODEL), row(PLE_DIM), full((D_MODEL, D_MODEL)), full((PLE_DIM, D_MODEL)),
                  vec, vec, vec, vec],
        out_specs=row(D_MODEL),
        out_shape=jax.ShapeDtypeStruct((n, D_MODEL), F32),
        compiler_params=_cparams(("parallel",)),
        name="ln_ple_ln",
    )(x2d, f2d, p2d, w_gate.astype(BF16), w_proj.astype(BF16),
      g2.reshape(1, -1), b2.reshape(1, -1), g3.reshape(1, -1), b3.reshape(1, -1))


def _mixer(x2d, b, t, i, w_in, cmp_pos, cmp_w1, cmp_w2, diff_lambda, diff_subln, sinks):
    slopes_a, slopes_b, slopes_c = _alibi_slopes()
    proj, gl = _project(x2d, _build_w_in(w_in))
    proj3 = proj.reshape(b, t, N_MAIN)
    kvc = _compress(proj3, cmp_pos, cmp_w1, cmp_w2)
    o_c, sel = _cmp_attn(proj3, kvc, slopes_a)
    o_s = _nsa_attn(proj3, KVA_BLK + 1, slopes_a, sel=sel)
    o_w = _nsa_attn(proj3, KVA_BLK + 2, slopes_a, window=NSA_WINDOW)
    lam_init = 0.8 - 0.6 * math.exp(-0.3 * i)
    o_b = _diff_attn(proj3, slopes_b, diff_lambda, diff_subln, lam_init)
    o_sw = _swa_attn(proj3, slopes_c, sinks)
    flat = lambda a: a.reshape(b * t, a.shape[-1])
    return flat(o_c), flat(o_s), flat(o_w), gl, flat(o_b), flat(o_sw)


def kernel(x, p, w_in, cmp_pos, cmp_w1, cmp_w2, diff_lambda, diff_subln, sinks, w_out, ln1_g, ln1_b,
           ffn_w_gate, ffn_w_up, ffn_w_down, moe_router, moe_w_gate, moe_w_up, moe_w_down, ln2_g, ln2_b,
           ple_gate, ple_proj, ln3_g, ln3_b):
    b, t, d = x.shape
    n = b * t
    x2d = x.reshape(n, d)
    for i in range(DEPTH):
        heads = _mixer(x2d, b, t, i, w_in[i], cmp_pos[i], cmp_w1[i], cmp_w2[i],
                       diff_lambda[i], diff_subln[i], sinks[i])
        x1 = _outproj(*heads, x2d, w_out[i], ln1_g[i], ln1_b[i])
        if i % 2 == 0:
            f = _ffn(x1, ffn_w_gate[i // 2], ffn_w_up[i // 2], ffn_w_down[i // 2])
        else:
            xb, gd, scol, srow, base, cnt = _route(x1, moe_router[i // 2])
            cnt_i = cnt[:, 0, :N_EXPERTS].astype(jnp.int32).reshape(-1)
            base_i = base[:, :, :N_EXPERTS].astype(jnp.int32).reshape(-1)
            f = _moe(xb, srow, scol, gd, cnt_i, base_i,
                     moe_w_gate[i // 2], moe_w_up[i // 2], moe_w_down[i // 2])
        x2d = _ple(x1, f, p[i].reshape(n, PLE_DIM), ple_gate[i], ple_proj[i],
                   ln2_g[i], ln2_b[i], ln3_g[i], ln3_b[i])
    return x2d.reshape(b, t, d)
```

```python
import functools
import math

import jax
import jax.numpy as jnp
import numpy as np
from jax import lax
from jax.experimental import pallas as pl
from jax.experimental.pallas import tpu as pltpu

F32 = jnp.float32
BF16 = jnp.bfloat16

D_MODEL = 1024
HEAD_DIM = 64
A_HEADS = 4
B_HEADS = 4
C_HEADS = 8
C_KV_HEADS = 2
DIFF_DK = 32
CMP_LEN = 32
CMP_STRIDE = 16
CMP_HIDDEN = 128
SLC_BLOCK = 64
SLC_SHIFT = 6
SLC_TOPN = 16
NSA_WINDOW = 512
SWA_WINDOW = 128
N_EXPERTS = 8
D_FF_EXPERT = 3584
D_FF_DENSE = 2816
PLE_DIM = 256
LN_EPS = 1e-5
NEG_INF = -1e30
FORCE_SCORE = 1e9
DEPTH = 2
DN_ALPHA = (2 * DEPTH) ** 0.25

LANES = 128
QA_BLK, QB_BLK, QC_BLK, KVB_BLK, KVC_BLK, KVA_BLK, N_MAIN_BLK = 0, 4, 8, 16, 20, 22, 25
N_MAIN = N_MAIN_BLK * LANES

LOG2E = 1.4426950408889634
TQ = 256
TK_DENSE = 512
TQ_DIFF = 256
TK_DIFF = 512
TM = 512
FFN_CHUNK = 1408
MOE_TILE = 2048
MOE_ROWS = 256
MOE_TB = 256
MOE_CHUNK = 512
VMEM_LIMIT = 56 * 1024 * 1024


def _cparams(sem):
    return pltpu.CompilerParams(dimension_semantics=sem, vmem_limit_bytes=VMEM_LIMIT)


def _alibi_slopes():
    n = A_HEADS + B_HEADS + C_HEADS
    s = [LOG2E * 2.0 ** (-8.0 * i / n) for i in range(1, n + 1)]
    rest = s[C_HEADS:]
    return rest[0::2], rest[1::2], s[:C_HEADS]


def _dot(a, b):
    return jnp.dot(a, b, preferred_element_type=F32)


def _dot_nt(a, b):
    return lax.dot_general(a, b, (((1,), (1,)), ((), ())), preferred_element_type=F32)


def _split2(x):
    hi = x.astype(BF16)
    lo = (x - hi.astype(F32)).astype(BF16)
    return hi, lo


def _layer_norm(y, g, b):
    mu = jnp.mean(y, axis=-1, keepdims=True)
    d = y - mu
    var = jnp.mean(d * d, axis=-1, keepdims=True)
    return d * lax.rsqrt(var + LN_EPS) * g + b


def _sigmoid(x):
    return 1.0 / (1.0 + jnp.exp(-x))


def _lane(shape):
    return lax.broadcasted_iota(jnp.int32, shape, 1)


def _pack_upper(parts):
    out = []
    for a, b in zip(parts[0::2], parts[1::2]):
        lane = _lane(a.shape)
        out.append(jnp.where(lane < HEAD_DIM, pltpu.roll(a, HEAD_DIM, 1), b))
    return out[0] if len(out) == 1 else jnp.concatenate(out, axis=1)


def _proj_kernel(x_ref, w_ref, o_ref, g_ref, t_ref):
    acc = _dot(x_ref[...].astype(BF16), w_ref[...])
    o_ref[...] = acc[:, :N_MAIN].astype(BF16)
    g_ref[...] = acc[:, N_MAIN:]
    t_ref[...] = jnp.transpose(acc[:, KVB_BLK * LANES:N_MAIN]).astype(BF16)


def _project(x2d, w_ext):
    n = x2d.shape[0]
    n_kv = N_MAIN - KVB_BLK * LANES
    return pl.pallas_call(
        _proj_kernel,
        grid=(n // TM,),
        in_specs=[pl.BlockSpec((TM, D_MODEL), lambda i: (i, 0)),
                  pl.BlockSpec((D_MODEL, N_MAIN + LANES), lambda i: (0, 0))],
        out_specs=[pl.BlockSpec((TM, N_MAIN), lambda i: (i, 0)),
                   pl.BlockSpec((TM, LANES), lambda i: (i, 0)),
                   pl.BlockSpec((n_kv, TM), lambda i: (0, i))],
        out_shape=[jax.ShapeDtypeStruct((n, N_MAIN), BF16),
                   jax.ShapeDtypeStruct((n, LANES), F32),
                   jax.ShapeDtypeStruct((n_kv, n), BF16)],
        compiler_params=_cparams(("parallel",)),
        name="in_proj",
    )(x2d, w_ext)


def _build_w_in(w):
    d = w.shape[0]
    o = np.cumsum([0, 256, 384, 12, 256, 256, 256, 512, 128, 128])
    q_a, kv_a, g_a, q_b, k_b, v_b, q_c, k_c, v_c = [w[:, o[i]:o[i + 1]] for i in range(9)]
    z = jnp.zeros((d, HEAD_DIM), F32)
    hd = HEAD_DIM
    cols = []
    for h in range(A_HEADS):
        cols += [q_a[:, h * hd:(h + 1) * hd] * (LOG2E * hd ** -0.5), z]
    for h in range(B_HEADS):
        cols += [q_b[:, h * hd:(h + 1) * hd] * (LOG2E * DIFF_DK ** -0.5), z]
    for h in range(C_HEADS):
        cols += [q_c[:, h * hd:(h + 1) * hd] * (LOG2E * hd ** -0.5), z]
    for h in range(B_HEADS):
        cols += [k_b[:, h * hd:(h + 1) * hd], v_b[:, h * hd:(h + 1) * hd]]
    for g in range(C_KV_HEADS):
        cols += [k_c[:, g * hd:(g + 1) * hd], v_c[:, g * hd:(g + 1) * hd]]
    cols.append(kv_a)
    cols += [g_a, jnp.zeros((d, LANES - 12), F32)]
    return jnp.concatenate(cols, axis=1).astype(BF16)


def _compress_kernel(ch_ref, wc_ref, pos_ref, w1_ref, w2_ref, o_ref, ot_ref):
    nck = ch_ref.shape[1]
    u = _dot(ch_ref[0], wc_ref[...])
    hs = []
    for j in range(2):
        pw = _dot(pos_ref[j].astype(BF16), w1_ref[j])[0:1]
        ua = u[:, (2 * j) * LANES:(2 * j + 1) * LANES]
        ub = u[:, (2 * j + 1) * LANES:(2 * j + 2) * LANES]
        pre = ua + pltpu.roll(ub, nck - 1, 0) + pw
        hs.append(0.5 * pre * (1.0 + jnp.tanh(0.7978845608028654 * (pre + 0.044715 * pre * pre * pre))))
    h = jnp.concatenate(hs, axis=1).astype(BF16)
    kvc = _dot(h, w2_ref[...])
    o_ref[0] = kvc.astype(BF16)
    ot_ref[0] = jnp.transpose(kvc).astype(BF16)


def _compress(proj3, cmp_pos, cmp_w1, cmp_w2):
    b, t, _ = proj3.shape
    nck = t // CMP_STRIDE
    kv = proj3[:, :, KVA_BLK * LANES:(KVA_BLK + 1) * LANES]
    chunks = kv.reshape(b, nck, CMP_STRIDE * LANES)
    w1 = cmp_w1.reshape(2, 2, CMP_STRIDE, HEAD_DIM, CMP_HIDDEN)
    z = jnp.zeros((CMP_STRIDE, HEAD_DIM, CMP_HIDDEN), F32)
    blocks = []
    for j in range(2):
        for half in range(2):
            pair = [w1[j, half], z] if j == 0 else [z, w1[j, half]]
            blocks.append(jnp.concatenate(pair, axis=1).reshape(CMP_STRIDE * LANES, CMP_HIDDEN))
    wc = jnp.concatenate(blocks, axis=1).astype(BF16)
    pos = jnp.zeros((2, 8, CMP_LEN * HEAD_DIM), F32).at[:, 0].set(cmp_pos.reshape(2, -1))
    z2 = jnp.zeros((CMP_HIDDEN, HEAD_DIM), F32)
    w2 = jnp.concatenate([jnp.concatenate([cmp_w2[0], z2], axis=1),
                          jnp.concatenate([z2, cmp_w2[1]], axis=1)], axis=0).astype(BF16)
    return pl.pallas_call(
        _compress_kernel,
        grid=(b,),
        in_specs=[pl.BlockSpec((1, nck, CMP_STRIDE * LANES), lambda i: (i, 0, 0)),
                  pl.BlockSpec(wc.shape, lambda i: (0, 0)),
                  pl.BlockSpec(pos.shape, lambda i: (0, 0, 0)),
                  pl.BlockSpec((2, CMP_LEN * HEAD_DIM, CMP_HIDDEN), lambda i: (0, 0, 0)),
                  pl.BlockSpec(w2.shape, lambda i: (0, 0))],
        out_specs=[pl.BlockSpec((1, nck, LANES), lambda i: (i, 0, 0)),
                   pl.BlockSpec((1, LANES, nck), lambda i: (i, 0, 0))],
        out_shape=[jax.ShapeDtypeStruct((b, nck, LANES), BF16),
                   jax.ShapeDtypeStruct((b, LANES, nck), BF16)],
        compiler_params=_cparams(("parallel",)),
        name="nsa_compress",
    )(chunks, wc, pos, cmp_w1.astype(BF16), w2)


def _cmp_attn_kernel_t(q_ref, kvc_ref, kvct_ref, ovt_ref, bias_ref, o_ref, sel_ref):
    tq = q_ref.shape[1]
    ncp = kvc_ref.shape[1]
    rows = A_HEADS * tq
    t0 = pl.program_id(1) * tq
    qs = _stack_heads(q_ref[0], A_HEADS)
    u = _dot_nt(kvc_ref[0], qs) + bias_ref[...]
    n = lax.broadcasted_iota(jnp.int32, (ncp, rows), 0)
    block_end = jnp.where(n == ncp - 1, jnp.int32(1 << 30), n * CMP_STRIDE + (CMP_LEN - 1))
    keep = block_end <= jnp.bitwise_and(_lane((ncp, rows)), tq - 1) + t0
    sh = jnp.where(keep, u, NEG_INF)
    m = jnp.max(sh, axis=0, keepdims=True)
    e = jnp.where(keep, jnp.exp2(sh - m), 0.0)
    p = e / jnp.maximum(jnp.sum(e, axis=0, keepdims=True), 1e-30)
    o_t = _dot(kvct_ref[0], p.astype(BF16))
    o_ref[0] = _pack_upper([jnp.transpose(o_t[:, h * tq:(h + 1) * tq]) for h in range(A_HEADS)]).astype(BF16)

    psum = p[:, 0:tq] + p[:, tq:2 * tq] + p[:, 2 * tq:3 * tq] + p[:, 3 * tq:4 * tq]
    hi = psum.astype(BF16)
    r1 = psum - hi.astype(F32)
    mid = r1.astype(BF16)
    lo = (r1 - mid.astype(F32)).astype(BF16)
    ovt = ovt_ref[...]
    imp = _dot(ovt, hi) + _dot(ovt, mid) + _dot(ovt, lo)
    nsl = ncp * CMP_STRIDE // SLC_BLOCK
    j = lax.broadcasted_iota(jnp.int32, (LANES, tq), 0)
    t_blk = jnp.right_shift(_lane((LANES, tq)) + t0, SLC_SHIFT)
    forced = (j == 0) | (j == t_blk) | (j == t_blk - 1)
    imp = jnp.where(forced, FORCE_SCORE, jnp.where(j > t_blk, -1.0, imp))
    work = jnp.where(j < nsl, imp, -3.0)
    jf = j.astype(F32)
    sel = jnp.zeros((LANES, tq), F32)
    for _ in range(SLC_TOPN):
        mx = jnp.max(work, axis=0, keepdims=True)
        first = jnp.min(jnp.where(work == mx, jf, float(LANES)), axis=0, keepdims=True)
        pick = jf == first
        sel = jnp.where(pick, 1.0, sel)
        work = jnp.where(pick, -4.0, work)
    sel_ref[0] = jnp.transpose(sel).astype(BF16)


def _cmp_attn(proj3, kvc, kvc_t, slopes):
    b, t, _ = proj3.shape
    ncp = kvc.shape[1]
    nsl = t // SLC_BLOCK
    c0 = np.arange(ncp)[None, :] * CMP_STRIDE
    s0 = np.arange(LANES)[:, None] * SLC_BLOCK
    ovt = np.clip(np.minimum(c0 + CMP_LEN, s0 + SLC_BLOCK) - np.maximum(c0, s0), 0, None) / CMP_LEN
    ovt[:, ncp - 1] = 0.0
    ovt[nsl:, :] = 0.0
    rel = np.arange(TQ)[None, :] - (np.arange(ncp)[:, None] * CMP_STRIDE + (CMP_LEN - 1))
    bias = np.concatenate([-s * rel for s in slopes], axis=1).astype(np.float32)
    return pl.pallas_call(
        _cmp_attn_kernel_t,
        grid=(b, t // TQ),
        in_specs=[pl.BlockSpec((1, TQ, A_HEADS * LANES), lambda bi, i: (bi, i, QA_BLK // A_HEADS)),
                  pl.BlockSpec((1, ncp, LANES), lambda bi, i: (bi, 0, 0)),
                  pl.BlockSpec((1, LANES, ncp), lambda bi, i: (bi, 0, 0)),
                  pl.BlockSpec((LANES, ncp), lambda bi, i: (0, 0)),
                  pl.BlockSpec((ncp, A_HEADS * TQ), lambda bi, i: (0, 0))],
        out_specs=[pl.BlockSpec((1, TQ, A_HEADS * HEAD_DIM), lambda bi, i: (bi, i, 0)),
                   pl.BlockSpec((1, TQ, LANES), lambda bi, i: (bi, i, 0))],
        out_shape=[jax.ShapeDtypeStruct((b, t, A_HEADS * HEAD_DIM), BF16),
                   jax.ShapeDtypeStruct((b, t, LANES), BF16)],
        compiler_params=_cparams(("parallel", "parallel")),
        name="nsa_cmp_attn",
    )(proj3, kvc, kvc_t, jnp.asarray(ovt, BF16), jnp.asarray(bias))


def _stack_heads(q, n):
    return jnp.concatenate([q[:, h * LANES:(h + 1) * LANES] for h in range(n)], axis=0)


def _window_bias(slopes, tq, window):
    wk = window + tq
    n_var = -(-window // tq) + 1
    j = np.arange(wk)[:, None]
    q = np.arange(tq)[None, :]
    out = np.empty((len(slopes), n_var, wk, len(slopes[0]) * tq), np.float32)
    for v in range(n_var):
        d = v * tq - max(v * tq - window, 0) + q - j
        keep = (d >= 0) & (d < window)
        for g, row in enumerate(slopes):
            for r, slope in enumerate(row):
                out[g, v, :, r * tq:(r + 1) * tq] = np.where(keep, -slope * d, NEG_INF)
    return jnp.asarray(out)


def _window_kernel_t(*refs, r_heads, window, has_sink):
    if has_sink:
        sink_ref, q_ref, kv_ref, kvt_ref, bias_ref, o_ref = refs
    else:
        q_ref, kv_ref, kvt_ref, bias_ref, o_ref = refs
    g, i = pl.program_id(1), pl.program_id(2)
    tq = q_ref.shape[1]
    wk = window + tq
    start = pl.multiple_of(jnp.maximum(i * tq - window, 0), LANES)
    qs = _stack_heads(q_ref[0], r_heads)
    u = _dot_nt(kv_ref[0, pl.ds(start, wk), :], qs) + bias_ref[0, 0]
    m = jnp.max(u, axis=0, keepdims=True)
    if has_sink:
        sink = jnp.concatenate([jnp.full((1, tq), sink_ref[g * r_heads + r], F32) for r in range(r_heads)], axis=1)
        m = jnp.maximum(m, sink)
    p = jnp.exp2(u - m)
    den = jnp.sum(p, axis=0, keepdims=True)
    if has_sink:
        den = den + jnp.exp2(sink - m)
    o_t = _dot(kvt_ref[:, pl.ds(start, wk)], p.astype(BF16)) / den
    o_ref[0] = _pack_upper([jnp.transpose(o_t[:, r * tq:(r + 1) * tq]) for r in range(r_heads)]).astype(BF16)


def _window_attn_t(proj3, kv_t, slopes, *, q_blk, kv_blk, window, sinks=None, name):
    b, t, _ = proj3.shape
    groups, r_heads = len(slopes), len(slopes[0])
    bias = _window_bias(slopes, TQ, window)
    n_var, wk = bias.shape[1], bias.shape[2]
    in_specs = [pl.BlockSpec((1, TQ, r_heads * LANES), lambda bi, g, i: (bi, i, q_blk // r_heads + g)),
                pl.BlockSpec((1, t, LANES), lambda bi, g, i: (bi, 0, kv_blk + g)),
                pl.BlockSpec((LANES, t), lambda bi, g, i: (kv_blk - KVB_BLK + g, bi)),
                pl.BlockSpec((1, 1, wk, r_heads * TQ), lambda bi, g, i: (g, jnp.minimum(i, n_var - 1), 0, 0))]
    args = [proj3, proj3, kv_t, bias]
    if sinks is not None:
        in_specs.insert(0, pl.BlockSpec(memory_space=pltpu.SMEM))
        args.insert(0, sinks.astype(F32) * LOG2E)
    return pl.pallas_call(
        functools.partial(_window_kernel_t, r_heads=r_heads, window=window, has_sink=sinks is not None),
        grid=(b, groups, t // TQ),
        in_specs=in_specs,
        out_specs=pl.BlockSpec((1, TQ, r_heads * HEAD_DIM), lambda bi, g, i: (bi, i, g)),
        out_shape=jax.ShapeDtypeStruct((b, t, groups * r_heads * HEAD_DIM), BF16),
        compiler_params=_cparams(("parallel", "parallel", "parallel")),
        name=name,
    )(*args)


def _flash_t(streams, kv_ref, kvt_ref, i, *, tq, tk, sel_t=None):
    ratio = tk // tq
    n_full = lax.div(i, jnp.int32(ratio))
    rows = streams[0][0].shape[0]
    jpos = lax.broadcasted_iota(jnp.int32, (tk, rows), 0)
    jposf = jpos.astype(F32)

    def tile(kt, carry, edge):
        k0 = pl.multiple_of(kt * tk, tk)
        k0f = (kt * tk).astype(F32)
        keep = None
        if edge:
            t_lane = jnp.bitwise_and(_lane((tk, rows)), tq - 1) + (i * tq - kt * tk)
            keep = jpos <= t_lane
        if sel_t is not None:
            blk = jnp.right_shift(lax.broadcasted_iota(jnp.int32, (tk, LANES), 0) + kt * tk, SLC_SHIFT)
            expand = jnp.where(_lane((tk, LANES)) == blk, 1.0, 0.0).astype(BF16)
            chosen = _dot(expand, sel_t)
            if keep is not None:
                chosen = jnp.where(keep, chosen, 0.0)
            keep = chosen > 0.5
        out = []
        for (qs, col, slope_lane), (m, l, acc) in zip(streams, carry):
            kvt = kv_ref[0, pl.ds(k0, tk), col:col + LANES]
            kv_f = kvt_ref[col:col + LANES, pl.ds(k0, tk)]
            u = _dot_nt(kvt, qs) + slope_lane * jposf
            if keep is not None:
                u = jnp.where(keep, u, NEG_INF)
            shift = slope_lane * k0f
            m_new = jnp.maximum(m, jnp.max(u, axis=0, keepdims=True) + shift)
            p = jnp.exp2(u - (m_new - shift))
            alpha = jnp.exp2(m - m_new)
            l = alpha * l + jnp.sum(p, axis=0, keepdims=True)
            acc = alpha * acc + _dot(kv_f, p.astype(BF16))
            out.append((m_new, l, acc))
        return tuple(out)

    init = tuple((jnp.full((1, rows), NEG_INF, F32), jnp.zeros((1, rows), F32), jnp.zeros((LANES, rows), F32))
                 for _ in streams)
    carry = lax.fori_loop(0, n_full, lambda kt, c: tile(kt, c, False), init)
    carry = tile(n_full, carry, True)
    return [(l, acc) for _, l, acc in carry]


def _selected_kernel_t(q_ref, kv_ref, kvt_ref, sel_ref, o_ref, *, slopes):
    tq = q_ref.shape[1]
    i = pl.program_id(1)
    qs = _stack_heads(q_ref[0], A_HEADS)
    sel_t = jnp.transpose(sel_ref[0].astype(F32)).astype(BF16)
    sel_t = jnp.concatenate([sel_t] * A_HEADS, axis=1)
    slope_lane = jnp.concatenate([jnp.full((1, tq), s, F32) for s in slopes], axis=1)
    ((l, acc),) = _flash_t([(qs, 0, slope_lane)], kv_ref, kvt_ref, i, tq=tq, tk=TK_DENSE, sel_t=sel_t)
    o_t = acc / l
    o_ref[0] = _pack_upper([jnp.transpose(o_t[:, h * tq:(h + 1) * tq]) for h in range(A_HEADS)]).astype(BF16)


def _selected_attn_t(proj3, kv_t, sel, slopes):
    b, t, _ = proj3.shape
    kv_blk = KVA_BLK + 1
    return pl.pallas_call(
        functools.partial(_selected_kernel_t, slopes=slopes),
        grid=(b, t // TQ),
        in_specs=[pl.BlockSpec((1, TQ, A_HEADS * LANES), lambda bi, i: (bi, i, QA_BLK // A_HEADS)),
                  pl.BlockSpec((1, t, LANES), lambda bi, i: (bi, 0, kv_blk)),
                  pl.BlockSpec((LANES, t), lambda bi, i: (kv_blk - KVB_BLK, bi)),
                  pl.BlockSpec((1, TQ, LANES), lambda bi, i: (bi, i, 0))],
        out_specs=pl.BlockSpec((1, TQ, A_HEADS * HEAD_DIM), lambda bi, i: (bi, i, 0)),
        out_shape=jax.ShapeDtypeStruct((b, t, A_HEADS * HEAD_DIM), BF16),
        compiler_params=_cparams(("parallel", "parallel")),
        name="nsa_selected",
    )(proj3, proj3, kv_t, sel)


def _diff_kernel_t(slope_ref, dl_ref, g_ref, q_ref, kv_ref, kvt_ref, o_ref, *, lam_init):
    tq = q_ref.shape[1]
    hp = pl.program_id(1)
    i = pl.program_id(2)
    dl = dl_ref[...]
    lam = (jnp.exp(jnp.sum(dl[0:1] * dl[1:2], axis=-1, keepdims=True))
           - jnp.exp(jnp.sum(dl[2:3] * dl[3:4], axis=-1, keepdims=True)) + lam_init)
    lane = _lane((tq, LANES))
    streams = []
    for hh in range(2):
        q = q_ref[0, :, hh * LANES:(hh + 1) * LANES]
        zero = jnp.zeros_like(q)
        qs = jnp.concatenate([jnp.where(lane < DIFF_DK, q, zero),
                              jnp.where(lane >= DIFF_DK, q, zero)], axis=0)
        slope_lane = jnp.full((1, 2 * tq), slope_ref[hp * 2 + hh], F32)
        streams.append((qs, hh * LANES, slope_lane))
    sub = lax.broadcasted_iota(jnp.int32, (LANES, tq), 0)
    outs = []
    for l, acc in _flash_t(streams, kv_ref, kvt_ref, i, tq=tq, tk=TK_DIFF):
        o = acc / l
        w = jnp.where(sub >= HEAD_DIM, o[:, :tq] - lam * o[:, tq:], 0.0)
        ms = jnp.sum(w * w, axis=0, keepdims=True) * (1.0 / HEAD_DIM)
        y = w * lax.rsqrt(ms + LN_EPS) * (1.0 - lam_init)
        outs.append(jnp.transpose(y) * g_ref[...])
    o_ref[0] = _pack_upper(outs).astype(BF16)


def _diff_attn_t(proj3, kv_t, slopes, diff_lambda, subln, lam_init):
    b, t, _ = proj3.shape
    g_ext = jnp.concatenate([jnp.zeros((1, HEAD_DIM), F32), subln.reshape(1, HEAD_DIM).astype(F32)], axis=1)
    return pl.pallas_call(
        functools.partial(_diff_kernel_t, lam_init=lam_init),
        grid=(b, B_HEADS // 2, t // TQ_DIFF),
        in_specs=[pl.BlockSpec(memory_space=pltpu.SMEM),
                  pl.BlockSpec((4, DIFF_DK), lambda bi, h, i: (0, 0)),
                  pl.BlockSpec((1, LANES), lambda bi, h, i: (0, 0)),
                  pl.BlockSpec((1, TQ_DIFF, 2 * LANES), lambda bi, h, i: (bi, i, QB_BLK // 2 + h)),
                  pl.BlockSpec((1, t, 2 * LANES), lambda bi, h, i: (bi, 0, KVB_BLK // 2 + h)),
                  pl.BlockSpec((2 * LANES, t), lambda bi, h, i: (h, bi))],
        out_specs=pl.BlockSpec((1, TQ_DIFF, 2 * HEAD_DIM), lambda bi, h, i: (bi, i, h)),
        out_shape=jax.ShapeDtypeStruct((b, t, B_HEADS * HEAD_DIM), BF16),
        compiler_params=_cparams(("parallel", "parallel", "parallel")),
        name="diff_attn",
    )(jnp.asarray(slopes, F32), diff_lambda.astype(F32), g_ext, proj3, proj3, kv_t)


def _outproj_kernel(oc_ref, os_ref, ow_ref, gl_ref, ob_ref, osw_ref, x_ref, w_ref, ex_ref, g_ref, b_ref, o_ref):
    hi, lo = _split2(_sigmoid(gl_ref[...]))
    oa = None
    for j, ref in enumerate((oc_ref, os_ref, ow_ref)):
        gate = _dot(hi, ex_ref[j]) + _dot(lo, ex_ref[j])
        term = gate * ref[...].astype(F32)
        oa = term if oa is None else oa + term
    cat = jnp.concatenate([oa.astype(BF16), ob_ref[...], osw_ref[...]], axis=1)
    y = DN_ALPHA * x_ref[...] + _dot(cat, w_ref[...])
    o_ref[...] = _layer_norm(y, g_ref[...], b_ref[...])


def _outproj(o_c, o_s, o_w, gl, o_b, o_sw, x2d, w_out, ln_g, ln_b):
    n = x2d.shape[0]
    ex = np.zeros((3, LANES, A_HEADS * HEAD_DIM), np.float32)
    for j in range(3):
        for h in range(A_HEADS):
            ex[j, h * 3 + j, h * HEAD_DIM:(h + 1) * HEAD_DIM] = 1.0
    row = lambda w: pl.BlockSpec((TM, w), lambda i: (i, 0))
    full = lambda s: pl.BlockSpec(s, lambda i: (0,) * len(s))
    return pl.pallas_call(
        _outproj_kernel,
        grid=(n // TM,),
        in_specs=[row(256), row(256), row(256), row(LANES), row(256), row(512), row(D_MODEL),
                  full((D_MODEL, D_MODEL)), full(ex.shape), full((1, D_MODEL)), full((1, D_MODEL))],
        out_specs=row(D_MODEL),
        out_shape=jax.ShapeDtypeStruct((n, D_MODEL), F32),
        compiler_params=_cparams(("parallel",)),
        name="out_proj_ln",
    )(o_c, o_s, o_w, gl, o_b, o_sw, x2d, w_out.astype(BF16), jnp.asarray(ex, BF16),
      ln_g.reshape(1, -1), ln_b.reshape(1, -1))


def _ffn_kernel(x_ref, wg_ref, wu_ref, wd_ref, o_ref, xb_ref):
    c = pl.program_id(1)

    @pl.when(c == 0)
    def _():
        xb_ref[...] = x_ref[...].astype(BF16)

    xb = xb_ref[...]
    g = _dot(xb, wg_ref[...])
    u = _dot(xb, wu_ref[...])
    y = _dot((g * _sigmoid(g) * u).astype(BF16), wd_ref[...])

    @pl.when(c == 0)
    def _():
        o_ref[...] = y

    @pl.when(c > 0)
    def _():
        o_ref[...] += y


def _ffn(x2d, wg, wu, wd):
    n = x2d.shape[0]
    nc = D_FF_DENSE // FFN_CHUNK
    return pl.pallas_call(
        _ffn_kernel,
        grid=(n // TM, nc),
        in_specs=[pl.BlockSpec((TM, D_MODEL), lambda i, c: (i, 0)),
                  pl.BlockSpec((D_MODEL, FFN_CHUNK), lambda i, c: (0, c)),
                  pl.BlockSpec((D_MODEL, FFN_CHUNK), lambda i, c: (0, c)),
                  pl.BlockSpec((FFN_CHUNK, D_MODEL), lambda i, c: (c, 0))],
        out_specs=pl.BlockSpec((TM, D_MODEL), lambda i, c: (i, 0)),
        out_shape=jax.ShapeDtypeStruct((n, D_MODEL), F32),
        scratch_shapes=[pltpu.VMEM((TM, D_MODEL), BF16)],
        compiler_params=_cparams(("parallel", "arbitrary")),
        name="dense_swiglu",
    )(x2d, wg.astype(BF16), wu.astype(BF16), wd.astype(BF16))


def _router_kernel(x_ref, w_ref, tri_ref, xb_ref, gd_ref, scol_ref, srow_ref, base_ref, cnt_ref):
    tt = x_ref.shape[0]
    x = x_ref[...]
    xb_ref[...] = x.astype(BF16)
    xh, xl = _split2(x)
    wh, wl = _split2(w_ref[...])
    logits = _dot(xh, wh) + _dot(xh, wl) + _dot(xl, wh)
    lane = _lane((tt, LANES))
    lf = lane.astype(F32)
    logits = jnp.where(lane < N_EXPERTS, logits, -jnp.inf)
    m1 = jnp.max(logits, axis=-1, keepdims=True)
    i1 = jnp.min(jnp.where(logits == m1, lf, float(LANES)), axis=-1, keepdims=True)
    rest = jnp.where(lf == i1, -jnp.inf, logits)
    m2 = jnp.max(rest, axis=-1, keepdims=True)
    i2 = jnp.min(jnp.where(rest == m2, lf, float(LANES)), axis=-1, keepdims=True)
    e2 = jnp.exp(m2 - m1)
    g1 = 1.0 / (1.0 + e2)
    g2 = e2 / (1.0 + e2)
    first, second = lf == i1, lf == i2
    gd_ref[...] = jnp.where(first, g1, 0.0) + jnp.where(second, g2, 0.0)
    ind = jnp.where(first, 1.0, jnp.where(second, 1.0, 0.0))
    tri = tri_ref[...]
    run = jnp.zeros((1, LANES), F32)
    slots, bases = [], []
    for tb in range(tt // MOE_TB):
        blk = ind[tb * MOE_TB:(tb + 1) * MOE_TB]
        rank = _dot(tri, blk.astype(BF16)) + run
        slots.append(jnp.where(blk > 0.5, rank, -1.0))
        bases.append(run)
        run = run + jnp.sum(blk, axis=0, keepdims=True)
    slot = jnp.concatenate(slots, axis=0)
    scol_ref[...] = slot
    srow_ref[...] = jnp.transpose(slot)[0:N_EXPERTS]
    base_ref[0] = jnp.concatenate(bases, axis=0)
    cnt_ref[0] = jnp.broadcast_to(run, (8, LANES))


def _route(x2d, w_router):
    n = x2d.shape[0]
    nt = n // MOE_TILE
    ntb = MOE_TILE // MOE_TB
    w = jnp.zeros((D_MODEL, LANES), F32).at[:, :N_EXPERTS].set(w_router)
    tri = jnp.asarray(np.tril(np.ones((MOE_TB, MOE_TB), np.float32), -1), BF16)
    return pl.pallas_call(
        _router_kernel,
        grid=(nt,),
        in_specs=[pl.BlockSpec((MOE_TILE, D_MODEL), lambda i: (i, 0)),
                  pl.BlockSpec((D_MODEL, LANES), lambda i: (0, 0)),
                  pl.BlockSpec((MOE_TB, MOE_TB), lambda i: (0, 0))],
        out_specs=[pl.BlockSpec((MOE_TILE, D_MODEL), lambda i: (i, 0)),
                   pl.BlockSpec((MOE_TILE, LANES), lambda i: (i, 0)),
                   pl.BlockSpec((MOE_TILE, LANES), lambda i: (i, 0)),
                   pl.BlockSpec((N_EXPERTS, MOE_TILE), lambda i: (0, i)),
                   pl.BlockSpec((1, ntb, LANES), lambda i: (i, 0, 0)),
                   pl.BlockSpec((1, 8, LANES), lambda i: (i, 0, 0))],
        out_shape=[jax.ShapeDtypeStruct((n, D_MODEL), BF16),
                   jax.ShapeDtypeStruct((n, LANES), F32),
                   jax.ShapeDtypeStruct((n, LANES), F32),
                   jax.ShapeDtypeStruct((N_EXPERTS, n), F32),
                   jax.ShapeDtypeStruct((nt, ntb, LANES), F32),
                   jax.ShapeDtypeStruct((nt, 8, LANES), F32)],
        compiler_params=_cparams(("parallel",)),
        name="moe_router",
    )(x2d, w, tri)


MOE_GATHER_W = MOE_TB + 16
MOE_SCATTER_W = MOE_TB + 8
MOE_CAP = MOE_TILE + 512


def _moe_kernel(cnt_ref, base_ref, xb_ref, srow_ref, scol_ref, gd_ref, wg_ref, wu_ref, wd_ref,
                o_ref, xc_ref, yc_ref):
    i, e, c = pl.program_id(0), pl.program_id(1), pl.program_id(2)
    nc = pl.num_programs(2)
    ntb = MOE_TILE // MOE_TB
    count = cnt_ref[i * N_EXPERTS + e]
    n_full = jnp.right_shift(count, MOE_ROWS.bit_length() - 1)
    rem = jnp.bitwise_and(count, MOE_ROWS - 1)

    @pl.when((i == 0) & (e == 0) & (c == 0))
    def _():
        yc_ref[...] = jnp.zeros_like(yc_ref)

    @pl.when((e == 0) & (c == 0))
    def _():
        o_ref[...] = jnp.zeros_like(o_ref)

    @pl.when(c == 0)
    def _():
        xc_ref[...] = jnp.zeros_like(xc_ref)
        srow = srow_ref[pl.ds(e, 1), :]
        rid = lax.broadcasted_iota(jnp.int32, (MOE_GATHER_W, MOE_TB), 0).astype(F32)
        for tb in range(ntb):
            b0 = base_ref[(i * ntb + tb) * N_EXPERTS + e]
            b_al = pl.multiple_of(jnp.left_shift(jnp.right_shift(b0, 4), 4), 16)
            rel = srow[:, tb * MOE_TB:(tb + 1) * MOE_TB] - b_al.astype(F32)
            onehot = jnp.where(rel == rid, 1.0, 0.0).astype(BF16)
            rows = _dot(onehot, xb_ref[tb * MOE_TB:(tb + 1) * MOE_TB, :])
            cur = xc_ref[pl.ds(b_al, MOE_GATHER_W), :].astype(F32)
            xc_ref[pl.ds(b_al, MOE_GATHER_W), :] = (cur + rows).astype(BF16)

    def expert_rows(r0, n_rows):
        xr = xc_ref[pl.ds(r0, n_rows), :]
        g = _dot(xr, wg_ref[0])
        u = _dot(xr, wu_ref[0])
        y = _dot((g * _sigmoid(g) * u).astype(BF16), wd_ref[0])
        prev = yc_ref[pl.ds(r0, n_rows), :]
        yc_ref[pl.ds(r0, n_rows), :] = jnp.where(c == 0, y, prev + y)

    def rows_body(rb, carry):
        expert_rows(pl.multiple_of(rb * MOE_ROWS, MOE_ROWS), MOE_ROWS)
        return carry

    lax.fori_loop(0, n_full + jnp.where(rem > MOE_ROWS // 2, 1, 0), rows_body, 0)

    @pl.when((rem > 0) & (rem <= MOE_ROWS // 2))
    def _():
        expert_rows(pl.multiple_of(n_full * MOE_ROWS, MOE_ROWS), MOE_ROWS // 2)

    @pl.when(c == nc - 1)
    def _():
        lane = _lane((MOE_TB, LANES))
        cid = lax.broadcasted_iota(jnp.int32, (MOE_TB, MOE_SCATTER_W), 1).astype(F32)
        for tb in range(ntb):
            b0 = base_ref[(i * ntb + tb) * N_EXPERTS + e]
            b_al = pl.multiple_of(jnp.left_shift(jnp.right_shift(b0, 3), 3), 8)
            rows = slice(tb * MOE_TB, (tb + 1) * MOE_TB)
            slot = jnp.sum(jnp.where(lane == e, scol_ref[rows, :], 0.0), axis=-1, keepdims=True)
            gate = jnp.sum(jnp.where(lane == e, gd_ref[rows, :], 0.0), axis=-1, keepdims=True)
            onehot = jnp.where(slot - b_al.astype(F32) == cid, 1.0, 0.0).astype(BF16)
            y_win = yc_ref[pl.ds(b_al, MOE_SCATTER_W), :].astype(BF16)
            o_ref[rows, :] += gate * _dot(onehot, y_win)


def _moe(xb, srow, scol, gd, cnt, base, wg, wu, wd):
    n = xb.shape[0]
    nt = n // MOE_TILE
    nc = D_FF_EXPERT // MOE_CHUNK
    grid_spec = pltpu.PrefetchScalarGridSpec(
        num_scalar_prefetch=2,
        grid=(nt, N_EXPERTS, nc),
        in_specs=[pl.BlockSpec((MOE_TILE, D_MODEL), lambda i, e, c, *_: (i, 0)),
                  pl.BlockSpec((N_EXPERTS, MOE_TILE), lambda i, e, c, *_: (0, i)),
                  pl.BlockSpec((MOE_TILE, LANES), lambda i, e, c, *_: (i, 0)),
                  pl.BlockSpec((MOE_TILE, LANES), lambda i, e, c, *_: (i, 0)),
                  pl.BlockSpec((1, D_MODEL, MOE_CHUNK), lambda i, e, c, *_: (e, 0, c)),
                  pl.BlockSpec((1, D_MODEL, MOE_CHUNK), lambda i, e, c, *_: (e, 0, c)),
                  pl.BlockSpec((1, MOE_CHUNK, D_MODEL), lambda i, e, c, *_: (e, c, 0))],
        out_specs=pl.BlockSpec((MOE_TILE, D_MODEL), lambda i, e, c, *_: (i, 0)),
        scratch_shapes=[pltpu.VMEM((MOE_CAP, D_MODEL), BF16), pltpu.VMEM((MOE_CAP, D_MODEL), F32)],
    )
    return pl.pallas_call(
        _moe_kernel,
        grid_spec=grid_spec,
        out_shape=jax.ShapeDtypeStruct((n, D_MODEL), F32),
        compiler_params=_cparams(("arbitrary", "arbitrary", "arbitrary")),
        name="moe_experts",
    )(cnt, base, xb, srow, scol, gd, wg.astype(BF16), wu.astype(BF16), wd.astype(BF16))


def _ple_kernel(x_ref, f_ref, p_ref, wg_ref, wp_ref, g2_ref, b2_ref, g3_ref, b3_ref, o_ref):
    x2 = _layer_norm(DN_ALPHA * x_ref[...] + f_ref[...], g2_ref[...], b2_ref[...])
    e = _sigmoid(_dot(x2.astype(BF16), wg_ref[...])) * _dot(p_ref[...].astype(BF16), wp_ref[...])
    o_ref[...] = _layer_norm(DN_ALPHA * x2 + e, g3_ref[...], b3_ref[...])


def _ple(x2d, f2d, p2d, w_gate, w_proj, g2, b2, g3, b3):
    n = x2d.shape[0]
    row = lambda w: pl.BlockSpec((TM, w), lambda i: (i, 0))
    full = lambda s: pl.BlockSpec(s, lambda i: (0,) * len(s))
    vec = full((1, D_MODEL))
    return pl.pallas_call(
        _ple_kernel,
        grid=(n // TM,),
        in_specs=[row(D_MODEL), row(D_MODEL), row(PLE_DIM), full((D_MODEL, D_MODEL)), full((PLE_DIM, D_MODEL)),
                  vec, vec, vec, vec],
        out_specs=row(D_MODEL),
        out_shape=jax.ShapeDtypeStruct((n, D_MODEL), F32),
        compiler_params=_cparams(("parallel",)),
        name="ln_ple_ln",
    )(x2d, f2d, p2d, w_gate.astype(BF16), w_proj.astype(BF16),
      g2.reshape(1, -1), b2.reshape(1, -1), g3.reshape(1, -1), b3.reshape(1, -1))


def _mixer(x2d, b, t, i, w_in, cmp_pos, cmp_w1, cmp_w2, diff_lambda, diff_subln, sinks):
    slopes_a, slopes_b, slopes_c = _alibi_slopes()
    proj, gl, kv_t = _project(x2d, _build_w_in(w_in))
    proj3 = proj.reshape(b, t, N_MAIN)
    kvc, kvc_t = _compress(proj3, cmp_pos, cmp_w1, cmp_w2)
    o_c, sel = _cmp_attn(proj3, kvc, kvc_t, slopes_a)
    o_s = _selected_attn_t(proj3, kv_t, sel, slopes_a)
    o_w = _window_attn_t(proj3, kv_t, [slopes_a], q_blk=QA_BLK, kv_blk=KVA_BLK + 2, window=NSA_WINDOW,
                         name="nsa_window")
    lam_init = 0.8 - 0.6 * math.exp(-0.3 * i)
    o_b = _diff_attn_t(proj3, kv_t, slopes_b, diff_lambda, diff_subln, lam_init)
    r_c = C_HEADS // C_KV_HEADS
    o_sw = _window_attn_t(proj3, kv_t, [slopes_c[g * r_c:(g + 1) * r_c] for g in range(C_KV_HEADS)],
                          q_blk=QC_BLK, kv_blk=KVC_BLK, window=SWA_WINDOW, sinks=sinks, name="swa_gqa")
    flat = lambda a: a.reshape(b * t, a.shape[-1])
    return flat(o_c), flat(o_s), flat(o_w), gl, flat(o_b), flat(o_sw)


def kernel(x, p, w_in, cmp_pos, cmp_w1, cmp_w2, diff_lambda, diff_subln, sinks, w_out, ln1_g, ln1_b,
           ffn_w_gate, ffn_w_up, ffn_w_down, moe_router, moe_w_gate, moe_w_up, moe_w_down, ln2_g, ln2_b,
           ple_gate, ple_proj, ln3_g, ln3_b):
    b, t, d = x.shape
    n = b * t
    x2d = x.reshape(n, d)
    for i in range(DEPTH):
        heads = _mixer(x2d, b, t, i, w_in[i], cmp_pos[i], cmp_w1[i], cmp_w2[i],
                       diff_lambda[i], diff_subln[i], sinks[i])
        x1 = _outproj(*heads, x2d, w_out[i], ln1_g[i], ln1_b[i])
        if i % 2 == 0:
            f = _ffn(x1, ffn_w_gate[i // 2], ffn_w_up[i // 2], ffn_w_down[i // 2])
        else:
            xb, gd, scol, srow, base, cnt = _route(x1, moe_router[i // 2])
            cnt_i = cnt[:, 0, :N_EXPERTS].astype(jnp.int32).reshape(-1)
            base_i = base[:, :, :N_EXPERTS].astype(jnp.int32).reshape(-1)
            f = _moe(xb, srow, scol, gd, cnt_i, base_i,
                     moe_w_gate[i // 2], moe_w_up[i // 2], moe_w_down[i // 2])
        x2d = _ple(x1, f, p[i].reshape(n, PLE_DIM), ple_gate[i], ple_proj[i],
                   ln2_g[i], ln2_b[i], ln3_g[i], ln3_b[i])
    return x2d.reshape(b, t, d)
```

```python
import functools
import math

import jax
import jax.numpy as jnp
import numpy as np
from jax import lax
from jax.experimental import pallas as pl
from jax.experimental.pallas import tpu as pltpu

F32 = jnp.float32
BF16 = jnp.bfloat16

D_MODEL = 1024
HEAD_DIM = 64
A_HEADS = 4
B_HEADS = 4
C_HEADS = 8
C_KV_HEADS = 2
DIFF_DK = 32
CMP_LEN = 32
CMP_STRIDE = 16
CMP_HIDDEN = 128
SLC_BLOCK = 64
SLC_SHIFT = 6
SLC_TOPN = 16
NSA_WINDOW = 512
SWA_WINDOW = 128
N_EXPERTS = 8
D_FF_EXPERT = 3584
D_FF_DENSE = 2816
PLE_DIM = 256
LN_EPS = 1e-5
NEG_INF = -1e30
FORCE_SCORE = 1e9
DEPTH = 2
DN_ALPHA = (2 * DEPTH) ** 0.25

LANES = 128
QA_BLK, QB_BLK, QC_BLK, KVB_BLK, KVC_BLK, KVA_BLK, N_MAIN_BLK = 0, 4, 8, 16, 20, 22, 25
N_MAIN = N_MAIN_BLK * LANES

LOG2E = 1.4426950408889634
TQ = 256
TK_DENSE = 512
TQ_DIFF = 256
TK_DIFF = 512
TM = 512
FFN_CHUNK = 1408
MOE_TILE = 2048
MOE_ROWS = 256
MOE_TB = 256
MOE_CHUNK = 512
VMEM_LIMIT = 56 * 1024 * 1024


def _cparams(sem):
    return pltpu.CompilerParams(dimension_semantics=sem, vmem_limit_bytes=VMEM_LIMIT)


def _alibi_slopes():
    n = A_HEADS + B_HEADS + C_HEADS
    s = [LOG2E * 2.0 ** (-8.0 * i / n) for i in range(1, n + 1)]
    rest = s[C_HEADS:]
    return rest[0::2], rest[1::2], s[:C_HEADS]


def _dot(a, b):
    return jnp.dot(a, b, preferred_element_type=F32)


def _dot_nt(a, b):
    return lax.dot_general(a, b, (((1,), (1,)), ((), ())), preferred_element_type=F32)


def _split2(x):
    hi = x.astype(BF16)
    lo = (x - hi.astype(F32)).astype(BF16)
    return hi, lo


def _layer_norm(y, g, b):
    mu = jnp.mean(y, axis=-1, keepdims=True)
    d = y - mu
    var = jnp.mean(d * d, axis=-1, keepdims=True)
    return d * lax.rsqrt(var + LN_EPS) * g + b


def _sigmoid(x):
    return 1.0 / (1.0 + jnp.exp(-x))


def _lane(shape):
    return lax.broadcasted_iota(jnp.int32, shape, 1)


def _heads_to_rows(parts):
    out = [jnp.transpose(jnp.concatenate([a, b], axis=0)) for a, b in zip(parts[0::2], parts[1::2])]
    return out[0] if len(out) == 1 else jnp.concatenate(out, axis=1)


def _proj_kernel(x_ref, w_ref, o_ref, g_ref, t_ref):
    acc = _dot(x_ref[...].astype(BF16), w_ref[...])
    o_ref[...] = acc[:, :N_MAIN].astype(BF16)
    g_ref[...] = acc[:, N_MAIN:]
    t_ref[...] = jnp.transpose(acc[:, KVB_BLK * LANES:N_MAIN]).astype(BF16)


def _project(x2d, w_ext):
    n = x2d.shape[0]
    n_kv = N_MAIN - KVB_BLK * LANES
    return pl.pallas_call(
        _proj_kernel,
        grid=(n // TM,),
        in_specs=[pl.BlockSpec((TM, D_MODEL), lambda i: (i, 0)),
                  pl.BlockSpec((D_MODEL, N_MAIN + LANES), lambda i: (0, 0))],
        out_specs=[pl.BlockSpec((TM, N_MAIN), lambda i: (i, 0)),
                   pl.BlockSpec((TM, LANES), lambda i: (i, 0)),
                   pl.BlockSpec((n_kv, TM), lambda i: (0, i))],
        out_shape=[jax.ShapeDtypeStruct((n, N_MAIN), BF16),
                   jax.ShapeDtypeStruct((n, LANES), F32),
                   jax.ShapeDtypeStruct((n_kv, n), BF16)],
        compiler_params=_cparams(("parallel",)),
        name="in_proj",
    )(x2d, w_ext)


def _build_w_in(w):
    d = w.shape[0]
    o = np.cumsum([0, 256, 384, 12, 256, 256, 256, 512, 128, 128])
    q_a, kv_a, g_a, q_b, k_b, v_b, q_c, k_c, v_c = [w[:, o[i]:o[i + 1]] for i in range(9)]
    z = jnp.zeros((d, HEAD_DIM), F32)
    hd = HEAD_DIM
    cols = []
    for h in range(A_HEADS):
        cols += [q_a[:, h * hd:(h + 1) * hd] * (LOG2E * hd ** -0.5), z]
    for h in range(B_HEADS):
        cols += [q_b[:, h * hd:(h + 1) * hd] * (LOG2E * DIFF_DK ** -0.5), z]
    for h in range(C_HEADS):
        cols += [q_c[:, h * hd:(h + 1) * hd] * (LOG2E * hd ** -0.5), z]
    for h in range(B_HEADS):
        cols += [k_b[:, h * hd:(h + 1) * hd], v_b[:, h * hd:(h + 1) * hd]]
    for g in range(C_KV_HEADS):
        cols += [k_c[:, g * hd:(g + 1) * hd], v_c[:, g * hd:(g + 1) * hd]]
    cols.append(kv_a)
    cols += [g_a, jnp.zeros((d, LANES - 12), F32)]
    return jnp.concatenate(cols, axis=1).astype(BF16)


def _compress_kernel(ch_ref, wc_ref, pos_ref, w1_ref, w2_ref, o_ref, ot_ref):
    nck = ch_ref.shape[1]
    u = _dot(ch_ref[0], wc_ref[...])
    hs = []
    for j in range(2):
        pw = _dot(pos_ref[j].astype(BF16), w1_ref[j])[0:1]
        ua = u[:, (2 * j) * LANES:(2 * j + 1) * LANES]
        ub = u[:, (2 * j + 1) * LANES:(2 * j + 2) * LANES]
        pre = ua + pltpu.roll(ub, nck - 1, 0) + pw
        hs.append(0.5 * pre * (1.0 + jnp.tanh(0.7978845608028654 * (pre + 0.044715 * pre * pre * pre))))
    h = jnp.concatenate(hs, axis=1).astype(BF16)
    kvc = _dot(h, w2_ref[...])
    o_ref[0] = kvc.astype(BF16)
    ot_ref[0] = jnp.transpose(kvc).astype(BF16)


def _compress(proj3, cmp_pos, cmp_w1, cmp_w2):
    b, t, _ = proj3.shape
    nck = t // CMP_STRIDE
    kv = proj3[:, :, KVA_BLK * LANES:(KVA_BLK + 1) * LANES]
    chunks = kv.reshape(b, nck, CMP_STRIDE * LANES)
    w1 = cmp_w1.reshape(2, 2, CMP_STRIDE, HEAD_DIM, CMP_HIDDEN)
    z = jnp.zeros((CMP_STRIDE, HEAD_DIM, CMP_HIDDEN), F32)
    blocks = []
    for j in range(2):
        for half in range(2):
            pair = [w1[j, half], z] if j == 0 else [z, w1[j, half]]
            blocks.append(jnp.concatenate(pair, axis=1).reshape(CMP_STRIDE * LANES, CMP_HIDDEN))
    wc = jnp.concatenate(blocks, axis=1).astype(BF16)
    pos = jnp.zeros((2, 8, CMP_LEN * HEAD_DIM), F32).at[:, 0].set(cmp_pos.reshape(2, -1))
    z2 = jnp.zeros((CMP_HIDDEN, HEAD_DIM), F32)
    w2 = jnp.concatenate([jnp.concatenate([cmp_w2[0], z2], axis=1),
                          jnp.concatenate([z2, cmp_w2[1]], axis=1)], axis=0).astype(BF16)
    return pl.pallas_call(
        _compress_kernel,
        grid=(b,),
        in_specs=[pl.BlockSpec((1, nck, CMP_STRIDE * LANES), lambda i: (i, 0, 0)),
                  pl.BlockSpec(wc.shape, lambda i: (0, 0)),
                  pl.BlockSpec(pos.shape, lambda i: (0, 0, 0)),
                  pl.BlockSpec((2, CMP_LEN * HEAD_DIM, CMP_HIDDEN), lambda i: (0, 0, 0)),
                  pl.BlockSpec(w2.shape, lambda i: (0, 0))],
        out_specs=[pl.BlockSpec((1, nck, LANES), lambda i: (i, 0, 0)),
                   pl.BlockSpec((1, LANES, nck), lambda i: (i, 0, 0))],
        out_shape=[jax.ShapeDtypeStruct((b, nck, LANES), BF16),
                   jax.ShapeDtypeStruct((b, LANES, nck), BF16)],
        compiler_params=_cparams(("parallel",)),
        name="nsa_compress",
    )(chunks, wc, pos, cmp_w1.astype(BF16), w2)


def _cmp_attn_kernel_t(q_ref, kvc_ref, kvct_ref, ovt_ref, bias_ref, o_ref, sel_ref):
    tq = q_ref.shape[1]
    ncp = kvc_ref.shape[1]
    rows = A_HEADS * tq
    t0 = pl.program_id(1) * tq
    qs = _stack_heads(q_ref[0], A_HEADS)
    u = _dot_nt(kvc_ref[0], qs) + bias_ref[...]
    n = lax.broadcasted_iota(jnp.int32, (ncp, rows), 0)
    block_end = jnp.where(n == ncp - 1, jnp.int32(1 << 30), n * CMP_STRIDE + (CMP_LEN - 1))
    keep = block_end <= jnp.bitwise_and(_lane((ncp, rows)), tq - 1) + t0
    sh = jnp.where(keep, u, NEG_INF)
    m = jnp.max(sh, axis=0, keepdims=True)
    e = jnp.where(keep, jnp.exp2(sh - m), 0.0)
    p = e / jnp.maximum(jnp.sum(e, axis=0, keepdims=True), 1e-30)
    o_t = _dot(kvct_ref[0, HEAD_DIM:, :], p.astype(BF16))
    o_ref[0] = _heads_to_rows([o_t[:, h * tq:(h + 1) * tq] for h in range(A_HEADS)]).astype(BF16)

    psum = p[:, 0:tq] + p[:, tq:2 * tq] + p[:, 2 * tq:3 * tq] + p[:, 3 * tq:4 * tq]
    hi = psum.astype(BF16)
    r1 = psum - hi.astype(F32)
    mid = r1.astype(BF16)
    lo = (r1 - mid.astype(F32)).astype(BF16)
    ovt = ovt_ref[...]
    imp = _dot(ovt, hi) + _dot(ovt, mid) + _dot(ovt, lo)
    nsl = ncp * CMP_STRIDE // SLC_BLOCK
    j = lax.broadcasted_iota(jnp.int32, (LANES, tq), 0)
    t_blk = jnp.right_shift(_lane((LANES, tq)) + t0, SLC_SHIFT)
    forced = (j == 0) | (j == t_blk) | (j == t_blk - 1)
    imp = jnp.where(forced, FORCE_SCORE, jnp.where(j > t_blk, -1.0, imp))
    work = jnp.where(j < nsl, imp, -3.0)
    jf = j.astype(F32)
    sel = jnp.zeros((LANES, tq), F32)
    for _ in range(SLC_TOPN):
        mx = jnp.max(work, axis=0, keepdims=True)
        first = jnp.min(jnp.where(work == mx, jf, float(LANES)), axis=0, keepdims=True)
        pick = jf == first
        sel = jnp.where(pick, 1.0, sel)
        work = jnp.where(pick, -4.0, work)
    sel_ref[0] = jnp.transpose(sel).astype(BF16)


def _cmp_attn(proj3, kvc, kvc_t, slopes):
    b, t, _ = proj3.shape
    ncp = kvc.shape[1]
    nsl = t // SLC_BLOCK
    c0 = np.arange(ncp)[None, :] * CMP_STRIDE
    s0 = np.arange(LANES)[:, None] * SLC_BLOCK
    ovt = np.clip(np.minimum(c0 + CMP_LEN, s0 + SLC_BLOCK) - np.maximum(c0, s0), 0, None) / CMP_LEN
    ovt[:, ncp - 1] = 0.0
    ovt[nsl:, :] = 0.0
    rel = np.arange(TQ)[None, :] - (np.arange(ncp)[:, None] * CMP_STRIDE + (CMP_LEN - 1))
    bias = np.concatenate([-s * rel for s in slopes], axis=1).astype(np.float32)
    return pl.pallas_call(
        _cmp_attn_kernel_t,
        grid=(b, t // TQ),
        in_specs=[pl.BlockSpec((1, TQ, A_HEADS * LANES), lambda bi, i: (bi, i, QA_BLK // A_HEADS)),
                  pl.BlockSpec((1, ncp, LANES), lambda bi, i: (bi, 0, 0)),
                  pl.BlockSpec((1, LANES, ncp), lambda bi, i: (bi, 0, 0)),
                  pl.BlockSpec((LANES, ncp), lambda bi, i: (0, 0)),
                  pl.BlockSpec((ncp, A_HEADS * TQ), lambda bi, i: (0, 0))],
        out_specs=[pl.BlockSpec((1, TQ, A_HEADS * HEAD_DIM), lambda bi, i: (bi, i, 0)),
                   pl.BlockSpec((1, TQ, LANES), lambda bi, i: (bi, i, 0))],
        out_shape=[jax.ShapeDtypeStruct((b, t, A_HEADS * HEAD_DIM), BF16),
                   jax.ShapeDtypeStruct((b, t, LANES), BF16)],
        compiler_params=_cparams(("parallel", "parallel")),
        name="nsa_cmp_attn",
    )(proj3, kvc, kvc_t, jnp.asarray(ovt, BF16), jnp.asarray(bias))


def _stack_heads(q, n):
    return jnp.concatenate([q[:, h * LANES:(h + 1) * LANES] for h in range(n)], axis=0)


def _window_bias(slopes, tq, window):
    wk = window + tq
    n_var = -(-window // tq) + 1
    j = np.arange(wk)[:, None]
    q = np.arange(tq)[None, :]
    out = np.empty((len(slopes), n_var, wk, len(slopes[0]) * tq), np.float32)
    for v in range(n_var):
        d = v * tq - max(v * tq - window, 0) + q - j
        keep = (d >= 0) & (d < window)
        for g, row in enumerate(slopes):
            for r, slope in enumerate(row):
                out[g, v, :, r * tq:(r + 1) * tq] = np.where(keep, -slope * d, NEG_INF)
    return jnp.asarray(out)


def _window_kernel_t(*refs, r_heads, window, has_sink):
    if has_sink:
        sink_ref, q_ref, kv_ref, kvt_ref, bias_ref, o_ref = refs
    else:
        q_ref, kv_ref, kvt_ref, bias_ref, o_ref = refs
    g, i = pl.program_id(1), pl.program_id(2)
    tq = q_ref.shape[1]
    wk = window + tq
    start = pl.multiple_of(jnp.maximum(i * tq - window, 0), LANES)
    qs = _stack_heads(q_ref[0], r_heads)
    u = _dot_nt(kv_ref[0, pl.ds(start, wk), :], qs) + bias_ref[0, 0]
    m = jnp.max(u, axis=0, keepdims=True)
    if has_sink:
        sink = jnp.concatenate([jnp.full((1, tq), sink_ref[g * r_heads + r], F32) for r in range(r_heads)], axis=1)
        m = jnp.maximum(m, sink)
    p = jnp.exp2(u - m)
    den = jnp.sum(p, axis=0, keepdims=True)
    if has_sink:
        den = den + jnp.exp2(sink - m)
    o_t = _dot(kvt_ref[HEAD_DIM:, pl.ds(start, wk)], p.astype(BF16)) / den
    o_ref[0] = _heads_to_rows([o_t[:, r * tq:(r + 1) * tq] for r in range(r_heads)]).astype(BF16)


def _window_attn_t(proj3, kv_t, slopes, *, q_blk, kv_blk, window, sinks=None, name):
    b, t, _ = proj3.shape
    groups, r_heads = len(slopes), len(slopes[0])
    bias = _window_bias(slopes, TQ, window)
    n_var, wk = bias.shape[1], bias.shape[2]
    in_specs = [pl.BlockSpec((1, TQ, r_heads * LANES), lambda bi, g, i: (bi, i, q_blk // r_heads + g)),
                pl.BlockSpec((1, t, LANES), lambda bi, g, i: (bi, 0, kv_blk + g)),
                pl.BlockSpec((LANES, t), lambda bi, g, i: (kv_blk - KVB_BLK + g, bi)),
                pl.BlockSpec((1, 1, wk, r_heads * TQ), lambda bi, g, i: (g, jnp.minimum(i, n_var - 1), 0, 0))]
    args = [proj3, proj3, kv_t, bias]
    if sinks is not None:
        in_specs.insert(0, pl.BlockSpec(memory_space=pltpu.SMEM))
        args.insert(0, sinks.astype(F32) * LOG2E)
    return pl.pallas_call(
        functools.partial(_window_kernel_t, r_heads=r_heads, window=window, has_sink=sinks is not None),
        grid=(b, groups, t // TQ),
        in_specs=in_specs,
        out_specs=pl.BlockSpec((1, TQ, r_heads * HEAD_DIM), lambda bi, g, i: (bi, i, g)),
        out_shape=jax.ShapeDtypeStruct((b, t, groups * r_heads * HEAD_DIM), BF16),
        compiler_params=_cparams(("parallel", "parallel", "parallel")),
        name=name,
    )(*args)


def _flash_bias(stream_slopes, tq, tk):
    ratio = tk // tq
    j = np.arange(tk)[:, None]
    q = np.arange(tq)[None, :]
    out = np.empty((len(stream_slopes), 1 + ratio, tk, len(stream_slopes[0]) * tq), np.float32)
    for s, slopes in enumerate(stream_slopes):
        for h, slope in enumerate(slopes):
            cols = slice(h * tq, (h + 1) * tq)
            out[s, 0, :, cols] = slope * j
            for r in range(ratio):
                out[s, 1 + r, :, cols] = np.where(j <= r * tq + q, slope * j, NEG_INF)
    return jnp.asarray(out)


def _flash_t(streams, kv_ref, kvt_ref, i, *, tq, tk, sel_t=None):
    ratio = tk // tq
    n_full = lax.div(i, jnp.int32(ratio))
    rows = streams[0][0].shape[0]

    def tile(kt, carry, edge):
        k0 = pl.multiple_of(kt * tk, tk)
        k0f = (kt * tk).astype(F32)
        variant = 1 + i - kt * ratio if edge else 0
        keep = None
        if sel_t is not None:
            blk = jnp.right_shift(lax.broadcasted_iota(jnp.int32, (tk, LANES), 0) + kt * tk, SLC_SHIFT)
            expand = jnp.where(_lane((tk, LANES)) == blk, 1.0, 0.0).astype(BF16)
            keep = _dot(expand, sel_t) > 0.5
        scores = [_dot_nt(kv_ref[0, pl.ds(k0, tk), s[1]:s[1] + LANES], s[0]) for s in streams]
        probs, stats = [], []
        for (_, _, slope_lane, bias), (m, l, _), u in zip(streams, carry, scores):
            u = u + bias(variant)
            if keep is not None:
                u = jnp.where(keep, u, NEG_INF)
            shift = slope_lane * k0f
            m_new = jnp.maximum(m, jnp.max(u, axis=0, keepdims=True) + shift)
            p = jnp.exp2(u - (m_new - shift))
            alpha = jnp.exp2(m - m_new)
            probs.append(p.astype(BF16))
            stats.append((m_new, alpha, alpha * l + jnp.sum(p, axis=0, keepdims=True)))
        out = []
        for (_, col, _, _), (_, _, acc), p, (m_new, alpha, l) in zip(streams, carry, probs, stats):
            kv_f = kvt_ref[col + HEAD_DIM:col + LANES, pl.ds(k0, tk)]
            out.append((m_new, l, alpha * acc + _dot(kv_f, p)))
        return tuple(out)

    init = tuple((jnp.full((1, rows), NEG_INF, F32), jnp.zeros((1, rows), F32), jnp.zeros((HEAD_DIM, rows), F32))
                 for _ in streams)
    carry = lax.fori_loop(0, n_full, lambda kt, c: tile(kt, c, False), init)
    carry = tile(n_full, carry, True)
    return [(l, acc) for _, l, acc in carry]


def _selected_kernel_t(q_ref, kv_ref, kvt_ref, sel_ref, bias_ref, o_ref, *, slopes):
    tq = q_ref.shape[1]
    i = pl.program_id(1)
    sel_t = jnp.transpose(sel_ref[0].astype(F32)).astype(BF16)
    sel_t = jnp.concatenate([sel_t, sel_t], axis=1)
    streams = []
    for pair in range(A_HEADS // 2):
        qs = jnp.concatenate([q_ref[0, :, h * LANES:(h + 1) * LANES] for h in (2 * pair, 2 * pair + 1)], axis=0)
        slope_lane = jnp.concatenate([jnp.full((1, tq), slopes[h], F32) for h in (2 * pair, 2 * pair + 1)], axis=1)
        streams.append((qs, 0, slope_lane, functools.partial(lambda s, v: bias_ref[s, v], pair)))
    outs = []
    for l, acc in _flash_t(streams, kv_ref, kvt_ref, i, tq=tq, tk=TK_DENSE, sel_t=sel_t):
        o_t = acc / l
        outs += [o_t[:, :tq], o_t[:, tq:]]
    o_ref[0] = _heads_to_rows(outs).astype(BF16)


def _selected_attn_t(proj3, kv_t, sel, slopes):
    b, t, _ = proj3.shape
    kv_blk = KVA_BLK + 1
    bias = _flash_bias([slopes[0:2], slopes[2:4]], TQ, TK_DENSE)
    return pl.pallas_call(
        functools.partial(_selected_kernel_t, slopes=slopes),
        grid=(b, t // TQ),
        in_specs=[pl.BlockSpec((1, TQ, A_HEADS * LANES), lambda bi, i: (bi, i, QA_BLK // A_HEADS)),
                  pl.BlockSpec((1, t, LANES), lambda bi, i: (bi, 0, kv_blk)),
                  pl.BlockSpec((LANES, t), lambda bi, i: (kv_blk - KVB_BLK, bi)),
                  pl.BlockSpec((1, TQ, LANES), lambda bi, i: (bi, i, 0)),
                  pl.BlockSpec(bias.shape, lambda bi, i: (0, 0, 0, 0))],
        out_specs=pl.BlockSpec((1, TQ, A_HEADS * HEAD_DIM), lambda bi, i: (bi, i, 0)),
        out_shape=jax.ShapeDtypeStruct((b, t, A_HEADS * HEAD_DIM), BF16),
        compiler_params=_cparams(("parallel", "parallel")),
        name="nsa_selected",
    )(proj3, proj3, kv_t, sel, bias)


def _diff_kernel_t(slope_ref, dl_ref, g_ref, q_ref, kv_ref, kvt_ref, bias_ref, o_ref, *, lam_init):
    tq = q_ref.shape[1]
    hp = pl.program_id(1)
    i = pl.program_id(2)
    dl = dl_ref[...]
    lam = (jnp.exp(jnp.sum(dl[0:1] * dl[1:2], axis=-1, keepdims=True))
           - jnp.exp(jnp.sum(dl[2:3] * dl[3:4], axis=-1, keepdims=True)) + lam_init)
    lane = _lane((tq, LANES))
    streams = []
    for hh in range(2):
        q = q_ref[0, :, hh * LANES:(hh + 1) * LANES]
        zero = jnp.zeros_like(q)
        qs = jnp.concatenate([jnp.where(lane < DIFF_DK, q, zero),
                              jnp.where(lane >= DIFF_DK, q, zero)], axis=0)
        slope_lane = jnp.full((1, 2 * tq), slope_ref[hp * 2 + hh], F32)
        streams.append((qs, hh * LANES, slope_lane, functools.partial(lambda s, v: bias_ref[s, v], hh)))
    outs = []
    for l, acc in _flash_t(streams, kv_ref, kvt_ref, i, tq=tq, tk=TK_DIFF):
        o = acc / l
        w = o[:, :tq] - lam * o[:, tq:]
        ms = jnp.sum(w * w, axis=0, keepdims=True) * (1.0 / HEAD_DIM)
        outs.append(w * lax.rsqrt(ms + LN_EPS) * (1.0 - lam_init))
    o_ref[0] = (_heads_to_rows(outs) * g_ref[...]).astype(BF16)


def _diff_attn_t(proj3, kv_t, slopes, diff_lambda, subln, lam_init):
    b, t, _ = proj3.shape
    g_ext = jnp.tile(subln.reshape(1, HEAD_DIM).astype(F32), (1, 2))
    bias = _flash_bias([[s, s] for s in slopes], TQ_DIFF, TK_DIFF)
    return pl.pallas_call(
        functools.partial(_diff_kernel_t, lam_init=lam_init),
        grid=(b, B_HEADS // 2, t // TQ_DIFF),
        in_specs=[pl.BlockSpec(memory_space=pltpu.SMEM),
                  pl.BlockSpec((4, DIFF_DK), lambda bi, h, i: (0, 0)),
                  pl.BlockSpec((1, LANES), lambda bi, h, i: (0, 0)),
                  pl.BlockSpec((1, TQ_DIFF, 2 * LANES), lambda bi, h, i: (bi, i, QB_BLK // 2 + h)),
                  pl.BlockSpec((1, t, 2 * LANES), lambda bi, h, i: (bi, 0, KVB_BLK // 2 + h)),
                  pl.BlockSpec((2 * LANES, t), lambda bi, h, i: (h, bi)),
                  pl.BlockSpec((2,) + bias.shape[1:], lambda bi, h, i: (h, 0, 0, 0))],
        out_specs=pl.BlockSpec((1, TQ_DIFF, 2 * HEAD_DIM), lambda bi, h, i: (bi, i, h)),
        out_shape=jax.ShapeDtypeStruct((b, t, B_HEADS * HEAD_DIM), BF16),
        compiler_params=_cparams(("parallel", "parallel", "parallel")),
        name="diff_attn",
    )(jnp.asarray(slopes, F32), diff_lambda.astype(F32), g_ext, proj3, proj3, kv_t, bias)


def _outproj_kernel(oc_ref, os_ref, ow_ref, gl_ref, ob_ref, osw_ref, x_ref, w_ref, ex_ref, g_ref, b_ref, o_ref):
    hi, lo = _split2(_sigmoid(gl_ref[...]))
    oa = None
    for j, ref in enumerate((oc_ref, os_ref, ow_ref)):
        gate = _dot(hi, ex_ref[j]) + _dot(lo, ex_ref[j])
        term = gate * ref[...].astype(F32)
        oa = term if oa is None else oa + term
    cat = jnp.concatenate([oa.astype(BF16), ob_ref[...], osw_ref[...]], axis=1)
    y = DN_ALPHA * x_ref[...] + _dot(cat, w_ref[...])
    o_ref[...] = _layer_norm(y, g_ref[...], b_ref[...])


def _outproj(o_c, o_s, o_w, gl, o_b, o_sw, x2d, w_out, ln_g, ln_b):
    n = x2d.shape[0]
    ex = np.zeros((3, LANES, A_HEADS * HEAD_DIM), np.float32)
    for j in range(3):
        for h in range(A_HEADS):
            ex[j, h * 3 + j, h * HEAD_DIM:(h + 1) * HEAD_DIM] = 1.0
    row = lambda w: pl.BlockSpec((TM, w), lambda i: (i, 0))
    full = lambda s: pl.BlockSpec(s, lambda i: (0,) * len(s))
    return pl.pallas_call(
        _outproj_kernel,
        grid=(n // TM,),
        in_specs=[row(256), row(256), row(256), row(LANES), row(256), row(512), row(D_MODEL),
                  full((D_MODEL, D_MODEL)), full(ex.shape), full((1, D_MODEL)), full((1, D_MODEL))],
        out_specs=row(D_MODEL),
        out_shape=jax.ShapeDtypeStruct((n, D_MODEL), F32),
        compiler_params=_cparams(("parallel",)),
        name="out_proj_ln",
    )(o_c, o_s, o_w, gl, o_b, o_sw, x2d, w_out.astype(BF16), jnp.asarray(ex, BF16),
      ln_g.reshape(1, -1), ln_b.reshape(1, -1))


def _ffn_kernel(x_ref, wg_ref, wu_ref, wd_ref, o_ref, xb_ref):
    c = pl.program_id(1)

    @pl.when(c == 0)
    def _():
        xb_ref[...] = x_ref[...].astype(BF16)

    xb = xb_ref[...]
    g = _dot(xb, wg_ref[...])
    u = _dot(xb, wu_ref[...])
    y = _dot((g * _sigmoid(g) * u).astype(BF16), wd_ref[...])

    @pl.when(c == 0)
    def _():
        o_ref[...] = y

    @pl.when(c > 0)
    def _():
        o_ref[...] += y


def _ffn(x2d, wg, wu, wd):
    n = x2d.shape[0]
    nc = D_FF_DENSE // FFN_CHUNK
    return pl.pallas_call(
        _ffn_kernel,
        grid=(n // TM, nc),
        in_specs=[pl.BlockSpec((TM, D_MODEL), lambda i, c: (i, 0)),
                  pl.BlockSpec((D_MODEL, FFN_CHUNK), lambda i, c: (0, c)),
                  pl.BlockSpec((D_MODEL, FFN_CHUNK), lambda i, c: (0, c)),
                  pl.BlockSpec((FFN_CHUNK, D_MODEL), lambda i, c: (c, 0))],
        out_specs=pl.BlockSpec((TM, D_MODEL), lambda i, c: (i, 0)),
        out_shape=jax.ShapeDtypeStruct((n, D_MODEL), F32),
        scratch_shapes=[pltpu.VMEM((TM, D_MODEL), BF16)],
        compiler_params=_cparams(("parallel", "arbitrary")),
        name="dense_swiglu",
    )(x2d, wg.astype(BF16), wu.astype(BF16), wd.astype(BF16))


def _router_kernel(x_ref, w_ref, tri_ref, xb_ref, gd_ref, scol_ref, srow_ref, base_ref, cnt_ref):
    tt = x_ref.shape[0]
    x = x_ref[...]
    xb_ref[...] = x.astype(BF16)
    xh, xl = _split2(x)
    wh, wl = _split2(w_ref[...])
    logits = _dot(xh, wh) + _dot(xh, wl) + _dot(xl, wh)
    lane = _lane((tt, LANES))
    lf = lane.astype(F32)
    logits = jnp.where(lane < N_EXPERTS, logits, -jnp.inf)
    m1 = jnp.max(logits, axis=-1, keepdims=True)
    i1 = jnp.min(jnp.where(logits == m1, lf, float(LANES)), axis=-1, keepdims=True)
    rest = jnp.where(lf == i1, -jnp.inf, logits)
    m2 = jnp.max(rest, axis=-1, keepdims=True)
    i2 = jnp.min(jnp.where(rest == m2, lf, float(LANES)), axis=-1, keepdims=True)
    e2 = jnp.exp(m2 - m1)
    g1 = 1.0 / (1.0 + e2)
    g2 = e2 / (1.0 + e2)
    first, second = lf == i1, lf == i2
    gd_ref[...] = jnp.where(first, g1, 0.0) + jnp.where(second, g2, 0.0)
    ind = jnp.where(first, 1.0, jnp.where(second, 1.0, 0.0))
    tri = tri_ref[...]
    run = jnp.zeros((1, LANES), F32)
    slots, bases = [], []
    for tb in range(tt // MOE_TB):
        blk = ind[tb * MOE_TB:(tb + 1) * MOE_TB]
        rank = _dot(tri, blk.astype(BF16)) + run
        slots.append(jnp.where(blk > 0.5, rank, -1.0))
        bases.append(run)
        run = run + jnp.sum(blk, axis=0, keepdims=True)
    slot = jnp.concatenate(slots, axis=0)
    scol_ref[...] = slot
    srow_ref[...] = jnp.transpose(slot)[0:N_EXPERTS]
    base_ref[0] = jnp.concatenate(bases, axis=0)
    cnt_ref[0] = jnp.broadcast_to(run, (8, LANES))


def _route(x2d, w_router):
    n = x2d.shape[0]
    nt = n // MOE_TILE
    ntb = MOE_TILE // MOE_TB
    w = jnp.zeros((D_MODEL, LANES), F32).at[:, :N_EXPERTS].set(w_router)
    tri = jnp.asarray(np.tril(np.ones((MOE_TB, MOE_TB), np.float32), -1), BF16)
    return pl.pallas_call(
        _router_kernel,
        grid=(nt,),
        in_specs=[pl.BlockSpec((MOE_TILE, D_MODEL), lambda i: (i, 0)),
                  pl.BlockSpec((D_MODEL, LANES), lambda i: (0, 0)),
                  pl.BlockSpec((MOE_TB, MOE_TB), lambda i: (0, 0))],
        out_specs=[pl.BlockSpec((MOE_TILE, D_MODEL), lambda i: (i, 0)),
                   pl.BlockSpec((MOE_TILE, LANES), lambda i: (i, 0)),
                   pl.BlockSpec((MOE_TILE, LANES), lambda i: (i, 0)),
                   pl.BlockSpec((N_EXPERTS, MOE_TILE), lambda i: (0, i)),
                   pl.BlockSpec((1, ntb, LANES), lambda i: (i, 0, 0)),
                   pl.BlockSpec((1, 8, LANES), lambda i: (i, 0, 0))],
        out_shape=[jax.ShapeDtypeStruct((n, D_MODEL), BF16),
                   jax.ShapeDtypeStruct((n, LANES), F32),
                   jax.ShapeDtypeStruct((n, LANES), F32),
                   jax.ShapeDtypeStruct((N_EXPERTS, n), F32),
                   jax.ShapeDtypeStruct((nt, ntb, LANES), F32),
                   jax.ShapeDtypeStruct((nt, 8, LANES), F32)],
        compiler_params=_cparams(("parallel",)),
        name="moe_router",
    )(x2d, w, tri)


MOE_GATHER_W = MOE_TB + 16
MOE_SCATTER_W = MOE_TB + 8
MOE_CAP = MOE_TILE + 512


def _moe_kernel(cnt_ref, base_ref, xb_ref, srow_ref, scol_ref, gd_ref, wg_ref, wu_ref, wd_ref,
                o_ref, xc_ref, yc_ref):
    i, e, c = pl.program_id(0), pl.program_id(1), pl.program_id(2)
    nc = pl.num_programs(2)
    ntb = MOE_TILE // MOE_TB
    count = cnt_ref[i * N_EXPERTS + e]
    n_full = jnp.right_shift(count, MOE_ROWS.bit_length() - 1)
    rem = jnp.bitwise_and(count, MOE_ROWS - 1)

    @pl.when((i == 0) & (e == 0) & (c == 0))
    def _():
        yc_ref[...] = jnp.zeros_like(yc_ref)

    @pl.when((e == 0) & (c == 0))
    def _():
        o_ref[...] = jnp.zeros_like(o_ref)

    @pl.when(c == 0)
    def _():
        xc_ref[...] = jnp.zeros_like(xc_ref)
        srow = srow_ref[pl.ds(e, 1), :]
        rid = lax.broadcasted_iota(jnp.int32, (MOE_GATHER_W, MOE_TB), 0).astype(F32)
        for tb in range(ntb):
            b0 = base_ref[(i * ntb + tb) * N_EXPERTS + e]
            b_al = pl.multiple_of(jnp.left_shift(jnp.right_shift(b0, 4), 4), 16)
            rel = srow[:, tb * MOE_TB:(tb + 1) * MOE_TB] - b_al.astype(F32)
            onehot = jnp.where(rel == rid, 1.0, 0.0).astype(BF16)
            rows = _dot(onehot, xb_ref[tb * MOE_TB:(tb + 1) * MOE_TB, :])
            cur = xc_ref[pl.ds(b_al, MOE_GATHER_W), :].astype(F32)
            xc_ref[pl.ds(b_al, MOE_GATHER_W), :] = (cur + rows).astype(BF16)

    def expert_rows(r0, n_rows):
        xr = xc_ref[pl.ds(r0, n_rows), :]
        g = _dot(xr, wg_ref[0])
        u = _dot(xr, wu_ref[0])
        y = _dot((g * _sigmoid(g) * u).astype(BF16), wd_ref[0])
        prev = yc_ref[pl.ds(r0, n_rows), :]
        yc_ref[pl.ds(r0, n_rows), :] = jnp.where(c == 0, y, prev + y)

    def rows_body(rb, carry):
        expert_rows(pl.multiple_of(rb * MOE_ROWS, MOE_ROWS), MOE_ROWS)
        return carry

    lax.fori_loop(0, n_full + jnp.where(rem > MOE_ROWS // 2, 1, 0), rows_body, 0)

    @pl.when((rem > 0) & (rem <= MOE_ROWS // 2))
    def _():
        expert_rows(pl.multiple_of(n_full * MOE_ROWS, MOE_ROWS), MOE_ROWS // 2)

    @pl.when(c == nc - 1)
    def _():
        lane = _lane((MOE_TB, LANES))
        cid = lax.broadcasted_iota(jnp.int32, (MOE_TB, MOE_SCATTER_W), 1).astype(F32)
        for tb in range(ntb):
            b0 = base_ref[(i * ntb + tb) * N_EXPERTS + e]
            b_al = pl.multiple_of(jnp.left_shift(jnp.right_shift(b0, 3), 3), 8)
            rows = slice(tb * MOE_TB, (tb + 1) * MOE_TB)
            slot = jnp.sum(jnp.where(lane == e, scol_ref[rows, :], 0.0), axis=-1, keepdims=True)
            gate = jnp.sum(jnp.where(lane == e, gd_ref[rows, :], 0.0), axis=-1, keepdims=True)
            onehot = jnp.where(slot - b_al.astype(F32) == cid, 1.0, 0.0).astype(BF16)
            y_win = yc_ref[pl.ds(b_al, MOE_SCATTER_W), :].astype(BF16)
            o_ref[rows, :] += gate * _dot(onehot, y_win)


def _moe(xb, srow, scol, gd, cnt, base, wg, wu, wd):
    n = xb.shape[0]
    nt = n // MOE_TILE
    nc = D_FF_EXPERT // MOE_CHUNK
    grid_spec = pltpu.PrefetchScalarGridSpec(
        num_scalar_prefetch=2,
        grid=(nt, N_EXPERTS, nc),
        in_specs=[pl.BlockSpec((MOE_TILE, D_MODEL), lambda i, e, c, *_: (i, 0)),
                  pl.BlockSpec((N_EXPERTS, MOE_TILE), lambda i, e, c, *_: (0, i)),
                  pl.BlockSpec((MOE_TILE, LANES), lambda i, e, c, *_: (i, 0)),
                  pl.BlockSpec((MOE_TILE, LANES), lambda i, e, c, *_: (i, 0)),
                  pl.BlockSpec((1, D_MODEL, MOE_CHUNK), lambda i, e, c, *_: (e, 0, c)),
                  pl.BlockSpec((1, D_MODEL, MOE_CHUNK), lambda i, e, c, *_: (e, 0, c)),
                  pl.BlockSpec((1, MOE_CHUNK, D_MODEL), lambda i, e, c, *_: (e, c, 0))],
        out_specs=pl.BlockSpec((MOE_TILE, D_MODEL), lambda i, e, c, *_: (i, 0)),
        scratch_shapes=[pltpu.VMEM((MOE_CAP, D_MODEL), BF16), pltpu.VMEM((MOE_CAP, D_MODEL), F32)],
    )
    return pl.pallas_call(
        _moe_kernel,
        grid_spec=grid_spec,
        out_shape=jax.ShapeDtypeStruct((n, D_MODEL), F32),
        compiler_params=_cparams(("arbitrary", "arbitrary", "arbitrary")),
        name="moe_experts",
    )(cnt, base, xb, srow, scol, gd, wg.astype(BF16), wu.astype(BF16), wd.astype(BF16))


def _ple_kernel(x_ref, f_ref, p_ref, wg_ref, wp_ref, g2_ref, b2_ref, g3_ref, b3_ref, o_ref):
    x2 = _layer_norm(DN_ALPHA * x_ref[...] + f_ref[...], g2_ref[...], b2_ref[...])
    e = _sigmoid(_dot(x2.astype(BF16), wg_ref[...])) * _dot(p_ref[...].astype(BF16), wp_ref[...])
    o_ref[...] = _layer_norm(DN_ALPHA * x2 + e, g3_ref[...], b3_ref[...])


def _ple(x2d, f2d, p2d, w_gate, w_proj, g2, b2, g3, b3):
    n = x2d.shape[0]
    row = lambda w: pl.BlockSpec((TM, w), lambda i: (i, 0))
    full = lambda s: pl.BlockSpec(s, lambda i: (0,) * len(s))
    vec = full((1, D_MODEL))
    return pl.pallas_call(
        _ple_kernel,
        grid=(n // TM,),
        in_specs=[row(D_MODEL), row(D_MODEL), row(PLE_DIM), full((D_MODEL, D_MODEL)), full((PLE_DIM, D_MODEL)),
                  vec, vec, vec, vec],
        out_specs=row(D_MODEL),
        out_shape=jax.ShapeDtypeStruct((n, D_MODEL), F32),
        compiler_params=_cparams(("parallel",)),
        name="ln_ple_ln",
    )(x2d, f2d, p2d, w_gate.astype(BF16), w_proj.astype(BF16),
      g2.reshape(1, -1), b2.reshape(1, -1), g3.reshape(1, -1), b3.reshape(1, -1))


def _mixer(x2d, b, t, i, w_in, cmp_pos, cmp_w1, cmp_w2, diff_lambda, diff_subln, sinks):
    slopes_a, slopes_b, slopes_c = _alibi_slopes()
    proj, gl, kv_t = _project(x2d, _build_w_in(w_in))
    proj3 = proj.reshape(b, t, N_MAIN)
    kvc, kvc_t = _compress(proj3, cmp_pos, cmp_w1, cmp_w2)
    o_c, sel = _cmp_attn(proj3, kvc, kvc_t, slopes_a)
    o_s = _selected_attn_t(proj3, kv_t, sel, slopes_a)
    o_w = _window_attn_t(proj3, kv_t, [slopes_a], q_blk=QA_BLK, kv_blk=KVA_BLK + 2, window=NSA_WINDOW,
                         name="nsa_window")
    lam_init = 0.8 - 0.6 * math.exp(-0.3 * i)
    o_b = _diff_attn_t(proj3, kv_t, slopes_b, diff_lambda, diff_subln, lam_init)
    r_c = C_HEADS // C_KV_HEADS
    o_sw = _window_attn_t(proj3, kv_t, [slopes_c[g * r_c:(g + 1) * r_c] for g in range(C_KV_HEADS)],
                          q_blk=QC_BLK, kv_blk=KVC_BLK, window=SWA_WINDOW, sinks=sinks, name="swa_gqa")
    flat = lambda a: a.reshape(b * t, a.shape[-1])
    return flat(o_c), flat(o_s), flat(o_w), gl, flat(o_b), flat(o_sw)


def kernel(x, p, w_in, cmp_pos, cmp_w1, cmp_w2, diff_lambda, diff_subln, sinks, w_out, ln1_g, ln1_b,
           ffn_w_gate, ffn_w_up, ffn_w_down, moe_router, moe_w_gate, moe_w_up, moe_w_down, ln2_g, ln2_b,
           ple_gate, ple_proj, ln3_g, ln3_b):
    b, t, d = x.shape
    n = b * t
    x2d = x.reshape(n, d)
    for i in range(DEPTH):
        heads = _mixer(x2d, b, t, i, w_in[i], cmp_pos[i], cmp_w1[i], cmp_w2[i],
                       diff_lambda[i], diff_subln[i], sinks[i])
        x1 = _outproj(*heads, x2d, w_out[i], ln1_g[i], ln1_b[i])
        if i % 2 == 0:
            f = _ffn(x1, ffn_w_gate[i // 2], ffn_w_up[i // 2], ffn_w_down[i // 2])
        else:
            xb, gd, scol, srow, base, cnt = _route(x1, moe_router[i // 2])
            cnt_i = cnt[:, 0, :N_EXPERTS].astype(jnp.int32).reshape(-1)
            base_i = base[:, :, :N_EXPERTS].astype(jnp.int32).reshape(-1)
            f = _moe(xb, srow, scol, gd, cnt_i, base_i,
                     moe_w_gate[i // 2], moe_w_up[i // 2], moe_w_down[i // 2])
        x2d = _ple(x1, f, p[i].reshape(n, PLE_DIM), ple_gate[i], ple_proj[i],
                   ln2_g[i], ln2_b[i], ln3_g[i], ln3_b[i])
    return x2d.reshape(b, t, d)
```

```python
import functools
import math

import jax
import jax.numpy as jnp
import numpy as np
from jax import lax
from jax.experimental import pallas as pl
from jax.experimental.pallas import tpu as pltpu

F32 = jnp.float32
BF16 = jnp.bfloat16

D_MODEL = 1024
HEAD_DIM = 64
A_HEADS = 4
B_HEADS = 4
C_HEADS = 8
C_KV_HEADS = 2
DIFF_DK = 32
CMP_LEN = 32
CMP_STRIDE = 16
CMP_HIDDEN = 128
SLC_BLOCK = 64
SLC_SHIFT = 6
SLC_TOPN = 16
NSA_WINDOW = 512
SWA_WINDOW = 128
N_EXPERTS = 8
D_FF_EXPERT = 3584
D_FF_DENSE = 2816
PLE_DIM = 256
LN_EPS = 1e-5
NEG_INF = -1e30
FORCE_SCORE = 1e9
DEPTH = 2
DN_ALPHA = (2 * DEPTH) ** 0.25

LANES = 128
QA_BLK, QB_BLK, QC_BLK, KVB_BLK, KVC_BLK, KVA_BLK, N_MAIN_BLK = 0, 4, 8, 16, 20, 22, 25
N_MAIN = N_MAIN_BLK * LANES

LOG2E = 1.4426950408889634
TQ = 256
TQ_DENSE = 512
TK_DENSE = 512
TM = 512
FFN_CHUNK = 1408
MOE_TILE = 2048
MOE_ROWS = 288
MOE_ROWS_RECIP = (7282, 21)
MOE_ROW_ALIGN = 16
MOE_TB = 256
MOE_CHUNK = 896
VMEM_LIMIT = 56 * 1024 * 1024


def _cparams(sem):
    return pltpu.CompilerParams(dimension_semantics=sem, vmem_limit_bytes=VMEM_LIMIT)


def _alibi_slopes():
    n = A_HEADS + B_HEADS + C_HEADS
    s = [LOG2E * 2.0 ** (-8.0 * i / n) for i in range(1, n + 1)]
    rest = s[C_HEADS:]
    return rest[0::2], rest[1::2], s[:C_HEADS]


def _dot(a, b):
    return jnp.dot(a, b, preferred_element_type=F32)


def _dot_nt(a, b):
    return lax.dot_general(a, b, (((1,), (1,)), ((), ())), preferred_element_type=F32)


def _split2(x):
    hi = x.astype(BF16)
    lo = (x - hi.astype(F32)).astype(BF16)
    return hi, lo


def _layer_norm(y, g, b):
    mu = jnp.mean(y, axis=-1, keepdims=True)
    d = y - mu
    var = jnp.mean(d * d, axis=-1, keepdims=True)
    return d * lax.rsqrt(var + LN_EPS) * g + b


def _sigmoid(x):
    return 1.0 / (1.0 + jnp.exp(-x))


def _lane(shape):
    return lax.broadcasted_iota(jnp.int32, shape, 1)


def _heads_to_rows(parts):
    out = [jnp.transpose(jnp.concatenate([a, b], axis=0)) for a, b in zip(parts[0::2], parts[1::2])]
    return out[0] if len(out) == 1 else jnp.concatenate(out, axis=1)


def _proj_kernel(x_ref, w_ref, o_ref, g_ref, t_ref):
    acc = _dot(x_ref[...].astype(BF16), w_ref[...])
    o_ref[...] = acc[:, :N_MAIN].astype(BF16)
    g_ref[...] = acc[:, N_MAIN:]
    t_ref[...] = jnp.transpose(acc[:, KVB_BLK * LANES:N_MAIN]).astype(BF16)


def _project(x2d, w_ext):
    n = x2d.shape[0]
    n_kv = N_MAIN - KVB_BLK * LANES
    return pl.pallas_call(
        _proj_kernel,
        grid=(n // TM,),
        in_specs=[pl.BlockSpec((TM, D_MODEL), lambda i: (i, 0)),
                  pl.BlockSpec((D_MODEL, N_MAIN + LANES), lambda i: (0, 0))],
        out_specs=[pl.BlockSpec((TM, N_MAIN), lambda i: (i, 0)),
                   pl.BlockSpec((TM, LANES), lambda i: (i, 0)),
                   pl.BlockSpec((n_kv, TM), lambda i: (0, i))],
        out_shape=[jax.ShapeDtypeStruct((n, N_MAIN), BF16),
                   jax.ShapeDtypeStruct((n, LANES), F32),
                   jax.ShapeDtypeStruct((n_kv, n), BF16)],
        compiler_params=_cparams(("parallel",)),
        name="in_proj",
    )(x2d, w_ext)


def _build_w_in(w):
    d = w.shape[0]
    o = np.cumsum([0, 256, 384, 12, 256, 256, 256, 512, 128, 128])
    q_a, kv_a, g_a, q_b, k_b, v_b, q_c, k_c, v_c = [w[:, o[i]:o[i + 1]] for i in range(9)]
    z = jnp.zeros((d, HEAD_DIM), F32)
    hd = HEAD_DIM
    cols = []
    for h in range(A_HEADS):
        cols += [q_a[:, h * hd:(h + 1) * hd] * (LOG2E * hd ** -0.5), z]
    for h in range(B_HEADS):
        cols += [q_b[:, h * hd:(h + 1) * hd] * (LOG2E * DIFF_DK ** -0.5), z]
    for h in range(C_HEADS):
        cols += [q_c[:, h * hd:(h + 1) * hd] * (LOG2E * hd ** -0.5), z]
    for h in range(B_HEADS):
        cols += [k_b[:, h * hd:(h + 1) * hd], v_b[:, h * hd:(h + 1) * hd]]
    for g in range(C_KV_HEADS):
        cols += [k_c[:, g * hd:(g + 1) * hd], v_c[:, g * hd:(g + 1) * hd]]
    cols.append(kv_a)
    cols += [g_a, jnp.zeros((d, LANES - 12), F32)]
    return jnp.concatenate(cols, axis=1).astype(BF16)


def _compress_kernel(ch_ref, wc_ref, pos_ref, w1_ref, w2_ref, o_ref, ot_ref):
    nck = ch_ref.shape[1]
    u = _dot(ch_ref[0], wc_ref[...])
    hs = []
    for j in range(2):
        pw = _dot(pos_ref[j].astype(BF16), w1_ref[j])[0:1]
        ua = u[:, (2 * j) * LANES:(2 * j + 1) * LANES]
        ub = u[:, (2 * j + 1) * LANES:(2 * j + 2) * LANES]
        pre = ua + pltpu.roll(ub, nck - 1, 0) + pw
        hs.append(0.5 * pre * (1.0 + jnp.tanh(0.7978845608028654 * (pre + 0.044715 * pre * pre * pre))))
    h = jnp.concatenate(hs, axis=1).astype(BF16)
    kvc = _dot(h, w2_ref[...])
    o_ref[0] = kvc.astype(BF16)
    ot_ref[0] = jnp.transpose(kvc).astype(BF16)


def _compress(proj3, cmp_pos, cmp_w1, cmp_w2):
    b, t, _ = proj3.shape
    nck = t // CMP_STRIDE
    kv = proj3[:, :, KVA_BLK * LANES:(KVA_BLK + 1) * LANES]
    chunks = kv.reshape(b, nck, CMP_STRIDE * LANES)
    w1 = cmp_w1.reshape(2, 2, CMP_STRIDE, HEAD_DIM, CMP_HIDDEN)
    z = jnp.zeros((CMP_STRIDE, HEAD_DIM, CMP_HIDDEN), F32)
    blocks = []
    for j in range(2):
        for half in range(2):
            pair = [w1[j, half], z] if j == 0 else [z, w1[j, half]]
            blocks.append(jnp.concatenate(pair, axis=1).reshape(CMP_STRIDE * LANES, CMP_HIDDEN))
    wc = jnp.concatenate(blocks, axis=1).astype(BF16)
    pos = jnp.zeros((2, 8, CMP_LEN * HEAD_DIM), F32).at[:, 0].set(cmp_pos.reshape(2, -1))
    z2 = jnp.zeros((CMP_HIDDEN, HEAD_DIM), F32)
    w2 = jnp.concatenate([jnp.concatenate([cmp_w2[0], z2], axis=1),
                          jnp.concatenate([z2, cmp_w2[1]], axis=1)], axis=0).astype(BF16)
    return pl.pallas_call(
        _compress_kernel,
        grid=(b,),
        in_specs=[pl.BlockSpec((1, nck, CMP_STRIDE * LANES), lambda i: (i, 0, 0)),
                  pl.BlockSpec(wc.shape, lambda i: (0, 0)),
                  pl.BlockSpec(pos.shape, lambda i: (0, 0, 0)),
                  pl.BlockSpec((2, CMP_LEN * HEAD_DIM, CMP_HIDDEN), lambda i: (0, 0, 0)),
                  pl.BlockSpec(w2.shape, lambda i: (0, 0))],
        out_specs=[pl.BlockSpec((1, nck, LANES), lambda i: (i, 0, 0)),
                   pl.BlockSpec((1, LANES, nck), lambda i: (i, 0, 0))],
        out_shape=[jax.ShapeDtypeStruct((b, nck, LANES), BF16),
                   jax.ShapeDtypeStruct((b, LANES, nck), BF16)],
        compiler_params=_cparams(("parallel",)),
        name="nsa_compress",
    )(chunks, wc, pos, cmp_w1.astype(BF16), w2)


def _cmp_attn_kernel_t(q_ref, kvc_ref, kvct_ref, ovt_ref, bias_ref, o_ref, sel_ref):
    tq = q_ref.shape[1]
    ncp = kvc_ref.shape[1]
    rows = A_HEADS * tq
    t0 = pl.program_id(1) * tq
    qs = _stack_heads(q_ref[0], A_HEADS)
    u = _dot_nt(kvc_ref[0], qs) + bias_ref[...]
    n = lax.broadcasted_iota(jnp.int32, (ncp, rows), 0)
    block_end = jnp.where(n == ncp - 1, jnp.int32(1 << 30), n * CMP_STRIDE + (CMP_LEN - 1))
    keep = block_end <= jnp.bitwise_and(_lane((ncp, rows)), tq - 1) + t0
    sh = jnp.where(keep, u, NEG_INF)
    m = jnp.max(sh, axis=0, keepdims=True)
    e = jnp.where(keep, jnp.exp2(sh - m), 0.0)
    p = e / jnp.maximum(jnp.sum(e, axis=0, keepdims=True), 1e-30)
    o_t = _dot(kvct_ref[0, HEAD_DIM:, :], p.astype(BF16))
    o_ref[0] = _heads_to_rows([o_t[:, h * tq:(h + 1) * tq] for h in range(A_HEADS)]).astype(BF16)

    psum = p[:, 0:tq] + p[:, tq:2 * tq] + p[:, 2 * tq:3 * tq] + p[:, 3 * tq:4 * tq]
    hi = psum.astype(BF16)
    r1 = psum - hi.astype(F32)
    mid = r1.astype(BF16)
    lo = (r1 - mid.astype(F32)).astype(BF16)
    ovt = ovt_ref[...]
    imp = _dot(ovt, hi) + _dot(ovt, mid) + _dot(ovt, lo)
    nsl = ncp * CMP_STRIDE // SLC_BLOCK
    j = lax.broadcasted_iota(jnp.int32, (LANES, tq), 0)
    t_blk = jnp.right_shift(_lane((LANES, tq)) + t0, SLC_SHIFT)
    forced = (j == 0) | (j == t_blk) | (j == t_blk - 1)
    imp = jnp.where(forced, FORCE_SCORE, jnp.where(j > t_blk, -1.0, imp))
    work = jnp.where(j < nsl, imp, -3.0)
    jf = j.astype(F32)
    sel = jnp.zeros((LANES, tq), F32)
    for _ in range(SLC_TOPN):
        mx = jnp.max(work, axis=0, keepdims=True)
        first = jnp.min(jnp.where(work == mx, jf, float(LANES)), axis=0, keepdims=True)
        pick = jf == first
        sel = jnp.where(pick, 1.0, sel)
        work = jnp.where(pick, -4.0, work)
    sel_ref[0] = jnp.transpose(sel).astype(BF16)


def _cmp_attn(proj3, kvc, kvc_t, slopes):
    b, t, _ = proj3.shape
    ncp = kvc.shape[1]
    nsl = t // SLC_BLOCK
    c0 = np.arange(ncp)[None, :] * CMP_STRIDE
    s0 = np.arange(LANES)[:, None] * SLC_BLOCK
    ovt = np.clip(np.minimum(c0 + CMP_LEN, s0 + SLC_BLOCK) - np.maximum(c0, s0), 0, None) / CMP_LEN
    ovt[:, ncp - 1] = 0.0
    ovt[nsl:, :] = 0.0
    rel = np.arange(TQ)[None, :] - (np.arange(ncp)[:, None] * CMP_STRIDE + (CMP_LEN - 1))
    bias = np.concatenate([-s * rel for s in slopes], axis=1).astype(np.float32)
    return pl.pallas_call(
        _cmp_attn_kernel_t,
        grid=(b, t // TQ),
        in_specs=[pl.BlockSpec((1, TQ, A_HEADS * LANES), lambda bi, i: (bi, i, QA_BLK // A_HEADS)),
                  pl.BlockSpec((1, ncp, LANES), lambda bi, i: (bi, 0, 0)),
                  pl.BlockSpec((1, LANES, ncp), lambda bi, i: (bi, 0, 0)),
                  pl.BlockSpec((LANES, ncp), lambda bi, i: (0, 0)),
                  pl.BlockSpec((ncp, A_HEADS * TQ), lambda bi, i: (0, 0))],
        out_specs=[pl.BlockSpec((1, TQ, A_HEADS * HEAD_DIM), lambda bi, i: (bi, i, 0)),
                   pl.BlockSpec((1, TQ, LANES), lambda bi, i: (bi, i, 0))],
        out_shape=[jax.ShapeDtypeStruct((b, t, A_HEADS * HEAD_DIM), BF16),
                   jax.ShapeDtypeStruct((b, t, LANES), BF16)],
        compiler_params=_cparams(("parallel", "parallel")),
        name="nsa_cmp_attn",
    )(proj3, kvc, kvc_t, jnp.asarray(ovt, BF16), jnp.asarray(bias))


def _stack_heads(q, n):
    return jnp.concatenate([q[:, h * LANES:(h + 1) * LANES] for h in range(n)], axis=0)


def _window_bias(slopes, tq, window):
    wk = window + tq
    n_var = -(-window // tq) + 1
    j = np.arange(wk)[:, None]
    q = np.arange(tq)[None, :]
    out = np.empty((len(slopes), n_var, wk, len(slopes[0]) * tq), np.float32)
    for v in range(n_var):
        d = v * tq - max(v * tq - window, 0) + q - j
        keep = (d >= 0) & (d < window)
        for g, row in enumerate(slopes):
            for r, slope in enumerate(row):
                out[g, v, :, r * tq:(r + 1) * tq] = np.where(keep, -slope * d, NEG_INF)
    return jnp.asarray(out)


def _window_kernel_t(*refs, r_heads, window, has_sink):
    if has_sink:
        sink_ref, q_ref, kv_ref, kvt_ref, bias_ref, o_ref = refs
    else:
        q_ref, kv_ref, kvt_ref, bias_ref, o_ref = refs
    g, i = pl.program_id(1), pl.program_id(2)
    tq = q_ref.shape[1]
    wk = window + tq
    start = pl.multiple_of(jnp.maximum(i * tq - window, 0), LANES)
    qs = _stack_heads(q_ref[0], r_heads)
    u = _dot_nt(kv_ref[0, pl.ds(start, wk), :], qs) + bias_ref[0, 0]
    m = jnp.max(u, axis=0, keepdims=True)
    if has_sink:
        sink = jnp.concatenate([jnp.full((1, tq), sink_ref[g * r_heads + r], F32) for r in range(r_heads)], axis=1)
        m = jnp.maximum(m, sink)
    p = jnp.exp2(u - m)
    den = jnp.sum(p, axis=0, keepdims=True)
    if has_sink:
        den = den + jnp.exp2(sink - m)
    o_t = _dot(kvt_ref[HEAD_DIM:, pl.ds(start, wk)], p.astype(BF16)) / den
    o_ref[0] = _heads_to_rows([o_t[:, r * tq:(r + 1) * tq] for r in range(r_heads)]).astype(BF16)


def _window_attn_t(proj3, kv_t, slopes, *, q_blk, kv_blk, window, sinks=None, name):
    b, t, _ = proj3.shape
    groups, r_heads = len(slopes), len(slopes[0])
    bias = _window_bias(slopes, TQ, window)
    n_var, wk = bias.shape[1], bias.shape[2]
    in_specs = [pl.BlockSpec((1, TQ, r_heads * LANES), lambda bi, g, i: (bi, i, q_blk // r_heads + g)),
                pl.BlockSpec((1, t, LANES), lambda bi, g, i: (bi, 0, kv_blk + g)),
                pl.BlockSpec((LANES, t), lambda bi, g, i: (kv_blk - KVB_BLK + g, bi)),
                pl.BlockSpec((1, 1, wk, r_heads * TQ), lambda bi, g, i: (g, jnp.minimum(i, n_var - 1), 0, 0))]
    args = [proj3, proj3, kv_t, bias]
    if sinks is not None:
        in_specs.insert(0, pl.BlockSpec(memory_space=pltpu.SMEM))
        args.insert(0, sinks.astype(F32) * LOG2E)
    return pl.pallas_call(
        functools.partial(_window_kernel_t, r_heads=r_heads, window=window, has_sink=sinks is not None),
        grid=(b, groups, t // TQ),
        in_specs=in_specs,
        out_specs=pl.BlockSpec((1, TQ, r_heads * HEAD_DIM), lambda bi, g, i: (bi, i, g)),
        out_shape=jax.ShapeDtypeStruct((b, t, groups * r_heads * HEAD_DIM), BF16),
        compiler_params=_cparams(("parallel", "parallel", "parallel")),
        name=name,
    )(*args)


def _flash_bias(stream_slopes, tq, tk):
    ratio = tk // tq
    j = np.arange(tk)[:, None]
    q = np.arange(tq)[None, :]
    out = np.empty((len(stream_slopes), 1 + ratio, tk, len(stream_slopes[0]) * tq), np.float32)
    for s, slopes in enumerate(stream_slopes):
        for h, slope in enumerate(slopes):
            cols = slice(h * tq, (h + 1) * tq)
            out[s, 0, :, cols] = slope * j
            for r in range(ratio):
                out[s, 1 + r, :, cols] = np.where(j <= r * tq + q, slope * j, NEG_INF)
    return jnp.asarray(out)


def _flash_t(streams, kv_ref, kvt_ref, i, *, tq, tk, sel_t=None):
    ratio = tk // tq
    n_full = lax.div(i, jnp.int32(ratio))
    rows = streams[0][0].shape[0]

    def tile(kt, carry, edge):
        k0 = pl.multiple_of(kt * tk, tk)
        k0f = (kt * tk).astype(F32)
        variant = 1 + i - kt * ratio if edge else 0
        keep = None
        if sel_t is not None:
            blk = jnp.right_shift(lax.broadcasted_iota(jnp.int32, (tk, LANES), 0) + kt * tk, SLC_SHIFT)
            expand = jnp.where(_lane((tk, LANES)) == blk, 1.0, 0.0).astype(BF16)
            keep = _dot(expand, sel_t) > 0.5
        scores = [_dot_nt(kv_ref[0, pl.ds(k0, tk), s[1]:s[1] + LANES], s[0]) for s in streams]
        probs, stats = [], []
        for (_, _, slope_lane, bias), (m, l, _), u in zip(streams, carry, scores):
            u = u + bias(variant)
            if keep is not None:
                u = jnp.where(keep, u, NEG_INF)
            shift = slope_lane * k0f
            m_new = jnp.maximum(m, jnp.max(u, axis=0, keepdims=True) + shift)
            p = jnp.exp2(u - (m_new - shift))
            alpha = jnp.exp2(m - m_new)
            probs.append(p.astype(BF16))
            stats.append((m_new, alpha, alpha * l + jnp.sum(p, axis=0, keepdims=True)))
        out = []
        for (_, col, _, _), (_, _, acc), p, (m_new, alpha, l) in zip(streams, carry, probs, stats):
            kv_f = kvt_ref[col + HEAD_DIM:col + LANES, pl.ds(k0, tk)]
            out.append((m_new, l, alpha * acc + _dot(kv_f, p)))
        return tuple(out)

    init = tuple((jnp.full((1, rows), NEG_INF, F32), jnp.zeros((1, rows), F32), jnp.zeros((HEAD_DIM, rows), F32))
                 for _ in streams)
    carry = lax.fori_loop(0, n_full, lambda kt, c: tile(kt, c, False), init)
    carry = tile(n_full, carry, True)
    return [(l, acc) for _, l, acc in carry]


def _selected_kernel_t(q_ref, kv_ref, kvt_ref, sel_ref, bias_ref, o_ref, *, slopes):
    tq = q_ref.shape[1]
    i = pl.program_id(1)
    sel_t = jnp.transpose(sel_ref[0].astype(F32)).astype(BF16)
    sel_t = jnp.concatenate([sel_t, sel_t], axis=1)
    streams = []
    for pair in range(A_HEADS // 2):
        qs = jnp.concatenate([q_ref[0, :, h * LANES:(h + 1) * LANES] for h in (2 * pair, 2 * pair + 1)], axis=0)
        slope_lane = jnp.concatenate([jnp.full((1, tq), slopes[h], F32) for h in (2 * pair, 2 * pair + 1)], axis=1)
        streams.append((qs, 0, slope_lane, functools.partial(lambda s, v: bias_ref[s, v], pair)))
    outs = []
    for l, acc in _flash_t(streams, kv_ref, kvt_ref, i, tq=tq, tk=TK_DENSE, sel_t=sel_t):
        o_t = acc / l
        outs += [o_t[:, :tq], o_t[:, tq:]]
    o_ref[0] = _heads_to_rows(outs).astype(BF16)


def _selected_attn_t(proj3, kv_t, sel, slopes):
    b, t, _ = proj3.shape
    kv_blk = KVA_BLK + 1
    bias = _flash_bias([slopes[0:2], slopes[2:4]], TQ_DENSE, TK_DENSE)
    return pl.pallas_call(
        functools.partial(_selected_kernel_t, slopes=slopes),
        grid=(b, t // TQ_DENSE),
        in_specs=[pl.BlockSpec((1, TQ_DENSE, A_HEADS * LANES), lambda bi, i: (bi, i, QA_BLK // A_HEADS)),
                  pl.BlockSpec((1, t, LANES), lambda bi, i: (bi, 0, kv_blk)),
                  pl.BlockSpec((LANES, t), lambda bi, i: (kv_blk - KVB_BLK, bi)),
                  pl.BlockSpec((1, TQ_DENSE, LANES), lambda bi, i: (bi, i, 0)),
                  pl.BlockSpec(bias.shape, lambda bi, i: (0, 0, 0, 0), pipeline_mode=pl.Buffered(1))],
        out_specs=pl.BlockSpec((1, TQ_DENSE, A_HEADS * HEAD_DIM), lambda bi, i: (bi, i, 0)),
        out_shape=jax.ShapeDtypeStruct((b, t, A_HEADS * HEAD_DIM), BF16),
        compiler_params=_cparams(("parallel", "parallel")),
        name="nsa_selected",
    )(proj3, proj3, kv_t, sel, bias)


def _diff_kernel_t(slope_ref, dl_ref, g_ref, q_ref, kv_ref, kvt_ref, bias_ref, o_ref, *, lam_init):
    tq = q_ref.shape[1]
    hp = pl.program_id(1)
    i = pl.program_id(2)
    dl = dl_ref[...]
    lam = (jnp.exp(jnp.sum(dl[0:1] * dl[1:2], axis=-1, keepdims=True))
           - jnp.exp(jnp.sum(dl[2:3] * dl[3:4], axis=-1, keepdims=True)) + lam_init)
    lane = _lane((tq, LANES))
    streams = []
    for hh in range(2):
        q = q_ref[0, :, hh * LANES:(hh + 1) * LANES]
        zero = jnp.zeros_like(q)
        qs = jnp.concatenate([jnp.where(lane < DIFF_DK, q, zero),
                              jnp.where(lane >= DIFF_DK, q, zero)], axis=0)
        slope_lane = jnp.full((1, 2 * tq), slope_ref[hp * 2 + hh], F32)
        streams.append((qs, hh * LANES, slope_lane, functools.partial(lambda s, v: bias_ref[s, v], hh)))
    outs = []
    for l, acc in _flash_t(streams, kv_ref, kvt_ref, i, tq=tq, tk=TK_DENSE):
        o = acc / l
        w = o[:, :tq] - lam * o[:, tq:]
        ms = jnp.sum(w * w, axis=0, keepdims=True) * (1.0 / HEAD_DIM)
        outs.append(w * lax.rsqrt(ms + LN_EPS) * (1.0 - lam_init))
    o_ref[0] = (_heads_to_rows(outs) * g_ref[...]).astype(BF16)


def _diff_attn_t(proj3, kv_t, slopes, diff_lambda, subln, lam_init):
    b, t, _ = proj3.shape
    g_ext = jnp.tile(subln.reshape(1, HEAD_DIM).astype(F32), (1, 2))
    bias = _flash_bias([[s, s] for s in slopes], TQ_DENSE, TK_DENSE)
    return pl.pallas_call(
        functools.partial(_diff_kernel_t, lam_init=lam_init),
        grid=(b, B_HEADS // 2, t // TQ_DENSE),
        in_specs=[pl.BlockSpec(memory_space=pltpu.SMEM),
                  pl.BlockSpec((4, DIFF_DK), lambda bi, h, i: (0, 0)),
                  pl.BlockSpec((1, LANES), lambda bi, h, i: (0, 0)),
                  pl.BlockSpec((1, TQ_DENSE, 2 * LANES), lambda bi, h, i: (bi, i, QB_BLK // 2 + h)),
                  pl.BlockSpec((1, t, 2 * LANES), lambda bi, h, i: (bi, 0, KVB_BLK // 2 + h)),
                  pl.BlockSpec((2 * LANES, t), lambda bi, h, i: (h, bi)),
                  pl.BlockSpec((2,) + bias.shape[1:], lambda bi, h, i: (h, 0, 0, 0))],
        out_specs=pl.BlockSpec((1, TQ_DENSE, 2 * HEAD_DIM), lambda bi, h, i: (bi, i, h)),
        out_shape=jax.ShapeDtypeStruct((b, t, B_HEADS * HEAD_DIM), BF16),
        compiler_params=_cparams(("parallel", "parallel", "parallel")),
        name="diff_attn",
    )(jnp.asarray(slopes, F32), diff_lambda.astype(F32), g_ext, proj3, proj3, kv_t, bias)


def _outproj_kernel(oc_ref, os_ref, ow_ref, gl_ref, ob_ref, osw_ref, x_ref, w_ref, ex_ref, g_ref, b_ref, o_ref):
    hi, lo = _split2(_sigmoid(gl_ref[...]))
    oa = None
    for j, ref in enumerate((oc_ref, os_ref, ow_ref)):
        gate = _dot(hi, ex_ref[j]) + _dot(lo, ex_ref[j])
        term = gate * ref[...].astype(F32)
        oa = term if oa is None else oa + term
    cat = jnp.concatenate([oa.astype(BF16), ob_ref[...], osw_ref[...]], axis=1)
    y = DN_ALPHA * x_ref[...] + _dot(cat, w_ref[...])
    o_ref[...] = _layer_norm(y, g_ref[...], b_ref[...])


def _outproj(o_c, o_s, o_w, gl, o_b, o_sw, x2d, w_out, ln_g, ln_b):
    n = x2d.shape[0]
    ex = np.zeros((3, LANES, A_HEADS * HEAD_DIM), np.float32)
    for j in range(3):
        for h in range(A_HEADS):
            ex[j, h * 3 + j, h * HEAD_DIM:(h + 1) * HEAD_DIM] = 1.0
    row = lambda w: pl.BlockSpec((TM, w), lambda i: (i, 0))
    full = lambda s: pl.BlockSpec(s, lambda i: (0,) * len(s))
    return pl.pallas_call(
        _outproj_kernel,
        grid=(n // TM,),
        in_specs=[row(256), row(256), row(256), row(LANES), row(256), row(512), row(D_MODEL),
                  full((D_MODEL, D_MODEL)), full(ex.shape), full((1, D_MODEL)), full((1, D_MODEL))],
        out_specs=row(D_MODEL),
        out_shape=jax.ShapeDtypeStruct((n, D_MODEL), F32),
        compiler_params=_cparams(("parallel",)),
        name="out_proj_ln",
    )(o_c, o_s, o_w, gl, o_b, o_sw, x2d, w_out.astype(BF16), jnp.asarray(ex, BF16),
      ln_g.reshape(1, -1), ln_b.reshape(1, -1))


def _ffn_kernel(x_ref, wg_ref, wu_ref, wd_ref, o_ref, xb_ref):
    c = pl.program_id(1)

    @pl.when(c == 0)
    def _():
        xb_ref[...] = x_ref[...].astype(BF16)

    xb = xb_ref[...]
    g = _dot(xb, wg_ref[...])
    u = _dot(xb, wu_ref[...])
    y = _dot((g * _sigmoid(g) * u).astype(BF16), wd_ref[...])

    @pl.when(c == 0)
    def _():
        o_ref[...] = y

    @pl.when(c > 0)
    def _():
        o_ref[...] += y


def _ffn(x2d, wg, wu, wd):
    n = x2d.shape[0]
    nc = D_FF_DENSE // FFN_CHUNK
    return pl.pallas_call(
        _ffn_kernel,
        grid=(n // TM, nc),
        in_specs=[pl.BlockSpec((TM, D_MODEL), lambda i, c: (i, 0)),
                  pl.BlockSpec((D_MODEL, FFN_CHUNK), lambda i, c: (0, c)),
                  pl.BlockSpec((D_MODEL, FFN_CHUNK), lambda i, c: (0, c)),
                  pl.BlockSpec((FFN_CHUNK, D_MODEL), lambda i, c: (c, 0))],
        out_specs=pl.BlockSpec((TM, D_MODEL), lambda i, c: (i, 0)),
        out_shape=jax.ShapeDtypeStruct((n, D_MODEL), F32),
        scratch_shapes=[pltpu.VMEM((TM, D_MODEL), BF16)],
        compiler_params=_cparams(("parallel", "arbitrary")),
        name="dense_swiglu",
    )(x2d, wg.astype(BF16), wu.astype(BF16), wd.astype(BF16))


def _router_kernel(x_ref, w_ref, tri_ref, xb_ref, gd_ref, scol_ref, srow_ref, base_ref, cnt_ref):
    tt = x_ref.shape[0]
    x = x_ref[...]
    xb_ref[...] = x.astype(BF16)
    xh, xl = _split2(x)
    wh, wl = _split2(w_ref[...])
    logits = _dot(xh, wh) + _dot(xh, wl) + _dot(xl, wh)
    lane = _lane((tt, LANES))
    lf = lane.astype(F32)
    logits = jnp.where(lane < N_EXPERTS, logits, -jnp.inf)
    m1 = jnp.max(logits, axis=-1, keepdims=True)
    i1 = jnp.min(jnp.where(logits == m1, lf, float(LANES)), axis=-1, keepdims=True)
    rest = jnp.where(lf == i1, -jnp.inf, logits)
    m2 = jnp.max(rest, axis=-1, keepdims=True)
    i2 = jnp.min(jnp.where(rest == m2, lf, float(LANES)), axis=-1, keepdims=True)
    e2 = jnp.exp(m2 - m1)
    g1 = 1.0 / (1.0 + e2)
    g2 = e2 / (1.0 + e2)
    first, second = lf == i1, lf == i2
    gd_ref[...] = jnp.where(first, g1, 0.0) + jnp.where(second, g2, 0.0)
    ind = jnp.where(first, 1.0, jnp.where(second, 1.0, 0.0))
    tri = tri_ref[...]
    run = jnp.zeros((1, LANES), F32)
    slots, bases = [], []
    for tb in range(tt // MOE_TB):
        blk = ind[tb * MOE_TB:(tb + 1) * MOE_TB]
        rank = _dot(tri, blk.astype(BF16)) + run
        slots.append(jnp.where(blk > 0.5, rank, -1.0))
        bases.append(run)
        run = run + jnp.sum(blk, axis=0, keepdims=True)
    slot = jnp.concatenate(slots, axis=0)
    scol_ref[...] = slot
    srow_ref[...] = jnp.transpose(slot)[0:N_EXPERTS]
    base_ref[0] = jnp.concatenate(bases, axis=0)
    cnt_ref[0] = jnp.broadcast_to(run, (8, LANES))


def _route(x2d, w_router):
    n = x2d.shape[0]
    nt = n // MOE_TILE
    ntb = MOE_TILE // MOE_TB
    w = jnp.zeros((D_MODEL, LANES), F32).at[:, :N_EXPERTS].set(w_router)
    tri = jnp.asarray(np.tril(np.ones((MOE_TB, MOE_TB), np.float32), -1), BF16)
    return pl.pallas_call(
        _router_kernel,
        grid=(nt,),
        in_specs=[pl.BlockSpec((MOE_TILE, D_MODEL), lambda i: (i, 0)),
                  pl.BlockSpec((D_MODEL, LANES), lambda i: (0, 0)),
                  pl.BlockSpec((MOE_TB, MOE_TB), lambda i: (0, 0))],
        out_specs=[pl.BlockSpec((MOE_TILE, D_MODEL), lambda i: (i, 0)),
                   pl.BlockSpec((MOE_TILE, LANES), lambda i: (i, 0)),
                   pl.BlockSpec((MOE_TILE, LANES), lambda i: (i, 0)),
                   pl.BlockSpec((N_EXPERTS, MOE_TILE), lambda i: (0, i)),
                   pl.BlockSpec((1, ntb, LANES), lambda i: (i, 0, 0)),
                   pl.BlockSpec((1, 8, LANES), lambda i: (i, 0, 0))],
        out_shape=[jax.ShapeDtypeStruct((n, D_MODEL), BF16),
                   jax.ShapeDtypeStruct((n, LANES), F32),
                   jax.ShapeDtypeStruct((n, LANES), F32),
                   jax.ShapeDtypeStruct((N_EXPERTS, n), F32),
                   jax.ShapeDtypeStruct((nt, ntb, LANES), F32),
                   jax.ShapeDtypeStruct((nt, 8, LANES), F32)],
        compiler_params=_cparams(("parallel",)),
        name="moe_router",
    )(x2d, w, tri)


MOE_GATHER_W = MOE_TB + 16
MOE_SCATTER_W = MOE_TB + 8
MOE_CAP = MOE_TILE + 512


def _moe_kernel(cnt_ref, base_ref, xb_ref, srow_ref, scol_ref, gd_ref, wg_ref, wu_ref, wd_ref,
                o_ref, xc_ref, yc_ref):
    i, e, c = pl.program_id(0), pl.program_id(1), pl.program_id(2)
    nc = pl.num_programs(2)
    ntb = MOE_TILE // MOE_TB
    count = cnt_ref[i * N_EXPERTS + e]
    n_full = jnp.right_shift(count * MOE_ROWS_RECIP[0], MOE_ROWS_RECIP[1])
    rem = count - n_full * MOE_ROWS

    @pl.when((i == 0) & (e == 0) & (c == 0))
    def _():
        yc_ref[...] = jnp.zeros_like(yc_ref)

    @pl.when((e == 0) & (c == 0))
    def _():
        o_ref[...] = jnp.zeros_like(o_ref)

    @pl.when(c == 0)
    def _():
        xc_ref[...] = jnp.zeros_like(xc_ref)
        srow = srow_ref[pl.ds(e, 1), :]
        rid = lax.broadcasted_iota(jnp.int32, (MOE_GATHER_W, MOE_TB), 0).astype(F32)
        for tb in range(ntb):
            b0 = base_ref[(i * ntb + tb) * N_EXPERTS + e]
            b_al = pl.multiple_of(jnp.left_shift(jnp.right_shift(b0, 4), 4), 16)
            rel = srow[:, tb * MOE_TB:(tb + 1) * MOE_TB] - b_al.astype(F32)
            onehot = jnp.where(rel == rid, 1.0, 0.0).astype(BF16)
            rows = _dot(onehot, xb_ref[tb * MOE_TB:(tb + 1) * MOE_TB, :])
            cur = xc_ref[pl.ds(b_al, MOE_GATHER_W), :].astype(F32)
            xc_ref[pl.ds(b_al, MOE_GATHER_W), :] = (cur + rows).astype(BF16)

    def expert_rows(r0, n_rows):
        xr = xc_ref[pl.ds(r0, n_rows), :]
        g = _dot(xr, wg_ref[0])
        u = _dot(xr, wu_ref[0])
        y = _dot((g * _sigmoid(g) * u).astype(BF16), wd_ref[0])
        prev = yc_ref[pl.ds(r0, n_rows), :]
        yc_ref[pl.ds(r0, n_rows), :] = jnp.where(c == 0, y, prev + y)

    def rows_body(rb, carry):
        expert_rows(pl.multiple_of(rb * MOE_ROWS, MOE_ROW_ALIGN), MOE_ROWS)
        return carry

    lax.fori_loop(0, n_full + jnp.where(rem > MOE_ROWS // 2, 1, 0), rows_body, 0)

    @pl.when((rem > 0) & (rem <= MOE_ROWS // 2))
    def _():
        expert_rows(pl.multiple_of(n_full * MOE_ROWS, MOE_ROW_ALIGN), MOE_ROWS // 2)

    @pl.when(c == nc - 1)
    def _():
        lane = _lane((MOE_TB, LANES))
        cid = lax.broadcasted_iota(jnp.int32, (MOE_TB, MOE_SCATTER_W), 1).astype(F32)
        for tb in range(ntb):
            b0 = base_ref[(i * ntb + tb) * N_EXPERTS + e]
            b_al = pl.multiple_of(jnp.left_shift(jnp.right_shift(b0, 3), 3), 8)
            rows = slice(tb * MOE_TB, (tb + 1) * MOE_TB)
            slot = jnp.sum(jnp.where(lane == e, scol_ref[rows, :], 0.0), axis=-1, keepdims=True)
            gate = jnp.sum(jnp.where(lane == e, gd_ref[rows, :], 0.0), axis=-1, keepdims=True)
            onehot = jnp.where(slot - b_al.astype(F32) == cid, 1.0, 0.0).astype(BF16)
            y_win = yc_ref[pl.ds(b_al, MOE_SCATTER_W), :].astype(BF16)
            o_ref[rows, :] += gate * _dot(onehot, y_win)


def _moe(xb, srow, scol, gd, cnt, base, wg, wu, wd):
    n = xb.shape[0]
    nt = n // MOE_TILE
    nc = D_FF_EXPERT // MOE_CHUNK
    grid_spec = pltpu.PrefetchScalarGridSpec(
        num_scalar_prefetch=2,
        grid=(nt, N_EXPERTS, nc),
        in_specs=[pl.BlockSpec((MOE_TILE, D_MODEL), lambda i, e, c, *_: (i, 0), pipeline_mode=pl.Buffered(1)),
                  pl.BlockSpec((N_EXPERTS, MOE_TILE), lambda i, e, c, *_: (0, i)),
                  pl.BlockSpec((MOE_TILE, LANES), lambda i, e, c, *_: (i, 0)),
                  pl.BlockSpec((MOE_TILE, LANES), lambda i, e, c, *_: (i, 0)),
                  pl.BlockSpec((1, D_MODEL, MOE_CHUNK), lambda i, e, c, *_: (e, 0, c)),
                  pl.BlockSpec((1, D_MODEL, MOE_CHUNK), lambda i, e, c, *_: (e, 0, c)),
                  pl.BlockSpec((1, MOE_CHUNK, D_MODEL), lambda i, e, c, *_: (e, c, 0))],
        out_specs=pl.BlockSpec((MOE_TILE, D_MODEL), lambda i, e, c, *_: (i, 0), pipeline_mode=pl.Buffered(1)),
        scratch_shapes=[pltpu.VMEM((MOE_CAP, D_MODEL), BF16), pltpu.VMEM((MOE_CAP, D_MODEL), F32)],
    )
    return pl.pallas_call(
        _moe_kernel,
        grid_spec=grid_spec,
        out_shape=jax.ShapeDtypeStruct((n, D_MODEL), F32),
        compiler_params=_cparams(("arbitrary", "arbitrary", "arbitrary")),
        name="moe_experts",
    )(cnt, base, xb, srow, scol, gd, wg.astype(BF16), wu.astype(BF16), wd.astype(BF16))


def _ple_kernel(x_ref, f_ref, p_ref, wg_ref, wp_ref, g2_ref, b2_ref, g3_ref, b3_ref, o_ref):
    x2 = _layer_norm(DN_ALPHA * x_ref[...] + f_ref[...], g2_ref[...], b2_ref[...])
    e = _sigmoid(_dot(x2.astype(BF16), wg_ref[...])) * _dot(p_ref[...].astype(BF16), wp_ref[...])
    o_ref[...] = _layer_norm(DN_ALPHA * x2 + e, g3_ref[...], b3_ref[...])


def _ple(x2d, f2d, p2d, w_gate, w_proj, g2, b2, g3, b3):
    n = x2d.shape[0]
    row = lambda w: pl.BlockSpec((TM, w), lambda i: (i, 0))
    full = lambda s: pl.BlockSpec(s, lambda i: (0,) * len(s))
    vec = full((1, D_MODEL))
    return pl.pallas_call(
        _ple_kernel,
        grid=(n // TM,),
        in_specs=[row(D_MODEL), row(D_MODEL), row(PLE_DIM), full((D_MODEL, D_MODEL)), full((PLE_DIM, D_MODEL)),
                  vec, vec, vec, vec],
        out_specs=row(D_MODEL),
        out_shape=jax.ShapeDtypeStruct((n, D_MODEL), F32),
        compiler_params=_cparams(("parallel",)),
        name="ln_ple_ln",
    )(x2d, f2d, p2d, w_gate.astype(BF16), w_proj.astype(BF16),
      g2.reshape(1, -1), b2.reshape(1, -1), g3.reshape(1, -1), b3.reshape(1, -1))


def _mixer(x2d, b, t, i, w_in, cmp_pos, cmp_w1, cmp_w2, diff_lambda, diff_subln, sinks):
    slopes_a, slopes_b, slopes_c = _alibi_slopes()
    proj, gl, kv_t = _project(x2d, _build_w_in(w_in))
    proj3 = proj.reshape(b, t, N_MAIN)
    kvc, kvc_t = _compress(proj3, cmp_pos, cmp_w1, cmp_w2)
    o_c, sel = _cmp_attn(proj3, kvc, kvc_t, slopes_a)
    o_s = _selected_attn_t(proj3, kv_t, sel, slopes_a)
    o_w = _window_attn_t(proj3, kv_t, [slopes_a], q_blk=QA_BLK, kv_blk=KVA_BLK + 2, window=NSA_WINDOW,
                         name="nsa_window")
    lam_init = 0.8 - 0.6 * math.exp(-0.3 * i)
    o_b = _diff_attn_t(proj3, kv_t, slopes_b, diff_lambda, diff_subln, lam_init)
    r_c = C_HEADS // C_KV_HEADS
    o_sw = _window_attn_t(proj3, kv_t, [slopes_c[g * r_c:(g + 1) * r_c] for g in range(C_KV_HEADS)],
                          q_blk=QC_BLK, kv_blk=KVC_BLK, window=SWA_WINDOW, sinks=sinks, name="swa_gqa")
    flat = lambda a: a.reshape(b * t, a.shape[-1])
    return flat(o_c), flat(o_s), flat(o_w), gl, flat(o_b), flat(o_sw)


def kernel(x, p, w_in, cmp_pos, cmp_w1, cmp_w2, diff_lambda, diff_subln, sinks, w_out, ln1_g, ln1_b,
           ffn_w_gate, ffn_w_up, ffn_w_down, moe_router, moe_w_gate, moe_w_up, moe_w_down, ln2_g, ln2_b,
           ple_gate, ple_proj, ln3_g, ln3_b):
    b, t, d = x.shape
    n = b * t
    x2d = x.reshape(n, d)
    for i in range(DEPTH):
        heads = _mixer(x2d, b, t, i, w_in[i], cmp_pos[i], cmp_w1[i], cmp_w2[i],
                       diff_lambda[i], diff_subln[i], sinks[i])
        x1 = _outproj(*heads, x2d, w_out[i], ln1_g[i], ln1_b[i])
        if i % 2 == 0:
            f = _ffn(x1, ffn_w_gate[i // 2], ffn_w_up[i // 2], ffn_w_down[i // 2])
        else:
            xb, gd, scol, srow, base, cnt = _route(x1, moe_router[i // 2])
            cnt_i = cnt[:, 0, :N_EXPERTS].astype(jnp.int32).reshape(-1)
            base_i = base[:, :, :N_EXPERTS].astype(jnp.int32).reshape(-1)
            f = _moe(xb, srow, scol, gd, cnt_i, base_i,
                     moe_w_gate[i // 2], moe_w_up[i // 2], moe_w_down[i // 2])
        x2d = _ple(x1, f, p[i].reshape(n, PLE_DIM), ple_gate[i], ple_proj[i],
                   ln2_g[i], ln2_b[i], ln3_g[i], ln3_b[i])
    return x2d.reshape(b, t, d)
```

```python
import functools
import math

import jax
import jax.numpy as jnp
import numpy as np
from jax import lax
from jax.experimental import pallas as pl
from jax.experimental.pallas import tpu as pltpu

F32 = jnp.float32
BF16 = jnp.bfloat16

D_MODEL = 1024
HEAD_DIM = 64
A_HEADS = 4
B_HEADS = 4
C_HEADS = 8
C_KV_HEADS = 2
DIFF_DK = 32
CMP_LEN = 32
CMP_STRIDE = 16
CMP_HIDDEN = 128
SLC_BLOCK = 64
SLC_SHIFT = 6
SLC_TOPN = 16
NSA_WINDOW = 512
SWA_WINDOW = 128
N_EXPERTS = 8
D_FF_EXPERT = 3584
D_FF_DENSE = 2816
PLE_DIM = 256
LN_EPS = 1e-5
NEG_INF = -1e30
FORCE_SCORE = 1e9
DEPTH = 2
DN_ALPHA = (2 * DEPTH) ** 0.25

LANES = 128
QA_BLK, QB_BLK, QC_BLK, KVB_BLK, KVC_BLK, KVA_BLK, N_MAIN_BLK = 0, 4, 8, 16, 20, 22, 25
N_MAIN = N_MAIN_BLK * LANES

LOG2E = 1.4426950408889634
TQ = 256
TQ_DENSE = 512
TK_DENSE = 512
TM = 512
FFN_CHUNK = 1408
MOE_TILE = 2048
MOE_ROWS = 288
MOE_ROWS_RECIP = (7282, 21)
MOE_ROW_ALIGN = 16
MOE_TB = 256
MOE_CHUNK = 896
VMEM_LIMIT = 56 * 1024 * 1024


def _cparams(sem):
    return pltpu.CompilerParams(dimension_semantics=sem, vmem_limit_bytes=VMEM_LIMIT)


def _alibi_slopes():
    n = A_HEADS + B_HEADS + C_HEADS
    s = [LOG2E * 2.0 ** (-8.0 * i / n) for i in range(1, n + 1)]
    rest = s[C_HEADS:]
    return rest[0::2], rest[1::2], s[:C_HEADS]


def _dot(a, b):
    return jnp.dot(a, b, preferred_element_type=F32)


def _dot_nt(a, b):
    return lax.dot_general(a, b, (((1,), (1,)), ((), ())), preferred_element_type=F32)


def _split2(x):
    hi = x.astype(BF16)
    lo = (x - hi.astype(F32)).astype(BF16)
    return hi, lo


def _layer_norm(y, g, b):
    mu = jnp.mean(y, axis=-1, keepdims=True)
    d = y - mu
    var = jnp.mean(d * d, axis=-1, keepdims=True)
    return d * lax.rsqrt(var + LN_EPS) * g + b


def _sigmoid(x):
    return 1.0 / (1.0 + jnp.exp(-x))


def _lane(shape):
    return lax.broadcasted_iota(jnp.int32, shape, 1)


def _heads_to_rows(parts):
    out = [jnp.transpose(jnp.concatenate([a, b], axis=0)) for a, b in zip(parts[0::2], parts[1::2])]
    return out[0] if len(out) == 1 else jnp.concatenate(out, axis=1)


def _proj_kernel(x_ref, w_ref, o_ref, g_ref, t_ref):
    acc = _dot(x_ref[...].astype(BF16), w_ref[...])
    o_ref[...] = acc[:, :N_MAIN].astype(BF16)
    g_ref[...] = acc[:, N_MAIN:]
    t_ref[...] = jnp.transpose(acc[:, KVB_BLK * LANES:N_MAIN]).astype(BF16)


def _project(x2d, w_ext):
    n = x2d.shape[0]
    n_kv = N_MAIN - KVB_BLK * LANES
    return pl.pallas_call(
        _proj_kernel,
        grid=(n // TM,),
        in_specs=[pl.BlockSpec((TM, D_MODEL), lambda i: (i, 0)),
                  pl.BlockSpec((D_MODEL, N_MAIN + LANES), lambda i: (0, 0))],
        out_specs=[pl.BlockSpec((TM, N_MAIN), lambda i: (i, 0)),
                   pl.BlockSpec((TM, LANES), lambda i: (i, 0)),
                   pl.BlockSpec((n_kv, TM), lambda i: (0, i))],
        out_shape=[jax.ShapeDtypeStruct((n, N_MAIN), BF16),
                   jax.ShapeDtypeStruct((n, LANES), F32),
                   jax.ShapeDtypeStruct((n_kv, n), BF16)],
        compiler_params=_cparams(("parallel",)),
        name="in_proj",
    )(x2d, w_ext)


def _build_w_in(w):
    d = w.shape[0]
    o = np.cumsum([0, 256, 384, 12, 256, 256, 256, 512, 128, 128])
    q_a, kv_a, g_a, q_b, k_b, v_b, q_c, k_c, v_c = [w[:, o[i]:o[i + 1]] for i in range(9)]
    z = jnp.zeros((d, HEAD_DIM), F32)
    hd = HEAD_DIM
    cols = []
    for h in range(A_HEADS):
        cols += [q_a[:, h * hd:(h + 1) * hd] * (LOG2E * hd ** -0.5), z]
    for h in range(B_HEADS):
        cols += [q_b[:, h * hd:(h + 1) * hd] * (LOG2E * DIFF_DK ** -0.5), z]
    for h in range(C_HEADS):
        cols += [q_c[:, h * hd:(h + 1) * hd] * (LOG2E * hd ** -0.5), z]
    for h in range(B_HEADS):
        cols += [k_b[:, h * hd:(h + 1) * hd], v_b[:, h * hd:(h + 1) * hd]]
    for g in range(C_KV_HEADS):
        cols += [k_c[:, g * hd:(g + 1) * hd], v_c[:, g * hd:(g + 1) * hd]]
    cols.append(kv_a)
    cols += [g_a, jnp.zeros((d, LANES - 12), F32)]
    return jnp.concatenate(cols, axis=1).astype(BF16)


def _compress_kernel(ch_ref, wc_ref, pos_ref, w1_ref, w2_ref, o_ref, ot_ref):
    nck = ch_ref.shape[1]
    u = _dot(ch_ref[0], wc_ref[...])
    hs = []
    for j in range(2):
        pw = _dot(pos_ref[j].astype(BF16), w1_ref[j])[0:1]
        ua = u[:, (2 * j) * LANES:(2 * j + 1) * LANES]
        ub = u[:, (2 * j + 1) * LANES:(2 * j + 2) * LANES]
        pre = ua + pltpu.roll(ub, nck - 1, 0) + pw
        hs.append(0.5 * pre * (1.0 + jnp.tanh(0.7978845608028654 * (pre + 0.044715 * pre * pre * pre))))
    h = jnp.concatenate(hs, axis=1).astype(BF16)
    kvc = _dot(h, w2_ref[...])
    o_ref[0] = kvc.astype(BF16)
    ot_ref[0] = jnp.transpose(kvc).astype(BF16)


def _compress(proj3, cmp_pos, cmp_w1, cmp_w2):
    b, t, _ = proj3.shape
    nck = t // CMP_STRIDE
    kv = proj3[:, :, KVA_BLK * LANES:(KVA_BLK + 1) * LANES]
    chunks = kv.reshape(b, nck, CMP_STRIDE * LANES)
    w1 = cmp_w1.reshape(2, 2, CMP_STRIDE, HEAD_DIM, CMP_HIDDEN)
    z = jnp.zeros((CMP_STRIDE, HEAD_DIM, CMP_HIDDEN), F32)
    blocks = []
    for j in range(2):
        for half in range(2):
            pair = [w1[j, half], z] if j == 0 else [z, w1[j, half]]
            blocks.append(jnp.concatenate(pair, axis=1).reshape(CMP_STRIDE * LANES, CMP_HIDDEN))
    wc = jnp.concatenate(blocks, axis=1).astype(BF16)
    pos = jnp.zeros((2, 8, CMP_LEN * HEAD_DIM), F32).at[:, 0].set(cmp_pos.reshape(2, -1))
    z2 = jnp.zeros((CMP_HIDDEN, HEAD_DIM), F32)
    w2 = jnp.concatenate([jnp.concatenate([cmp_w2[0], z2], axis=1),
                          jnp.concatenate([z2, cmp_w2[1]], axis=1)], axis=0).astype(BF16)
    return pl.pallas_call(
        _compress_kernel,
        grid=(b,),
        in_specs=[pl.BlockSpec((1, nck, CMP_STRIDE * LANES), lambda i: (i, 0, 0)),
                  pl.BlockSpec(wc.shape, lambda i: (0, 0)),
                  pl.BlockSpec(pos.shape, lambda i: (0, 0, 0)),
                  pl.BlockSpec((2, CMP_LEN * HEAD_DIM, CMP_HIDDEN), lambda i: (0, 0, 0)),
                  pl.BlockSpec(w2.shape, lambda i: (0, 0))],
        out_specs=[pl.BlockSpec((1, nck, LANES), lambda i: (i, 0, 0)),
                   pl.BlockSpec((1, LANES, nck), lambda i: (i, 0, 0))],
        out_shape=[jax.ShapeDtypeStruct((b, nck, LANES), BF16),
                   jax.ShapeDtypeStruct((b, LANES, nck), BF16)],
        compiler_params=_cparams(("parallel",)),
        name="nsa_compress",
    )(chunks, wc, pos, cmp_w1.astype(BF16), w2)


def _cmp_attn_kernel_t(q_ref, kvc_ref, kvct_ref, ovt_ref, bias_ref, o_ref, sel_ref):
    tq = q_ref.shape[1]
    ncp = kvc_ref.shape[1]
    rows = A_HEADS * tq
    t0 = pl.program_id(1) * tq
    qs = _stack_heads(q_ref[0], A_HEADS)
    u = _dot_nt(kvc_ref[0], qs) + bias_ref[...]
    n = lax.broadcasted_iota(jnp.int32, (ncp, rows), 0)
    block_end = jnp.where(n == ncp - 1, jnp.int32(1 << 30), n * CMP_STRIDE + (CMP_LEN - 1))
    keep = block_end <= jnp.bitwise_and(_lane((ncp, rows)), tq - 1) + t0
    sh = jnp.where(keep, u, NEG_INF)
    m = jnp.max(sh, axis=0, keepdims=True)
    e = jnp.where(keep, jnp.exp2(sh - m), 0.0)
    p = e / jnp.maximum(jnp.sum(e, axis=0, keepdims=True), 1e-30)
    o_t = _dot(kvct_ref[0, HEAD_DIM:, :], p.astype(BF16))
    o_ref[0] = _heads_to_rows([o_t[:, h * tq:(h + 1) * tq] for h in range(A_HEADS)]).astype(BF16)

    psum = p[:, 0:tq] + p[:, tq:2 * tq] + p[:, 2 * tq:3 * tq] + p[:, 3 * tq:4 * tq]
    hi = psum.astype(BF16)
    r1 = psum - hi.astype(F32)
    mid = r1.astype(BF16)
    lo = (r1 - mid.astype(F32)).astype(BF16)
    ovt = ovt_ref[...]
    imp = _dot(ovt, hi) + _dot(ovt, mid) + _dot(ovt, lo)
    nsl = ncp * CMP_STRIDE // SLC_BLOCK
    j = lax.broadcasted_iota(jnp.int32, (LANES, tq), 0)
    t_blk = jnp.right_shift(_lane((LANES, tq)) + t0, SLC_SHIFT)
    forced = (j == 0) | (j == t_blk) | (j == t_blk - 1)
    imp = jnp.where(forced, FORCE_SCORE, jnp.where(j > t_blk, -1.0, imp))
    work = jnp.where(j < nsl, imp, -3.0)
    jf = j.astype(F32)
    sel = jnp.zeros((LANES, tq), F32)
    for _ in range(SLC_TOPN):
        mx = jnp.max(work, axis=0, keepdims=True)
        first = jnp.min(jnp.where(work == mx, jf, float(LANES)), axis=0, keepdims=True)
        pick = jf == first
        sel = jnp.where(pick, 1.0, sel)
        work = jnp.where(pick, -4.0, work)
    sel_ref[0] = jnp.transpose(sel).astype(BF16)


def _cmp_attn(proj3, kvc, kvc_t, slopes):
    b, t, _ = proj3.shape
    ncp = kvc.shape[1]
    nsl = t // SLC_BLOCK
    c0 = np.arange(ncp)[None, :] * CMP_STRIDE
    s0 = np.arange(LANES)[:, None] * SLC_BLOCK
    ovt = np.clip(np.minimum(c0 + CMP_LEN, s0 + SLC_BLOCK) - np.maximum(c0, s0), 0, None) / CMP_LEN
    ovt[:, ncp - 1] = 0.0
    ovt[nsl:, :] = 0.0
    rel = np.arange(TQ)[None, :] - (np.arange(ncp)[:, None] * CMP_STRIDE + (CMP_LEN - 1))
    bias = np.concatenate([-s * rel for s in slopes], axis=1).astype(np.float32)
    return pl.pallas_call(
        _cmp_attn_kernel_t,
        grid=(b, t // TQ),
        in_specs=[pl.BlockSpec((1, TQ, A_HEADS * LANES), lambda bi, i: (bi, i, QA_BLK // A_HEADS)),
                  pl.BlockSpec((1, ncp, LANES), lambda bi, i: (bi, 0, 0)),
                  pl.BlockSpec((1, LANES, ncp), lambda bi, i: (bi, 0, 0)),
                  pl.BlockSpec((LANES, ncp), lambda bi, i: (0, 0)),
                  pl.BlockSpec((ncp, A_HEADS * TQ), lambda bi, i: (0, 0))],
        out_specs=[pl.BlockSpec((1, TQ, A_HEADS * HEAD_DIM), lambda bi, i: (bi, i, 0)),
                   pl.BlockSpec((1, TQ, LANES), lambda bi, i: (bi, i, 0))],
        out_shape=[jax.ShapeDtypeStruct((b, t, A_HEADS * HEAD_DIM), BF16),
                   jax.ShapeDtypeStruct((b, t, LANES), BF16)],
        compiler_params=_cparams(("parallel", "parallel")),
        name="nsa_cmp_attn",
    )(proj3, kvc, kvc_t, jnp.asarray(ovt, BF16), jnp.asarray(bias))


def _stack_heads(q, n):
    return jnp.concatenate([q[:, h * LANES:(h + 1) * LANES] for h in range(n)], axis=0)


def _window_bias(slopes, tq, window):
    wk = window + tq
    n_var = -(-window // tq) + 1
    j = np.arange(wk)[:, None]
    q = np.arange(tq)[None, :]
    out = np.empty((len(slopes), n_var, wk, len(slopes[0]) * tq), np.float32)
    for v in range(n_var):
        d = v * tq - max(v * tq - window, 0) + q - j
        keep = (d >= 0) & (d < window)
        for g, row in enumerate(slopes):
            for r, slope in enumerate(row):
                out[g, v, :, r * tq:(r + 1) * tq] = np.where(keep, -slope * d, NEG_INF)
    return jnp.asarray(out)


def _window_kernel_t(*refs, r_heads, window, has_sink):
    if has_sink:
        sink_ref, q_ref, kv_ref, kvt_ref, bias_ref, o_ref = refs
    else:
        q_ref, kv_ref, kvt_ref, bias_ref, o_ref = refs
    g, i = pl.program_id(1), pl.program_id(2)
    tq = q_ref.shape[1]
    wk = window + tq
    start = pl.multiple_of(jnp.maximum(i * tq - window, 0), LANES)
    qs = _stack_heads(q_ref[0], r_heads)
    u = _dot_nt(kv_ref[0, pl.ds(start, wk), :], qs) + bias_ref[0, 0]
    m = jnp.max(u, axis=0, keepdims=True)
    if has_sink:
        sink = jnp.concatenate([jnp.full((1, tq), sink_ref[g * r_heads + r], F32) for r in range(r_heads)], axis=1)
        m = jnp.maximum(m, sink)
    p = jnp.exp2(u - m)
    den = jnp.sum(p, axis=0, keepdims=True)
    if has_sink:
        den = den + jnp.exp2(sink - m)
    o_t = _dot(kvt_ref[HEAD_DIM:, pl.ds(start, wk)], p.astype(BF16)) / den
    o_ref[0] = _heads_to_rows([o_t[:, r * tq:(r + 1) * tq] for r in range(r_heads)]).astype(BF16)


def _window_attn_t(proj3, kv_t, slopes, *, q_blk, kv_blk, window, sinks=None, name):
    b, t, _ = proj3.shape
    groups, r_heads = len(slopes), len(slopes[0])
    bias = _window_bias(slopes, TQ, window)
    n_var, wk = bias.shape[1], bias.shape[2]
    in_specs = [pl.BlockSpec((1, TQ, r_heads * LANES), lambda bi, g, i: (bi, i, q_blk // r_heads + g)),
                pl.BlockSpec((1, t, LANES), lambda bi, g, i: (bi, 0, kv_blk + g)),
                pl.BlockSpec((LANES, t), lambda bi, g, i: (kv_blk - KVB_BLK + g, bi)),
                pl.BlockSpec((1, 1, wk, r_heads * TQ), lambda bi, g, i: (g, jnp.minimum(i, n_var - 1), 0, 0))]
    args = [proj3, proj3, kv_t, bias]
    if sinks is not None:
        in_specs.insert(0, pl.BlockSpec(memory_space=pltpu.SMEM))
        args.insert(0, sinks.astype(F32) * LOG2E)
    return pl.pallas_call(
        functools.partial(_window_kernel_t, r_heads=r_heads, window=window, has_sink=sinks is not None),
        grid=(b, groups, t // TQ),
        in_specs=in_specs,
        out_specs=pl.BlockSpec((1, TQ, r_heads * HEAD_DIM), lambda bi, g, i: (bi, i, g)),
        out_shape=jax.ShapeDtypeStruct((b, t, groups * r_heads * HEAD_DIM), BF16),
        compiler_params=_cparams(("parallel", "parallel", "parallel")),
        name=name,
    )(*args)


def _flash_bias(stream_slopes, tq, tk):
    ratio = tk // tq
    j = np.arange(tk)[:, None]
    q = np.arange(tq)[None, :]
    out = np.empty((len(stream_slopes), 1 + ratio, tk, len(stream_slopes[0]) * tq), np.float32)
    for s, slopes in enumerate(stream_slopes):
        for h, slope in enumerate(slopes):
            cols = slice(h * tq, (h + 1) * tq)
            out[s, 0, :, cols] = slope * j
            for r in range(ratio):
                out[s, 1 + r, :, cols] = np.where(j <= r * tq + q, slope * j, NEG_INF)
    return jnp.asarray(out)


def _flash_t(streams, kv_ref, kvt_ref, i, *, tq, tk, sel_t=None, interior=None):
    ratio = tk // tq
    n_full = lax.div(i, jnp.int32(ratio))
    rows = streams[0][0].shape[0]

    def tiles(kts, carry, edge):
        starts = [pl.multiple_of(kt * tk, tk) for kt in kts]
        scores = [[_dot_nt(kv_ref[0, pl.ds(k0, tk), s[1]:s[1] + LANES], s[0]) for s in streams]
                  for k0 in starts]
        state = [(m, l) for m, l, _ in carry]
        updates = []
        for n, kt in enumerate(kts):
            variant = 1 + i - kt * ratio if (edge and n == len(kts) - 1) else 0
            keep = None
            if sel_t is not None:
                blk = jnp.right_shift(lax.broadcasted_iota(jnp.int32, (tk, LANES), 0) + kt * tk, SLC_SHIFT)
                expand = jnp.where(_lane((tk, LANES)) == blk, 1.0, 0.0).astype(BF16)
                keep = _dot(expand, sel_t) > 0.5
            k0f = (kt * tk).astype(F32)
            row = []
            for s, (_, _, slope_lane, bias) in enumerate(streams):
                m, l = state[s]
                u = scores[n][s] + bias(variant)
                if keep is not None:
                    u = jnp.where(keep, u, NEG_INF)
                shift = slope_lane * k0f
                m_new = jnp.maximum(m, jnp.max(u, axis=0, keepdims=True) + shift)
                p = jnp.exp2(u - (m_new - shift))
                alpha = jnp.exp2(m - m_new)
                state[s] = (m_new, alpha * l + jnp.sum(p, axis=0, keepdims=True))
                row.append((alpha, p.astype(BF16)))
            updates.append(row)
        accs = [acc for _, _, acc in carry]
        for k0, row in zip(starts, updates):
            for s, (alpha, p) in enumerate(row):
                col = streams[s][1]
                kv_f = kvt_ref[col + HEAD_DIM:col + LANES, pl.ds(k0, tk)]
                accs[s] = alpha * accs[s] + _dot(kv_f, p)
        return tuple((m, l, acc) for (m, l), acc in zip(state, accs))

    init = tuple((jnp.full((1, rows), NEG_INF, F32), jnp.zeros((1, rows), F32), jnp.zeros((HEAD_DIM, rows), F32))
                 for _ in streams)
    if interior is None:
        count, tile_id = n_full, lambda j: j
    else:
        count, tile_id = interior
    carry = lax.fori_loop(0, jnp.right_shift(count, 1),
                          lambda j, c: tiles([tile_id(2 * j), tile_id(2 * j + 1)], c, False), init)
    carry = lax.cond(jnp.bitwise_and(count, 1) == 1,
                     lambda c: tiles([tile_id(count - 1), n_full], c, True),
                     lambda c: tiles([n_full], c, True), carry)
    return [(l, acc) for _, l, acc in carry]


def _selected_kernel_t(count_ref, ids_ref, q_ref, kv_ref, kvt_ref, sel_ref, bias_ref, o_ref, *, slopes):
    tq = q_ref.shape[1]
    i = pl.program_id(1)
    step = pl.program_id(0) * pl.num_programs(1) + i
    n_kt = kv_ref.shape[1] // TK_DENSE
    interior = (count_ref[step], lambda j: ids_ref[step * n_kt + j])
    sel_t = jnp.transpose(sel_ref[0].astype(F32)).astype(BF16)
    sel_t = jnp.concatenate([sel_t, sel_t], axis=1)
    streams = []
    for pair in range(A_HEADS // 2):
        qs = jnp.concatenate([q_ref[0, :, h * LANES:(h + 1) * LANES] for h in (2 * pair, 2 * pair + 1)], axis=0)
        slope_lane = jnp.concatenate([jnp.full((1, tq), slopes[h], F32) for h in (2 * pair, 2 * pair + 1)], axis=1)
        streams.append((qs, 0, slope_lane, functools.partial(lambda s, v: bias_ref[s, v], pair)))
    outs = []
    for l, acc in _flash_t(streams, kv_ref, kvt_ref, i, tq=tq, tk=TK_DENSE, sel_t=sel_t, interior=interior):
        o_t = acc / l
        outs += [o_t[:, :tq], o_t[:, tq:]]
    o_ref[0] = _heads_to_rows(outs).astype(BF16)


def _selected_attn_t(proj3, kv_t, sel, slopes):
    b, t, _ = proj3.shape
    kv_blk = KVA_BLK + 1
    bias = _flash_bias([slopes[0:2], slopes[2:4]], TQ_DENSE, TK_DENSE)
    nq, nk, per = t // TQ_DENSE, t // TK_DENSE, TK_DENSE // SLC_BLOCK
    picked = jnp.max(sel.reshape(b, nq, TQ_DENSE, LANES)[..., :nk * per], axis=2) > 0
    active = jnp.any(picked.reshape(b, nq, nk, per), axis=-1)
    past = jnp.arange(nk)[None, :] < (jnp.arange(nq) * TQ_DENSE // TK_DENSE)[:, None]
    active = active & past[None]
    ids = jnp.argsort(~active, axis=-1, stable=True).astype(jnp.int32).reshape(-1)
    counts = jnp.sum(active, axis=-1).astype(jnp.int32).reshape(-1)
    grid_spec = pltpu.PrefetchScalarGridSpec(
        num_scalar_prefetch=2,
        grid=(b, nq),
        in_specs=[pl.BlockSpec((1, TQ_DENSE, A_HEADS * LANES), lambda bi, i, *_: (bi, i, QA_BLK // A_HEADS)),
                  pl.BlockSpec((1, t, LANES), lambda bi, i, *_: (bi, 0, kv_blk)),
                  pl.BlockSpec((LANES, t), lambda bi, i, *_: (kv_blk - KVB_BLK, bi)),
                  pl.BlockSpec((1, TQ_DENSE, LANES), lambda bi, i, *_: (bi, i, 0)),
                  pl.BlockSpec(bias.shape, lambda bi, i, *_: (0, 0, 0, 0), pipeline_mode=pl.Buffered(1))],
        out_specs=pl.BlockSpec((1, TQ_DENSE, A_HEADS * HEAD_DIM), lambda bi, i, *_: (bi, i, 0)),
    )
    return pl.pallas_call(
        functools.partial(_selected_kernel_t, slopes=slopes),
        grid_spec=grid_spec,
        out_shape=jax.ShapeDtypeStruct((b, t, A_HEADS * HEAD_DIM), BF16),
        compiler_params=_cparams(("parallel", "parallel")),
        name="nsa_selected",
    )(counts, ids, proj3, proj3, kv_t, sel, bias)


def _diff_kernel_t(slope_ref, dl_ref, g_ref, q_ref, kv_ref, kvt_ref, bias_ref, o_ref, *, lam_init):
    tq = q_ref.shape[1]
    hp = pl.program_id(1)
    i = pl.program_id(2)
    dl = dl_ref[...]
    lam = (jnp.exp(jnp.sum(dl[0:1] * dl[1:2], axis=-1, keepdims=True))
           - jnp.exp(jnp.sum(dl[2:3] * dl[3:4], axis=-1, keepdims=True)) + lam_init)
    lane = _lane((tq, LANES))
    streams = []
    for hh in range(2):
        q = q_ref[0, :, hh * LANES:(hh + 1) * LANES]
        zero = jnp.zeros_like(q)
        qs = jnp.concatenate([jnp.where(lane < DIFF_DK, q, zero),
                              jnp.where(lane >= DIFF_DK, q, zero)], axis=0)
        slope_lane = jnp.full((1, 2 * tq), slope_ref[hp * 2 + hh], F32)
        streams.append((qs, hh * LANES, slope_lane, functools.partial(lambda s, v: bias_ref[s, v], hh)))
    outs = []
    for l, acc in _flash_t(streams, kv_ref, kvt_ref, i, tq=tq, tk=TK_DENSE):
        o = acc / l
        w = o[:, :tq] - lam * o[:, tq:]
        ms = jnp.sum(w * w, axis=0, keepdims=True) * (1.0 / HEAD_DIM)
        outs.append(w * lax.rsqrt(ms + LN_EPS) * (1.0 - lam_init))
    o_ref[0] = (_heads_to_rows(outs) * g_ref[...]).astype(BF16)


def _diff_attn_t(proj3, kv_t, slopes, diff_lambda, subln, lam_init):
    b, t, _ = proj3.shape
    g_ext = jnp.tile(subln.reshape(1, HEAD_DIM).astype(F32), (1, 2))
    bias = _flash_bias([[s, s] for s in slopes], TQ_DENSE, TK_DENSE)
    return pl.pallas_call(
        functools.partial(_diff_kernel_t, lam_init=lam_init),
        grid=(b, B_HEADS // 2, t // TQ_DENSE),
        in_specs=[pl.BlockSpec(memory_space=pltpu.SMEM),
                  pl.BlockSpec((4, DIFF_DK), lambda bi, h, i: (0, 0)),
                  pl.BlockSpec((1, LANES), lambda bi, h, i: (0, 0)),
                  pl.BlockSpec((1, TQ_DENSE, 2 * LANES), lambda bi, h, i: (bi, i, QB_BLK // 2 + h)),
                  pl.BlockSpec((1, t, 2 * LANES), lambda bi, h, i: (bi, 0, KVB_BLK // 2 + h)),
                  pl.BlockSpec((2 * LANES, t), lambda bi, h, i: (h, bi)),
                  pl.BlockSpec((2,) + bias.shape[1:], lambda bi, h, i: (h, 0, 0, 0))],
        out_specs=pl.BlockSpec((1, TQ_DENSE, 2 * HEAD_DIM), lambda bi, h, i: (bi, i, h)),
        out_shape=jax.ShapeDtypeStruct((b, t, B_HEADS * HEAD_DIM), BF16),
        compiler_params=_cparams(("parallel", "parallel", "parallel")),
        name="diff_attn",
    )(jnp.asarray(slopes, F32), diff_lambda.astype(F32), g_ext, proj3, proj3, kv_t, bias)


def _outproj_kernel(oc_ref, os_ref, ow_ref, gl_ref, ob_ref, osw_ref, x_ref, w_ref, ex_ref, g_ref, b_ref, o_ref):
    hi, lo = _split2(_sigmoid(gl_ref[...]))
    oa = None
    for j, ref in enumerate((oc_ref, os_ref, ow_ref)):
        gate = _dot(hi, ex_ref[j]) + _dot(lo, ex_ref[j])
        term = gate * ref[...].astype(F32)
        oa = term if oa is None else oa + term
    cat = jnp.concatenate([oa.astype(BF16), ob_ref[...], osw_ref[...]], axis=1)
    y = DN_ALPHA * x_ref[...] + _dot(cat, w_ref[...])
    o_ref[...] = _layer_norm(y, g_ref[...], b_ref[...])


def _outproj(o_c, o_s, o_w, gl, o_b, o_sw, x2d, w_out, ln_g, ln_b):
    n = x2d.shape[0]
    ex = np.zeros((3, LANES, A_HEADS * HEAD_DIM), np.float32)
    for j in range(3):
        for h in range(A_HEADS):
            ex[j, h * 3 + j, h * HEAD_DIM:(h + 1) * HEAD_DIM] = 1.0
    row = lambda w: pl.BlockSpec((TM, w), lambda i: (i, 0))
    full = lambda s: pl.BlockSpec(s, lambda i: (0,) * len(s))
    return pl.pallas_call(
        _outproj_kernel,
        grid=(n // TM,),
        in_specs=[row(256), row(256), row(256), row(LANES), row(256), row(512), row(D_MODEL),
                  full((D_MODEL, D_MODEL)), full(ex.shape), full((1, D_MODEL)), full((1, D_MODEL))],
        out_specs=row(D_MODEL),
        out_shape=jax.ShapeDtypeStruct((n, D_MODEL), F32),
        compiler_params=_cparams(("parallel",)),
        name="out_proj_ln",
    )(o_c, o_s, o_w, gl, o_b, o_sw, x2d, w_out.astype(BF16), jnp.asarray(ex, BF16),
      ln_g.reshape(1, -1), ln_b.reshape(1, -1))


def _ffn_kernel(x_ref, wg_ref, wu_ref, wd_ref, o_ref, xb_ref):
    c = pl.program_id(1)

    @pl.when(c == 0)
    def _():
        xb_ref[...] = x_ref[...].astype(BF16)

    xb = xb_ref[...]
    g = _dot(xb, wg_ref[...])
    u = _dot(xb, wu_ref[...])
    y = _dot((g * _sigmoid(g) * u).astype(BF16), wd_ref[...])

    @pl.when(c == 0)
    def _():
        o_ref[...] = y

    @pl.when(c > 0)
    def _():
        o_ref[...] += y


def _ffn(x2d, wg, wu, wd):
    n = x2d.shape[0]
    nc = D_FF_DENSE // FFN_CHUNK
    return pl.pallas_call(
        _ffn_kernel,
        grid=(n // TM, nc),
        in_specs=[pl.BlockSpec((TM, D_MODEL), lambda i, c: (i, 0)),
                  pl.BlockSpec((D_MODEL, FFN_CHUNK), lambda i, c: (0, c)),
                  pl.BlockSpec((D_MODEL, FFN_CHUNK), lambda i, c: (0, c)),
                  pl.BlockSpec((FFN_CHUNK, D_MODEL), lambda i, c: (c, 0))],
        out_specs=pl.BlockSpec((TM, D_MODEL), lambda i, c: (i, 0)),
        out_shape=jax.ShapeDtypeStruct((n, D_MODEL), F32),
        scratch_shapes=[pltpu.VMEM((TM, D_MODEL), BF16)],
        compiler_params=_cparams(("parallel", "arbitrary")),
        name="dense_swiglu",
    )(x2d, wg.astype(BF16), wu.astype(BF16), wd.astype(BF16))


def _router_kernel(x_ref, w_ref, tri_ref, xb_ref, gd_ref, scol_ref, srow_ref, base_ref, cnt_ref):
    tt = x_ref.shape[0]
    x = x_ref[...]
    xb_ref[...] = x.astype(BF16)
    xh, xl = _split2(x)
    wh, wl = _split2(w_ref[...])
    logits = _dot(xh, wh) + _dot(xh, wl) + _dot(xl, wh)
    lane = _lane((tt, LANES))
    lf = lane.astype(F32)
    logits = jnp.where(lane < N_EXPERTS, logits, -jnp.inf)
    m1 = jnp.max(logits, axis=-1, keepdims=True)
    i1 = jnp.min(jnp.where(logits == m1, lf, float(LANES)), axis=-1, keepdims=True)
    rest = jnp.where(lf == i1, -jnp.inf, logits)
    m2 = jnp.max(rest, axis=-1, keepdims=True)
    i2 = jnp.min(jnp.where(rest == m2, lf, float(LANES)), axis=-1, keepdims=True)
    e2 = jnp.exp(m2 - m1)
    g1 = 1.0 / (1.0 + e2)
    g2 = e2 / (1.0 + e2)
    first, second = lf == i1, lf == i2
    gd_ref[...] = jnp.where(first, g1, 0.0) + jnp.where(second, g2, 0.0)
    ind = jnp.where(first, 1.0, jnp.where(second, 1.0, 0.0))
    tri = tri_ref[...]
    run = jnp.zeros((1, LANES), F32)
    slots, bases = [], []
    for tb in range(tt // MOE_TB):
        blk = ind[tb * MOE_TB:(tb + 1) * MOE_TB]
        rank = _dot(tri, blk.astype(BF16)) + run
        slots.append(jnp.where(blk > 0.5, rank, -1.0))
        bases.append(run)
        run = run + jnp.sum(blk, axis=0, keepdims=True)
    slot = jnp.concatenate(slots, axis=0)
    scol_ref[...] = slot
    srow_ref[...] = jnp.transpose(slot)[0:N_EXPERTS]
    base_ref[0] = jnp.concatenate(bases, axis=0)
    cnt_ref[0] = jnp.broadcast_to(run, (8, LANES))


def _route(x2d, w_router):
    n = x2d.shape[0]
    nt = n // MOE_TILE
    ntb = MOE_TILE // MOE_TB
    w = jnp.zeros((D_MODEL, LANES), F32).at[:, :N_EXPERTS].set(w_router)
    tri = jnp.asarray(np.tril(np.ones((MOE_TB, MOE_TB), np.float32), -1), BF16)
    return pl.pallas_call(
        _router_kernel,
        grid=(nt,),
        in_specs=[pl.BlockSpec((MOE_TILE, D_MODEL), lambda i: (i, 0)),
                  pl.BlockSpec((D_MODEL, LANES), lambda i: (0, 0)),
                  pl.BlockSpec((MOE_TB, MOE_TB), lambda i: (0, 0))],
        out_specs=[pl.BlockSpec((MOE_TILE, D_MODEL), lambda i: (i, 0)),
                   pl.BlockSpec((MOE_TILE, LANES), lambda i: (i, 0)),
                   pl.BlockSpec((MOE_TILE, LANES), lambda i: (i, 0)),
                   pl.BlockSpec((N_EXPERTS, MOE_TILE), lambda i: (0, i)),
                   pl.BlockSpec((1, ntb, LANES), lambda i: (i, 0, 0)),
                   pl.BlockSpec((1, 8, LANES), lambda i: (i, 0, 0))],
        out_shape=[jax.ShapeDtypeStruct((n, D_MODEL), BF16),
                   jax.ShapeDtypeStruct((n, LANES), F32),
                   jax.ShapeDtypeStruct((n, LANES), F32),
                   jax.ShapeDtypeStruct((N_EXPERTS, n), F32),
                   jax.ShapeDtypeStruct((nt, ntb, LANES), F32),
                   jax.ShapeDtypeStruct((nt, 8, LANES), F32)],
        compiler_params=_cparams(("parallel",)),
        name="moe_router",
    )(x2d, w, tri)


MOE_GATHER_W = MOE_TB + 16
MOE_SCATTER_W = MOE_TB + 8
MOE_CAP = MOE_TILE + 512


def _moe_kernel(cnt_ref, base_ref, xb_ref, srow_ref, scol_ref, gd_ref, wg_ref, wu_ref, wd_ref,
                o_ref, xc_ref, yc_ref):
    i, e, c = pl.program_id(0), pl.program_id(1), pl.program_id(2)
    nc = pl.num_programs(2)
    ntb = MOE_TILE // MOE_TB
    count = cnt_ref[i * N_EXPERTS + e]
    n_full = jnp.right_shift(count * MOE_ROWS_RECIP[0], MOE_ROWS_RECIP[1])
    rem = count - n_full * MOE_ROWS

    @pl.when((i == 0) & (e == 0) & (c == 0))
    def _():
        yc_ref[...] = jnp.zeros_like(yc_ref)

    @pl.when((e == 0) & (c == 0))
    def _():
        o_ref[...] = jnp.zeros_like(o_ref)

    @pl.when(c == 0)
    def _():
        xc_ref[...] = jnp.zeros_like(xc_ref)
        srow = srow_ref[pl.ds(e, 1), :]
        rid = lax.broadcasted_iota(jnp.int32, (MOE_GATHER_W, MOE_TB), 0).astype(F32)
        for tb in range(ntb):
            b0 = base_ref[(i * ntb + tb) * N_EXPERTS + e]
            b_al = pl.multiple_of(jnp.left_shift(jnp.right_shift(b0, 4), 4), 16)
            rel = srow[:, tb * MOE_TB:(tb + 1) * MOE_TB] - b_al.astype(F32)
            onehot = jnp.where(rel == rid, 1.0, 0.0).astype(BF16)
            rows = _dot(onehot, xb_ref[tb * MOE_TB:(tb + 1) * MOE_TB, :])
            cur = xc_ref[pl.ds(b_al, MOE_GATHER_W), :].astype(F32)
            xc_ref[pl.ds(b_al, MOE_GATHER_W), :] = (cur + rows).astype(BF16)

    def expert_rows(r0, n_rows):
        xr = xc_ref[pl.ds(r0, n_rows), :]
        g = _dot(xr, wg_ref[0])
        u = _dot(xr, wu_ref[0])
        y = _dot((g * _sigmoid(g) * u).astype(BF16), wd_ref[0])
        prev = yc_ref[pl.ds(r0, n_rows), :]
        yc_ref[pl.ds(r0, n_rows), :] = jnp.where(c == 0, y, prev + y)

    def rows_body(rb, carry):
        expert_rows(pl.multiple_of(rb * MOE_ROWS, MOE_ROW_ALIGN), MOE_ROWS)
        return carry

    lax.fori_loop(0, n_full + jnp.where(rem > MOE_ROWS // 2, 1, 0), rows_body, 0)

    @pl.when((rem > 0) & (rem <= MOE_ROWS // 2))
    def _():
        expert_rows(pl.multiple_of(n_full * MOE_ROWS, MOE_ROW_ALIGN), MOE_ROWS // 2)

    @pl.when(c == nc - 1)
    def _():
        lane = _lane((MOE_TB, LANES))
        cid = lax.broadcasted_iota(jnp.int32, (MOE_TB, MOE_SCATTER_W), 1).astype(F32)
        for tb in range(ntb):
            b0 = base_ref[(i * ntb + tb) * N_EXPERTS + e]
            b_al = pl.multiple_of(jnp.left_shift(jnp.right_shift(b0, 3), 3), 8)
            rows = slice(tb * MOE_TB, (tb + 1) * MOE_TB)
            slot = jnp.sum(jnp.where(lane == e, scol_ref[rows, :], 0.0), axis=-1, keepdims=True)
            gate = jnp.sum(jnp.where(lane == e, gd_ref[rows, :], 0.0), axis=-1, keepdims=True)
            onehot = jnp.where(slot - b_al.astype(F32) == cid, 1.0, 0.0).astype(BF16)
            y_win = yc_ref[pl.ds(b_al, MOE_SCATTER_W), :].astype(BF16)
            o_ref[rows, :] += gate * _dot(onehot, y_win)


def _moe(xb, srow, scol, gd, cnt, base, wg, wu, wd):
    n = xb.shape[0]
    nt = n // MOE_TILE
    nc = D_FF_EXPERT // MOE_CHUNK
    grid_spec = pltpu.PrefetchScalarGridSpec(
        num_scalar_prefetch=2,
        grid=(nt, N_EXPERTS, nc),
        in_specs=[pl.BlockSpec((MOE_TILE, D_MODEL), lambda i, e, c, *_: (i, 0), pipeline_mode=pl.Buffered(1)),
                  pl.BlockSpec((N_EXPERTS, MOE_TILE), lambda i, e, c, *_: (0, i)),
                  pl.BlockSpec((MOE_TILE, LANES), lambda i, e, c, *_: (i, 0)),
                  pl.BlockSpec((MOE_TILE, LANES), lambda i, e, c, *_: (i, 0)),
                  pl.BlockSpec((1, D_MODEL, MOE_CHUNK), lambda i, e, c, *_: (e, 0, c)),
                  pl.BlockSpec((1, D_MODEL, MOE_CHUNK), lambda i, e, c, *_: (e, 0, c)),
                  pl.BlockSpec((1, MOE_CHUNK, D_MODEL), lambda i, e, c, *_: (e, c, 0))],
        out_specs=pl.BlockSpec((MOE_TILE, D_MODEL), lambda i, e, c, *_: (i, 0), pipeline_mode=pl.Buffered(1)),
        scratch_shapes=[pltpu.VMEM((MOE_CAP, D_MODEL), BF16), pltpu.VMEM((MOE_CAP, D_MODEL), F32)],
    )
    return pl.pallas_call(
        _moe_kernel,
        grid_spec=grid_spec,
        out_shape=jax.ShapeDtypeStruct((n, D_MODEL), F32),
        compiler_params=_cparams(("arbitrary", "arbitrary", "arbitrary")),
        name="moe_experts",
    )(cnt, base, xb, srow, scol, gd, wg.astype(BF16), wu.astype(BF16), wd.astype(BF16))


def _ple_kernel(x_ref, f_ref, p_ref, wg_ref, wp_ref, g2_ref, b2_ref, g3_ref, b3_ref, o_ref):
    x2 = _layer_norm(DN_ALPHA * x_ref[...] + f_ref[...], g2_ref[...], b2_ref[...])
    e = _sigmoid(_dot(x2.astype(BF16), wg_ref[...])) * _dot(p_ref[...].astype(BF16), wp_ref[...])
    o_ref[...] = _layer_norm(DN_ALPHA * x2 + e, g3_ref[...], b3_ref[...])


def _ple(x2d, f2d, p2d, w_gate, w_proj, g2, b2, g3, b3):
    n = x2d.shape[0]
    row = lambda w: pl.BlockSpec((TM, w), lambda i: (i, 0))
    full = lambda s: pl.BlockSpec(s, lambda i: (0,) * len(s))
    vec = full((1, D_MODEL))
    return pl.pallas_call(
        _ple_kernel,
        grid=(n // TM,),
        in_specs=[row(D_MODEL), row(D_MODEL), row(PLE_DIM), full((D_MODEL, D_MODEL)), full((PLE_DIM, D_MODEL)),
                  vec, vec, vec, vec],
        out_specs=row(D_MODEL),
        out_shape=jax.ShapeDtypeStruct((n, D_MODEL), F32),
        compiler_params=_cparams(("parallel",)),
        name="ln_ple_ln",
    )(x2d, f2d, p2d, w_gate.astype(BF16), w_proj.astype(BF16),
      g2.reshape(1, -1), b2.reshape(1, -1), g3.reshape(1, -1), b3.reshape(1, -1))


def _mixer(x2d, b, t, i, w_in, cmp_pos, cmp_w1, cmp_w2, diff_lambda, diff_subln, sinks):
    slopes_a, slopes_b, slopes_c = _alibi_slopes()
    proj, gl, kv_t = _project(x2d, _build_w_in(w_in))
    proj3 = proj.reshape(b, t, N_MAIN)
    kvc, kvc_t = _compress(proj3, cmp_pos, cmp_w1, cmp_w2)
    o_c, sel = _cmp_attn(proj3, kvc, kvc_t, slopes_a)
    o_s = _selected_attn_t(proj3, kv_t, sel, slopes_a)
    o_w = _window_attn_t(proj3, kv_t, [slopes_a], q_blk=QA_BLK, kv_blk=KVA_BLK + 2, window=NSA_WINDOW,
                         name="nsa_window")
    lam_init = 0.8 - 0.6 * math.exp(-0.3 * i)
    o_b = _diff_attn_t(proj3, kv_t, slopes_b, diff_lambda, diff_subln, lam_init)
    r_c = C_HEADS // C_KV_HEADS
    o_sw = _window_attn_t(proj3, kv_t, [slopes_c[g * r_c:(g + 1) * r_c] for g in range(C_KV_HEADS)],
                          q_blk=QC_BLK, kv_blk=KVC_BLK, window=SWA_WINDOW, sinks=sinks, name="swa_gqa")
    flat = lambda a: a.reshape(b * t, a.shape[-1])
    return flat(o_c), flat(o_s), flat(o_w), gl, flat(o_b), flat(o_sw)


def kernel(x, p, w_in, cmp_pos, cmp_w1, cmp_w2, diff_lambda, diff_subln, sinks, w_out, ln1_g, ln1_b,
           ffn_w_gate, ffn_w_up, ffn_w_down, moe_router, moe_w_gate, moe_w_up, moe_w_down, ln2_g, ln2_b,
           ple_gate, ple_proj, ln3_g, ln3_b):
    b, t, d = x.shape
    n = b * t
    x2d = x.reshape(n, d)
    for i in range(DEPTH):
        heads = _mixer(x2d, b, t, i, w_in[i], cmp_pos[i], cmp_w1[i], cmp_w2[i],
                       diff_lambda[i], diff_subln[i], sinks[i])
        x1 = _outproj(*heads, x2d, w_out[i], ln1_g[i], ln1_b[i])
        if i % 2 == 0:
            f = _ffn(x1, ffn_w_gate[i // 2], ffn_w_up[i // 2], ffn_w_down[i // 2])
        else:
            xb, gd, scol, srow, base, cnt = _route(x1, moe_router[i // 2])
            cnt_i = cnt[:, 0, :N_EXPERTS].astype(jnp.int32).reshape(-1)
            base_i = base[:, :, :N_EXPERTS].astype(jnp.int32).reshape(-1)
            f = _moe(xb, srow, scol, gd, cnt_i, base_i,
                     moe_w_gate[i // 2], moe_w_up[i // 2], moe_w_down[i // 2])
        x2d = _ple(x1, f, p[i].reshape(n, PLE_DIM), ple_gate[i], ple_proj[i],
                   ln2_g[i], ln2_b[i], ln3_g[i], ln3_b[i])
    return x2d.reshape(b, t, d)
```

```python
import functools
import math

import jax
import jax.numpy as jnp
import numpy as np
from jax import lax
from jax.experimental import pallas as pl
from jax.experimental.pallas import tpu as pltpu

F32 = jnp.float32
BF16 = jnp.bfloat16

D_MODEL = 1024
HEAD_DIM = 64
A_HEADS = 4
B_HEADS = 4
C_HEADS = 8
C_KV_HEADS = 2
DIFF_DK = 32
CMP_LEN = 32
CMP_STRIDE = 16
CMP_HIDDEN = 128
SLC_BLOCK = 64
SLC_SHIFT = 6
SLC_TOPN = 16
NSA_WINDOW = 512
SWA_WINDOW = 128
N_EXPERTS = 8
D_FF_EXPERT = 3584
D_FF_DENSE = 2816
PLE_DIM = 256
LN_EPS = 1e-5
NEG_INF = -1e30
FORCE_SCORE = 1e9
DEPTH = 2
DN_ALPHA = (2 * DEPTH) ** 0.25

LANES = 128
QA_BLK, QB_BLK, QC_BLK, KVB_BLK, KVC_BLK, KVA_BLK, N_MAIN_BLK = 0, 4, 8, 16, 20, 22, 25
N_MAIN = N_MAIN_BLK * LANES

LOG2E = 1.4426950408889634
TQ = 256
TQ_DENSE = 512
TK_DENSE = 512
TM = 512
FFN_CHUNK = 1408
MOE_TILE = 2048
MOE_ROWS = 288
MOE_ROWS_RECIP = (7282, 21)
MOE_ROW_ALIGN = 16
MOE_TB = 256
MOE_CHUNK = 896
VMEM_LIMIT = 56 * 1024 * 1024


def _cparams(sem):
    return pltpu.CompilerParams(dimension_semantics=sem, vmem_limit_bytes=VMEM_LIMIT)


def _alibi_slopes():
    n = A_HEADS + B_HEADS + C_HEADS
    s = [LOG2E * 2.0 ** (-8.0 * i / n) for i in range(1, n + 1)]
    rest = s[C_HEADS:]
    return rest[0::2], rest[1::2], s[:C_HEADS]


def _dot(a, b):
    return jnp.dot(a, b, preferred_element_type=F32)


def _dot_nt(a, b):
    return lax.dot_general(a, b, (((1,), (1,)), ((), ())), preferred_element_type=F32)


def _split2(x):
    hi = x.astype(BF16)
    lo = (x - hi.astype(F32)).astype(BF16)
    return hi, lo


def _layer_norm(y, g, b):
    mu = jnp.mean(y, axis=-1, keepdims=True)
    d = y - mu
    var = jnp.mean(d * d, axis=-1, keepdims=True)
    return d * lax.rsqrt(var + LN_EPS) * g + b


def _sigmoid(x):
    return 1.0 / (1.0 + jnp.exp(-x))


def _lane(shape):
    return lax.broadcasted_iota(jnp.int32, shape, 1)


def _heads_to_rows(parts):
    out = [jnp.transpose(jnp.concatenate([a, b], axis=0)) for a, b in zip(parts[0::2], parts[1::2])]
    return out[0] if len(out) == 1 else jnp.concatenate(out, axis=1)


def _proj_kernel(x_ref, w_ref, o_ref, g_ref, t_ref):
    acc = _dot(x_ref[...].astype(BF16), w_ref[...])
    o_ref[...] = acc[:, :N_MAIN].astype(BF16)
    g_ref[...] = acc[:, N_MAIN:]
    t_ref[...] = jnp.transpose(acc[:, KVB_BLK * LANES:N_MAIN]).astype(BF16)


def _project(x2d, w_ext):
    n = x2d.shape[0]
    n_kv = N_MAIN - KVB_BLK * LANES
    return pl.pallas_call(
        _proj_kernel,
        grid=(n // TM,),
        in_specs=[pl.BlockSpec((TM, D_MODEL), lambda i: (i, 0)),
                  pl.BlockSpec((D_MODEL, N_MAIN + LANES), lambda i: (0, 0))],
        out_specs=[pl.BlockSpec((TM, N_MAIN), lambda i: (i, 0)),
                   pl.BlockSpec((TM, LANES), lambda i: (i, 0)),
                   pl.BlockSpec((n_kv, TM), lambda i: (0, i))],
        out_shape=[jax.ShapeDtypeStruct((n, N_MAIN), BF16),
                   jax.ShapeDtypeStruct((n, LANES), F32),
                   jax.ShapeDtypeStruct((n_kv, n), BF16)],
        compiler_params=_cparams(("parallel",)),
        name="in_proj",
    )(x2d, w_ext)


def _build_w_in(w):
    d = w.shape[0]
    o = np.cumsum([0, 256, 384, 12, 256, 256, 256, 512, 128, 128])
    q_a, kv_a, g_a, q_b, k_b, v_b, q_c, k_c, v_c = [w[:, o[i]:o[i + 1]] for i in range(9)]
    z = jnp.zeros((d, HEAD_DIM), F32)
    hd = HEAD_DIM
    cols = []
    for h in range(A_HEADS):
        cols += [q_a[:, h * hd:(h + 1) * hd] * (LOG2E * hd ** -0.5), z]
    for h in range(B_HEADS):
        cols += [q_b[:, h * hd:(h + 1) * hd] * (LOG2E * DIFF_DK ** -0.5), z]
    for h in range(C_HEADS):
        cols += [q_c[:, h * hd:(h + 1) * hd] * (LOG2E * hd ** -0.5), z]
    for h in range(B_HEADS):
        cols += [k_b[:, h * hd:(h + 1) * hd], v_b[:, h * hd:(h + 1) * hd]]
    for g in range(C_KV_HEADS):
        cols += [k_c[:, g * hd:(g + 1) * hd], v_c[:, g * hd:(g + 1) * hd]]
    cols.append(kv_a)
    cols += [g_a, jnp.zeros((d, LANES - 12), F32)]
    return jnp.concatenate(cols, axis=1).astype(BF16)


def _compress_kernel(ch_ref, wc_ref, pos_ref, w1_ref, w2_ref, o_ref, ot_ref):
    nck = ch_ref.shape[1]
    u = _dot(ch_ref[0], wc_ref[...])
    hs = []
    for j in range(2):
        pw = _dot(pos_ref[j].astype(BF16), w1_ref[j])[0:1]
        ua = u[:, (2 * j) * LANES:(2 * j + 1) * LANES]
        ub = u[:, (2 * j + 1) * LANES:(2 * j + 2) * LANES]
        pre = ua + pltpu.roll(ub, nck - 1, 0) + pw
        hs.append(0.5 * pre * (1.0 + jnp.tanh(0.7978845608028654 * (pre + 0.044715 * pre * pre * pre))))
    h = jnp.concatenate(hs, axis=1).astype(BF16)
    kvc = _dot(h, w2_ref[...])
    o_ref[0] = kvc.astype(BF16)
    ot_ref[0] = jnp.transpose(kvc).astype(BF16)


def _compress(proj3, cmp_pos, cmp_w1, cmp_w2):
    b, t, _ = proj3.shape
    nck = t // CMP_STRIDE
    kv = proj3[:, :, KVA_BLK * LANES:(KVA_BLK + 1) * LANES]
    chunks = kv.reshape(b, nck, CMP_STRIDE * LANES)
    w1 = cmp_w1.reshape(2, 2, CMP_STRIDE, HEAD_DIM, CMP_HIDDEN)
    z = jnp.zeros((CMP_STRIDE, HEAD_DIM, CMP_HIDDEN), F32)
    blocks = []
    for j in range(2):
        for half in range(2):
            pair = [w1[j, half], z] if j == 0 else [z, w1[j, half]]
            blocks.append(jnp.concatenate(pair, axis=1).reshape(CMP_STRIDE * LANES, CMP_HIDDEN))
    wc = jnp.concatenate(blocks, axis=1).astype(BF16)
    pos = jnp.zeros((2, 8, CMP_LEN * HEAD_DIM), F32).at[:, 0].set(cmp_pos.reshape(2, -1))
    z2 = jnp.zeros((CMP_HIDDEN, HEAD_DIM), F32)
    w2 = jnp.concatenate([jnp.concatenate([cmp_w2[0], z2], axis=1),
                          jnp.concatenate([z2, cmp_w2[1]], axis=1)], axis=0).astype(BF16)
    return pl.pallas_call(
        _compress_kernel,
        grid=(b,),
        in_specs=[pl.BlockSpec((1, nck, CMP_STRIDE * LANES), lambda i: (i, 0, 0)),
                  pl.BlockSpec(wc.shape, lambda i: (0, 0)),
                  pl.BlockSpec(pos.shape, lambda i: (0, 0, 0)),
                  pl.BlockSpec((2, CMP_LEN * HEAD_DIM, CMP_HIDDEN), lambda i: (0, 0, 0)),
                  pl.BlockSpec(w2.shape, lambda i: (0, 0))],
        out_specs=[pl.BlockSpec((1, nck, LANES), lambda i: (i, 0, 0)),
                   pl.BlockSpec((1, LANES, nck), lambda i: (i, 0, 0))],
        out_shape=[jax.ShapeDtypeStruct((b, nck, LANES), BF16),
                   jax.ShapeDtypeStruct((b, LANES, nck), BF16)],
        compiler_params=_cparams(("parallel",)),
        name="nsa_compress",
    )(chunks, wc, pos, cmp_w1.astype(BF16), w2)


def _cmp_attn_kernel_t(q_ref, kvc_ref, kvct_ref, ovt_ref, bias_ref, o_ref, sel_ref):
    tq = q_ref.shape[1]
    ncp = kvc_ref.shape[1]
    rows = A_HEADS * tq
    t0 = pl.program_id(1) * tq
    qs = _stack_heads(q_ref[0], A_HEADS)
    u = _dot_nt(kvc_ref[0], qs) + bias_ref[...]
    n = lax.broadcasted_iota(jnp.int32, (ncp, rows), 0)
    block_end = jnp.where(n == ncp - 1, jnp.int32(1 << 30), n * CMP_STRIDE + (CMP_LEN - 1))
    keep = block_end <= jnp.bitwise_and(_lane((ncp, rows)), tq - 1) + t0
    sh = jnp.where(keep, u, NEG_INF)
    m = jnp.max(sh, axis=0, keepdims=True)
    e = jnp.where(keep, jnp.exp2(sh - m), 0.0)
    p = e / jnp.maximum(jnp.sum(e, axis=0, keepdims=True), 1e-30)
    o_t = _dot(kvct_ref[0, HEAD_DIM:, :], p.astype(BF16))
    o_ref[0] = _heads_to_rows([o_t[:, h * tq:(h + 1) * tq] for h in range(A_HEADS)]).astype(BF16)

    psum = p[:, 0:tq] + p[:, tq:2 * tq] + p[:, 2 * tq:3 * tq] + p[:, 3 * tq:4 * tq]
    hi = psum.astype(BF16)
    r1 = psum - hi.astype(F32)
    mid = r1.astype(BF16)
    lo = (r1 - mid.astype(F32)).astype(BF16)
    ovt = ovt_ref[...]
    imp = _dot(ovt, hi) + _dot(ovt, mid) + _dot(ovt, lo)
    nsl = ncp * CMP_STRIDE // SLC_BLOCK
    j = lax.broadcasted_iota(jnp.int32, (LANES, tq), 0)
    t_blk = jnp.right_shift(_lane((LANES, tq)) + t0, SLC_SHIFT)
    forced = (j == 0) | (j == t_blk) | (j == t_blk - 1)
    imp = jnp.where(forced, FORCE_SCORE, jnp.where(j > t_blk, -1.0, imp))
    work = jnp.where(j < nsl, imp, -3.0)
    jf = j.astype(F32)
    sel = jnp.zeros((LANES, tq), F32)
    for _ in range(SLC_TOPN):
        mx = jnp.max(work, axis=0, keepdims=True)
        first = jnp.min(jnp.where(work == mx, jf, float(LANES)), axis=0, keepdims=True)
        pick = jf == first
        sel = jnp.where(pick, 1.0, sel)
        work = jnp.where(pick, -4.0, work)
    sel_ref[0] = jnp.transpose(sel).astype(BF16)


def _cmp_attn(proj3, kvc, kvc_t, slopes):
    b, t, _ = proj3.shape
    ncp = kvc.shape[1]
    nsl = t // SLC_BLOCK
    c0 = np.arange(ncp)[None, :] * CMP_STRIDE
    s0 = np.arange(LANES)[:, None] * SLC_BLOCK
    ovt = np.clip(np.minimum(c0 + CMP_LEN, s0 + SLC_BLOCK) - np.maximum(c0, s0), 0, None) / CMP_LEN
    ovt[:, ncp - 1] = 0.0
    ovt[nsl:, :] = 0.0
    rel = np.arange(TQ)[None, :] - (np.arange(ncp)[:, None] * CMP_STRIDE + (CMP_LEN - 1))
    bias = np.concatenate([-s * rel for s in slopes], axis=1).astype(np.float32)
    return pl.pallas_call(
        _cmp_attn_kernel_t,
        grid=(b, t // TQ),
        in_specs=[pl.BlockSpec((1, TQ, A_HEADS * LANES), lambda bi, i: (bi, i, QA_BLK // A_HEADS)),
                  pl.BlockSpec((1, ncp, LANES), lambda bi, i: (bi, 0, 0)),
                  pl.BlockSpec((1, LANES, ncp), lambda bi, i: (bi, 0, 0)),
                  pl.BlockSpec((LANES, ncp), lambda bi, i: (0, 0)),
                  pl.BlockSpec((ncp, A_HEADS * TQ), lambda bi, i: (0, 0))],
        out_specs=[pl.BlockSpec((1, TQ, A_HEADS * HEAD_DIM), lambda bi, i: (bi, i, 0)),
                   pl.BlockSpec((1, TQ, LANES), lambda bi, i: (bi, i, 0))],
        out_shape=[jax.ShapeDtypeStruct((b, t, A_HEADS * HEAD_DIM), BF16),
                   jax.ShapeDtypeStruct((b, t, LANES), BF16)],
        compiler_params=_cparams(("parallel", "parallel")),
        name="nsa_cmp_attn",
    )(proj3, kvc, kvc_t, jnp.asarray(ovt, BF16), jnp.asarray(bias))


def _stack_heads(q, n):
    return jnp.concatenate([q[:, h * LANES:(h + 1) * LANES] for h in range(n)], axis=0)


def _window_bias(slopes, tq, window):
    wk = window + tq
    n_var = -(-window // tq) + 1
    j = np.arange(wk)[:, None]
    q = np.arange(tq)[None, :]
    out = np.empty((len(slopes), n_var, wk, len(slopes[0]) * tq), np.float32)
    for v in range(n_var):
        d = v * tq - max(v * tq - window, 0) + q - j
        keep = (d >= 0) & (d < window)
        for g, row in enumerate(slopes):
            for r, slope in enumerate(row):
                out[g, v, :, r * tq:(r + 1) * tq] = np.where(keep, -slope * d, NEG_INF)
    return jnp.asarray(out)


def _window_kernel_t(*refs, r_heads, window, has_sink):
    if has_sink:
        sink_ref, q_ref, kv_ref, kvt_ref, bias_ref, o_ref = refs
    else:
        q_ref, kv_ref, kvt_ref, bias_ref, o_ref = refs
    g, i = pl.program_id(1), pl.program_id(2)
    tq = q_ref.shape[1]
    wk = window + tq
    start = pl.multiple_of(jnp.maximum(i * tq - window, 0), LANES)
    qs = _stack_heads(q_ref[0], r_heads)
    u = _dot_nt(kv_ref[0, pl.ds(start, wk), :], qs) + bias_ref[0, 0]
    m = jnp.max(u, axis=0, keepdims=True)
    if has_sink:
        sink = jnp.concatenate([jnp.full((1, tq), sink_ref[g * r_heads + r], F32) for r in range(r_heads)], axis=1)
        m = jnp.maximum(m, sink)
    p = jnp.exp2(u - m)
    den = jnp.sum(p, axis=0, keepdims=True)
    if has_sink:
        den = den + jnp.exp2(sink - m)
    o_t = _dot(kvt_ref[HEAD_DIM:, pl.ds(start, wk)], p.astype(BF16)) / den
    o_ref[0] = _heads_to_rows([o_t[:, r * tq:(r + 1) * tq] for r in range(r_heads)]).astype(BF16)


def _window_attn_t(proj3, kv_t, slopes, *, q_blk, kv_blk, window, sinks=None, name):
    b, t, _ = proj3.shape
    groups, r_heads = len(slopes), len(slopes[0])
    bias = _window_bias(slopes, TQ, window)
    n_var, wk = bias.shape[1], bias.shape[2]
    in_specs = [pl.BlockSpec((1, TQ, r_heads * LANES), lambda bi, g, i: (bi, i, q_blk // r_heads + g)),
                pl.BlockSpec((1, t, LANES), lambda bi, g, i: (bi, 0, kv_blk + g)),
                pl.BlockSpec((LANES, t), lambda bi, g, i: (kv_blk - KVB_BLK + g, bi)),
                pl.BlockSpec((1, 1, wk, r_heads * TQ), lambda bi, g, i: (g, jnp.minimum(i, n_var - 1), 0, 0))]
    args = [proj3, proj3, kv_t, bias]
    if sinks is not None:
        in_specs.insert(0, pl.BlockSpec(memory_space=pltpu.SMEM))
        args.insert(0, sinks.astype(F32) * LOG2E)
    return pl.pallas_call(
        functools.partial(_window_kernel_t, r_heads=r_heads, window=window, has_sink=sinks is not None),
        grid=(b, groups, t // TQ),
        in_specs=in_specs,
        out_specs=pl.BlockSpec((1, TQ, r_heads * HEAD_DIM), lambda bi, g, i: (bi, i, g)),
        out_shape=jax.ShapeDtypeStruct((b, t, groups * r_heads * HEAD_DIM), BF16),
        compiler_params=_cparams(("parallel", "parallel", "parallel")),
        name=name,
    )(*args)


def _flash_bias(stream_slopes, tq, tk):
    ratio = tk // tq
    j = np.arange(tk)[:, None]
    q = np.arange(tq)[None, :]
    out = np.empty((len(stream_slopes), 1 + ratio, tk, len(stream_slopes[0]) * tq), np.float32)
    for s, slopes in enumerate(stream_slopes):
        for h, slope in enumerate(slopes):
            cols = slice(h * tq, (h + 1) * tq)
            out[s, 0, :, cols] = slope * j
            for r in range(ratio):
                out[s, 1 + r, :, cols] = np.where(j <= r * tq + q, slope * j, NEG_INF)
    return jnp.asarray(out)


def _flash_t(streams, kv_ref, kvt_ref, i, *, tq, tk, sel_t=None, interior=None):
    ratio = tk // tq
    n_full = lax.div(i, jnp.int32(ratio))
    rows = streams[0][0].shape[0]

    def tiles(kts, carry, edge):
        starts = [pl.multiple_of(kt * tk, tk) for kt in kts]
        scores = [[_dot_nt(kv_ref[0, pl.ds(k0, tk), s[1]:s[1] + LANES], s[0]) for s in streams]
                  for k0 in starts]
        state = [(m, l) for m, l, _ in carry]
        updates = []
        for n, kt in enumerate(kts):
            variant = 1 + i - kt * ratio if (edge and n == len(kts) - 1) else 0
            keep = None
            if sel_t is not None:
                blk = jnp.right_shift(lax.broadcasted_iota(jnp.int32, (tk, LANES), 0) + kt * tk, SLC_SHIFT)
                expand = jnp.where(_lane((tk, LANES)) == blk, 1.0, 0.0).astype(BF16)
                keep = _dot(expand, sel_t) > 0.5
            k0f = (kt * tk).astype(F32)
            row = []
            for s, (_, _, slope_lane, bias) in enumerate(streams):
                m, l = state[s]
                u = scores[n][s] + bias(variant)
                if keep is not None:
                    u = jnp.where(keep, u, NEG_INF)
                shift = slope_lane * k0f
                m_new = jnp.maximum(m, jnp.max(u, axis=0, keepdims=True) + shift)
                p = jnp.exp2(u - (m_new - shift))
                alpha = jnp.exp2(m - m_new)
                state[s] = (m_new, alpha * l + jnp.sum(p, axis=0, keepdims=True))
                row.append((alpha, p.astype(BF16)))
            updates.append(row)
        accs = [acc for _, _, acc in carry]
        for k0, row in zip(starts, updates):
            for s, (alpha, p) in enumerate(row):
                col = streams[s][1]
                kv_f = kvt_ref[col + HEAD_DIM:col + LANES, pl.ds(k0, tk)]
                accs[s] = alpha * accs[s] + _dot(kv_f, p)
        return tuple((m, l, acc) for (m, l), acc in zip(state, accs))

    init = tuple((jnp.full((1, rows), NEG_INF, F32), jnp.zeros((1, rows), F32), jnp.zeros((HEAD_DIM, rows), F32))
                 for _ in streams)
    if interior is None:
        count, tile_id = n_full, lambda j: j
    else:
        count, tile_id = interior
    carry = lax.fori_loop(0, jnp.right_shift(count, 1),
                          lambda j, c: tiles([tile_id(2 * j), tile_id(2 * j + 1)], c, False), init)
    carry = lax.cond(jnp.bitwise_and(count, 1) == 1,
                     lambda c: tiles([tile_id(count - 1), n_full], c, True),
                     lambda c: tiles([n_full], c, True), carry)
    return [(l, acc) for _, l, acc in carry]


def _selected_kernel_t(count_ref, ids_ref, q_ref, kv_ref, kvt_ref, sel_ref, bias_ref, o_ref, *, slopes):
    tq = q_ref.shape[1]
    i = pl.program_id(1)
    step = pl.program_id(0) * pl.num_programs(1) + i
    n_kt = kv_ref.shape[1] // TK_DENSE
    interior = (count_ref[step], lambda j: ids_ref[step * n_kt + j])
    sel_t = jnp.transpose(sel_ref[0].astype(F32)).astype(BF16)
    sel_t = jnp.concatenate([sel_t, sel_t], axis=1)
    streams = []
    for pair in range(A_HEADS // 2):
        qs = jnp.concatenate([q_ref[0, :, h * LANES:(h + 1) * LANES] for h in (2 * pair, 2 * pair + 1)], axis=0)
        slope_lane = jnp.concatenate([jnp.full((1, tq), slopes[h], F32) for h in (2 * pair, 2 * pair + 1)], axis=1)
        streams.append((qs, 0, slope_lane, functools.partial(lambda s, v: bias_ref[s, v], pair)))
    outs = []
    for l, acc in _flash_t(streams, kv_ref, kvt_ref, i, tq=tq, tk=TK_DENSE, sel_t=sel_t, interior=interior):
        o_t = acc / l
        outs += [o_t[:, :tq], o_t[:, tq:]]
    o_ref[0] = _heads_to_rows(outs).astype(BF16)


def _selected_attn_t(proj3, kv_t, sel, slopes):
    b, t, _ = proj3.shape
    kv_blk = KVA_BLK + 1
    bias = _flash_bias([slopes[0:2], slopes[2:4]], TQ_DENSE, TK_DENSE)
    nq, nk, per = t // TQ_DENSE, t // TK_DENSE, TK_DENSE // SLC_BLOCK
    picked = jnp.max(sel.reshape(b, nq, TQ_DENSE, LANES)[..., :nk * per], axis=2) > 0
    active = jnp.any(picked.reshape(b, nq, nk, per), axis=-1)
    past = jnp.arange(nk)[None, :] < (jnp.arange(nq) * TQ_DENSE // TK_DENSE)[:, None]
    active = active & past[None]
    ids = jnp.argsort(~active, axis=-1, stable=True).astype(jnp.int32).reshape(-1)
    counts = jnp.sum(active, axis=-1).astype(jnp.int32).reshape(-1)
    grid_spec = pltpu.PrefetchScalarGridSpec(
        num_scalar_prefetch=2,
        grid=(b, nq),
        in_specs=[pl.BlockSpec((1, TQ_DENSE, A_HEADS * LANES), lambda bi, i, *_: (bi, i, QA_BLK // A_HEADS)),
                  pl.BlockSpec((1, t, LANES), lambda bi, i, *_: (bi, 0, kv_blk)),
                  pl.BlockSpec((LANES, t), lambda bi, i, *_: (kv_blk - KVB_BLK, bi)),
                  pl.BlockSpec((1, TQ_DENSE, LANES), lambda bi, i, *_: (bi, i, 0)),
                  pl.BlockSpec(bias.shape, lambda bi, i, *_: (0, 0, 0, 0), pipeline_mode=pl.Buffered(1))],
        out_specs=pl.BlockSpec((1, TQ_DENSE, A_HEADS * HEAD_DIM), lambda bi, i, *_: (bi, i, 0)),
    )
    return pl.pallas_call(
        functools.partial(_selected_kernel_t, slopes=slopes),
        grid_spec=grid_spec,
        out_shape=jax.ShapeDtypeStruct((b, t, A_HEADS * HEAD_DIM), BF16),
        compiler_params=_cparams(("parallel", "parallel")),
        name="nsa_selected",
    )(counts, ids, proj3, proj3, kv_t, sel, bias)


def _diff_kernel_t(slope_ref, dl_ref, g_ref, q_ref, kv_ref, kvt_ref, bias_ref, o_ref, *, lam_init):
    tq = q_ref.shape[1]
    hp = pl.program_id(1)
    i = pl.program_id(2)
    dl = dl_ref[...]
    lam = (jnp.exp(jnp.sum(dl[0:1] * dl[1:2], axis=-1, keepdims=True))
           - jnp.exp(jnp.sum(dl[2:3] * dl[3:4], axis=-1, keepdims=True)) + lam_init)
    lane = _lane((tq, LANES))
    streams = []
    for hh in range(2):
        q = q_ref[0, :, hh * LANES:(hh + 1) * LANES]
        zero = jnp.zeros_like(q)
        qs = jnp.concatenate([jnp.where(lane < DIFF_DK, q, zero),
                              jnp.where(lane >= DIFF_DK, q, zero)], axis=0)
        slope_lane = jnp.full((1, 2 * tq), slope_ref[hp * 2 + hh], F32)
        streams.append((qs, hh * LANES, slope_lane, functools.partial(lambda s, v: bias_ref[s, v], hh)))
    outs = []
    for l, acc in _flash_t(streams, kv_ref, kvt_ref, i, tq=tq, tk=TK_DENSE):
        o = acc / l
        w = o[:, :tq] - lam * o[:, tq:]
        ms = jnp.sum(w * w, axis=0, keepdims=True) * (1.0 / HEAD_DIM)
        outs.append(w * lax.rsqrt(ms + LN_EPS) * (1.0 - lam_init))
    o_ref[0] = (_heads_to_rows(outs) * g_ref[...]).astype(BF16)


def _diff_attn_t(proj3, kv_t, slopes, diff_lambda, subln, lam_init):
    b, t, _ = proj3.shape
    g_ext = jnp.tile(subln.reshape(1, HEAD_DIM).astype(F32), (1, 2))
    bias = _flash_bias([[s, s] for s in slopes], TQ_DENSE, TK_DENSE)
    return pl.pallas_call(
        functools.partial(_diff_kernel_t, lam_init=lam_init),
        grid=(b, B_HEADS // 2, t // TQ_DENSE),
        in_specs=[pl.BlockSpec(memory_space=pltpu.SMEM),
                  pl.BlockSpec((4, DIFF_DK), lambda bi, h, i: (0, 0)),
                  pl.BlockSpec((1, LANES), lambda bi, h, i: (0, 0)),
                  pl.BlockSpec((1, TQ_DENSE, 2 * LANES), lambda bi, h, i: (bi, i, QB_BLK // 2 + h)),
                  pl.BlockSpec((1, t, 2 * LANES), lambda bi, h, i: (bi, 0, KVB_BLK // 2 + h)),
                  pl.BlockSpec((2 * LANES, t), lambda bi, h, i: (h, bi)),
                  pl.BlockSpec((2,) + bias.shape[1:], lambda bi, h, i: (h, 0, 0, 0))],
        out_specs=pl.BlockSpec((1, TQ_DENSE, 2 * HEAD_DIM), lambda bi, h, i: (bi, i, h)),
        out_shape=jax.ShapeDtypeStruct((b, t, B_HEADS * HEAD_DIM), BF16),
        compiler_params=_cparams(("parallel", "parallel", "parallel")),
        name="diff_attn",
    )(jnp.asarray(slopes, F32), diff_lambda.astype(F32), g_ext, proj3, proj3, kv_t, bias)


def _outproj_kernel(oc_ref, os_ref, ow_ref, gl_ref, ob_ref, osw_ref, x_ref, w_ref, ex_ref, g_ref, b_ref, o_ref):
    hi, lo = _split2(_sigmoid(gl_ref[...]))
    oa = None
    for j, ref in enumerate((oc_ref, os_ref, ow_ref)):
        gate = _dot(hi, ex_ref[j]) + _dot(lo, ex_ref[j])
        term = gate * ref[...].astype(F32)
        oa = term if oa is None else oa + term
    cat = jnp.concatenate([oa.astype(BF16), ob_ref[...], osw_ref[...]], axis=1)
    y = DN_ALPHA * x_ref[...] + _dot(cat, w_ref[...])
    o_ref[...] = _layer_norm(y, g_ref[...], b_ref[...])


def _outproj(o_c, o_s, o_w, gl, o_b, o_sw, x2d, w_out, ln_g, ln_b):
    n = x2d.shape[0]
    ex = np.zeros((3, LANES, A_HEADS * HEAD_DIM), np.float32)
    for j in range(3):
        for h in range(A_HEADS):
            ex[j, h * 3 + j, h * HEAD_DIM:(h + 1) * HEAD_DIM] = 1.0
    row = lambda w: pl.BlockSpec((TM, w), lambda i: (i, 0))
    full = lambda s: pl.BlockSpec(s, lambda i: (0,) * len(s))
    return pl.pallas_call(
        _outproj_kernel,
        grid=(n // TM,),
        in_specs=[row(256), row(256), row(256), row(LANES), row(256), row(512), row(D_MODEL),
                  full((D_MODEL, D_MODEL)), full(ex.shape), full((1, D_MODEL)), full((1, D_MODEL))],
        out_specs=row(D_MODEL),
        out_shape=jax.ShapeDtypeStruct((n, D_MODEL), F32),
        compiler_params=_cparams(("parallel",)),
        name="out_proj_ln",
    )(o_c, o_s, o_w, gl, o_b, o_sw, x2d, w_out.astype(BF16), jnp.asarray(ex, BF16),
      ln_g.reshape(1, -1), ln_b.reshape(1, -1))


def _ffn_kernel(x_ref, wg_ref, wu_ref, wd_ref, o_ref):
    xb = x_ref[...].astype(BF16)
    acc = None
    for c in range(D_FF_DENSE // FFN_CHUNK):
        cols = slice(c * FFN_CHUNK, (c + 1) * FFN_CHUNK)
        g = _dot(xb, wg_ref[:, cols])
        u = _dot(xb, wu_ref[:, cols])
        y = _dot((g * _sigmoid(g) * u).astype(BF16), wd_ref[cols, :])
        acc = y if acc is None else acc + y
    o_ref[...] = acc


def _ffn(x2d, wg, wu, wd):
    n = x2d.shape[0]
    resident = lambda shape: pl.BlockSpec(shape, lambda i: (0, 0), pipeline_mode=pl.Buffered(1))
    return pl.pallas_call(
        _ffn_kernel,
        grid=(n // TM,),
        in_specs=[pl.BlockSpec((TM, D_MODEL), lambda i: (i, 0)),
                  resident((D_MODEL, D_FF_DENSE)), resident((D_MODEL, D_FF_DENSE)), resident((D_FF_DENSE, D_MODEL))],
        out_specs=pl.BlockSpec((TM, D_MODEL), lambda i: (i, 0)),
        out_shape=jax.ShapeDtypeStruct((n, D_MODEL), F32),
        compiler_params=_cparams(("parallel",)),
        name="dense_swiglu",
    )(x2d, wg.astype(BF16), wu.astype(BF16), wd.astype(BF16))


def _router_kernel(x_ref, w_ref, tri_ref, xb_ref, gd_ref, scol_ref, srow_ref, base_ref, cnt_ref):
    tt = x_ref.shape[0]
    x = x_ref[...]
    xb_ref[...] = x.astype(BF16)
    xh, xl = _split2(x)
    wh, wl = _split2(w_ref[...])
    logits = _dot(xh, wh) + _dot(xh, wl) + _dot(xl, wh)
    lane = _lane((tt, LANES))
    lf = lane.astype(F32)
    logits = jnp.where(lane < N_EXPERTS, logits, -jnp.inf)
    m1 = jnp.max(logits, axis=-1, keepdims=True)
    i1 = jnp.min(jnp.where(logits == m1, lf, float(LANES)), axis=-1, keepdims=True)
    rest = jnp.where(lf == i1, -jnp.inf, logits)
    m2 = jnp.max(rest, axis=-1, keepdims=True)
    i2 = jnp.min(jnp.where(rest == m2, lf, float(LANES)), axis=-1, keepdims=True)
    e2 = jnp.exp(m2 - m1)
    g1 = 1.0 / (1.0 + e2)
    g2 = e2 / (1.0 + e2)
    first, second = lf == i1, lf == i2
    gd_ref[...] = jnp.where(first, g1, 0.0) + jnp.where(second, g2, 0.0)
    ind = jnp.where(first, 1.0, jnp.where(second, 1.0, 0.0))
    tri = tri_ref[...]
    run = jnp.zeros((1, LANES), F32)
    slots, bases = [], []
    for tb in range(tt // MOE_TB):
        blk = ind[tb * MOE_TB:(tb + 1) * MOE_TB]
        rank = _dot(tri, blk.astype(BF16)) + run
        slots.append(jnp.where(blk > 0.5, rank, -1.0))
        bases.append(run)
        run = run + jnp.sum(blk, axis=0, keepdims=True)
    slot = jnp.concatenate(slots, axis=0)
    scol_ref[...] = slot
    srow_ref[...] = jnp.transpose(slot)[0:N_EXPERTS]
    base_ref[0] = jnp.concatenate(bases, axis=0)
    cnt_ref[0] = jnp.broadcast_to(run, (8, LANES))


def _route(x2d, w_router):
    n = x2d.shape[0]
    nt = n // MOE_TILE
    ntb = MOE_TILE // MOE_TB
    w = jnp.zeros((D_MODEL, LANES), F32).at[:, :N_EXPERTS].set(w_router)
    tri = jnp.asarray(np.tril(np.ones((MOE_TB, MOE_TB), np.float32), -1), BF16)
    return pl.pallas_call(
        _router_kernel,
        grid=(nt,),
        in_specs=[pl.BlockSpec((MOE_TILE, D_MODEL), lambda i: (i, 0)),
                  pl.BlockSpec((D_MODEL, LANES), lambda i: (0, 0)),
                  pl.BlockSpec((MOE_TB, MOE_TB), lambda i: (0, 0))],
        out_specs=[pl.BlockSpec((MOE_TILE, D_MODEL), lambda i: (i, 0)),
                   pl.BlockSpec((MOE_TILE, LANES), lambda i: (i, 0)),
                   pl.BlockSpec((MOE_TILE, LANES), lambda i: (i, 0)),
                   pl.BlockSpec((N_EXPERTS, MOE_TILE), lambda i: (0, i)),
                   pl.BlockSpec((1, ntb, LANES), lambda i: (i, 0, 0)),
                   pl.BlockSpec((1, 8, LANES), lambda i: (i, 0, 0))],
        out_shape=[jax.ShapeDtypeStruct((n, D_MODEL), BF16),
                   jax.ShapeDtypeStruct((n, LANES), F32),
                   jax.ShapeDtypeStruct((n, LANES), F32),
                   jax.ShapeDtypeStruct((N_EXPERTS, n), F32),
                   jax.ShapeDtypeStruct((nt, ntb, LANES), F32),
                   jax.ShapeDtypeStruct((nt, 8, LANES), F32)],
        compiler_params=_cparams(("parallel",)),
        name="moe_router",
    )(x2d, w, tri)


MOE_GATHER_W = MOE_TB + 16
MOE_SCATTER_W = MOE_TB + 8
MOE_CAP = MOE_TILE + 512


def _moe_kernel(cnt_ref, base_ref, xb_ref, srow_ref, scol_ref, gd_ref, wg_ref, wu_ref, wd_ref,
                o_ref, xc_ref, yc_ref):
    i, e, c = pl.program_id(0), pl.program_id(1), pl.program_id(2)
    nc = pl.num_programs(2)
    ntb = MOE_TILE // MOE_TB
    count = cnt_ref[i * N_EXPERTS + e]
    n_full = jnp.right_shift(count * MOE_ROWS_RECIP[0], MOE_ROWS_RECIP[1])
    rem = count - n_full * MOE_ROWS

    @pl.when((i == 0) & (e == 0) & (c == 0))
    def _():
        yc_ref[...] = jnp.zeros_like(yc_ref)

    @pl.when((e == 0) & (c == 0))
    def _():
        o_ref[...] = jnp.zeros_like(o_ref)

    @pl.when(c == 0)
    def _():
        xc_ref[...] = jnp.zeros_like(xc_ref)
        srow = srow_ref[pl.ds(e, 1), :]
        rid = lax.broadcasted_iota(jnp.int32, (MOE_GATHER_W, MOE_TB), 0).astype(F32)
        for tb in range(ntb):
            b0 = base_ref[(i * ntb + tb) * N_EXPERTS + e]
            b_al = pl.multiple_of(jnp.left_shift(jnp.right_shift(b0, 4), 4), 16)
            rel = srow[:, tb * MOE_TB:(tb + 1) * MOE_TB] - b_al.astype(F32)
            onehot = jnp.where(rel == rid, 1.0, 0.0).astype(BF16)
            rows = _dot(onehot, xb_ref[tb * MOE_TB:(tb + 1) * MOE_TB, :])
            cur = xc_ref[pl.ds(b_al, MOE_GATHER_W), :].astype(F32)
            xc_ref[pl.ds(b_al, MOE_GATHER_W), :] = (cur + rows).astype(BF16)

    def expert_rows(r0, n_rows):
        xr = xc_ref[pl.ds(r0, n_rows), :]
        g = _dot(xr, wg_ref[0])
        u = _dot(xr, wu_ref[0])
        y = _dot((g * _sigmoid(g) * u).astype(BF16), wd_ref[0])
        prev = yc_ref[pl.ds(r0, n_rows), :]
        yc_ref[pl.ds(r0, n_rows), :] = jnp.where(c == 0, y, prev + y)

    def rows_body(rb, carry):
        expert_rows(pl.multiple_of(rb * MOE_ROWS, MOE_ROW_ALIGN), MOE_ROWS)
        return carry

    lax.fori_loop(0, n_full + jnp.where(rem > MOE_ROWS // 2, 1, 0), rows_body, 0)

    @pl.when((rem > 0) & (rem <= MOE_ROWS // 2))
    def _():
        expert_rows(pl.multiple_of(n_full * MOE_ROWS, MOE_ROW_ALIGN), MOE_ROWS // 2)

    @pl.when(c == nc - 1)
    def _():
        lane = _lane((MOE_TB, LANES))
        cid = lax.broadcasted_iota(jnp.int32, (MOE_TB, MOE_SCATTER_W), 1).astype(F32)
        for tb in range(ntb):
            b0 = base_ref[(i * ntb + tb) * N_EXPERTS + e]
            b_al = pl.multiple_of(jnp.left_shift(jnp.right_shift(b0, 3), 3), 8)
            rows = slice(tb * MOE_TB, (tb + 1) * MOE_TB)
            slot = jnp.sum(jnp.where(lane == e, scol_ref[rows, :], 0.0), axis=-1, keepdims=True)
            gate = jnp.sum(jnp.where(lane == e, gd_ref[rows, :], 0.0), axis=-1, keepdims=True)
            onehot = jnp.where(slot - b_al.astype(F32) == cid, 1.0, 0.0).astype(BF16)
            y_win = yc_ref[pl.ds(b_al, MOE_SCATTER_W), :].astype(BF16)
            o_ref[rows, :] += gate * _dot(onehot, y_win)


def _moe(xb, srow, scol, gd, cnt, base, wg, wu, wd):
    n = xb.shape[0]
    nt = n // MOE_TILE
    nc = D_FF_EXPERT // MOE_CHUNK

    def by_chunk(w):
        w = w.astype(BF16).reshape(N_EXPERTS, D_MODEL, nc, MOE_CHUNK)
        return jnp.transpose(w, (0, 2, 1, 3)).reshape(N_EXPERTS * nc, D_MODEL, MOE_CHUNK)

    grid_spec = pltpu.PrefetchScalarGridSpec(
        num_scalar_prefetch=2,
        grid=(nt, N_EXPERTS, nc),
        in_specs=[pl.BlockSpec((MOE_TILE, D_MODEL), lambda i, e, c, *_: (i, 0), pipeline_mode=pl.Buffered(1)),
                  pl.BlockSpec((N_EXPERTS, MOE_TILE), lambda i, e, c, *_: (0, i)),
                  pl.BlockSpec((MOE_TILE, LANES), lambda i, e, c, *_: (i, 0)),
                  pl.BlockSpec((MOE_TILE, LANES), lambda i, e, c, *_: (i, 0)),
                  pl.BlockSpec((1, D_MODEL, MOE_CHUNK), lambda i, e, c, *_: (e * nc + c, 0, 0)),
                  pl.BlockSpec((1, D_MODEL, MOE_CHUNK), lambda i, e, c, *_: (e * nc + c, 0, 0)),
                  pl.BlockSpec((1, MOE_CHUNK, D_MODEL), lambda i, e, c, *_: (e * nc + c, 0, 0))],
        out_specs=pl.BlockSpec((MOE_TILE, D_MODEL), lambda i, e, c, *_: (i, 0), pipeline_mode=pl.Buffered(1)),
        scratch_shapes=[pltpu.VMEM((MOE_CAP, D_MODEL), BF16), pltpu.VMEM((MOE_CAP, D_MODEL), F32)],
    )
    return pl.pallas_call(
        _moe_kernel,
        grid_spec=grid_spec,
        out_shape=jax.ShapeDtypeStruct((n, D_MODEL), F32),
        compiler_params=_cparams(("arbitrary", "arbitrary", "arbitrary")),
        name="moe_experts",
    )(cnt, base, xb, srow, scol, gd, by_chunk(wg), by_chunk(wu),
      wd.astype(BF16).reshape(N_EXPERTS * nc, MOE_CHUNK, D_MODEL))


def _ple_kernel(x_ref, f_ref, p_ref, wg_ref, wp_ref, g2_ref, b2_ref, g3_ref, b3_ref, o_ref):
    x2 = _layer_norm(DN_ALPHA * x_ref[...] + f_ref[...], g2_ref[...], b2_ref[...])
    e = _sigmoid(_dot(x2.astype(BF16), wg_ref[...])) * _dot(p_ref[...].astype(BF16), wp_ref[...])
    o_ref[...] = _layer_norm(DN_ALPHA * x2 + e, g3_ref[...], b3_ref[...])


def _ple(x2d, f2d, p2d, w_gate, w_proj, g2, b2, g3, b3):
    n = x2d.shape[0]
    row = lambda w: pl.BlockSpec((TM, w), lambda i: (i, 0))
    full = lambda s: pl.BlockSpec(s, lambda i: (0,) * len(s))
    vec = full((1, D_MODEL))
    return pl.pallas_call(
        _ple_kernel,
        grid=(n // TM,),
        in_specs=[row(D_MODEL), row(D_MODEL), row(PLE_DIM), full((D_MODEL, D_MODEL)), full((PLE_DIM, D_MODEL)),
                  vec, vec, vec, vec],
        out_specs=row(D_MODEL),
        out_shape=jax.ShapeDtypeStruct((n, D_MODEL), F32),
        compiler_params=_cparams(("parallel",)),
        name="ln_ple_ln",
    )(x2d, f2d, p2d, w_gate.astype(BF16), w_proj.astype(BF16),
      g2.reshape(1, -1), b2.reshape(1, -1), g3.reshape(1, -1), b3.reshape(1, -1))


def _mixer(x2d, b, t, i, w_in, cmp_pos, cmp_w1, cmp_w2, diff_lambda, diff_subln, sinks):
    slopes_a, slopes_b, slopes_c = _alibi_slopes()
    proj, gl, kv_t = _project(x2d, _build_w_in(w_in))
    proj3 = proj.reshape(b, t, N_MAIN)
    kvc, kvc_t = _compress(proj3, cmp_pos, cmp_w1, cmp_w2)
    o_c, sel = _cmp_attn(proj3, kvc, kvc_t, slopes_a)
    o_s = _selected_attn_t(proj3, kv_t, sel, slopes_a)
    o_w = _window_attn_t(proj3, kv_t, [slopes_a], q_blk=QA_BLK, kv_blk=KVA_BLK + 2, window=NSA_WINDOW,
                         name="nsa_window")
    lam_init = 0.8 - 0.6 * math.exp(-0.3 * i)
    o_b = _diff_attn_t(proj3, kv_t, slopes_b, diff_lambda, diff_subln, lam_init)
    r_c = C_HEADS // C_KV_HEADS
    o_sw = _window_attn_t(proj3, kv_t, [slopes_c[g * r_c:(g + 1) * r_c] for g in range(C_KV_HEADS)],
                          q_blk=QC_BLK, kv_blk=KVC_BLK, window=SWA_WINDOW, sinks=sinks, name="swa_gqa")
    flat = lambda a: a.reshape(b * t, a.shape[-1])
    return flat(o_c), flat(o_s), flat(o_w), gl, flat(o_b), flat(o_sw)


def kernel(x, p, w_in, cmp_pos, cmp_w1, cmp_w2, diff_lambda, diff_subln, sinks, w_out, ln1_g, ln1_b,
           ffn_w_gate, ffn_w_up, ffn_w_down, moe_router, moe_w_gate, moe_w_up, moe_w_down, ln2_g, ln2_b,
           ple_gate, ple_proj, ln3_g, ln3_b):
    b, t, d = x.shape
    n = b * t
    x2d = x.reshape(n, d)
    for i in range(DEPTH):
        heads = _mixer(x2d, b, t, i, w_in[i], cmp_pos[i], cmp_w1[i], cmp_w2[i],
                       diff_lambda[i], diff_subln[i], sinks[i])
        x1 = _outproj(*heads, x2d, w_out[i], ln1_g[i], ln1_b[i])
        if i % 2 == 0:
            f = _ffn(x1, ffn_w_gate[i // 2], ffn_w_up[i // 2], ffn_w_down[i // 2])
        else:
            xb, gd, scol, srow, base, cnt = _route(x1, moe_router[i // 2])
            cnt_i = cnt[:, 0, :N_EXPERTS].astype(jnp.int32).reshape(-1)
            base_i = base[:, :, :N_EXPERTS].astype(jnp.int32).reshape(-1)
            f = _moe(xb, srow, scol, gd, cnt_i, base_i,
                     moe_w_gate[i // 2], moe_w_up[i // 2], moe_w_down[i // 2])
        x2d = _ple(x1, f, p[i].reshape(n, PLE_DIM), ple_gate[i], ple_proj[i],
                   ln2_g[i], ln2_b[i], ln3_g[i], ln3_b[i])
    return x2d.reshape(b, t, d)
```

```python
import functools
import math

import jax
import jax.numpy as jnp
import numpy as np
from jax import lax
from jax.experimental import pallas as pl
from jax.experimental.pallas import tpu as pltpu

F32 = jnp.float32
BF16 = jnp.bfloat16

D_MODEL = 1024
HEAD_DIM = 64
A_HEADS = 4
B_HEADS = 4
C_HEADS = 8
C_KV_HEADS = 2
DIFF_DK = 32
CMP_LEN = 32
CMP_STRIDE = 16
CMP_HIDDEN = 128
SLC_BLOCK = 64
SLC_SHIFT = 6
SLC_TOPN = 16
NSA_WINDOW = 512
SWA_WINDOW = 128
N_EXPERTS = 8
D_FF_EXPERT = 3584
D_FF_DENSE = 2816
PLE_DIM = 256
LN_EPS = 1e-5
NEG_INF = -1e30
FORCE_SCORE = 1e9
DEPTH = 2
DN_ALPHA = (2 * DEPTH) ** 0.25

LANES = 128
QA_BLK, QB_BLK, QC_BLK, KVB_BLK, KVC_BLK, KVA_BLK, N_MAIN_BLK = 0, 4, 8, 16, 20, 22, 25
N_MAIN = N_MAIN_BLK * LANES

LOG2E = 1.4426950408889634
TQ = 256
TQ_DENSE = 512
TK_DENSE = 512
TM = 512
FFN_CHUNK = 1408
MOE_TILE = 2048
MOE_ROWS = 576
MOE_ROWS_RECIP = (1821, 20)
MOE_ROW_ALIGN = 16
MOE_TB = 256
MOE_CHUNK = 896
VMEM_LIMIT = 56 * 1024 * 1024


def _cparams(sem):
    return pltpu.CompilerParams(dimension_semantics=sem, vmem_limit_bytes=VMEM_LIMIT)


def _alibi_slopes():
    n = A_HEADS + B_HEADS + C_HEADS
    s = [LOG2E * 2.0 ** (-8.0 * i / n) for i in range(1, n + 1)]
    rest = s[C_HEADS:]
    return rest[0::2], rest[1::2], s[:C_HEADS]


def _dot(a, b):
    return jnp.dot(a, b, preferred_element_type=F32)


def _dot_nt(a, b):
    return lax.dot_general(a, b, (((1,), (1,)), ((), ())), preferred_element_type=F32)


def _split2(x):
    hi = x.astype(BF16)
    lo = (x - hi.astype(F32)).astype(BF16)
    return hi, lo


def _layer_norm(y, g, b):
    mu = jnp.mean(y, axis=-1, keepdims=True)
    d = y - mu
    var = jnp.mean(d * d, axis=-1, keepdims=True)
    return d * lax.rsqrt(var + LN_EPS) * g + b


def _sigmoid(x):
    return 1.0 / (1.0 + jnp.exp(-x))


def _lane(shape):
    return lax.broadcasted_iota(jnp.int32, shape, 1)


def _heads_to_rows(parts):
    out = [jnp.transpose(jnp.concatenate([a, b], axis=0)) for a, b in zip(parts[0::2], parts[1::2])]
    return out[0] if len(out) == 1 else jnp.concatenate(out, axis=1)


def _proj_kernel(x_ref, w_ref, o_ref, g_ref, t_ref):
    acc = _dot(x_ref[...].astype(BF16), w_ref[...])
    o_ref[...] = acc[:, :N_MAIN].astype(BF16)
    g_ref[...] = acc[:, N_MAIN:]
    t_ref[...] = jnp.transpose(acc[:, KVB_BLK * LANES:N_MAIN]).astype(BF16)


def _project(x2d, w_ext):
    n = x2d.shape[0]
    n_kv = N_MAIN - KVB_BLK * LANES
    return pl.pallas_call(
        _proj_kernel,
        grid=(n // TM,),
        in_specs=[pl.BlockSpec((TM, D_MODEL), lambda i: (i, 0)),
                  pl.BlockSpec((D_MODEL, N_MAIN + LANES), lambda i: (0, 0))],
        out_specs=[pl.BlockSpec((TM, N_MAIN), lambda i: (i, 0)),
                   pl.BlockSpec((TM, LANES), lambda i: (i, 0)),
                   pl.BlockSpec((n_kv, TM), lambda i: (0, i))],
        out_shape=[jax.ShapeDtypeStruct((n, N_MAIN), BF16),
                   jax.ShapeDtypeStruct((n, LANES), F32),
                   jax.ShapeDtypeStruct((n_kv, n), BF16)],
        compiler_params=_cparams(("parallel",)),
        name="in_proj",
    )(x2d, w_ext)


def _build_w_in(w):
    d = w.shape[0]
    o = np.cumsum([0, 256, 384, 12, 256, 256, 256, 512, 128, 128])
    q_a, kv_a, g_a, q_b, k_b, v_b, q_c, k_c, v_c = [w[:, o[i]:o[i + 1]] for i in range(9)]
    z = jnp.zeros((d, HEAD_DIM), F32)
    hd = HEAD_DIM
    cols = []
    for h in range(A_HEADS):
        cols += [q_a[:, h * hd:(h + 1) * hd] * (LOG2E * hd ** -0.5), z]
    for h in range(B_HEADS):
        cols += [q_b[:, h * hd:(h + 1) * hd] * (LOG2E * DIFF_DK ** -0.5), z]
    for h in range(C_HEADS):
        cols += [q_c[:, h * hd:(h + 1) * hd] * (LOG2E * hd ** -0.5), z]
    for h in range(B_HEADS):
        cols += [k_b[:, h * hd:(h + 1) * hd], v_b[:, h * hd:(h + 1) * hd]]
    for g in range(C_KV_HEADS):
        cols += [k_c[:, g * hd:(g + 1) * hd], v_c[:, g * hd:(g + 1) * hd]]
    cols.append(kv_a)
    cols += [g_a, jnp.zeros((d, LANES - 12), F32)]
    return jnp.concatenate(cols, axis=1).astype(BF16)


def _compress_kernel(ch_ref, wc_ref, pos_ref, w1_ref, w2_ref, o_ref, ot_ref):
    nck = ch_ref.shape[1]
    u = _dot(ch_ref[0], wc_ref[...])
    hs = []
    for j in range(2):
        pw = _dot(pos_ref[j].astype(BF16), w1_ref[j])[0:1]
        ua = u[:, (2 * j) * LANES:(2 * j + 1) * LANES]
        ub = u[:, (2 * j + 1) * LANES:(2 * j + 2) * LANES]
        pre = ua + pltpu.roll(ub, nck - 1, 0) + pw
        hs.append(0.5 * pre * (1.0 + jnp.tanh(0.7978845608028654 * (pre + 0.044715 * pre * pre * pre))))
    h = jnp.concatenate(hs, axis=1).astype(BF16)
    kvc = _dot(h, w2_ref[...])
    o_ref[0] = kvc.astype(BF16)
    ot_ref[0] = jnp.transpose(kvc).astype(BF16)


def _compress(proj3, cmp_pos, cmp_w1, cmp_w2):
    b, t, _ = proj3.shape
    nck = t // CMP_STRIDE
    kv = proj3[:, :, KVA_BLK * LANES:(KVA_BLK + 1) * LANES]
    chunks = kv.reshape(b, nck, CMP_STRIDE * LANES)
    w1 = cmp_w1.reshape(2, 2, CMP_STRIDE, HEAD_DIM, CMP_HIDDEN)
    z = jnp.zeros((CMP_STRIDE, HEAD_DIM, CMP_HIDDEN), F32)
    blocks = []
    for j in range(2):
        for half in range(2):
            pair = [w1[j, half], z] if j == 0 else [z, w1[j, half]]
            blocks.append(jnp.concatenate(pair, axis=1).reshape(CMP_STRIDE * LANES, CMP_HIDDEN))
    wc = jnp.concatenate(blocks, axis=1).astype(BF16)
    pos = jnp.zeros((2, 8, CMP_LEN * HEAD_DIM), F32).at[:, 0].set(cmp_pos.reshape(2, -1))
    z2 = jnp.zeros((CMP_HIDDEN, HEAD_DIM), F32)
    w2 = jnp.concatenate([jnp.concatenate([cmp_w2[0], z2], axis=1),
                          jnp.concatenate([z2, cmp_w2[1]], axis=1)], axis=0).astype(BF16)
    return pl.pallas_call(
        _compress_kernel,
        grid=(b,),
        in_specs=[pl.BlockSpec((1, nck, CMP_STRIDE * LANES), lambda i: (i, 0, 0)),
                  pl.BlockSpec(wc.shape, lambda i: (0, 0)),
                  pl.BlockSpec(pos.shape, lambda i: (0, 0, 0)),
                  pl.BlockSpec((2, CMP_LEN * HEAD_DIM, CMP_HIDDEN), lambda i: (0, 0, 0)),
                  pl.BlockSpec(w2.shape, lambda i: (0, 0))],
        out_specs=[pl.BlockSpec((1, nck, LANES), lambda i: (i, 0, 0)),
                   pl.BlockSpec((1, LANES, nck), lambda i: (i, 0, 0))],
        out_shape=[jax.ShapeDtypeStruct((b, nck, LANES), BF16),
                   jax.ShapeDtypeStruct((b, LANES, nck), BF16)],
        compiler_params=_cparams(("parallel",)),
        name="nsa_compress",
    )(chunks, wc, pos, cmp_w1.astype(BF16), w2)


def _cmp_attn_kernel_t(q_ref, kvc_ref, kvct_ref, ovt_ref, bias_ref, o_ref, sel_ref):
    tq = q_ref.shape[1]
    ncp = kvc_ref.shape[1]
    rows = A_HEADS * tq
    t0 = pl.program_id(1) * tq
    qs = _stack_heads(q_ref[0], A_HEADS)
    u = _dot_nt(kvc_ref[0], qs) + bias_ref[...]
    n = lax.broadcasted_iota(jnp.int32, (ncp, rows), 0)
    block_end = jnp.where(n == ncp - 1, jnp.int32(1 << 30), n * CMP_STRIDE + (CMP_LEN - 1))
    keep = block_end <= jnp.bitwise_and(_lane((ncp, rows)), tq - 1) + t0
    sh = jnp.where(keep, u, NEG_INF)
    m = jnp.max(sh, axis=0, keepdims=True)
    e = jnp.where(keep, jnp.exp2(sh - m), 0.0)
    p = e / jnp.maximum(jnp.sum(e, axis=0, keepdims=True), 1e-30)
    o_t = _dot(kvct_ref[0, HEAD_DIM:, :], p.astype(BF16))
    o_ref[0] = _heads_to_rows([o_t[:, h * tq:(h + 1) * tq] for h in range(A_HEADS)]).astype(BF16)

    psum = p[:, 0:tq] + p[:, tq:2 * tq] + p[:, 2 * tq:3 * tq] + p[:, 3 * tq:4 * tq]
    hi = psum.astype(BF16)
    r1 = psum - hi.astype(F32)
    mid = r1.astype(BF16)
    lo = (r1 - mid.astype(F32)).astype(BF16)
    ovt = ovt_ref[...]
    imp = _dot(ovt, hi) + _dot(ovt, mid) + _dot(ovt, lo)
    nsl = ncp * CMP_STRIDE // SLC_BLOCK
    j = lax.broadcasted_iota(jnp.int32, (LANES, tq), 0)
    t_blk = jnp.right_shift(_lane((LANES, tq)) + t0, SLC_SHIFT)
    forced = (j == 0) | (j == t_blk) | (j == t_blk - 1)
    imp = jnp.where(forced, FORCE_SCORE, jnp.where(j > t_blk, -1.0, imp))
    work = jnp.where(j < nsl, imp, -3.0)
    jf = j.astype(F32)
    sel = jnp.zeros((LANES, tq), F32)
    for _ in range(SLC_TOPN):
        mx = jnp.max(work, axis=0, keepdims=True)
        first = jnp.min(jnp.where(work == mx, jf, float(LANES)), axis=0, keepdims=True)
        pick = jf == first
        sel = jnp.where(pick, 1.0, sel)
        work = jnp.where(pick, -4.0, work)
    sel_ref[0] = jnp.transpose(sel).astype(BF16)


def _cmp_attn(proj3, kvc, kvc_t, slopes):
    b, t, _ = proj3.shape
    ncp = kvc.shape[1]
    nsl = t // SLC_BLOCK
    c0 = np.arange(ncp)[None, :] * CMP_STRIDE
    s0 = np.arange(LANES)[:, None] * SLC_BLOCK
    ovt = np.clip(np.minimum(c0 + CMP_LEN, s0 + SLC_BLOCK) - np.maximum(c0, s0), 0, None) / CMP_LEN
    ovt[:, ncp - 1] = 0.0
    ovt[nsl:, :] = 0.0
    rel = np.arange(TQ)[None, :] - (np.arange(ncp)[:, None] * CMP_STRIDE + (CMP_LEN - 1))
    bias = np.concatenate([-s * rel for s in slopes], axis=1).astype(np.float32)
    return pl.pallas_call(
        _cmp_attn_kernel_t,
        grid=(b, t // TQ),
        in_specs=[pl.BlockSpec((1, TQ, A_HEADS * LANES), lambda bi, i: (bi, i, QA_BLK // A_HEADS)),
                  pl.BlockSpec((1, ncp, LANES), lambda bi, i: (bi, 0, 0)),
                  pl.BlockSpec((1, LANES, ncp), lambda bi, i: (bi, 0, 0)),
                  pl.BlockSpec((LANES, ncp), lambda bi, i: (0, 0)),
                  pl.BlockSpec((ncp, A_HEADS * TQ), lambda bi, i: (0, 0))],
        out_specs=[pl.BlockSpec((1, TQ, A_HEADS * HEAD_DIM), lambda bi, i: (bi, i, 0)),
                   pl.BlockSpec((1, TQ, LANES), lambda bi, i: (bi, i, 0))],
        out_shape=[jax.ShapeDtypeStruct((b, t, A_HEADS * HEAD_DIM), BF16),
                   jax.ShapeDtypeStruct((b, t, LANES), BF16)],
        compiler_params=_cparams(("parallel", "parallel")),
        name="nsa_cmp_attn",
    )(proj3, kvc, kvc_t, jnp.asarray(ovt, BF16), jnp.asarray(bias))


def _stack_heads(q, n):
    return jnp.concatenate([q[:, h * LANES:(h + 1) * LANES] for h in range(n)], axis=0)


def _window_bias(slopes, tq, window):
    wk = window + tq
    n_var = -(-window // tq) + 1
    j = np.arange(wk)[:, None]
    q = np.arange(tq)[None, :]
    out = np.empty((len(slopes), n_var, wk, len(slopes[0]) * tq), np.float32)
    for v in range(n_var):
        d = v * tq - max(v * tq - window, 0) + q - j
        keep = (d >= 0) & (d < window)
        for g, row in enumerate(slopes):
            for r, slope in enumerate(row):
                out[g, v, :, r * tq:(r + 1) * tq] = np.where(keep, -slope * d, NEG_INF)
    return jnp.asarray(out)


def _window_kernel_t(*refs, r_heads, window, has_sink):
    if has_sink:
        sink_ref, q_ref, kv_ref, kvt_ref, bias_ref, o_ref = refs
    else:
        q_ref, kv_ref, kvt_ref, bias_ref, o_ref = refs
    g, i = pl.program_id(1), pl.program_id(2)
    tq = q_ref.shape[1]
    wk = window + tq
    start = pl.multiple_of(jnp.maximum(i * tq - window, 0), LANES)
    qs = _stack_heads(q_ref[0], r_heads)
    u = _dot_nt(kv_ref[0, pl.ds(start, wk), :], qs) + bias_ref[0, 0]
    m = jnp.max(u, axis=0, keepdims=True)
    if has_sink:
        sink = jnp.concatenate([jnp.full((1, tq), sink_ref[g * r_heads + r], F32) for r in range(r_heads)], axis=1)
        m = jnp.maximum(m, sink)
    p = jnp.exp2(u - m)
    den = jnp.sum(p, axis=0, keepdims=True)
    if has_sink:
        den = den + jnp.exp2(sink - m)
    o_t = _dot(kvt_ref[HEAD_DIM:, pl.ds(start, wk)], p.astype(BF16)) / den
    o_ref[0] = _heads_to_rows([o_t[:, r * tq:(r + 1) * tq] for r in range(r_heads)]).astype(BF16)


def _window_attn_t(proj3, kv_t, slopes, *, q_blk, kv_blk, window, sinks=None, name):
    b, t, _ = proj3.shape
    groups, r_heads = len(slopes), len(slopes[0])
    bias = _window_bias(slopes, TQ, window)
    n_var, wk = bias.shape[1], bias.shape[2]
    in_specs = [pl.BlockSpec((1, TQ, r_heads * LANES), lambda bi, g, i: (bi, i, q_blk // r_heads + g)),
                pl.BlockSpec((1, t, LANES), lambda bi, g, i: (bi, 0, kv_blk + g)),
                pl.BlockSpec((LANES, t), lambda bi, g, i: (kv_blk - KVB_BLK + g, bi)),
                pl.BlockSpec((1, 1, wk, r_heads * TQ), lambda bi, g, i: (g, jnp.minimum(i, n_var - 1), 0, 0))]
    args = [proj3, proj3, kv_t, bias]
    if sinks is not None:
        in_specs.insert(0, pl.BlockSpec(memory_space=pltpu.SMEM))
        args.insert(0, sinks.astype(F32) * LOG2E)
    return pl.pallas_call(
        functools.partial(_window_kernel_t, r_heads=r_heads, window=window, has_sink=sinks is not None),
        grid=(b, groups, t // TQ),
        in_specs=in_specs,
        out_specs=pl.BlockSpec((1, TQ, r_heads * HEAD_DIM), lambda bi, g, i: (bi, i, g)),
        out_shape=jax.ShapeDtypeStruct((b, t, groups * r_heads * HEAD_DIM), BF16),
        compiler_params=_cparams(("parallel", "parallel", "parallel")),
        name=name,
    )(*args)


def _flash_bias(stream_slopes, tq, tk):
    ratio = tk // tq
    j = np.arange(tk)[:, None]
    q = np.arange(tq)[None, :]
    out = np.empty((len(stream_slopes), 1 + ratio, tk, len(stream_slopes[0]) * tq), np.float32)
    for s, slopes in enumerate(stream_slopes):
        for h, slope in enumerate(slopes):
            cols = slice(h * tq, (h + 1) * tq)
            out[s, 0, :, cols] = slope * j
            for r in range(ratio):
                out[s, 1 + r, :, cols] = np.where(j <= r * tq + q, slope * j, NEG_INF)
    return jnp.asarray(out)


def _flash_t(streams, kv_ref, kvt_ref, i, *, tq, tk, sel_t=None, interior=None):
    ratio = tk // tq
    n_full = lax.div(i, jnp.int32(ratio))
    rows = streams[0][0].shape[0]

    def tiles(kts, carry, edge):
        starts = [pl.multiple_of(kt * tk, tk) for kt in kts]
        scores = [[_dot_nt(kv_ref[0, pl.ds(k0, tk), s[1]:s[1] + LANES], s[0]) for s in streams]
                  for k0 in starts]
        state = [(m, l) for m, l, _ in carry]
        updates = []
        for n, kt in enumerate(kts):
            variant = 1 + i - kt * ratio if (edge and n == len(kts) - 1) else 0
            keep = None
            if sel_t is not None:
                blk = jnp.right_shift(lax.broadcasted_iota(jnp.int32, (tk, LANES), 0) + kt * tk, SLC_SHIFT)
                expand = jnp.where(_lane((tk, LANES)) == blk, 1.0, 0.0).astype(BF16)
                keep = _dot(expand, sel_t) > 0.5
            k0f = (kt * tk).astype(F32)
            row = []
            for s, (_, _, slope_lane, bias) in enumerate(streams):
                m, l = state[s]
                u = scores[n][s] + bias(variant)
                if keep is not None:
                    u = jnp.where(keep, u, NEG_INF)
                shift = slope_lane * k0f
                m_new = jnp.maximum(m, jnp.max(u, axis=0, keepdims=True) + shift)
                p = jnp.exp2(u - (m_new - shift))
                alpha = jnp.exp2(m - m_new)
                state[s] = (m_new, alpha * l + jnp.sum(p, axis=0, keepdims=True))
                row.append((alpha, p.astype(BF16)))
            updates.append(row)
        accs = [acc for _, _, acc in carry]
        for k0, row in zip(starts, updates):
            for s, (alpha, p) in enumerate(row):
                col = streams[s][1]
                kv_f = kvt_ref[col + HEAD_DIM:col + LANES, pl.ds(k0, tk)]
                accs[s] = alpha * accs[s] + _dot(kv_f, p)
        return tuple((m, l, acc) for (m, l), acc in zip(state, accs))

    init = tuple((jnp.full((1, rows), NEG_INF, F32), jnp.zeros((1, rows), F32), jnp.zeros((HEAD_DIM, rows), F32))
                 for _ in streams)
    if interior is None:
        count, tile_id = n_full, lambda j: j
    else:
        count, tile_id = interior
    carry = lax.fori_loop(0, jnp.right_shift(count, 1),
                          lambda j, c: tiles([tile_id(2 * j), tile_id(2 * j + 1)], c, False), init)
    carry = lax.cond(jnp.bitwise_and(count, 1) == 1,
                     lambda c: tiles([tile_id(count - 1), n_full], c, True),
                     lambda c: tiles([n_full], c, True), carry)
    return [(l, acc) for _, l, acc in carry]


def _selected_kernel_t(count_ref, ids_ref, q_ref, kv_ref, kvt_ref, sel_ref, bias_ref, o_ref, *, slopes):
    tq = q_ref.shape[1]
    i = pl.program_id(1)
    step = pl.program_id(0) * pl.num_programs(1) + i
    n_kt = kv_ref.shape[1] // TK_DENSE
    interior = (count_ref[step], lambda j: ids_ref[step * n_kt + j])
    sel_t = jnp.transpose(sel_ref[0].astype(F32)).astype(BF16)
    sel_t = jnp.concatenate([sel_t, sel_t], axis=1)
    streams = []
    for pair in range(A_HEADS // 2):
        qs = jnp.concatenate([q_ref[0, :, h * LANES:(h + 1) * LANES] for h in (2 * pair, 2 * pair + 1)], axis=0)
        slope_lane = jnp.concatenate([jnp.full((1, tq), slopes[h], F32) for h in (2 * pair, 2 * pair + 1)], axis=1)
        streams.append((qs, 0, slope_lane, functools.partial(lambda s, v: bias_ref[s, v], pair)))
    outs = []
    for l, acc in _flash_t(streams, kv_ref, kvt_ref, i, tq=tq, tk=TK_DENSE, sel_t=sel_t, interior=interior):
        o_t = acc / l
        outs += [o_t[:, :tq], o_t[:, tq:]]
    o_ref[0] = _heads_to_rows(outs).astype(BF16)


def _selected_attn_t(proj3, kv_t, sel, slopes):
    b, t, _ = proj3.shape
    kv_blk = KVA_BLK + 1
    bias = _flash_bias([slopes[0:2], slopes[2:4]], TQ_DENSE, TK_DENSE)
    nq, nk, per = t // TQ_DENSE, t // TK_DENSE, TK_DENSE // SLC_BLOCK
    picked = jnp.max(sel.reshape(b, nq, TQ_DENSE, LANES)[..., :nk * per], axis=2) > 0
    active = jnp.any(picked.reshape(b, nq, nk, per), axis=-1)
    past = jnp.arange(nk)[None, :] < (jnp.arange(nq) * TQ_DENSE // TK_DENSE)[:, None]
    active = active & past[None]
    ids = jnp.argsort(~active, axis=-1, stable=True).astype(jnp.int32).reshape(-1)
    counts = jnp.sum(active, axis=-1).astype(jnp.int32).reshape(-1)
    grid_spec = pltpu.PrefetchScalarGridSpec(
        num_scalar_prefetch=2,
        grid=(b, nq),
        in_specs=[pl.BlockSpec((1, TQ_DENSE, A_HEADS * LANES), lambda bi, i, *_: (bi, i, QA_BLK // A_HEADS)),
                  pl.BlockSpec((1, t, LANES), lambda bi, i, *_: (bi, 0, kv_blk)),
                  pl.BlockSpec((LANES, t), lambda bi, i, *_: (kv_blk - KVB_BLK, bi)),
                  pl.BlockSpec((1, TQ_DENSE, LANES), lambda bi, i, *_: (bi, i, 0)),
                  pl.BlockSpec(bias.shape, lambda bi, i, *_: (0, 0, 0, 0), pipeline_mode=pl.Buffered(1))],
        out_specs=pl.BlockSpec((1, TQ_DENSE, A_HEADS * HEAD_DIM), lambda bi, i, *_: (bi, i, 0)),
    )
    return pl.pallas_call(
        functools.partial(_selected_kernel_t, slopes=slopes),
        grid_spec=grid_spec,
        out_shape=jax.ShapeDtypeStruct((b, t, A_HEADS * HEAD_DIM), BF16),
        compiler_params=_cparams(("parallel", "parallel")),
        name="nsa_selected",
    )(counts, ids, proj3, proj3, kv_t, sel, bias)


def _diff_kernel_t(slope_ref, dl_ref, g_ref, q_ref, kv_ref, kvt_ref, bias_ref, o_ref, *, lam_init):
    tq = q_ref.shape[1]
    hp = pl.program_id(1)
    i = pl.program_id(2)
    dl = dl_ref[...]
    lam = (jnp.exp(jnp.sum(dl[0:1] * dl[1:2], axis=-1, keepdims=True))
           - jnp.exp(jnp.sum(dl[2:3] * dl[3:4], axis=-1, keepdims=True)) + lam_init)
    lane = _lane((tq, LANES))
    streams = []
    for hh in range(2):
        q = q_ref[0, :, hh * LANES:(hh + 1) * LANES]
        zero = jnp.zeros_like(q)
        qs = jnp.concatenate([jnp.where(lane < DIFF_DK, q, zero),
                              jnp.where(lane >= DIFF_DK, q, zero)], axis=0)
        slope_lane = jnp.full((1, 2 * tq), slope_ref[hp * 2 + hh], F32)
        streams.append((qs, hh * LANES, slope_lane, functools.partial(lambda s, v: bias_ref[s, v], hh)))
    outs = []
    for l, acc in _flash_t(streams, kv_ref, kvt_ref, i, tq=tq, tk=TK_DENSE):
        o = acc / l
        w = o[:, :tq] - lam * o[:, tq:]
        ms = jnp.sum(w * w, axis=0, keepdims=True) * (1.0 / HEAD_DIM)
        outs.append(w * lax.rsqrt(ms + LN_EPS) * (1.0 - lam_init))
    o_ref[0] = (_heads_to_rows(outs) * g_ref[...]).astype(BF16)


def _diff_attn_t(proj3, kv_t, slopes, diff_lambda, subln, lam_init):
    b, t, _ = proj3.shape
    g_ext = jnp.tile(subln.reshape(1, HEAD_DIM).astype(F32), (1, 2))
    bias = _flash_bias([[s, s] for s in slopes], TQ_DENSE, TK_DENSE)
    return pl.pallas_call(
        functools.partial(_diff_kernel_t, lam_init=lam_init),
        grid=(b, B_HEADS // 2, t // TQ_DENSE),
        in_specs=[pl.BlockSpec(memory_space=pltpu.SMEM),
                  pl.BlockSpec((4, DIFF_DK), lambda bi, h, i: (0, 0)),
                  pl.BlockSpec((1, LANES), lambda bi, h, i: (0, 0)),
                  pl.BlockSpec((1, TQ_DENSE, 2 * LANES), lambda bi, h, i: (bi, i, QB_BLK // 2 + h)),
                  pl.BlockSpec((1, t, 2 * LANES), lambda bi, h, i: (bi, 0, KVB_BLK // 2 + h)),
                  pl.BlockSpec((2 * LANES, t), lambda bi, h, i: (h, bi)),
                  pl.BlockSpec((2,) + bias.shape[1:], lambda bi, h, i: (h, 0, 0, 0))],
        out_specs=pl.BlockSpec((1, TQ_DENSE, 2 * HEAD_DIM), lambda bi, h, i: (bi, i, h)),
        out_shape=jax.ShapeDtypeStruct((b, t, B_HEADS * HEAD_DIM), BF16),
        compiler_params=_cparams(("parallel", "parallel", "parallel")),
        name="diff_attn",
    )(jnp.asarray(slopes, F32), diff_lambda.astype(F32), g_ext, proj3, proj3, kv_t, bias)


def _outproj_kernel(oc_ref, os_ref, ow_ref, gl_ref, ob_ref, osw_ref, x_ref, w_ref, ex_ref, g_ref, b_ref, o_ref):
    hi, lo = _split2(_sigmoid(gl_ref[...]))
    oa = None
    for j, ref in enumerate((oc_ref, os_ref, ow_ref)):
        gate = _dot(hi, ex_ref[j]) + _dot(lo, ex_ref[j])
        term = gate * ref[...].astype(F32)
        oa = term if oa is None else oa + term
    cat = jnp.concatenate([oa.astype(BF16), ob_ref[...], osw_ref[...]], axis=1)
    y = DN_ALPHA * x_ref[...] + _dot(cat, w_ref[...])
    o_ref[...] = _layer_norm(y, g_ref[...], b_ref[...])


def _outproj(o_c, o_s, o_w, gl, o_b, o_sw, x2d, w_out, ln_g, ln_b):
    n = x2d.shape[0]
    ex = np.zeros((3, LANES, A_HEADS * HEAD_DIM), np.float32)
    for j in range(3):
        for h in range(A_HEADS):
            ex[j, h * 3 + j, h * HEAD_DIM:(h + 1) * HEAD_DIM] = 1.0
    row = lambda w: pl.BlockSpec((TM, w), lambda i: (i, 0))
    full = lambda s: pl.BlockSpec(s, lambda i: (0,) * len(s))
    return pl.pallas_call(
        _outproj_kernel,
        grid=(n // TM,),
        in_specs=[row(256), row(256), row(256), row(LANES), row(256), row(512), row(D_MODEL),
                  full((D_MODEL, D_MODEL)), full(ex.shape), full((1, D_MODEL)), full((1, D_MODEL))],
        out_specs=row(D_MODEL),
        out_shape=jax.ShapeDtypeStruct((n, D_MODEL), F32),
        compiler_params=_cparams(("parallel",)),
        name="out_proj_ln",
    )(o_c, o_s, o_w, gl, o_b, o_sw, x2d, w_out.astype(BF16), jnp.asarray(ex, BF16),
      ln_g.reshape(1, -1), ln_b.reshape(1, -1))


def _ffn_kernel(x_ref, wg_ref, wu_ref, wd_ref, o_ref):
    xb = x_ref[...].astype(BF16)
    acc = None
    for c in range(D_FF_DENSE // FFN_CHUNK):
        cols = slice(c * FFN_CHUNK, (c + 1) * FFN_CHUNK)
        g = _dot(xb, wg_ref[:, cols])
        u = _dot(xb, wu_ref[:, cols])
        y = _dot((g * _sigmoid(g) * u).astype(BF16), wd_ref[cols, :])
        acc = y if acc is None else acc + y
    o_ref[...] = acc


def _ffn(x2d, wg, wu, wd):
    n = x2d.shape[0]
    resident = lambda shape: pl.BlockSpec(shape, lambda i: (0, 0), pipeline_mode=pl.Buffered(1))
    return pl.pallas_call(
        _ffn_kernel,
        grid=(n // TM,),
        in_specs=[pl.BlockSpec((TM, D_MODEL), lambda i: (i, 0)),
                  resident((D_MODEL, D_FF_DENSE)), resident((D_MODEL, D_FF_DENSE)), resident((D_FF_DENSE, D_MODEL))],
        out_specs=pl.BlockSpec((TM, D_MODEL), lambda i: (i, 0)),
        out_shape=jax.ShapeDtypeStruct((n, D_MODEL), F32),
        compiler_params=_cparams(("parallel",)),
        name="dense_swiglu",
    )(x2d, wg.astype(BF16), wu.astype(BF16), wd.astype(BF16))


def _router_kernel(x_ref, w_ref, tri_ref, xb_ref, gd_ref, scol_ref, srow_ref, base_ref, cnt_ref):
    tt = x_ref.shape[0]
    x = x_ref[...]
    xb_ref[...] = x.astype(BF16)
    xh, xl = _split2(x)
    wh, wl = _split2(w_ref[...])
    logits = _dot(xh, wh) + _dot(xh, wl) + _dot(xl, wh)
    lane = _lane((tt, LANES))
    lf = lane.astype(F32)
    logits = jnp.where(lane < N_EXPERTS, logits, -jnp.inf)
    m1 = jnp.max(logits, axis=-1, keepdims=True)
    i1 = jnp.min(jnp.where(logits == m1, lf, float(LANES)), axis=-1, keepdims=True)
    rest = jnp.where(lf == i1, -jnp.inf, logits)
    m2 = jnp.max(rest, axis=-1, keepdims=True)
    i2 = jnp.min(jnp.where(rest == m2, lf, float(LANES)), axis=-1, keepdims=True)
    e2 = jnp.exp(m2 - m1)
    g1 = 1.0 / (1.0 + e2)
    g2 = e2 / (1.0 + e2)
    first, second = lf == i1, lf == i2
    gd_ref[...] = jnp.where(first, g1, 0.0) + jnp.where(second, g2, 0.0)
    ind = jnp.where(first, 1.0, jnp.where(second, 1.0, 0.0))
    tri = tri_ref[...]
    run = jnp.zeros((1, LANES), F32)
    slots, bases = [], []
    for tb in range(tt // MOE_TB):
        blk = ind[tb * MOE_TB:(tb + 1) * MOE_TB]
        rank = _dot(tri, blk.astype(BF16)) + run
        slots.append(jnp.where(blk > 0.5, rank, -1.0))
        bases.append(run)
        run = run + jnp.sum(blk, axis=0, keepdims=True)
    slot = jnp.concatenate(slots, axis=0)
    scol_ref[...] = slot
    srow_ref[...] = jnp.transpose(slot)[0:N_EXPERTS]
    base_ref[0] = jnp.concatenate(bases, axis=0)
    cnt_ref[0] = jnp.broadcast_to(run, (8, LANES))


def _route(x2d, w_router):
    n = x2d.shape[0]
    nt = n // MOE_TILE
    ntb = MOE_TILE // MOE_TB
    w = jnp.zeros((D_MODEL, LANES), F32).at[:, :N_EXPERTS].set(w_router)
    tri = jnp.asarray(np.tril(np.ones((MOE_TB, MOE_TB), np.float32), -1), BF16)
    return pl.pallas_call(
        _router_kernel,
        grid=(nt,),
        in_specs=[pl.BlockSpec((MOE_TILE, D_MODEL), lambda i: (i, 0)),
                  pl.BlockSpec((D_MODEL, LANES), lambda i: (0, 0)),
                  pl.BlockSpec((MOE_TB, MOE_TB), lambda i: (0, 0))],
        out_specs=[pl.BlockSpec((MOE_TILE, D_MODEL), lambda i: (i, 0)),
                   pl.BlockSpec((MOE_TILE, LANES), lambda i: (i, 0)),
                   pl.BlockSpec((MOE_TILE, LANES), lambda i: (i, 0)),
                   pl.BlockSpec((N_EXPERTS, MOE_TILE), lambda i: (0, i)),
                   pl.BlockSpec((1, ntb, LANES), lambda i: (i, 0, 0)),
                   pl.BlockSpec((1, 8, LANES), lambda i: (i, 0, 0))],
        out_shape=[jax.ShapeDtypeStruct((n, D_MODEL), BF16),
                   jax.ShapeDtypeStruct((n, LANES), F32),
                   jax.ShapeDtypeStruct((n, LANES), F32),
                   jax.ShapeDtypeStruct((N_EXPERTS, n), F32),
                   jax.ShapeDtypeStruct((nt, ntb, LANES), F32),
                   jax.ShapeDtypeStruct((nt, 8, LANES), F32)],
        compiler_params=_cparams(("parallel",)),
        name="moe_router",
    )(x2d, w, tri)


MOE_GATHER_W = MOE_TB + 16
MOE_SCATTER_W = MOE_TB + 8
MOE_CAP = MOE_TILE + 512


def _moe_kernel(cnt_ref, base_ref, xb_ref, srow_ref, scol_ref, gd_ref, wg_ref, wu_ref, wd_ref,
                o_ref, xc_ref, yc_ref):
    i, e, c = pl.program_id(0), pl.program_id(1), pl.program_id(2)
    nc = pl.num_programs(2)
    ntb = MOE_TILE // MOE_TB
    count = cnt_ref[i * N_EXPERTS + e]
    n_full = jnp.right_shift(count * MOE_ROWS_RECIP[0], MOE_ROWS_RECIP[1])
    rem = count - n_full * MOE_ROWS

    @pl.when((i == 0) & (e == 0) & (c == 0))
    def _():
        yc_ref[...] = jnp.zeros_like(yc_ref)

    @pl.when((e == 0) & (c == 0))
    def _():
        o_ref[...] = jnp.zeros_like(o_ref)

    @pl.when(c == 0)
    def _():
        xc_ref[...] = jnp.zeros_like(xc_ref)
        srow = srow_ref[pl.ds(e, 1), :]
        rid = lax.broadcasted_iota(jnp.int32, (MOE_GATHER_W, MOE_TB), 0).astype(F32)
        for tb in range(ntb):
            b0 = base_ref[(i * ntb + tb) * N_EXPERTS + e]
            b_al = pl.multiple_of(jnp.left_shift(jnp.right_shift(b0, 4), 4), 16)
            rel = srow[:, tb * MOE_TB:(tb + 1) * MOE_TB] - b_al.astype(F32)
            onehot = jnp.where(rel == rid, 1.0, 0.0).astype(BF16)
            rows = _dot(onehot, xb_ref[tb * MOE_TB:(tb + 1) * MOE_TB, :])
            cur = xc_ref[pl.ds(b_al, MOE_GATHER_W), :].astype(F32)
            xc_ref[pl.ds(b_al, MOE_GATHER_W), :] = (cur + rows).astype(BF16)

    def expert_rows(r0, n_rows):
        xr = xc_ref[pl.ds(r0, n_rows), :]
        g = _dot(xr, wg_ref[0])
        u = _dot(xr, wu_ref[0])
        y = _dot((g * _sigmoid(g) * u).astype(BF16), wd_ref[0])
        prev = yc_ref[pl.ds(r0, n_rows), :]
        yc_ref[pl.ds(r0, n_rows), :] = jnp.where(c == 0, y, prev + y)

    def rows_body(rb, carry):
        expert_rows(pl.multiple_of(rb * MOE_ROWS, MOE_ROW_ALIGN), MOE_ROWS)
        return carry

    lax.fori_loop(0, n_full + jnp.where(rem > MOE_ROWS // 2, 1, 0), rows_body, 0)

    @pl.when((rem > 0) & (rem <= MOE_ROWS // 2))
    def _():
        expert_rows(pl.multiple_of(n_full * MOE_ROWS, MOE_ROW_ALIGN), MOE_ROWS // 2)

    @pl.when(c == nc - 1)
    def _():
        lane = _lane((MOE_TB, LANES))
        cid = lax.broadcasted_iota(jnp.int32, (MOE_TB, MOE_SCATTER_W), 1).astype(F32)
        for tb in range(ntb):
            b0 = base_ref[(i * ntb + tb) * N_EXPERTS + e]
            b_al = pl.multiple_of(jnp.left_shift(jnp.right_shift(b0, 3), 3), 8)
            rows = slice(tb * MOE_TB, (tb + 1) * MOE_TB)
            slot = jnp.sum(jnp.where(lane == e, scol_ref[rows, :], 0.0), axis=-1, keepdims=True)
            gate = jnp.sum(jnp.where(lane == e, gd_ref[rows, :], 0.0), axis=-1, keepdims=True)
            onehot = jnp.where(slot - b_al.astype(F32) == cid, 1.0, 0.0).astype(BF16)
            y_win = yc_ref[pl.ds(b_al, MOE_SCATTER_W), :].astype(BF16)
            o_ref[rows, :] += gate * _dot(onehot, y_win)


def _moe(xb, srow, scol, gd, cnt, base, wg, wu, wd):
    n = xb.shape[0]
    nt = n // MOE_TILE
    nc = D_FF_EXPERT // MOE_CHUNK
    grid_spec = pltpu.PrefetchScalarGridSpec(
        num_scalar_prefetch=2,
        grid=(nt, N_EXPERTS, nc),
        in_specs=[pl.BlockSpec((MOE_TILE, D_MODEL), lambda i, e, c, *_: (i, 0), pipeline_mode=pl.Buffered(1)),
                  pl.BlockSpec((N_EXPERTS, MOE_TILE), lambda i, e, c, *_: (0, i)),
                  pl.BlockSpec((MOE_TILE, LANES), lambda i, e, c, *_: (i, 0)),
                  pl.BlockSpec((MOE_TILE, LANES), lambda i, e, c, *_: (i, 0)),
                  pl.BlockSpec((1, D_MODEL, MOE_CHUNK), lambda i, e, c, *_: (e, 0, c)),
                  pl.BlockSpec((1, D_MODEL, MOE_CHUNK), lambda i, e, c, *_: (e, 0, c)),
                  pl.BlockSpec((1, MOE_CHUNK, D_MODEL), lambda i, e, c, *_: (e, c, 0))],
        out_specs=pl.BlockSpec((MOE_TILE, D_MODEL), lambda i, e, c, *_: (i, 0), pipeline_mode=pl.Buffered(1)),
        scratch_shapes=[pltpu.VMEM((MOE_CAP, D_MODEL), BF16), pltpu.VMEM((MOE_CAP, D_MODEL), F32)],
    )
    return pl.pallas_call(
        _moe_kernel,
        grid_spec=grid_spec,
        out_shape=jax.ShapeDtypeStruct((n, D_MODEL), F32),
        compiler_params=_cparams(("arbitrary", "arbitrary", "arbitrary")),
        name="moe_experts",
    )(cnt, base, xb, srow, scol, gd, wg.astype(BF16), wu.astype(BF16), wd.astype(BF16))


def _ple_kernel(x_ref, f_ref, p_ref, wg_ref, wp_ref, g2_ref, b2_ref, g3_ref, b3_ref, o_ref):
    x2 = _layer_norm(DN_ALPHA * x_ref[...] + f_ref[...], g2_ref[...], b2_ref[...])
    e = _sigmoid(_dot(x2.astype(BF16), wg_ref[...])) * _dot(p_ref[...].astype(BF16), wp_ref[...])
    o_ref[...] = _layer_norm(DN_ALPHA * x2 + e, g3_ref[...], b3_ref[...])


def _ple(x2d, f2d, p2d, w_gate, w_proj, g2, b2, g3, b3):
    n = x2d.shape[0]
    row = lambda w: pl.BlockSpec((TM, w), lambda i: (i, 0))
    full = lambda s: pl.BlockSpec(s, lambda i: (0,) * len(s))
    vec = full((1, D_MODEL))
    return pl.pallas_call(
        _ple_kernel,
        grid=(n // TM,),
        in_specs=[row(D_MODEL), row(D_MODEL), row(PLE_DIM), full((D_MODEL, D_MODEL)), full((PLE_DIM, D_MODEL)),
                  vec, vec, vec, vec],
        out_specs=row(D_MODEL),
        out_shape=jax.ShapeDtypeStruct((n, D_MODEL), F32),
        compiler_params=_cparams(("parallel",)),
        name="ln_ple_ln",
    )(x2d, f2d, p2d, w_gate.astype(BF16), w_proj.astype(BF16),
      g2.reshape(1, -1), b2.reshape(1, -1), g3.reshape(1, -1), b3.reshape(1, -1))


def _mixer(x2d, b, t, i, w_in, cmp_pos, cmp_w1, cmp_w2, diff_lambda, diff_subln, sinks):
    slopes_a, slopes_b, slopes_c = _alibi_slopes()
    proj, gl, kv_t = _project(x2d, _build_w_in(w_in))
    proj3 = proj.reshape(b, t, N_MAIN)
    kvc, kvc_t = _compress(proj3, cmp_pos, cmp_w1, cmp_w2)
    o_c, sel = _cmp_attn(proj3, kvc, kvc_t, slopes_a)
    o_s = _selected_attn_t(proj3, kv_t, sel, slopes_a)
    o_w = _window_attn_t(proj3, kv_t, [slopes_a], q_blk=QA_BLK, kv_blk=KVA_BLK + 2, window=NSA_WINDOW,
                         name="nsa_window")
    lam_init = 0.8 - 0.6 * math.exp(-0.3 * i)
    o_b = _diff_attn_t(proj3, kv_t, slopes_b, diff_lambda, diff_subln, lam_init)
    r_c = C_HEADS // C_KV_HEADS
    o_sw = _window_attn_t(proj3, kv_t, [slopes_c[g * r_c:(g + 1) * r_c] for g in range(C_KV_HEADS)],
                          q_blk=QC_BLK, kv_blk=KVC_BLK, window=SWA_WINDOW, sinks=sinks, name="swa_gqa")
    flat = lambda a: a.reshape(b * t, a.shape[-1])
    return flat(o_c), flat(o_s), flat(o_w), gl, flat(o_b), flat(o_sw)


def kernel(x, p, w_in, cmp_pos, cmp_w1, cmp_w2, diff_lambda, diff_subln, sinks, w_out, ln1_g, ln1_b,
           ffn_w_gate, ffn_w_up, ffn_w_down, moe_router, moe_w_gate, moe_w_up, moe_w_down, ln2_g, ln2_b,
           ple_gate, ple_proj, ln3_g, ln3_b):
    b, t, d = x.shape
    n = b * t
    x2d = x.reshape(n, d)
    for i in range(DEPTH):
        heads = _mixer(x2d, b, t, i, w_in[i], cmp_pos[i], cmp_w1[i], cmp_w2[i],
                       diff_lambda[i], diff_subln[i], sinks[i])
        x1 = _outproj(*heads, x2d, w_out[i], ln1_g[i], ln1_b[i])
        if i % 2 == 0:
            f = _ffn(x1, ffn_w_gate[i // 2], ffn_w_up[i // 2], ffn_w_down[i // 2])
        else:
            xb, gd, scol, srow, base, cnt = _route(x1, moe_router[i // 2])
            cnt_i = cnt[:, 0, :N_EXPERTS].astype(jnp.int32).reshape(-1)
            base_i = base[:, :, :N_EXPERTS].astype(jnp.int32).reshape(-1)
            f = _moe(xb, srow, scol, gd, cnt_i, base_i,
                     moe_w_gate[i // 2], moe_w_up[i // 2], moe_w_down[i // 2])
        x2d = _ple(x1, f, p[i].reshape(n, PLE_DIM), ple_gate[i], ple_proj[i],
                   ln2_g[i], ln2_b[i], ln3_g[i], ln3_b[i])
    return x2d.reshape(b, t, d)
```

```python
import functools
import math

import jax
import jax.numpy as jnp
import numpy as np
from jax import lax
from jax.experimental import pallas as pl
from jax.experimental.pallas import tpu as pltpu

F32 = jnp.float32
BF16 = jnp.bfloat16

D_MODEL = 1024
HEAD_DIM = 64
A_HEADS = 4
B_HEADS = 4
C_HEADS = 8
C_KV_HEADS = 2
DIFF_DK = 32
CMP_LEN = 32
CMP_STRIDE = 16
CMP_HIDDEN = 128
SLC_BLOCK = 64
SLC_SHIFT = 6
SLC_TOPN = 16
NSA_WINDOW = 512
SWA_WINDOW = 128
N_EXPERTS = 8
D_FF_EXPERT = 3584
D_FF_DENSE = 2816
PLE_DIM = 256
LN_EPS = 1e-5
NEG_INF = -1e30
FORCE_SCORE = 1e9
DEPTH = 2
DN_ALPHA = (2 * DEPTH) ** 0.25

LANES = 128
QA_BLK, QB_BLK, QC_BLK, KVB_BLK, KVC_BLK, KVA_BLK, N_MAIN_BLK = 0, 4, 8, 16, 20, 22, 25
N_MAIN = N_MAIN_BLK * LANES

LOG2E = 1.4426950408889634
TQ = 256
TQ_DENSE = 512
TK_DENSE = 512
TM = 512
FFN_CHUNK = 1408
MOE_TILE = 2048
MOE_ROWS = 576
MOE_ROWS_RECIP = (1821, 20)
MOE_ROW_ALIGN = 16
MOE_TB = 256
MOE_CHUNK = 896


def _cparams(sem, vmem_mib):
    return pltpu.CompilerParams(dimension_semantics=sem, vmem_limit_bytes=vmem_mib * 1024 * 1024)


def _alibi_slopes():
    n = A_HEADS + B_HEADS + C_HEADS
    s = [LOG2E * 2.0 ** (-8.0 * i / n) for i in range(1, n + 1)]
    rest = s[C_HEADS:]
    return rest[0::2], rest[1::2], s[:C_HEADS]


def _dot(a, b):
    return jnp.dot(a, b, preferred_element_type=F32)


def _dot_nt(a, b):
    return lax.dot_general(a, b, (((1,), (1,)), ((), ())), preferred_element_type=F32)


def _split2(x):
    hi = x.astype(BF16)
    lo = (x - hi.astype(F32)).astype(BF16)
    return hi, lo


def _layer_norm(y, g, b):
    mu = jnp.mean(y, axis=-1, keepdims=True)
    d = y - mu
    var = jnp.mean(d * d, axis=-1, keepdims=True)
    return d * lax.rsqrt(var + LN_EPS) * g + b


def _sigmoid(x):
    return 1.0 / (1.0 + jnp.exp(-x))


def _lane(shape):
    return lax.broadcasted_iota(jnp.int32, shape, 1)


def _heads_to_rows(parts):
    out = [jnp.transpose(jnp.concatenate([a, b], axis=0)) for a, b in zip(parts[0::2], parts[1::2])]
    return out[0] if len(out) == 1 else jnp.concatenate(out, axis=1)


def _proj_kernel(x_ref, w_ref, o_ref, g_ref, t_ref):
    acc = _dot(x_ref[...].astype(BF16), w_ref[...])
    o_ref[...] = acc[:, :N_MAIN].astype(BF16)
    g_ref[...] = acc[:, N_MAIN:]
    t_ref[...] = jnp.transpose(acc[:, KVB_BLK * LANES:N_MAIN]).astype(BF16)


def _project(x2d, w_ext):
    n = x2d.shape[0]
    n_kv = N_MAIN - KVB_BLK * LANES
    return pl.pallas_call(
        _proj_kernel,
        grid=(n // TM,),
        in_specs=[pl.BlockSpec((TM, D_MODEL), lambda i: (i, 0)),
                  pl.BlockSpec((D_MODEL, N_MAIN + LANES), lambda i: (0, 0))],
        out_specs=[pl.BlockSpec((TM, N_MAIN), lambda i: (i, 0)),
                   pl.BlockSpec((TM, LANES), lambda i: (i, 0)),
                   pl.BlockSpec((n_kv, TM), lambda i: (0, i))],
        out_shape=[jax.ShapeDtypeStruct((n, N_MAIN), BF16),
                   jax.ShapeDtypeStruct((n, LANES), F32),
                   jax.ShapeDtypeStruct((n_kv, n), BF16)],
        compiler_params=_cparams(("parallel",), 32),
        name="in_proj",
    )(x2d, w_ext)


def _build_w_in(w):
    d = w.shape[0]
    o = np.cumsum([0, 256, 384, 12, 256, 256, 256, 512, 128, 128])
    q_a, kv_a, g_a, q_b, k_b, v_b, q_c, k_c, v_c = [w[:, o[i]:o[i + 1]] for i in range(9)]
    hd = HEAD_DIM

    def q_blocks(q, scale):
        q = (q * scale).astype(BF16).reshape(d, -1, hd)
        return jnp.pad(q, ((0, 0), (0, 0), (0, LANES - hd))).reshape(d, -1)

    def kv_blocks(k, v):
        return jnp.stack([k.reshape(d, -1, hd), v.reshape(d, -1, hd)], axis=2).astype(BF16).reshape(d, -1)

    return jnp.concatenate([q_blocks(q_a, LOG2E * hd ** -0.5), q_blocks(q_b, LOG2E * DIFF_DK ** -0.5),
                            q_blocks(q_c, LOG2E * hd ** -0.5), kv_blocks(k_b, v_b), kv_blocks(k_c, v_c),
                            kv_a.astype(BF16), jnp.pad(g_a.astype(BF16), ((0, 0), (0, LANES - 12)))], axis=1)


def _compress_kernel(ch_ref, wc_ref, pos_ref, w1_ref, w2_ref, o_ref, ot_ref):
    nck = ch_ref.shape[1]
    u = _dot(ch_ref[0], wc_ref[...])
    hs = []
    for j in range(2):
        pw = _dot(pos_ref[j].astype(BF16), w1_ref[j])[0:1]
        ua = u[:, (2 * j) * LANES:(2 * j + 1) * LANES]
        ub = u[:, (2 * j + 1) * LANES:(2 * j + 2) * LANES]
        pre = ua + pltpu.roll(ub, nck - 1, 0) + pw
        hs.append(0.5 * pre * (1.0 + jnp.tanh(0.7978845608028654 * (pre + 0.044715 * pre * pre * pre))))
    h = jnp.concatenate(hs, axis=1).astype(BF16)
    kvc = _dot(h, w2_ref[...])
    o_ref[0] = kvc.astype(BF16)
    ot_ref[0] = jnp.transpose(kvc).astype(BF16)


def _compress(proj3, cmp_pos, cmp_w1, cmp_w2):
    b, t, _ = proj3.shape
    nck = t // CMP_STRIDE
    kv = proj3[:, :, KVA_BLK * LANES:(KVA_BLK + 1) * LANES]
    chunks = kv.reshape(b, nck, CMP_STRIDE * LANES)
    w1 = cmp_w1.reshape(2, 2, CMP_STRIDE, HEAD_DIM, CMP_HIDDEN)
    z = jnp.zeros((CMP_STRIDE, HEAD_DIM, CMP_HIDDEN), F32)
    blocks = []
    for j in range(2):
        for half in range(2):
            pair = [w1[j, half], z] if j == 0 else [z, w1[j, half]]
            blocks.append(jnp.concatenate(pair, axis=1).reshape(CMP_STRIDE * LANES, CMP_HIDDEN))
    wc = jnp.concatenate(blocks, axis=1).astype(BF16)
    pos = jnp.zeros((2, 8, CMP_LEN * HEAD_DIM), F32).at[:, 0].set(cmp_pos.reshape(2, -1))
    z2 = jnp.zeros((CMP_HIDDEN, HEAD_DIM), F32)
    w2 = jnp.concatenate([jnp.concatenate([cmp_w2[0], z2], axis=1),
                          jnp.concatenate([z2, cmp_w2[1]], axis=1)], axis=0).astype(BF16)
    return pl.pallas_call(
        _compress_kernel,
        grid=(b,),
        in_specs=[pl.BlockSpec((1, nck, CMP_STRIDE * LANES), lambda i: (i, 0, 0)),
                  pl.BlockSpec(wc.shape, lambda i: (0, 0)),
                  pl.BlockSpec(pos.shape, lambda i: (0, 0, 0)),
                  pl.BlockSpec((2, CMP_LEN * HEAD_DIM, CMP_HIDDEN), lambda i: (0, 0, 0)),
                  pl.BlockSpec(w2.shape, lambda i: (0, 0))],
        out_specs=[pl.BlockSpec((1, nck, LANES), lambda i: (i, 0, 0)),
                   pl.BlockSpec((1, LANES, nck), lambda i: (i, 0, 0))],
        out_shape=[jax.ShapeDtypeStruct((b, nck, LANES), BF16),
                   jax.ShapeDtypeStruct((b, LANES, nck), BF16)],
        compiler_params=_cparams(("parallel",), 16),
        name="nsa_compress",
    )(chunks, wc, pos, cmp_w1.astype(BF16), w2)


def _cmp_attn_kernel_t(q_ref, kvc_ref, kvct_ref, ovt_ref, bias_ref, o_ref, sel_ref):
    tq = q_ref.shape[1]
    ncp = kvc_ref.shape[1]
    rows = A_HEADS * tq
    t0 = pl.program_id(1) * tq
    qs = _stack_heads(q_ref[0], A_HEADS)
    u = _dot_nt(kvc_ref[0], qs) + bias_ref[...]
    n = lax.broadcasted_iota(jnp.int32, (ncp, rows), 0)
    block_end = jnp.where(n == ncp - 1, jnp.int32(1 << 30), n * CMP_STRIDE + (CMP_LEN - 1))
    keep = block_end <= jnp.bitwise_and(_lane((ncp, rows)), tq - 1) + t0
    sh = jnp.where(keep, u, NEG_INF)
    m = jnp.max(sh, axis=0, keepdims=True)
    e = jnp.where(keep, jnp.exp2(sh - m), 0.0)
    p = e / jnp.maximum(jnp.sum(e, axis=0, keepdims=True), 1e-30)
    o_t = _dot(kvct_ref[0, HEAD_DIM:, :], p.astype(BF16))
    o_ref[0] = _heads_to_rows([o_t[:, h * tq:(h + 1) * tq] for h in range(A_HEADS)]).astype(BF16)

    psum = p[:, 0:tq] + p[:, tq:2 * tq] + p[:, 2 * tq:3 * tq] + p[:, 3 * tq:4 * tq]
    hi = psum.astype(BF16)
    r1 = psum - hi.astype(F32)
    mid = r1.astype(BF16)
    lo = (r1 - mid.astype(F32)).astype(BF16)
    ovt = ovt_ref[...]
    imp = _dot(ovt, hi) + _dot(ovt, mid) + _dot(ovt, lo)
    nsl = ncp * CMP_STRIDE // SLC_BLOCK
    j = lax.broadcasted_iota(jnp.int32, (LANES, tq), 0)
    t_blk = jnp.right_shift(_lane((LANES, tq)) + t0, SLC_SHIFT)
    forced = (j == 0) | (j == t_blk) | (j == t_blk - 1)
    imp = jnp.where(forced, FORCE_SCORE, jnp.where(j > t_blk, -1.0, imp))
    work = jnp.where(j < nsl, imp, -3.0)
    jf = j.astype(F32)
    sel = jnp.zeros((LANES, tq), F32)
    for _ in range(SLC_TOPN):
        mx = jnp.max(work, axis=0, keepdims=True)
        first = jnp.min(jnp.where(work == mx, jf, float(LANES)), axis=0, keepdims=True)
        pick = jf == first
        sel = jnp.where(pick, 1.0, sel)
        work = jnp.where(pick, -4.0, work)
    sel_ref[0] = jnp.transpose(sel).astype(BF16)


def _cmp_attn(proj3, kvc, kvc_t, slopes):
    b, t, _ = proj3.shape
    ncp = kvc.shape[1]
    nsl = t // SLC_BLOCK
    c0 = np.arange(ncp)[None, :] * CMP_STRIDE
    s0 = np.arange(LANES)[:, None] * SLC_BLOCK
    ovt = np.clip(np.minimum(c0 + CMP_LEN, s0 + SLC_BLOCK) - np.maximum(c0, s0), 0, None) / CMP_LEN
    ovt[:, ncp - 1] = 0.0
    ovt[nsl:, :] = 0.0
    rel = np.arange(TQ)[None, :] - (np.arange(ncp)[:, None] * CMP_STRIDE + (CMP_LEN - 1))
    bias = np.concatenate([-s * rel for s in slopes], axis=1).astype(np.float32)
    return pl.pallas_call(
        _cmp_attn_kernel_t,
        grid=(b, t // TQ),
        in_specs=[pl.BlockSpec((1, TQ, A_HEADS * LANES), lambda bi, i: (bi, i, QA_BLK // A_HEADS)),
                  pl.BlockSpec((1, ncp, LANES), lambda bi, i: (bi, 0, 0)),
                  pl.BlockSpec((1, LANES, ncp), lambda bi, i: (bi, 0, 0)),
                  pl.BlockSpec((LANES, ncp), lambda bi, i: (0, 0)),
                  pl.BlockSpec((ncp, A_HEADS * TQ), lambda bi, i: (0, 0))],
        out_specs=[pl.BlockSpec((1, TQ, A_HEADS * HEAD_DIM), lambda bi, i: (bi, i, 0)),
                   pl.BlockSpec((1, TQ, LANES), lambda bi, i: (bi, i, 0))],
        out_shape=[jax.ShapeDtypeStruct((b, t, A_HEADS * HEAD_DIM), BF16),
                   jax.ShapeDtypeStruct((b, t, LANES), BF16)],
        compiler_params=_cparams(("parallel", "parallel"), 16),
        name="nsa_cmp_attn",
    )(proj3, kvc, kvc_t, jnp.asarray(ovt, BF16), jnp.asarray(bias))


def _stack_heads(q, n):
    return jnp.concatenate([q[:, h * LANES:(h + 1) * LANES] for h in range(n)], axis=0)


def _window_bias(slopes, tq, window):
    wk = window + tq
    n_var = -(-window // tq) + 1
    j = np.arange(wk)[:, None]
    q = np.arange(tq)[None, :]
    out = np.empty((len(slopes), n_var, wk, len(slopes[0]) * tq), np.float32)
    for v in range(n_var):
        d = v * tq - max(v * tq - window, 0) + q - j
        keep = (d >= 0) & (d < window)
        for g, row in enumerate(slopes):
            for r, slope in enumerate(row):
                out[g, v, :, r * tq:(r + 1) * tq] = np.where(keep, -slope * d, NEG_INF)
    return jnp.asarray(out)


def _window_kernel_t(*refs, r_heads, window, has_sink):
    if has_sink:
        sink_ref, q_ref, kv_ref, kvt_ref, bias_ref, o_ref = refs
    else:
        q_ref, kv_ref, kvt_ref, bias_ref, o_ref = refs
    g, i = pl.program_id(1), pl.program_id(2)
    tq = q_ref.shape[1]
    wk = window + tq
    start = pl.multiple_of(jnp.maximum(i * tq - window, 0), LANES)
    qs = _stack_heads(q_ref[0], r_heads)
    u = _dot_nt(kv_ref[0, pl.ds(start, wk), :], qs) + bias_ref[0, 0]
    m = jnp.max(u, axis=0, keepdims=True)
    if has_sink:
        sink = jnp.concatenate([jnp.full((1, tq), sink_ref[g * r_heads + r], F32) for r in range(r_heads)], axis=1)
        m = jnp.maximum(m, sink)
    p = jnp.exp2(u - m)
    den = jnp.sum(p, axis=0, keepdims=True)
    if has_sink:
        den = den + jnp.exp2(sink - m)
    o_t = _dot(kvt_ref[HEAD_DIM:, pl.ds(start, wk)], p.astype(BF16)) / den
    o_ref[0] = _heads_to_rows([o_t[:, r * tq:(r + 1) * tq] for r in range(r_heads)]).astype(BF16)


def _window_attn_t(proj3, kv_t, slopes, *, q_blk, kv_blk, window, sinks=None, name):
    b, t, _ = proj3.shape
    groups, r_heads = len(slopes), len(slopes[0])
    bias = _window_bias(slopes, TQ, window)
    n_var, wk = bias.shape[1], bias.shape[2]
    in_specs = [pl.BlockSpec((1, TQ, r_heads * LANES), lambda bi, g, i: (bi, i, q_blk // r_heads + g)),
                pl.BlockSpec((1, t, LANES), lambda bi, g, i: (bi, 0, kv_blk + g)),
                pl.BlockSpec((LANES, t), lambda bi, g, i: (kv_blk - KVB_BLK + g, bi)),
                pl.BlockSpec((1, 1, wk, r_heads * TQ), lambda bi, g, i: (g, jnp.minimum(i, n_var - 1), 0, 0))]
    args = [proj3, proj3, kv_t, bias]
    if sinks is not None:
        in_specs.insert(0, pl.BlockSpec(memory_space=pltpu.SMEM))
        args.insert(0, sinks.astype(F32) * LOG2E)
    return pl.pallas_call(
        functools.partial(_window_kernel_t, r_heads=r_heads, window=window, has_sink=sinks is not None),
        grid=(b, groups, t // TQ),
        in_specs=in_specs,
        out_specs=pl.BlockSpec((1, TQ, r_heads * HEAD_DIM), lambda bi, g, i: (bi, i, g)),
        out_shape=jax.ShapeDtypeStruct((b, t, groups * r_heads * HEAD_DIM), BF16),
        compiler_params=_cparams(("parallel", "parallel", "parallel"), 24),
        name=name,
    )(*args)


def _flash_bias(stream_slopes, tq, tk):
    ratio = tk // tq
    j = np.arange(tk)[:, None]
    q = np.arange(tq)[None, :]
    out = np.empty((len(stream_slopes), 1 + ratio, tk, len(stream_slopes[0]) * tq), np.float32)
    for s, slopes in enumerate(stream_slopes):
        for h, slope in enumerate(slopes):
            cols = slice(h * tq, (h + 1) * tq)
            out[s, 0, :, cols] = slope * j
            for r in range(ratio):
                out[s, 1 + r, :, cols] = np.where(j <= r * tq + q, slope * j, NEG_INF)
    return jnp.asarray(out)


def _flash_t(streams, kv_ref, kvt_ref, i, *, tq, tk, sel_t=None, interior=None):
    ratio = tk // tq
    n_full = lax.div(i, jnp.int32(ratio))
    rows = streams[0][0].shape[0]

    def tiles(kts, carry, edge):
        starts = [pl.multiple_of(kt * tk, tk) for kt in kts]
        scores = [[_dot_nt(kv_ref[0, pl.ds(k0, tk), s[1]:s[1] + LANES], s[0]) for s in streams]
                  for k0 in starts]
        state = [(m, l) for m, l, _ in carry]
        updates = []
        for n, kt in enumerate(kts):
            variant = 1 + i - kt * ratio if (edge and n == len(kts) - 1) else 0
            keep = None
            if sel_t is not None:
                blk = jnp.right_shift(lax.broadcasted_iota(jnp.int32, (tk, LANES), 0) + kt * tk, SLC_SHIFT)
                expand = jnp.where(_lane((tk, LANES)) == blk, 1.0, 0.0).astype(BF16)
                keep = _dot(expand, sel_t) > 0.5
            k0f = (kt * tk).astype(F32)
            row = []
            for s, (_, _, slope_lane, bias) in enumerate(streams):
                m, l = state[s]
                u = scores[n][s] + bias(variant)
                if keep is not None:
                    u = jnp.where(keep, u, NEG_INF)
                shift = slope_lane * k0f
                m_new = jnp.maximum(m, jnp.max(u, axis=0, keepdims=True) + shift)
                p = jnp.exp2(u - (m_new - shift))
                alpha = jnp.exp2(m - m_new)
                state[s] = (m_new, alpha * l + jnp.sum(p, axis=0, keepdims=True))
                row.append((alpha, p.astype(BF16)))
            updates.append(row)
        accs = [acc for _, _, acc in carry]
        for k0, row in zip(starts, updates):
            for s, (alpha, p) in enumerate(row):
                col = streams[s][1]
                kv_f = kvt_ref[col + HEAD_DIM:col + LANES, pl.ds(k0, tk)]
                accs[s] = alpha * accs[s] + _dot(kv_f, p)
        return tuple((m, l, acc) for (m, l), acc in zip(state, accs))

    init = tuple((jnp.full((1, rows), NEG_INF, F32), jnp.zeros((1, rows), F32), jnp.zeros((HEAD_DIM, rows), F32))
                 for _ in streams)
    if interior is None:
        count, tile_id = n_full, lambda j: j
    else:
        count, tile_id = interior
    carry = lax.fori_loop(0, jnp.right_shift(count, 1),
                          lambda j, c: tiles([tile_id(2 * j), tile_id(2 * j + 1)], c, False), init)
    carry = lax.cond(jnp.bitwise_and(count, 1) == 1,
                     lambda c: tiles([tile_id(count - 1), n_full], c, True),
                     lambda c: tiles([n_full], c, True), carry)
    return [(l, acc) for _, l, acc in carry]


def _selected_kernel_t(count_ref, ids_ref, q_ref, kv_ref, kvt_ref, sel_ref, bias_ref, o_ref, *, slopes):
    tq = q_ref.shape[1]
    i = pl.program_id(1)
    step = pl.program_id(0) * pl.num_programs(1) + i
    n_kt = kv_ref.shape[1] // TK_DENSE
    interior = (count_ref[step], lambda j: ids_ref[step * n_kt + j])
    sel_t = jnp.transpose(sel_ref[0].astype(F32)).astype(BF16)
    sel_t = jnp.concatenate([sel_t, sel_t], axis=1)
    streams = []
    for pair in range(A_HEADS // 2):
        qs = jnp.concatenate([q_ref[0, :, h * LANES:(h + 1) * LANES] for h in (2 * pair, 2 * pair + 1)], axis=0)
        slope_lane = jnp.concatenate([jnp.full((1, tq), slopes[h], F32) for h in (2 * pair, 2 * pair + 1)], axis=1)
        streams.append((qs, 0, slope_lane, functools.partial(lambda s, v: bias_ref[s, v], pair)))
    outs = []
    for l, acc in _flash_t(streams, kv_ref, kvt_ref, i, tq=tq, tk=TK_DENSE, sel_t=sel_t, interior=interior):
        o_t = acc / l
        outs += [o_t[:, :tq], o_t[:, tq:]]
    o_ref[0] = _heads_to_rows(outs).astype(BF16)


def _selected_attn_t(proj3, kv_t, sel, slopes):
    b, t, _ = proj3.shape
    kv_blk = KVA_BLK + 1
    bias = _flash_bias([slopes[0:2], slopes[2:4]], TQ_DENSE, TK_DENSE)
    nq, nk, per = t // TQ_DENSE, t // TK_DENSE, TK_DENSE // SLC_BLOCK
    picked = jnp.max(sel.reshape(b, nq, TQ_DENSE, LANES)[..., :nk * per], axis=2) > 0
    active = jnp.any(picked.reshape(b, nq, nk, per), axis=-1)
    past = jnp.arange(nk)[None, :] < (jnp.arange(nq) * TQ_DENSE // TK_DENSE)[:, None]
    active = active & past[None]
    ids = jnp.argsort(~active, axis=-1, stable=True).astype(jnp.int32).reshape(-1)
    counts = jnp.sum(active, axis=-1).astype(jnp.int32).reshape(-1)
    grid_spec = pltpu.PrefetchScalarGridSpec(
        num_scalar_prefetch=2,
        grid=(b, nq),
        in_specs=[pl.BlockSpec((1, TQ_DENSE, A_HEADS * LANES), lambda bi, i, *_: (bi, i, QA_BLK // A_HEADS)),
                  pl.BlockSpec((1, t, LANES), lambda bi, i, *_: (bi, 0, kv_blk)),
                  pl.BlockSpec((LANES, t), lambda bi, i, *_: (kv_blk - KVB_BLK, bi)),
                  pl.BlockSpec((1, TQ_DENSE, LANES), lambda bi, i, *_: (bi, i, 0)),
                  pl.BlockSpec(bias.shape, lambda bi, i, *_: (0, 0, 0, 0), pipeline_mode=pl.Buffered(1))],
        out_specs=pl.BlockSpec((1, TQ_DENSE, A_HEADS * HEAD_DIM), lambda bi, i, *_: (bi, i, 0)),
    )
    return pl.pallas_call(
        functools.partial(_selected_kernel_t, slopes=slopes),
        grid_spec=grid_spec,
        out_shape=jax.ShapeDtypeStruct((b, t, A_HEADS * HEAD_DIM), BF16),
        compiler_params=_cparams(("parallel", "parallel"), 40),
        name="nsa_selected",
    )(counts, ids, proj3, proj3, kv_t, sel, bias)


def _diff_kernel_t(slope_ref, dl_ref, g_ref, q_ref, kv_ref, kvt_ref, bias_ref, o_ref, *, lam_init):
    tq = q_ref.shape[1]
    hp = pl.program_id(1)
    i = pl.program_id(2)
    dl = dl_ref[...]
    lam = (jnp.exp(jnp.sum(dl[0:1] * dl[1:2], axis=-1, keepdims=True))
           - jnp.exp(jnp.sum(dl[2:3] * dl[3:4], axis=-1, keepdims=True)) + lam_init)
    lane = _lane((tq, LANES))
    streams = []
    for hh in range(2):
        q = q_ref[0, :, hh * LANES:(hh + 1) * LANES]
        zero = jnp.zeros_like(q)
        qs = jnp.concatenate([jnp.where(lane < DIFF_DK, q, zero),
                              jnp.where(lane >= DIFF_DK, q, zero)], axis=0)
        slope_lane = jnp.full((1, 2 * tq), slope_ref[hp * 2 + hh], F32)
        streams.append((qs, hh * LANES, slope_lane, functools.partial(lambda s, v: bias_ref[s, v], hh)))
    outs = []
    for l, acc in _flash_t(streams, kv_ref, kvt_ref, i, tq=tq, tk=TK_DENSE):
        o = acc / l
        w = o[:, :tq] - lam * o[:, tq:]
        ms = jnp.sum(w * w, axis=0, keepdims=True) * (1.0 / HEAD_DIM)
        outs.append(w * lax.rsqrt(ms + LN_EPS) * (1.0 - lam_init))
    o_ref[0] = (_heads_to_rows(outs) * g_ref[...]).astype(BF16)


def _diff_attn_t(proj3, kv_t, slopes, diff_lambda, subln, lam_init):
    b, t, _ = proj3.shape
    g_ext = jnp.tile(subln.reshape(1, HEAD_DIM).astype(F32), (1, 2))
    bias = _flash_bias([[s, s] for s in slopes], TQ_DENSE, TK_DENSE)
    return pl.pallas_call(
        functools.partial(_diff_kernel_t, lam_init=lam_init),
        grid=(b, B_HEADS // 2, t // TQ_DENSE),
        in_specs=[pl.BlockSpec(memory_space=pltpu.SMEM),
                  pl.BlockSpec((4, DIFF_DK), lambda bi, h, i: (0, 0)),
                  pl.BlockSpec((1, LANES), lambda bi, h, i: (0, 0)),
                  pl.BlockSpec((1, TQ_DENSE, 2 * LANES), lambda bi, h, i: (bi, i, QB_BLK // 2 + h)),
                  pl.BlockSpec((1, t, 2 * LANES), lambda bi, h, i: (bi, 0, KVB_BLK // 2 + h)),
                  pl.BlockSpec((2 * LANES, t), lambda bi, h, i: (h, bi)),
                  pl.BlockSpec((2,) + bias.shape[1:], lambda bi, h, i: (h, 0, 0, 0))],
        out_specs=pl.BlockSpec((1, TQ_DENSE, 2 * HEAD_DIM), lambda bi, h, i: (bi, i, h)),
        out_shape=jax.ShapeDtypeStruct((b, t, B_HEADS * HEAD_DIM), BF16),
        compiler_params=_cparams(("parallel", "parallel", "parallel"), 48),
        name="diff_attn",
    )(jnp.asarray(slopes, F32), diff_lambda.astype(F32), g_ext, proj3, proj3, kv_t, bias)


def _outproj_kernel(oc_ref, os_ref, ow_ref, gl_ref, ob_ref, osw_ref, x_ref, w_ref, ex_ref, g_ref, b_ref, o_ref):
    hi, lo = _split2(_sigmoid(gl_ref[...]))
    oa = None
    for j, ref in enumerate((oc_ref, os_ref, ow_ref)):
        gate = _dot(hi, ex_ref[j]) + _dot(lo, ex_ref[j])
        term = gate * ref[...].astype(F32)
        oa = term if oa is None else oa + term
    cat = jnp.concatenate([oa.astype(BF16), ob_ref[...], osw_ref[...]], axis=1)
    y = DN_ALPHA * x_ref[...] + _dot(cat, w_ref[...])
    o_ref[...] = _layer_norm(y, g_ref[...], b_ref[...])


def _outproj(o_c, o_s, o_w, gl, o_b, o_sw, x2d, w_out, ln_g, ln_b):
    n = x2d.shape[0]
    ex = np.zeros((3, LANES, A_HEADS * HEAD_DIM), np.float32)
    for j in range(3):
        for h in range(A_HEADS):
            ex[j, h * 3 + j, h * HEAD_DIM:(h + 1) * HEAD_DIM] = 1.0
    row = lambda w: pl.BlockSpec((TM, w), lambda i: (i, 0))
    full = lambda s: pl.BlockSpec(s, lambda i: (0,) * len(s))
    return pl.pallas_call(
        _outproj_kernel,
        grid=(n // TM,),
        in_specs=[row(256), row(256), row(256), row(LANES), row(256), row(512), row(D_MODEL),
                  full((D_MODEL, D_MODEL)), full(ex.shape), full((1, D_MODEL)), full((1, D_MODEL))],
        out_specs=row(D_MODEL),
        out_shape=jax.ShapeDtypeStruct((n, D_MODEL), F32),
        compiler_params=_cparams(("parallel",), 24),
        name="out_proj_ln",
    )(o_c, o_s, o_w, gl, o_b, o_sw, x2d, w_out.astype(BF16), jnp.asarray(ex, BF16),
      ln_g.reshape(1, -1), ln_b.reshape(1, -1))


def _ffn_kernel(x_ref, wg_ref, wu_ref, wd_ref, o_ref):
    xb = x_ref[...].astype(BF16)
    acc = None
    for c in range(D_FF_DENSE // FFN_CHUNK):
        cols = slice(c * FFN_CHUNK, (c + 1) * FFN_CHUNK)
        g = _dot(xb, wg_ref[:, cols])
        u = _dot(xb, wu_ref[:, cols])
        y = _dot((g * _sigmoid(g) * u).astype(BF16), wd_ref[cols, :])
        acc = y if acc is None else acc + y
    o_ref[...] = acc


def _ffn(x2d, wg, wu, wd):
    n = x2d.shape[0]
    resident = lambda shape: pl.BlockSpec(shape, lambda i: (0, 0), pipeline_mode=pl.Buffered(1))
    return pl.pallas_call(
        _ffn_kernel,
        grid=(n // TM,),
        in_specs=[pl.BlockSpec((TM, D_MODEL), lambda i: (i, 0)),
                  resident((D_MODEL, D_FF_DENSE)), resident((D_MODEL, D_FF_DENSE)), resident((D_FF_DENSE, D_MODEL))],
        out_specs=pl.BlockSpec((TM, D_MODEL), lambda i: (i, 0)),
        out_shape=jax.ShapeDtypeStruct((n, D_MODEL), F32),
        compiler_params=_cparams(("parallel",), 40),
        name="dense_swiglu",
    )(x2d, wg.astype(BF16), wu.astype(BF16), wd.astype(BF16))


def _router_kernel(x_ref, w_ref, tri_ref, xb_ref, gd_ref, scol_ref, srow_ref, base_ref, cnt_ref):
    tt = x_ref.shape[0]
    x = x_ref[...]
    xb_ref[...] = x.astype(BF16)
    xh, xl = _split2(x)
    wh, wl = _split2(w_ref[...])
    logits = _dot(xh, wh) + _dot(xh, wl) + _dot(xl, wh)
    lane = _lane((tt, LANES))
    lf = lane.astype(F32)
    logits = jnp.where(lane < N_EXPERTS, logits, -jnp.inf)
    m1 = jnp.max(logits, axis=-1, keepdims=True)
    i1 = jnp.min(jnp.where(logits == m1, lf, float(LANES)), axis=-1, keepdims=True)
    rest = jnp.where(lf == i1, -jnp.inf, logits)
    m2 = jnp.max(rest, axis=-1, keepdims=True)
    i2 = jnp.min(jnp.where(rest == m2, lf, float(LANES)), axis=-1, keepdims=True)
    e2 = jnp.exp(m2 - m1)
    g1 = 1.0 / (1.0 + e2)
    g2 = e2 / (1.0 + e2)
    first, second = lf == i1, lf == i2
    gd_ref[...] = jnp.where(first, g1, 0.0) + jnp.where(second, g2, 0.0)
    ind = jnp.where(first, 1.0, jnp.where(second, 1.0, 0.0))
    tri = tri_ref[...]
    run = jnp.zeros((1, LANES), F32)
    slots, bases = [], []
    for tb in range(tt // MOE_TB):
        blk = ind[tb * MOE_TB:(tb + 1) * MOE_TB]
        rank = _dot(tri, blk.astype(BF16)) + run
        slots.append(jnp.where(blk > 0.5, rank, -1.0))
        bases.append(run)
        run = run + jnp.sum(blk, axis=0, keepdims=True)
    slot = jnp.concatenate(slots, axis=0)
    scol_ref[...] = slot
    srow_ref[...] = jnp.transpose(slot)[0:N_EXPERTS]
    base_ref[0] = jnp.concatenate(bases, axis=0)
    cnt_ref[0] = jnp.broadcast_to(run, (8, LANES))


def _route(x2d, w_router):
    n = x2d.shape[0]
    nt = n // MOE_TILE
    ntb = MOE_TILE // MOE_TB
    w = jnp.zeros((D_MODEL, LANES), F32).at[:, :N_EXPERTS].set(w_router)
    tri = jnp.asarray(np.tril(np.ones((MOE_TB, MOE_TB), np.float32), -1), BF16)
    return pl.pallas_call(
        _router_kernel,
        grid=(nt,),
        in_specs=[pl.BlockSpec((MOE_TILE, D_MODEL), lambda i: (i, 0)),
                  pl.BlockSpec((D_MODEL, LANES), lambda i: (0, 0)),
                  pl.BlockSpec((MOE_TB, MOE_TB), lambda i: (0, 0))],
        out_specs=[pl.BlockSpec((MOE_TILE, D_MODEL), lambda i: (i, 0)),
                   pl.BlockSpec((MOE_TILE, LANES), lambda i: (i, 0)),
                   pl.BlockSpec((MOE_TILE, LANES), lambda i: (i, 0)),
                   pl.BlockSpec((N_EXPERTS, MOE_TILE), lambda i: (0, i)),
                   pl.BlockSpec((1, ntb, LANES), lambda i: (i, 0, 0)),
                   pl.BlockSpec((1, 8, LANES), lambda i: (i, 0, 0))],
        out_shape=[jax.ShapeDtypeStruct((n, D_MODEL), BF16),
                   jax.ShapeDtypeStruct((n, LANES), F32),
                   jax.ShapeDtypeStruct((n, LANES), F32),
                   jax.ShapeDtypeStruct((N_EXPERTS, n), F32),
                   jax.ShapeDtypeStruct((nt, ntb, LANES), F32),
                   jax.ShapeDtypeStruct((nt, 8, LANES), F32)],
        compiler_params=_cparams(("parallel",), 48),
        name="moe_router",
    )(x2d, w, tri)


MOE_GATHER_W = MOE_TB + 16
MOE_SCATTER_W = MOE_TB + 8
MOE_CAP = MOE_TILE + 512


def _moe_kernel(cnt_ref, base_ref, xb_ref, srow_ref, scol_ref, gd_ref, wg_ref, wu_ref, wd_ref,
                o_ref, xc_ref, yc_ref):
    i, e, c = pl.program_id(0), pl.program_id(1), pl.program_id(2)
    nc = pl.num_programs(2)
    ntb = MOE_TILE // MOE_TB
    count = cnt_ref[i * N_EXPERTS + e]
    n_full = jnp.right_shift(count * MOE_ROWS_RECIP[0], MOE_ROWS_RECIP[1])
    rem = count - n_full * MOE_ROWS

    @pl.when((i == 0) & (e == 0) & (c == 0))
    def _():
        yc_ref[...] = jnp.zeros_like(yc_ref)

    @pl.when((e == 0) & (c == 0))
    def _():
        o_ref[...] = jnp.zeros_like(o_ref)

    @pl.when(c == 0)
    def _():
        xc_ref[...] = jnp.zeros_like(xc_ref)
        srow = srow_ref[pl.ds(e, 1), :]
        rid = lax.broadcasted_iota(jnp.int32, (MOE_GATHER_W, MOE_TB), 0).astype(F32)
        for tb in range(ntb):
            b0 = base_ref[(i * ntb + tb) * N_EXPERTS + e]
            b_al = pl.multiple_of(jnp.left_shift(jnp.right_shift(b0, 4), 4), 16)
            rel = srow[:, tb * MOE_TB:(tb + 1) * MOE_TB] - b_al.astype(F32)
            onehot = jnp.where(rel == rid, 1.0, 0.0).astype(BF16)
            rows = _dot(onehot, xb_ref[tb * MOE_TB:(tb + 1) * MOE_TB, :])
            cur = xc_ref[pl.ds(b_al, MOE_GATHER_W), :].astype(F32)
            xc_ref[pl.ds(b_al, MOE_GATHER_W), :] = (cur + rows).astype(BF16)

    def expert_rows(r0, n_rows):
        xr = xc_ref[pl.ds(r0, n_rows), :]
        g = _dot(xr, wg_ref[0])
        u = _dot(xr, wu_ref[0])
        y = _dot((g * _sigmoid(g) * u).astype(BF16), wd_ref[0])
        prev = yc_ref[pl.ds(r0, n_rows), :]
        yc_ref[pl.ds(r0, n_rows), :] = jnp.where(c == 0, y, prev + y)

    def rows_body(rb, carry):
        expert_rows(pl.multiple_of(rb * MOE_ROWS, MOE_ROW_ALIGN), MOE_ROWS)
        return carry

    lax.fori_loop(0, n_full + jnp.where(rem > MOE_ROWS // 2, 1, 0), rows_body, 0)

    @pl.when((rem > 0) & (rem <= MOE_ROWS // 2))
    def _():
        expert_rows(pl.multiple_of(n_full * MOE_ROWS, MOE_ROW_ALIGN), MOE_ROWS // 2)

    @pl.when(c == nc - 1)
    def _():
        lane = _lane((MOE_TB, LANES))
        cid = lax.broadcasted_iota(jnp.int32, (MOE_TB, MOE_SCATTER_W), 1).astype(F32)
        for tb in range(ntb):
            b0 = base_ref[(i * ntb + tb) * N_EXPERTS + e]
            b_al = pl.multiple_of(jnp.left_shift(jnp.right_shift(b0, 3), 3), 8)
            rows = slice(tb * MOE_TB, (tb + 1) * MOE_TB)
            slot = jnp.sum(jnp.where(lane == e, scol_ref[rows, :], 0.0), axis=-1, keepdims=True)
            gate = jnp.sum(jnp.where(lane == e, gd_ref[rows, :], 0.0), axis=-1, keepdims=True)
            onehot = jnp.where(slot - b_al.astype(F32) == cid, 1.0, 0.0).astype(BF16)
            y_win = yc_ref[pl.ds(b_al, MOE_SCATTER_W), :].astype(BF16)
            o_ref[rows, :] += gate * _dot(onehot, y_win)


def _moe(xb, srow, scol, gd, cnt, base, wg, wu, wd):
    n = xb.shape[0]
    nt = n // MOE_TILE
    nc = D_FF_EXPERT // MOE_CHUNK
    grid_spec = pltpu.PrefetchScalarGridSpec(
        num_scalar_prefetch=2,
        grid=(nt, N_EXPERTS, nc),
        in_specs=[pl.BlockSpec((MOE_TILE, D_MODEL), lambda i, e, c, *_: (i, 0), pipeline_mode=pl.Buffered(1)),
                  pl.BlockSpec((N_EXPERTS, MOE_TILE), lambda i, e, c, *_: (0, i)),
                  pl.BlockSpec((MOE_TILE, LANES), lambda i, e, c, *_: (i, 0)),
                  pl.BlockSpec((MOE_TILE, LANES), lambda i, e, c, *_: (i, 0)),
                  pl.BlockSpec((1, D_MODEL, MOE_CHUNK), lambda i, e, c, *_: (e, 0, c)),
                  pl.BlockSpec((1, D_MODEL, MOE_CHUNK), lambda i, e, c, *_: (e, 0, c)),
                  pl.BlockSpec((1, MOE_CHUNK, D_MODEL), lambda i, e, c, *_: (e, c, 0))],
        out_specs=pl.BlockSpec((MOE_TILE, D_MODEL), lambda i, e, c, *_: (i, 0), pipeline_mode=pl.Buffered(1)),
        scratch_shapes=[pltpu.VMEM((MOE_CAP, D_MODEL), BF16), pltpu.VMEM((MOE_CAP, D_MODEL), F32)],
    )
    return pl.pallas_call(
        _moe_kernel,
        grid_spec=grid_spec,
        out_shape=jax.ShapeDtypeStruct((n, D_MODEL), F32),
        compiler_params=_cparams(("arbitrary", "arbitrary", "arbitrary"), 56),
        name="moe_experts",
    )(cnt, base, xb, srow, scol, gd, wg.astype(BF16), wu.astype(BF16), wd.astype(BF16))


def _ple_kernel(x_ref, f_ref, p_ref, wg_ref, wp_ref, g2_ref, b2_ref, g3_ref, b3_ref, o_ref):
    x2 = _layer_norm(DN_ALPHA * x_ref[...] + f_ref[...], g2_ref[...], b2_ref[...])
    e = _sigmoid(_dot(x2.astype(BF16), wg_ref[...])) * _dot(p_ref[...].astype(BF16), wp_ref[...])
    o_ref[...] = _layer_norm(DN_ALPHA * x2 + e, g3_ref[...], b3_ref[...])


def _ple(x2d, f2d, p2d, w_gate, w_proj, g2, b2, g3, b3):
    n = x2d.shape[0]
    row = lambda w: pl.BlockSpec((TM, w), lambda i: (i, 0))
    full = lambda s: pl.BlockSpec(s, lambda i: (0,) * len(s))
    vec = full((1, D_MODEL))
    return pl.pallas_call(
        _ple_kernel,
        grid=(n // TM,),
        in_specs=[row(D_MODEL), row(D_MODEL), row(PLE_DIM), full((D_MODEL, D_MODEL)), full((PLE_DIM, D_MODEL)),
                  vec, vec, vec, vec],
        out_specs=row(D_MODEL),
        out_shape=jax.ShapeDtypeStruct((n, D_MODEL), F32),
        compiler_params=_cparams(("parallel",), 32),
        name="ln_ple_ln",
    )(x2d, f2d, p2d, w_gate.astype(BF16), w_proj.astype(BF16),
      g2.reshape(1, -1), b2.reshape(1, -1), g3.reshape(1, -1), b3.reshape(1, -1))


def _mixer(x2d, b, t, i, w_in, cmp_pos, cmp_w1, cmp_w2, diff_lambda, diff_subln, sinks):
    slopes_a, slopes_b, slopes_c = _alibi_slopes()
    proj, gl, kv_t = _project(x2d, _build_w_in(w_in))
    proj3 = proj.reshape(b, t, N_MAIN)
    kvc, kvc_t = _compress(proj3, cmp_pos, cmp_w1, cmp_w2)
    o_c, sel = _cmp_attn(proj3, kvc, kvc_t, slopes_a)
    o_s = _selected_attn_t(proj3, kv_t, sel, slopes_a)
    o_w = _window_attn_t(proj3, kv_t, [slopes_a], q_blk=QA_BLK, kv_blk=KVA_BLK + 2, window=NSA_WINDOW,
                         name="nsa_window")
    lam_init = 0.8 - 0.6 * math.exp(-0.3 * i)
    o_b = _diff_attn_t(proj3, kv_t, slopes_b, diff_lambda, diff_subln, lam_init)
    r_c = C_HEADS // C_KV_HEADS
    o_sw = _window_attn_t(proj3, kv_t, [slopes_c[g * r_c:(g + 1) * r_c] for g in range(C_KV_HEADS)],
                          q_blk=QC_BLK, kv_blk=KVC_BLK, window=SWA_WINDOW, sinks=sinks, name="swa_gqa")
    flat = lambda a: a.reshape(b * t, a.shape[-1])
    return flat(o_c), flat(o_s), flat(o_w), gl, flat(o_b), flat(o_sw)


def kernel(x, p, w_in, cmp_pos, cmp_w1, cmp_w2, diff_lambda, diff_subln, sinks, w_out, ln1_g, ln1_b,
           ffn_w_gate, ffn_w_up, ffn_w_down, moe_router, moe_w_gate, moe_w_up, moe_w_down, ln2_g, ln2_b,
           ple_gate, ple_proj, ln3_g, ln3_b):
    b, t, d = x.shape
    n = b * t
    x2d = x.reshape(n, d)
    for i in range(DEPTH):
        heads = _mixer(x2d, b, t, i, w_in[i], cmp_pos[i], cmp_w1[i], cmp_w2[i],
                       diff_lambda[i], diff_subln[i], sinks[i])
        x1 = _outproj(*heads, x2d, w_out[i], ln1_g[i], ln1_b[i])
        if i % 2 == 0:
            f = _ffn(x1, ffn_w_gate[i // 2], ffn_w_up[i // 2], ffn_w_down[i // 2])
        else:
            xb, gd, scol, srow, base, cnt = _route(x1, moe_router[i // 2])
            cnt_i = cnt[:, 0, :N_EXPERTS].astype(jnp.int32).reshape(-1)
            base_i = base[:, :, :N_EXPERTS].astype(jnp.int32).reshape(-1)
            f = _moe(xb, srow, scol, gd, cnt_i, base_i,
                     moe_w_gate[i // 2], moe_w_up[i // 2], moe_w_down[i // 2])
        x2d = _ple(x1, f, p[i].reshape(n, PLE_DIM), ple_gate[i], ple_proj[i],
                   ln2_g[i], ln2_b[i], ln3_g[i], ln3_b[i])
    return x2d.reshape(b, t, d)
```

```python
import functools
import math

import jax
import jax.numpy as jnp
import numpy as np
from jax import lax
from jax.experimental import pallas as pl
from jax.experimental.pallas import tpu as pltpu

F32 = jnp.float32
BF16 = jnp.bfloat16

D_MODEL = 1024
HEAD_DIM = 64
A_HEADS = 4
B_HEADS = 4
C_HEADS = 8
C_KV_HEADS = 2
DIFF_DK = 32
CMP_LEN = 32
CMP_STRIDE = 16
CMP_HIDDEN = 128
SLC_BLOCK = 64
SLC_SHIFT = 6
SLC_TOPN = 16
NSA_WINDOW = 512
SWA_WINDOW = 128
N_EXPERTS = 8
D_FF_EXPERT = 3584
D_FF_DENSE = 2816
PLE_DIM = 256
LN_EPS = 1e-5
NEG_INF = -1e30
FORCE_SCORE = 1e9
DEPTH = 2
DN_ALPHA = (2 * DEPTH) ** 0.25

LANES = 128
QA_BLK, QB_BLK, QC_BLK, KVB_BLK, KVC_BLK, KVA_BLK, N_MAIN_BLK = 0, 4, 8, 16, 20, 22, 25
N_MAIN = N_MAIN_BLK * LANES

LOG2E = 1.4426950408889634
TQ = 256
TQ_CMP = 512
TQ_DENSE = 512
TK_DENSE = 512
TM = 512
FFN_CHUNK = 1408
MOE_TILE = 2048
MOE_ROWS = 576
MOE_ROWS_RECIP = (1821, 20)
MOE_ROW_ALIGN = 16
MOE_TB = 256
MOE_CHUNK = 896


def _cparams(sem, vmem_mib):
    return pltpu.CompilerParams(dimension_semantics=sem, vmem_limit_bytes=vmem_mib * 1024 * 1024)


def _alibi_slopes():
    n = A_HEADS + B_HEADS + C_HEADS
    s = [LOG2E * 2.0 ** (-8.0 * i / n) for i in range(1, n + 1)]
    rest = s[C_HEADS:]
    return rest[0::2], rest[1::2], s[:C_HEADS]


def _dot(a, b):
    return jnp.dot(a, b, preferred_element_type=F32)


def _dot_nt(a, b):
    return lax.dot_general(a, b, (((1,), (1,)), ((), ())), preferred_element_type=F32)


def _split2(x):
    hi = x.astype(BF16)
    lo = (x - hi.astype(F32)).astype(BF16)
    return hi, lo


def _layer_norm(y, g, b):
    mu = jnp.mean(y, axis=-1, keepdims=True)
    d = y - mu
    var = jnp.mean(d * d, axis=-1, keepdims=True)
    return d * lax.rsqrt(var + LN_EPS) * g + b


def _sigmoid(x):
    return 1.0 / (1.0 + jnp.exp(-x))


def _lane(shape):
    return lax.broadcasted_iota(jnp.int32, shape, 1)


def _heads_to_rows(parts):
    out = [jnp.transpose(jnp.concatenate([a, b], axis=0)) for a, b in zip(parts[0::2], parts[1::2])]
    return out[0] if len(out) == 1 else jnp.concatenate(out, axis=1)


def _proj_kernel(x_ref, w_ref, o_ref, g_ref, t_ref):
    acc = _dot(x_ref[...].astype(BF16), w_ref[...])
    o_ref[...] = acc[:, :N_MAIN].astype(BF16)
    g_ref[...] = acc[:, N_MAIN:]
    t_ref[...] = jnp.transpose(acc[:, KVB_BLK * LANES:N_MAIN]).astype(BF16)


def _project(x2d, w_ext):
    n = x2d.shape[0]
    n_kv = N_MAIN - KVB_BLK * LANES
    return pl.pallas_call(
        _proj_kernel,
        grid=(n // TM,),
        in_specs=[pl.BlockSpec((TM, D_MODEL), lambda i: (i, 0)),
                  pl.BlockSpec((D_MODEL, N_MAIN + LANES), lambda i: (0, 0))],
        out_specs=[pl.BlockSpec((TM, N_MAIN), lambda i: (i, 0)),
                   pl.BlockSpec((TM, LANES), lambda i: (i, 0)),
                   pl.BlockSpec((n_kv, TM), lambda i: (0, i))],
        out_shape=[jax.ShapeDtypeStruct((n, N_MAIN), BF16),
                   jax.ShapeDtypeStruct((n, LANES), F32),
                   jax.ShapeDtypeStruct((n_kv, n), BF16)],
        compiler_params=_cparams(("parallel",), 32),
        name="in_proj",
    )(x2d, w_ext)


def _build_w_in(w):
    d = w.shape[0]
    o = np.cumsum([0, 256, 384, 12, 256, 256, 256, 512, 128, 128])
    q_a, kv_a, g_a, q_b, k_b, v_b, q_c, k_c, v_c = [w[:, o[i]:o[i + 1]] for i in range(9)]
    z = jnp.zeros((d, HEAD_DIM), F32)
    hd = HEAD_DIM
    cols = []
    for h in range(A_HEADS):
        cols += [q_a[:, h * hd:(h + 1) * hd] * (LOG2E * hd ** -0.5), z]
    for h in range(B_HEADS):
        cols += [q_b[:, h * hd:(h + 1) * hd] * (LOG2E * DIFF_DK ** -0.5), z]
    for h in range(C_HEADS):
        cols += [q_c[:, h * hd:(h + 1) * hd] * (LOG2E * hd ** -0.5), z]
    for h in range(B_HEADS):
        cols += [k_b[:, h * hd:(h + 1) * hd], v_b[:, h * hd:(h + 1) * hd]]
    for g in range(C_KV_HEADS):
        cols += [k_c[:, g * hd:(g + 1) * hd], v_c[:, g * hd:(g + 1) * hd]]
    cols.append(kv_a)
    cols += [g_a, jnp.zeros((d, LANES - 12), F32)]
    return jnp.concatenate(cols, axis=1).astype(BF16)


def _compress_kernel(ch_ref, wc_ref, pos_ref, w1_ref, w2_ref, o_ref, ot_ref):
    nck = ch_ref.shape[1]
    u = _dot(ch_ref[0], wc_ref[...])
    hs = []
    for j in range(2):
        pw = _dot(pos_ref[j].astype(BF16), w1_ref[j])[0:1]
        ua = u[:, (2 * j) * LANES:(2 * j + 1) * LANES]
        ub = u[:, (2 * j + 1) * LANES:(2 * j + 2) * LANES]
        pre = ua + pltpu.roll(ub, nck - 1, 0) + pw
        hs.append(0.5 * pre * (1.0 + jnp.tanh(0.7978845608028654 * (pre + 0.044715 * pre * pre * pre))))
    h = jnp.concatenate(hs, axis=1).astype(BF16)
    kvc = _dot(h, w2_ref[...])
    o_ref[0] = kvc.astype(BF16)
    ot_ref[0] = jnp.transpose(kvc).astype(BF16)


def _compress(proj3, cmp_pos, cmp_w1, cmp_w2):
    b, t, _ = proj3.shape
    nck = t // CMP_STRIDE
    kv = proj3[:, :, KVA_BLK * LANES:(KVA_BLK + 1) * LANES]
    chunks = kv.reshape(b, nck, CMP_STRIDE * LANES)
    w1 = cmp_w1.reshape(2, 2, CMP_STRIDE, HEAD_DIM, CMP_HIDDEN)
    z = jnp.zeros((CMP_STRIDE, HEAD_DIM, CMP_HIDDEN), F32)
    blocks = []
    for j in range(2):
        for half in range(2):
            pair = [w1[j, half], z] if j == 0 else [z, w1[j, half]]
            blocks.append(jnp.concatenate(pair, axis=1).reshape(CMP_STRIDE * LANES, CMP_HIDDEN))
    wc = jnp.concatenate(blocks, axis=1).astype(BF16)
    pos = jnp.zeros((2, 8, CMP_LEN * HEAD_DIM), F32).at[:, 0].set(cmp_pos.reshape(2, -1))
    z2 = jnp.zeros((CMP_HIDDEN, HEAD_DIM), F32)
    w2 = jnp.concatenate([jnp.concatenate([cmp_w2[0], z2], axis=1),
                          jnp.concatenate([z2, cmp_w2[1]], axis=1)], axis=0).astype(BF16)
    return pl.pallas_call(
        _compress_kernel,
        grid=(b,),
        in_specs=[pl.BlockSpec((1, nck, CMP_STRIDE * LANES), lambda i: (i, 0, 0)),
                  pl.BlockSpec(wc.shape, lambda i: (0, 0)),
                  pl.BlockSpec(pos.shape, lambda i: (0, 0, 0)),
                  pl.BlockSpec((2, CMP_LEN * HEAD_DIM, CMP_HIDDEN), lambda i: (0, 0, 0)),
                  pl.BlockSpec(w2.shape, lambda i: (0, 0))],
        out_specs=[pl.BlockSpec((1, nck, LANES), lambda i: (i, 0, 0)),
                   pl.BlockSpec((1, LANES, nck), lambda i: (i, 0, 0))],
        out_shape=[jax.ShapeDtypeStruct((b, nck, LANES), BF16),
                   jax.ShapeDtypeStruct((b, LANES, nck), BF16)],
        compiler_params=_cparams(("parallel",), 16),
        name="nsa_compress",
    )(chunks, wc, pos, cmp_w1.astype(BF16), w2)


def _cmp_attn_kernel_t(q_ref, kvc_ref, kvct_ref, ovt_ref, bias_ref, o_ref, sel_ref):
    tq = q_ref.shape[1]
    ncp = kvc_ref.shape[1]
    rows = A_HEADS * tq
    t0 = pl.program_id(1) * tq
    qs = _stack_heads(q_ref[0], A_HEADS)
    u = _dot_nt(kvc_ref[0], qs) + bias_ref[...]
    n = lax.broadcasted_iota(jnp.int32, (ncp, rows), 0)
    block_end = jnp.where(n == ncp - 1, jnp.int32(1 << 30), n * CMP_STRIDE + (CMP_LEN - 1))
    keep = block_end <= jnp.bitwise_and(_lane((ncp, rows)), tq - 1) + t0
    sh = jnp.where(keep, u, NEG_INF)
    m = jnp.max(sh, axis=0, keepdims=True)
    e = jnp.where(keep, jnp.exp2(sh - m), 0.0)
    p = e / jnp.maximum(jnp.sum(e, axis=0, keepdims=True), 1e-30)
    o_t = _dot(kvct_ref[0, HEAD_DIM:, :], p.astype(BF16))
    o_ref[0] = _heads_to_rows([o_t[:, h * tq:(h + 1) * tq] for h in range(A_HEADS)]).astype(BF16)

    psum = p[:, 0:tq] + p[:, tq:2 * tq] + p[:, 2 * tq:3 * tq] + p[:, 3 * tq:4 * tq]
    hi = psum.astype(BF16)
    r1 = psum - hi.astype(F32)
    mid = r1.astype(BF16)
    lo = (r1 - mid.astype(F32)).astype(BF16)
    ovt = ovt_ref[...]
    imp = _dot(ovt, hi) + _dot(ovt, mid) + _dot(ovt, lo)
    nsl = ncp * CMP_STRIDE // SLC_BLOCK
    j = lax.broadcasted_iota(jnp.int32, (LANES, tq), 0)
    t_blk = jnp.right_shift(_lane((LANES, tq)) + t0, SLC_SHIFT)
    forced = (j == 0) | (j == t_blk) | (j == t_blk - 1)
    imp = jnp.where(forced, FORCE_SCORE, jnp.where(j > t_blk, -1.0, imp))
    work = jnp.where(j < nsl, imp, -3.0)
    jf = j.astype(F32)
    sel = jnp.zeros((LANES, tq), F32)
    for _ in range(SLC_TOPN):
        mx = jnp.max(work, axis=0, keepdims=True)
        first = jnp.min(jnp.where(work == mx, jf, float(LANES)), axis=0, keepdims=True)
        pick = jf == first
        sel = jnp.where(pick, 1.0, sel)
        work = jnp.where(pick, -4.0, work)
    sel_ref[0] = jnp.transpose(sel).astype(BF16)


def _cmp_attn(proj3, kvc, kvc_t, slopes):
    b, t, _ = proj3.shape
    ncp = kvc.shape[1]
    nsl = t // SLC_BLOCK
    c0 = np.arange(ncp)[None, :] * CMP_STRIDE
    s0 = np.arange(LANES)[:, None] * SLC_BLOCK
    ovt = np.clip(np.minimum(c0 + CMP_LEN, s0 + SLC_BLOCK) - np.maximum(c0, s0), 0, None) / CMP_LEN
    ovt[:, ncp - 1] = 0.0
    ovt[nsl:, :] = 0.0
    rel = np.arange(TQ_CMP)[None, :] - (np.arange(ncp)[:, None] * CMP_STRIDE + (CMP_LEN - 1))
    bias = np.concatenate([-s * rel for s in slopes], axis=1).astype(np.float32)
    return pl.pallas_call(
        _cmp_attn_kernel_t,
        grid=(b, t // TQ_CMP),
        in_specs=[pl.BlockSpec((1, TQ_CMP, A_HEADS * LANES), lambda bi, i: (bi, i, QA_BLK // A_HEADS)),
                  pl.BlockSpec((1, ncp, LANES), lambda bi, i: (bi, 0, 0)),
                  pl.BlockSpec((1, LANES, ncp), lambda bi, i: (bi, 0, 0)),
                  pl.BlockSpec((LANES, ncp), lambda bi, i: (0, 0)),
                  pl.BlockSpec((ncp, A_HEADS * TQ_CMP), lambda bi, i: (0, 0))],
        out_specs=[pl.BlockSpec((1, TQ_CMP, A_HEADS * HEAD_DIM), lambda bi, i: (bi, i, 0)),
                   pl.BlockSpec((1, TQ_CMP, LANES), lambda bi, i: (bi, i, 0))],
        out_shape=[jax.ShapeDtypeStruct((b, t, A_HEADS * HEAD_DIM), BF16),
                   jax.ShapeDtypeStruct((b, t, LANES), BF16)],
        compiler_params=_cparams(("parallel", "parallel"), 16),
        name="nsa_cmp_attn",
    )(proj3, kvc, kvc_t, jnp.asarray(ovt, BF16), jnp.asarray(bias))


def _stack_heads(q, n):
    return jnp.concatenate([q[:, h * LANES:(h + 1) * LANES] for h in range(n)], axis=0)


def _window_bias(slopes, tq, window):
    wk = window + tq
    n_var = -(-window // tq) + 1
    j = np.arange(wk)[:, None]
    q = np.arange(tq)[None, :]
    out = np.empty((len(slopes), n_var, wk, len(slopes[0]) * tq), np.float32)
    for v in range(n_var):
        d = v * tq - max(v * tq - window, 0) + q - j
        keep = (d >= 0) & (d < window)
        for g, row in enumerate(slopes):
            for r, slope in enumerate(row):
                out[g, v, :, r * tq:(r + 1) * tq] = np.where(keep, -slope * d, NEG_INF)
    return jnp.asarray(out)


def _window_kernel_t(*refs, r_heads, window, has_sink):
    if has_sink:
        sink_ref, q_ref, kv_ref, kvt_ref, bias_ref, o_ref = refs
    else:
        q_ref, kv_ref, kvt_ref, bias_ref, o_ref = refs
    g, i = pl.program_id(1), pl.program_id(2)
    tq = q_ref.shape[1]
    wk = window + tq
    start = pl.multiple_of(jnp.maximum(i * tq - window, 0), LANES)
    qs = _stack_heads(q_ref[0], r_heads)
    u = _dot_nt(kv_ref[0, pl.ds(start, wk), :], qs) + bias_ref[0, 0]
    m = jnp.max(u, axis=0, keepdims=True)
    if has_sink:
        sink = jnp.concatenate([jnp.full((1, tq), sink_ref[g * r_heads + r], F32) for r in range(r_heads)], axis=1)
        m = jnp.maximum(m, sink)
    p = jnp.exp2(u - m)
    den = jnp.sum(p, axis=0, keepdims=True)
    if has_sink:
        den = den + jnp.exp2(sink - m)
    o_t = _dot(kvt_ref[HEAD_DIM:, pl.ds(start, wk)], p.astype(BF16)) / den
    o_ref[0] = _heads_to_rows([o_t[:, r * tq:(r + 1) * tq] for r in range(r_heads)]).astype(BF16)


def _window_attn_t(proj3, kv_t, slopes, *, q_blk, kv_blk, window, sinks=None, name):
    b, t, _ = proj3.shape
    groups, r_heads = len(slopes), len(slopes[0])
    bias = _window_bias(slopes, TQ, window)
    n_var, wk = bias.shape[1], bias.shape[2]
    in_specs = [pl.BlockSpec((1, TQ, r_heads * LANES), lambda bi, g, i: (bi, i, q_blk // r_heads + g)),
                pl.BlockSpec((1, t, LANES), lambda bi, g, i: (bi, 0, kv_blk + g)),
                pl.BlockSpec((LANES, t), lambda bi, g, i: (kv_blk - KVB_BLK + g, bi)),
                pl.BlockSpec((1, 1, wk, r_heads * TQ), lambda bi, g, i: (g, jnp.minimum(i, n_var - 1), 0, 0))]
    args = [proj3, proj3, kv_t, bias]
    if sinks is not None:
        in_specs.insert(0, pl.BlockSpec(memory_space=pltpu.SMEM))
        args.insert(0, sinks.astype(F32) * LOG2E)
    return pl.pallas_call(
        functools.partial(_window_kernel_t, r_heads=r_heads, window=window, has_sink=sinks is not None),
        grid=(b, groups, t // TQ),
        in_specs=in_specs,
        out_specs=pl.BlockSpec((1, TQ, r_heads * HEAD_DIM), lambda bi, g, i: (bi, i, g)),
        out_shape=jax.ShapeDtypeStruct((b, t, groups * r_heads * HEAD_DIM), BF16),
        compiler_params=_cparams(("parallel", "parallel", "parallel"), 24),
        name=name,
    )(*args)


def _flash_bias(stream_slopes, tq, tk):
    ratio = tk // tq
    j = np.arange(tk)[:, None]
    q = np.arange(tq)[None, :]
    out = np.empty((len(stream_slopes), 1 + ratio, tk, len(stream_slopes[0]) * tq), np.float32)
    for s, slopes in enumerate(stream_slopes):
        for h, slope in enumerate(slopes):
            cols = slice(h * tq, (h + 1) * tq)
            out[s, 0, :, cols] = slope * j
            for r in range(ratio):
                out[s, 1 + r, :, cols] = np.where(j <= r * tq + q, slope * j, NEG_INF)
    return jnp.asarray(out)


def _flash_t(streams, kv_ref, kvt_ref, i, *, tq, tk, sel_t=None, interior=None):
    ratio = tk // tq
    n_full = lax.div(i, jnp.int32(ratio))
    rows = streams[0][0].shape[0]

    def tiles(kts, carry, edge):
        starts = [pl.multiple_of(kt * tk, tk) for kt in kts]
        scores = [[_dot_nt(kv_ref[0, pl.ds(k0, tk), s[1]:s[1] + LANES], s[0]) for s in streams]
                  for k0 in starts]
        state = [(m, l) for m, l, _ in carry]
        updates = []
        for n, kt in enumerate(kts):
            variant = 1 + i - kt * ratio if (edge and n == len(kts) - 1) else 0
            keep = None
            if sel_t is not None:
                blk = jnp.right_shift(lax.broadcasted_iota(jnp.int32, (tk, LANES), 0) + kt * tk, SLC_SHIFT)
                expand = jnp.where(_lane((tk, LANES)) == blk, 1.0, 0.0).astype(BF16)
                keep = _dot(expand, sel_t) > 0.5
            k0f = (kt * tk).astype(F32)
            row = []
            for s, (_, _, slope_lane, bias) in enumerate(streams):
                m, l = state[s]
                u = scores[n][s] + bias(variant)
                if keep is not None:
                    u = jnp.where(keep, u, NEG_INF)
                shift = slope_lane * k0f
                m_new = jnp.maximum(m, jnp.max(u, axis=0, keepdims=True) + shift)
                p = jnp.exp2(u - (m_new - shift))
                alpha = jnp.exp2(m - m_new)
                state[s] = (m_new, alpha * l + jnp.sum(p, axis=0, keepdims=True))
                row.append((alpha, p.astype(BF16)))
            updates.append(row)
        accs = [acc for _, _, acc in carry]
        for k0, row in zip(starts, updates):
            for s, (alpha, p) in enumerate(row):
                col = streams[s][1]
                kv_f = kvt_ref[col + HEAD_DIM:col + LANES, pl.ds(k0, tk)]
                accs[s] = alpha * accs[s] + _dot(kv_f, p)
        return tuple((m, l, acc) for (m, l), acc in zip(state, accs))

    init = tuple((jnp.full((1, rows), NEG_INF, F32), jnp.zeros((1, rows), F32), jnp.zeros((HEAD_DIM, rows), F32))
                 for _ in streams)
    if interior is None:
        count, tile_id = n_full, lambda j: j
    else:
        count, tile_id = interior
    carry = lax.fori_loop(0, jnp.right_shift(count, 1),
                          lambda j, c: tiles([tile_id(2 * j), tile_id(2 * j + 1)], c, False), init)
    carry = lax.cond(jnp.bitwise_and(count, 1) == 1,
                     lambda c: tiles([tile_id(count - 1), n_full], c, True),
                     lambda c: tiles([n_full], c, True), carry)
    return [(l, acc) for _, l, acc in carry]


def _selected_kernel_t(count_ref, ids_ref, q_ref, kv_ref, kvt_ref, sel_ref, bias_ref, o_ref, *, slopes):
    tq = q_ref.shape[1]
    i = pl.program_id(1)
    step = pl.program_id(0) * pl.num_programs(1) + i
    n_kt = kv_ref.shape[1] // TK_DENSE
    interior = (count_ref[step], lambda j: ids_ref[step * n_kt + j])
    sel_t = jnp.transpose(sel_ref[0].astype(F32)).astype(BF16)
    sel_t = jnp.concatenate([sel_t, sel_t], axis=1)
    streams = []
    for pair in range(A_HEADS // 2):
        qs = jnp.concatenate([q_ref[0, :, h * LANES:(h + 1) * LANES] for h in (2 * pair, 2 * pair + 1)], axis=0)
        slope_lane = jnp.concatenate([jnp.full((1, tq), slopes[h], F32) for h in (2 * pair, 2 * pair + 1)], axis=1)
        streams.append((qs, 0, slope_lane, functools.partial(lambda s, v: bias_ref[s, v], pair)))
    outs = []
    for l, acc in _flash_t(streams, kv_ref, kvt_ref, i, tq=tq, tk=TK_DENSE, sel_t=sel_t, interior=interior):
        o_t = acc / l
        outs += [o_t[:, :tq], o_t[:, tq:]]
    o_ref[0] = _heads_to_rows(outs).astype(BF16)


def _selected_attn_t(proj3, kv_t, sel, slopes):
    b, t, _ = proj3.shape
    kv_blk = KVA_BLK + 1
    bias = _flash_bias([slopes[0:2], slopes[2:4]], TQ_DENSE, TK_DENSE)
    nq, nk, per = t // TQ_DENSE, t // TK_DENSE, TK_DENSE // SLC_BLOCK
    picked = jnp.max(sel.reshape(b, nq, TQ_DENSE, LANES)[..., :nk * per], axis=2) > 0
    active = jnp.any(picked.reshape(b, nq, nk, per), axis=-1)
    past = jnp.arange(nk)[None, :] < (jnp.arange(nq) * TQ_DENSE // TK_DENSE)[:, None]
    active = active & past[None]
    ids = jnp.argsort(~active, axis=-1, stable=True).astype(jnp.int32).reshape(-1)
    counts = jnp.sum(active, axis=-1).astype(jnp.int32).reshape(-1)
    grid_spec = pltpu.PrefetchScalarGridSpec(
        num_scalar_prefetch=2,
        grid=(b, nq),
        in_specs=[pl.BlockSpec((1, TQ_DENSE, A_HEADS * LANES), lambda bi, i, *_: (bi, i, QA_BLK // A_HEADS)),
                  pl.BlockSpec((1, t, LANES), lambda bi, i, *_: (bi, 0, kv_blk)),
                  pl.BlockSpec((LANES, t), lambda bi, i, *_: (kv_blk - KVB_BLK, bi)),
                  pl.BlockSpec((1, TQ_DENSE, LANES), lambda bi, i, *_: (bi, i, 0)),
                  pl.BlockSpec(bias.shape, lambda bi, i, *_: (0, 0, 0, 0), pipeline_mode=pl.Buffered(1))],
        out_specs=pl.BlockSpec((1, TQ_DENSE, A_HEADS * HEAD_DIM), lambda bi, i, *_: (bi, i, 0)),
    )
    return pl.pallas_call(
        functools.partial(_selected_kernel_t, slopes=slopes),
        grid_spec=grid_spec,
        out_shape=jax.ShapeDtypeStruct((b, t, A_HEADS * HEAD_DIM), BF16),
        compiler_params=_cparams(("parallel", "parallel"), 40),
        name="nsa_selected",
    )(counts, ids, proj3, proj3, kv_t, sel, bias)


def _diff_kernel_t(slope_ref, dl_ref, g_ref, q_ref, kv_ref, kvt_ref, bias_ref, o_ref, *, lam_init):
    tq = q_ref.shape[1]
    hp = pl.program_id(1)
    i = pl.program_id(2)
    dl = dl_ref[...]
    lam = (jnp.exp(jnp.sum(dl[0:1] * dl[1:2], axis=-1, keepdims=True))
           - jnp.exp(jnp.sum(dl[2:3] * dl[3:4], axis=-1, keepdims=True)) + lam_init)
    lane = _lane((tq, LANES))
    streams = []
    for hh in range(2):
        q = q_ref[0, :, hh * LANES:(hh + 1) * LANES]
        zero = jnp.zeros_like(q)
        qs = jnp.concatenate([jnp.where(lane < DIFF_DK, q, zero),
                              jnp.where(lane >= DIFF_DK, q, zero)], axis=0)
        slope_lane = jnp.full((1, 2 * tq), slope_ref[hp * 2 + hh], F32)
        streams.append((qs, hh * LANES, slope_lane, functools.partial(lambda s, v: bias_ref[s, v], hh)))
    outs = []
    for l, acc in _flash_t(streams, kv_ref, kvt_ref, i, tq=tq, tk=TK_DENSE):
        o = acc / l
        w = o[:, :tq] - lam * o[:, tq:]
        ms = jnp.sum(w * w, axis=0, keepdims=True) * (1.0 / HEAD_DIM)
        outs.append(w * lax.rsqrt(ms + LN_EPS) * (1.0 - lam_init))
    o_ref[0] = (_heads_to_rows(outs) * g_ref[...]).astype(BF16)


def _diff_attn_t(proj3, kv_t, slopes, diff_lambda, subln, lam_init):
    b, t, _ = proj3.shape
    g_ext = jnp.tile(subln.reshape(1, HEAD_DIM).astype(F32), (1, 2))
    bias = _flash_bias([[s, s] for s in slopes], TQ_DENSE, TK_DENSE)
    return pl.pallas_call(
        functools.partial(_diff_kernel_t, lam_init=lam_init),
        grid=(b, B_HEADS // 2, t // TQ_DENSE),
        in_specs=[pl.BlockSpec(memory_space=pltpu.SMEM),
                  pl.BlockSpec((4, DIFF_DK), lambda bi, h, i: (0, 0)),
                  pl.BlockSpec((1, LANES), lambda bi, h, i: (0, 0)),
                  pl.BlockSpec((1, TQ_DENSE, 2 * LANES), lambda bi, h, i: (bi, i, QB_BLK // 2 + h)),
                  pl.BlockSpec((1, t, 2 * LANES), lambda bi, h, i: (bi, 0, KVB_BLK // 2 + h)),
                  pl.BlockSpec((2 * LANES, t), lambda bi, h, i: (h, bi)),
                  pl.BlockSpec((2,) + bias.shape[1:], lambda bi, h, i: (h, 0, 0, 0))],
        out_specs=pl.BlockSpec((1, TQ_DENSE, 2 * HEAD_DIM), lambda bi, h, i: (bi, i, h)),
        out_shape=jax.ShapeDtypeStruct((b, t, B_HEADS * HEAD_DIM), BF16),
        compiler_params=_cparams(("parallel", "parallel", "parallel"), 48),
        name="diff_attn",
    )(jnp.asarray(slopes, F32), diff_lambda.astype(F32), g_ext, proj3, proj3, kv_t, bias)


def _outproj_kernel(oc_ref, os_ref, ow_ref, gl_ref, ob_ref, osw_ref, x_ref, w_ref, ex_ref, g_ref, b_ref, o_ref):
    hi, lo = _split2(_sigmoid(gl_ref[...]))
    oa = None
    for j, ref in enumerate((oc_ref, os_ref, ow_ref)):
        gate = _dot(hi, ex_ref[j]) + _dot(lo, ex_ref[j])
        term = gate * ref[...].astype(F32)
        oa = term if oa is None else oa + term
    cat = jnp.concatenate([oa.astype(BF16), ob_ref[...], osw_ref[...]], axis=1)
    y = DN_ALPHA * x_ref[...] + _dot(cat, w_ref[...])
    o_ref[...] = _layer_norm(y, g_ref[...], b_ref[...])


def _outproj(o_c, o_s, o_w, gl, o_b, o_sw, x2d, w_out, ln_g, ln_b):
    n = x2d.shape[0]
    ex = np.zeros((3, LANES, A_HEADS * HEAD_DIM), np.float32)
    for j in range(3):
        for h in range(A_HEADS):
            ex[j, h * 3 + j, h * HEAD_DIM:(h + 1) * HEAD_DIM] = 1.0
    row = lambda w: pl.BlockSpec((TM, w), lambda i: (i, 0))
    full = lambda s: pl.BlockSpec(s, lambda i: (0,) * len(s))
    return pl.pallas_call(
        _outproj_kernel,
        grid=(n // TM,),
        in_specs=[row(256), row(256), row(256), row(LANES), row(256), row(512), row(D_MODEL),
                  full((D_MODEL, D_MODEL)), full(ex.shape), full((1, D_MODEL)), full((1, D_MODEL))],
        out_specs=row(D_MODEL),
        out_shape=jax.ShapeDtypeStruct((n, D_MODEL), F32),
        compiler_params=_cparams(("parallel",), 24),
        name="out_proj_ln",
    )(o_c, o_s, o_w, gl, o_b, o_sw, x2d, w_out.astype(BF16), jnp.asarray(ex, BF16),
      ln_g.reshape(1, -1), ln_b.reshape(1, -1))


def _ffn_kernel(x_ref, wg_ref, wu_ref, wd_ref, o_ref):
    xb = x_ref[...].astype(BF16)
    acc = None
    for c in range(D_FF_DENSE // FFN_CHUNK):
        cols = slice(c * FFN_CHUNK, (c + 1) * FFN_CHUNK)
        g = _dot(xb, wg_ref[:, cols])
        u = _dot(xb, wu_ref[:, cols])
        y = _dot((g * _sigmoid(g) * u).astype(BF16), wd_ref[cols, :])
        acc = y if acc is None else acc + y
    o_ref[...] = acc


def _ffn(x2d, wg, wu, wd):
    n = x2d.shape[0]
    resident = lambda shape: pl.BlockSpec(shape, lambda i: (0, 0), pipeline_mode=pl.Buffered(1))
    return pl.pallas_call(
        _ffn_kernel,
        grid=(n // TM,),
        in_specs=[pl.BlockSpec((TM, D_MODEL), lambda i: (i, 0)),
                  resident((D_MODEL, D_FF_DENSE)), resident((D_MODEL, D_FF_DENSE)), resident((D_FF_DENSE, D_MODEL))],
        out_specs=pl.BlockSpec((TM, D_MODEL), lambda i: (i, 0)),
        out_shape=jax.ShapeDtypeStruct((n, D_MODEL), F32),
        compiler_params=_cparams(("parallel",), 40),
        name="dense_swiglu",
    )(x2d, wg.astype(BF16), wu.astype(BF16), wd.astype(BF16))


def _router_kernel(x_ref, w_ref, tri_ref, xb_ref, gd_ref, scol_ref, srow_ref, base_ref, cnt_ref):
    tt = x_ref.shape[0]
    x = x_ref[...]
    xb_ref[...] = x.astype(BF16)
    xh, xl = _split2(x)
    wh, wl = _split2(w_ref[...])
    logits = _dot(xh, wh) + _dot(xh, wl) + _dot(xl, wh)
    lane = _lane((tt, LANES))
    lf = lane.astype(F32)
    logits = jnp.where(lane < N_EXPERTS, logits, -jnp.inf)
    m1 = jnp.max(logits, axis=-1, keepdims=True)
    i1 = jnp.min(jnp.where(logits == m1, lf, float(LANES)), axis=-1, keepdims=True)
    rest = jnp.where(lf == i1, -jnp.inf, logits)
    m2 = jnp.max(rest, axis=-1, keepdims=True)
    i2 = jnp.min(jnp.where(rest == m2, lf, float(LANES)), axis=-1, keepdims=True)
    e2 = jnp.exp(m2 - m1)
    g1 = 1.0 / (1.0 + e2)
    g2 = e2 / (1.0 + e2)
    first, second = lf == i1, lf == i2
    gd_ref[...] = jnp.where(first, g1, 0.0) + jnp.where(second, g2, 0.0)
    ind = jnp.where(first, 1.0, jnp.where(second, 1.0, 0.0))
    tri = tri_ref[...]
    run = jnp.zeros((1, LANES), F32)
    slots, bases = [], []
    for tb in range(tt // MOE_TB):
        blk = ind[tb * MOE_TB:(tb + 1) * MOE_TB]
        rank = _dot(tri, blk.astype(BF16)) + run
        slots.append(jnp.where(blk > 0.5, rank, -1.0))
        bases.append(run)
        run = run + jnp.sum(blk, axis=0, keepdims=True)
    slot = jnp.concatenate(slots, axis=0)
    scol_ref[...] = slot
    srow_ref[...] = jnp.transpose(slot)[0:N_EXPERTS]
    base_ref[0] = jnp.concatenate(bases, axis=0)
    cnt_ref[0] = jnp.broadcast_to(run, (8, LANES))


def _route(x2d, w_router):
    n = x2d.shape[0]
    nt = n // MOE_TILE
    ntb = MOE_TILE // MOE_TB
    w = jnp.zeros((D_MODEL, LANES), F32).at[:, :N_EXPERTS].set(w_router)
    tri = jnp.asarray(np.tril(np.ones((MOE_TB, MOE_TB), np.float32), -1), BF16)
    return pl.pallas_call(
        _router_kernel,
        grid=(nt,),
        in_specs=[pl.BlockSpec((MOE_TILE, D_MODEL), lambda i: (i, 0)),
                  pl.BlockSpec((D_MODEL, LANES), lambda i: (0, 0)),
                  pl.BlockSpec((MOE_TB, MOE_TB), lambda i: (0, 0))],
        out_specs=[pl.BlockSpec((MOE_TILE, D_MODEL), lambda i: (i, 0)),
                   pl.BlockSpec((MOE_TILE, LANES), lambda i: (i, 0)),
                   pl.BlockSpec((MOE_TILE, LANES), lambda i: (i, 0)),
                   pl.BlockSpec((N_EXPERTS, MOE_TILE), lambda i: (0, i)),
                   pl.BlockSpec((1, ntb, LANES), lambda i: (i, 0, 0)),
                   pl.BlockSpec((1, 8, LANES), lambda i: (i, 0, 0))],
        out_shape=[jax.ShapeDtypeStruct((n, D_MODEL), BF16),
                   jax.ShapeDtypeStruct((n, LANES), F32),
                   jax.ShapeDtypeStruct((n, LANES), F32),
                   jax.ShapeDtypeStruct((N_EXPERTS, n), F32),
                   jax.ShapeDtypeStruct((nt, ntb, LANES), F32),
                   jax.ShapeDtypeStruct((nt, 8, LANES), F32)],
        compiler_params=_cparams(("parallel",), 48),
        name="moe_router",
    )(x2d, w, tri)


MOE_GATHER_W = MOE_TB + 16
MOE_SCATTER_W = MOE_TB + 8
MOE_CAP = MOE_TILE + 512


def _moe_kernel(cnt_ref, base_ref, xb_ref, srow_ref, scol_ref, gd_ref, wg_ref, wu_ref, wd_ref,
                o_ref, xc_ref, yc_ref):
    i, e, c = pl.program_id(0), pl.program_id(1), pl.program_id(2)
    nc = pl.num_programs(2)
    ntb = MOE_TILE // MOE_TB
    count = cnt_ref[i * N_EXPERTS + e]
    n_full = jnp.right_shift(count * MOE_ROWS_RECIP[0], MOE_ROWS_RECIP[1])
    rem = count - n_full * MOE_ROWS

    @pl.when((i == 0) & (e == 0) & (c == 0))
    def _():
        yc_ref[...] = jnp.zeros_like(yc_ref)

    @pl.when((e == 0) & (c == 0))
    def _():
        o_ref[...] = jnp.zeros_like(o_ref)

    @pl.when(c == 0)
    def _():
        xc_ref[...] = jnp.zeros_like(xc_ref)
        srow = srow_ref[pl.ds(e, 1), :]
        rid = lax.broadcasted_iota(jnp.int32, (MOE_GATHER_W, MOE_TB), 0).astype(F32)
        for tb in range(ntb):
            b0 = base_ref[(i * ntb + tb) * N_EXPERTS + e]
            b_al = pl.multiple_of(jnp.left_shift(jnp.right_shift(b0, 4), 4), 16)
            rel = srow[:, tb * MOE_TB:(tb + 1) * MOE_TB] - b_al.astype(F32)
            onehot = jnp.where(rel == rid, 1.0, 0.0).astype(BF16)
            rows = _dot(onehot, xb_ref[tb * MOE_TB:(tb + 1) * MOE_TB, :])
            cur = xc_ref[pl.ds(b_al, MOE_GATHER_W), :].astype(F32)
            xc_ref[pl.ds(b_al, MOE_GATHER_W), :] = (cur + rows).astype(BF16)

    def expert_rows(r0, n_rows):
        xr = xc_ref[pl.ds(r0, n_rows), :]
        g = _dot(xr, wg_ref[0])
        u = _dot(xr, wu_ref[0])
        y = _dot((g * _sigmoid(g) * u).astype(BF16), wd_ref[0])
        prev = yc_ref[pl.ds(r0, n_rows), :]
        yc_ref[pl.ds(r0, n_rows), :] = jnp.where(c == 0, y, prev + y)

    def rows_body(rb, carry):
        expert_rows(pl.multiple_of(rb * MOE_ROWS, MOE_ROW_ALIGN), MOE_ROWS)
        return carry

    lax.fori_loop(0, n_full + jnp.where(rem > MOE_ROWS // 2, 1, 0), rows_body, 0)

    @pl.when((rem > 0) & (rem <= MOE_ROWS // 2))
    def _():
        expert_rows(pl.multiple_of(n_full * MOE_ROWS, MOE_ROW_ALIGN), MOE_ROWS // 2)

    @pl.when(c == nc - 1)
    def _():
        lane = _lane((MOE_TB, LANES))
        cid = lax.broadcasted_iota(jnp.int32, (MOE_TB, MOE_SCATTER_W), 1).astype(F32)
        for tb in range(ntb):
            b0 = base_ref[(i * ntb + tb) * N_EXPERTS + e]
            b_al = pl.multiple_of(jnp.left_shift(jnp.right_shift(b0, 3), 3), 8)
            rows = slice(tb * MOE_TB, (tb + 1) * MOE_TB)
            slot = jnp.sum(jnp.where(lane == e, scol_ref[rows, :], 0.0), axis=-1, keepdims=True)
            gate = jnp.sum(jnp.where(lane == e, gd_ref[rows, :], 0.0), axis=-1, keepdims=True)
            onehot = jnp.where(slot - b_al.astype(F32) == cid, 1.0, 0.0).astype(BF16)
            y_win = yc_ref[pl.ds(b_al, MOE_SCATTER_W), :].astype(BF16)
            o_ref[rows, :] += gate * _dot(onehot, y_win)


def _moe(xb, srow, scol, gd, cnt, base, wg, wu, wd):
    n = xb.shape[0]
    nt = n // MOE_TILE
    nc = D_FF_EXPERT // MOE_CHUNK
    grid_spec = pltpu.PrefetchScalarGridSpec(
        num_scalar_prefetch=2,
        grid=(nt, N_EXPERTS, nc),
        in_specs=[pl.BlockSpec((MOE_TILE, D_MODEL), lambda i, e, c, *_: (i, 0), pipeline_mode=pl.Buffered(1)),
                  pl.BlockSpec((N_EXPERTS, MOE_TILE), lambda i, e, c, *_: (0, i)),
                  pl.BlockSpec((MOE_TILE, LANES), lambda i, e, c, *_: (i, 0)),
                  pl.BlockSpec((MOE_TILE, LANES), lambda i, e, c, *_: (i, 0)),
                  pl.BlockSpec((1, D_MODEL, MOE_CHUNK), lambda i, e, c, *_: (e, 0, c)),
                  pl.BlockSpec((1, D_MODEL, MOE_CHUNK), lambda i, e, c, *_: (e, 0, c)),
                  pl.BlockSpec((1, MOE_CHUNK, D_MODEL), lambda i, e, c, *_: (e, c, 0))],
        out_specs=pl.BlockSpec((MOE_TILE, D_MODEL), lambda i, e, c, *_: (i, 0), pipeline_mode=pl.Buffered(1)),
        scratch_shapes=[pltpu.VMEM((MOE_CAP, D_MODEL), BF16), pltpu.VMEM((MOE_CAP, D_MODEL), F32)],
    )
    return pl.pallas_call(
        _moe_kernel,
        grid_spec=grid_spec,
        out_shape=jax.ShapeDtypeStruct((n, D_MODEL), F32),
        compiler_params=_cparams(("arbitrary", "arbitrary", "arbitrary"), 56),
        name="moe_experts",
    )(cnt, base, xb, srow, scol, gd, wg.astype(BF16), wu.astype(BF16), wd.astype(BF16))


def _ple_kernel(x_ref, f_ref, p_ref, wg_ref, wp_ref, g2_ref, b2_ref, g3_ref, b3_ref, o_ref):
    x2 = _layer_norm(DN_ALPHA * x_ref[...] + f_ref[...], g2_ref[...], b2_ref[...])
    e = _sigmoid(_dot(x2.astype(BF16), wg_ref[...])) * _dot(p_ref[...].astype(BF16), wp_ref[...])
    o_ref[...] = _layer_norm(DN_ALPHA * x2 + e, g3_ref[...], b3_ref[...])


def _ple(x2d, f2d, p2d, w_gate, w_proj, g2, b2, g3, b3):
    n = x2d.shape[0]
    row = lambda w: pl.BlockSpec((TM, w), lambda i: (i, 0))
    full = lambda s: pl.BlockSpec(s, lambda i: (0,) * len(s))
    vec = full((1, D_MODEL))
    return pl.pallas_call(
        _ple_kernel,
        grid=(n // TM,),
        in_specs=[row(D_MODEL), row(D_MODEL), row(PLE_DIM), full((D_MODEL, D_MODEL)), full((PLE_DIM, D_MODEL)),
                  vec, vec, vec, vec],
        out_specs=row(D_MODEL),
        out_shape=jax.ShapeDtypeStruct((n, D_MODEL), F32),
        compiler_params=_cparams(("parallel",), 32),
        name="ln_ple_ln",
    )(x2d, f2d, p2d, w_gate.astype(BF16), w_proj.astype(BF16),
      g2.reshape(1, -1), b2.reshape(1, -1), g3.reshape(1, -1), b3.reshape(1, -1))


def _mixer(x2d, b, t, i, w_in, cmp_pos, cmp_w1, cmp_w2, diff_lambda, diff_subln, sinks):
    slopes_a, slopes_b, slopes_c = _alibi_slopes()
    proj, gl, kv_t = _project(x2d, _build_w_in(w_in))
    proj3 = proj.reshape(b, t, N_MAIN)
    kvc, kvc_t = _compress(proj3, cmp_pos, cmp_w1, cmp_w2)
    o_c, sel = _cmp_attn(proj3, kvc, kvc_t, slopes_a)
    o_s = _selected_attn_t(proj3, kv_t, sel, slopes_a)
    o_w = _window_attn_t(proj3, kv_t, [slopes_a], q_blk=QA_BLK, kv_blk=KVA_BLK + 2, window=NSA_WINDOW,
                         name="nsa_window")
    lam_init = 0.8 - 0.6 * math.exp(-0.3 * i)
    o_b = _diff_attn_t(proj3, kv_t, slopes_b, diff_lambda, diff_subln, lam_init)
    r_c = C_HEADS // C_KV_HEADS
    o_sw = _window_attn_t(proj3, kv_t, [slopes_c[g * r_c:(g + 1) * r_c] for g in range(C_KV_HEADS)],
                          q_blk=QC_BLK, kv_blk=KVC_BLK, window=SWA_WINDOW, sinks=sinks, name="swa_gqa")
    flat = lambda a: a.reshape(b * t, a.shape[-1])
    return flat(o_c), flat(o_s), flat(o_w), gl, flat(o_b), flat(o_sw)


def kernel(x, p, w_in, cmp_pos, cmp_w1, cmp_w2, diff_lambda, diff_subln, sinks, w_out, ln1_g, ln1_b,
           ffn_w_gate, ffn_w_up, ffn_w_down, moe_router, moe_w_gate, moe_w_up, moe_w_down, ln2_g, ln2_b,
           ple_gate, ple_proj, ln3_g, ln3_b):
    b, t, d = x.shape
    n = b * t
    x2d = x.reshape(n, d)
    for i in range(DEPTH):
        heads = _mixer(x2d, b, t, i, w_in[i], cmp_pos[i], cmp_w1[i], cmp_w2[i],
                       diff_lambda[i], diff_subln[i], sinks[i])
        x1 = _outproj(*heads, x2d, w_out[i], ln1_g[i], ln1_b[i])
        if i % 2 == 0:
            f = _ffn(x1, ffn_w_gate[i // 2], ffn_w_up[i // 2], ffn_w_down[i // 2])
        else:
            xb, gd, scol, srow, base, cnt = _route(x1, moe_router[i // 2])
            cnt_i = cnt[:, 0, :N_EXPERTS].astype(jnp.int32).reshape(-1)
            base_i = base[:, :, :N_EXPERTS].astype(jnp.int32).reshape(-1)
            f = _moe(xb, srow, scol, gd, cnt_i, base_i,
                     moe_w_gate[i // 2], moe_w_up[i // 2], moe_w_down[i // 2])
        x2d = _ple(x1, f, p[i].reshape(n, PLE_DIM), ple_gate[i], ple_proj[i],
                   ln2_g[i], ln2_b[i], ln3_g[i], ln3_b[i])
    return x2d.reshape(b, t, d)
```

```python
import functools
import math

import jax
import jax.numpy as jnp
import numpy as np
from jax import lax
from jax.experimental import pallas as pl
from jax.experimental.pallas import tpu as pltpu

F32 = jnp.float32
BF16 = jnp.bfloat16

D_MODEL = 1024
HEAD_DIM = 64
A_HEADS = 4
B_HEADS = 4
C_HEADS = 8
C_KV_HEADS = 2
DIFF_DK = 32
CMP_LEN = 32
CMP_STRIDE = 16
CMP_HIDDEN = 128
SLC_BLOCK = 64
SLC_SHIFT = 6
SLC_TOPN = 16
NSA_WINDOW = 512
SWA_WINDOW = 128
N_EXPERTS = 8
D_FF_EXPERT = 3584
D_FF_DENSE = 2816
PLE_DIM = 256
LN_EPS = 1e-5
NEG_INF = -1e30
FORCE_SCORE = 1e9
DEPTH = 2
DN_ALPHA = (2 * DEPTH) ** 0.25

LANES = 128
QA_BLK, QB_BLK, QC_BLK, KVB_BLK, KVC_BLK, KVA_BLK, N_MAIN_BLK = 0, 4, 8, 16, 20, 22, 25
N_MAIN = N_MAIN_BLK * LANES

LOG2E = 1.4426950408889634
TQ = 256
TQ_CMP = 512
TQ_DENSE = 512
TK_DENSE = 512
TM = 512
FFN_CHUNK = 1408
MOE_TILE = 2048
MOE_ROWS = 576
MOE_ROWS_RECIP = (1821, 20)
MOE_ROW_ALIGN = 16
MOE_TB = 256
MOE_CHUNK = 896


def _cparams(sem, vmem_mib):
    return pltpu.CompilerParams(dimension_semantics=sem, vmem_limit_bytes=vmem_mib * 1024 * 1024)


def _alibi_slopes():
    n = A_HEADS + B_HEADS + C_HEADS
    s = [LOG2E * 2.0 ** (-8.0 * i / n) for i in range(1, n + 1)]
    rest = s[C_HEADS:]
    return rest[0::2], rest[1::2], s[:C_HEADS]


def _dot(a, b):
    return jnp.dot(a, b, preferred_element_type=F32)


def _dot_nt(a, b):
    return lax.dot_general(a, b, (((1,), (1,)), ((), ())), preferred_element_type=F32)


def _split2(x):
    hi = x.astype(BF16)
    lo = (x - hi.astype(F32)).astype(BF16)
    return hi, lo


def _layer_norm(y, g, b):
    mu = jnp.mean(y, axis=-1, keepdims=True)
    d = y - mu
    var = jnp.mean(d * d, axis=-1, keepdims=True)
    return d * lax.rsqrt(var + LN_EPS) * g + b


def _sigmoid(x):
    return 1.0 / (1.0 + jnp.exp(-x))


def _lane(shape):
    return lax.broadcasted_iota(jnp.int32, shape, 1)


def _heads_to_rows(parts):
    out = [jnp.transpose(jnp.concatenate([a, b], axis=0)) for a, b in zip(parts[0::2], parts[1::2])]
    return out[0] if len(out) == 1 else jnp.concatenate(out, axis=1)


def _proj_kernel(x_ref, w_ref, o_ref, g_ref, t_ref):
    acc = _dot(x_ref[...].astype(BF16), w_ref[...])
    o_ref[...] = acc[:, :N_MAIN].astype(BF16)
    g_ref[...] = acc[:, N_MAIN:]
    t_ref[...] = jnp.transpose(acc[:, KVB_BLK * LANES:N_MAIN]).astype(BF16)


def _project(x2d, w_ext):
    n = x2d.shape[0]
    n_kv = N_MAIN - KVB_BLK * LANES
    return pl.pallas_call(
        _proj_kernel,
        grid=(n // TM,),
        in_specs=[pl.BlockSpec((TM, D_MODEL), lambda i: (i, 0)),
                  pl.BlockSpec((D_MODEL, N_MAIN + LANES), lambda i: (0, 0))],
        out_specs=[pl.BlockSpec((TM, N_MAIN), lambda i: (i, 0)),
                   pl.BlockSpec((TM, LANES), lambda i: (i, 0)),
                   pl.BlockSpec((n_kv, TM), lambda i: (0, i))],
        out_shape=[jax.ShapeDtypeStruct((n, N_MAIN), BF16),
                   jax.ShapeDtypeStruct((n, LANES), F32),
                   jax.ShapeDtypeStruct((n_kv, n), BF16)],
        compiler_params=_cparams(("parallel",), 20),
        name="in_proj",
    )(x2d, w_ext)


def _build_w_in(w):
    d = w.shape[0]
    o = np.cumsum([0, 256, 384, 12, 256, 256, 256, 512, 128, 128])
    q_a, kv_a, g_a, q_b, k_b, v_b, q_c, k_c, v_c = [w[:, o[i]:o[i + 1]] for i in range(9)]
    z = jnp.zeros((d, HEAD_DIM), F32)
    hd = HEAD_DIM
    cols = []
    for h in range(A_HEADS):
        cols += [q_a[:, h * hd:(h + 1) * hd] * (LOG2E * hd ** -0.5), z]
    for h in range(B_HEADS):
        cols += [q_b[:, h * hd:(h + 1) * hd] * (LOG2E * DIFF_DK ** -0.5), z]
    for h in range(C_HEADS):
        cols += [q_c[:, h * hd:(h + 1) * hd] * (LOG2E * hd ** -0.5), z]
    for h in range(B_HEADS):
        cols += [k_b[:, h * hd:(h + 1) * hd], v_b[:, h * hd:(h + 1) * hd]]
    for g in range(C_KV_HEADS):
        cols += [k_c[:, g * hd:(g + 1) * hd], v_c[:, g * hd:(g + 1) * hd]]
    cols.append(kv_a)
    cols += [g_a, jnp.zeros((d, LANES - 12), F32)]
    return jnp.concatenate(cols, axis=1).astype(BF16)


def _compress_kernel(ch_ref, wc_ref, pos_ref, w1_ref, w2_ref, o_ref, ot_ref):
    nck = ch_ref.shape[1]
    u = _dot(ch_ref[0], wc_ref[...])
    hs = []
    for j in range(2):
        pw = _dot(pos_ref[j].astype(BF16), w1_ref[j])[0:1]
        ua = u[:, (2 * j) * LANES:(2 * j + 1) * LANES]
        ub = u[:, (2 * j + 1) * LANES:(2 * j + 2) * LANES]
        pre = ua + pltpu.roll(ub, nck - 1, 0) + pw
        hs.append(0.5 * pre * (1.0 + jnp.tanh(0.7978845608028654 * (pre + 0.044715 * pre * pre * pre))))
    h = jnp.concatenate(hs, axis=1).astype(BF16)
    kvc = _dot(h, w2_ref[...])
    o_ref[0] = kvc.astype(BF16)
    ot_ref[0] = jnp.transpose(kvc).astype(BF16)


def _compress(proj3, cmp_pos, cmp_w1, cmp_w2):
    b, t, _ = proj3.shape
    nck = t // CMP_STRIDE
    kv = proj3[:, :, KVA_BLK * LANES:(KVA_BLK + 1) * LANES]
    chunks = kv.reshape(b, nck, CMP_STRIDE * LANES)
    w1 = cmp_w1.reshape(2, 2, CMP_STRIDE, HEAD_DIM, CMP_HIDDEN)
    z = jnp.zeros((CMP_STRIDE, HEAD_DIM, CMP_HIDDEN), F32)
    blocks = []
    for j in range(2):
        for half in range(2):
            pair = [w1[j, half], z] if j == 0 else [z, w1[j, half]]
            blocks.append(jnp.concatenate(pair, axis=1).reshape(CMP_STRIDE * LANES, CMP_HIDDEN))
    wc = jnp.concatenate(blocks, axis=1).astype(BF16)
    pos = jnp.zeros((2, 8, CMP_LEN * HEAD_DIM), F32).at[:, 0].set(cmp_pos.reshape(2, -1))
    z2 = jnp.zeros((CMP_HIDDEN, HEAD_DIM), F32)
    w2 = jnp.concatenate([jnp.concatenate([cmp_w2[0], z2], axis=1),
                          jnp.concatenate([z2, cmp_w2[1]], axis=1)], axis=0).astype(BF16)
    return pl.pallas_call(
        _compress_kernel,
        grid=(b,),
        in_specs=[pl.BlockSpec((1, nck, CMP_STRIDE * LANES), lambda i: (i, 0, 0)),
                  pl.BlockSpec(wc.shape, lambda i: (0, 0)),
                  pl.BlockSpec(pos.shape, lambda i: (0, 0, 0)),
                  pl.BlockSpec((2, CMP_LEN * HEAD_DIM, CMP_HIDDEN), lambda i: (0, 0, 0)),
                  pl.BlockSpec(w2.shape, lambda i: (0, 0))],
        out_specs=[pl.BlockSpec((1, nck, LANES), lambda i: (i, 0, 0)),
                   pl.BlockSpec((1, LANES, nck), lambda i: (i, 0, 0))],
        out_shape=[jax.ShapeDtypeStruct((b, nck, LANES), BF16),
                   jax.ShapeDtypeStruct((b, LANES, nck), BF16)],
        compiler_params=_cparams(("parallel",), 8),
        name="nsa_compress",
    )(chunks, wc, pos, cmp_w1.astype(BF16), w2)


def _cmp_attn_kernel_t(q_ref, kvc_ref, kvct_ref, ovt_ref, bias_ref, o_ref, sel_ref):
    tq = q_ref.shape[1]
    ncp = kvc_ref.shape[1]
    rows = A_HEADS * tq
    t0 = pl.program_id(1) * tq
    qs = _stack_heads(q_ref[0], A_HEADS)
    u = _dot_nt(kvc_ref[0], qs) + bias_ref[...]
    n = lax.broadcasted_iota(jnp.int32, (ncp, rows), 0)
    block_end = jnp.where(n == ncp - 1, jnp.int32(1 << 30), n * CMP_STRIDE + (CMP_LEN - 1))
    keep = block_end <= jnp.bitwise_and(_lane((ncp, rows)), tq - 1) + t0
    sh = jnp.where(keep, u, NEG_INF)
    m = jnp.max(sh, axis=0, keepdims=True)
    e = jnp.where(keep, jnp.exp2(sh - m), 0.0)
    p = e / jnp.maximum(jnp.sum(e, axis=0, keepdims=True), 1e-30)
    o_t = _dot(kvct_ref[0, HEAD_DIM:, :], p.astype(BF16))
    o_ref[0] = _heads_to_rows([o_t[:, h * tq:(h + 1) * tq] for h in range(A_HEADS)]).astype(BF16)

    psum = p[:, 0:tq] + p[:, tq:2 * tq] + p[:, 2 * tq:3 * tq] + p[:, 3 * tq:4 * tq]
    hi = psum.astype(BF16)
    r1 = psum - hi.astype(F32)
    mid = r1.astype(BF16)
    lo = (r1 - mid.astype(F32)).astype(BF16)
    ovt = ovt_ref[...]
    imp = _dot(ovt, hi) + _dot(ovt, mid) + _dot(ovt, lo)
    nsl = ncp * CMP_STRIDE // SLC_BLOCK
    j = lax.broadcasted_iota(jnp.int32, (LANES, tq), 0)
    t_blk = jnp.right_shift(_lane((LANES, tq)) + t0, SLC_SHIFT)
    forced = (j == 0) | (j == t_blk) | (j == t_blk - 1)
    imp = jnp.where(forced, FORCE_SCORE, jnp.where(j > t_blk, -1.0, imp))
    work = jnp.where(j < nsl, imp, -3.0)
    jf = j.astype(F32)
    sel = jnp.zeros((LANES, tq), F32)
    for _ in range(SLC_TOPN):
        mx = jnp.max(work, axis=0, keepdims=True)
        first = jnp.min(jnp.where(work == mx, jf, float(LANES)), axis=0, keepdims=True)
        pick = jf == first
        sel = jnp.where(pick, 1.0, sel)
        work = jnp.where(pick, -4.0, work)
    sel_ref[0] = jnp.transpose(sel).astype(BF16)


def _cmp_attn(proj3, kvc, kvc_t, slopes):
    b, t, _ = proj3.shape
    ncp = kvc.shape[1]
    nsl = t // SLC_BLOCK
    c0 = np.arange(ncp)[None, :] * CMP_STRIDE
    s0 = np.arange(LANES)[:, None] * SLC_BLOCK
    ovt = np.clip(np.minimum(c0 + CMP_LEN, s0 + SLC_BLOCK) - np.maximum(c0, s0), 0, None) / CMP_LEN
    ovt[:, ncp - 1] = 0.0
    ovt[nsl:, :] = 0.0
    rel = np.arange(TQ_CMP)[None, :] - (np.arange(ncp)[:, None] * CMP_STRIDE + (CMP_LEN - 1))
    bias = np.concatenate([-s * rel for s in slopes], axis=1).astype(np.float32)
    return pl.pallas_call(
        _cmp_attn_kernel_t,
        grid=(b, t // TQ_CMP),
        in_specs=[pl.BlockSpec((1, TQ_CMP, A_HEADS * LANES), lambda bi, i: (bi, i, QA_BLK // A_HEADS)),
                  pl.BlockSpec((1, ncp, LANES), lambda bi, i: (bi, 0, 0)),
                  pl.BlockSpec((1, LANES, ncp), lambda bi, i: (bi, 0, 0)),
                  pl.BlockSpec((LANES, ncp), lambda bi, i: (0, 0)),
                  pl.BlockSpec((ncp, A_HEADS * TQ_CMP), lambda bi, i: (0, 0))],
        out_specs=[pl.BlockSpec((1, TQ_CMP, A_HEADS * HEAD_DIM), lambda bi, i: (bi, i, 0)),
                   pl.BlockSpec((1, TQ_CMP, LANES), lambda bi, i: (bi, i, 0))],
        out_shape=[jax.ShapeDtypeStruct((b, t, A_HEADS * HEAD_DIM), BF16),
                   jax.ShapeDtypeStruct((b, t, LANES), BF16)],
        compiler_params=_cparams(("parallel", "parallel"), 8),
        name="nsa_cmp_attn",
    )(proj3, kvc, kvc_t, jnp.asarray(ovt, BF16), jnp.asarray(bias))


def _stack_heads(q, n):
    return jnp.concatenate([q[:, h * LANES:(h + 1) * LANES] for h in range(n)], axis=0)


def _window_bias(slopes, tq, window):
    wk = window + tq
    n_var = -(-window // tq) + 1
    j = np.arange(wk)[:, None]
    q = np.arange(tq)[None, :]
    out = np.empty((len(slopes), n_var, wk, len(slopes[0]) * tq), np.float32)
    for v in range(n_var):
        d = v * tq - max(v * tq - window, 0) + q - j
        keep = (d >= 0) & (d < window)
        for g, row in enumerate(slopes):
            for r, slope in enumerate(row):
                out[g, v, :, r * tq:(r + 1) * tq] = np.where(keep, -slope * d, NEG_INF)
    return jnp.asarray(out)


def _window_kernel_t(*refs, r_heads, window, has_sink):
    if has_sink:
        sink_ref, q_ref, kv_ref, kvt_ref, bias_ref, o_ref = refs
    else:
        q_ref, kv_ref, kvt_ref, bias_ref, o_ref = refs
    g, i = pl.program_id(1), pl.program_id(2)
    tq = q_ref.shape[1]
    wk = window + tq
    start = pl.multiple_of(jnp.maximum(i * tq - window, 0), LANES)
    qs = _stack_heads(q_ref[0], r_heads)
    u = _dot_nt(kv_ref[0, pl.ds(start, wk), :], qs) + bias_ref[0, 0]
    m = jnp.max(u, axis=0, keepdims=True)
    if has_sink:
        sink = jnp.concatenate([jnp.full((1, tq), sink_ref[g * r_heads + r], F32) for r in range(r_heads)], axis=1)
        m = jnp.maximum(m, sink)
    p = jnp.exp2(u - m)
    den = jnp.sum(p, axis=0, keepdims=True)
    if has_sink:
        den = den + jnp.exp2(sink - m)
    o_t = _dot(kvt_ref[HEAD_DIM:, pl.ds(start, wk)], p.astype(BF16)) / den
    o_ref[0] = _heads_to_rows([o_t[:, r * tq:(r + 1) * tq] for r in range(r_heads)]).astype(BF16)


def _window_attn_t(proj3, kv_t, slopes, *, q_blk, kv_blk, window, sinks=None, name):
    b, t, _ = proj3.shape
    groups, r_heads = len(slopes), len(slopes[0])
    bias = _window_bias(slopes, TQ, window)
    n_var, wk = bias.shape[1], bias.shape[2]
    in_specs = [pl.BlockSpec((1, TQ, r_heads * LANES), lambda bi, g, i: (bi, i, q_blk // r_heads + g)),
                pl.BlockSpec((1, t, LANES), lambda bi, g, i: (bi, 0, kv_blk + g)),
                pl.BlockSpec((LANES, t), lambda bi, g, i: (kv_blk - KVB_BLK + g, bi)),
                pl.BlockSpec((1, 1, wk, r_heads * TQ), lambda bi, g, i: (g, jnp.minimum(i, n_var - 1), 0, 0))]
    args = [proj3, proj3, kv_t, bias]
    if sinks is not None:
        in_specs.insert(0, pl.BlockSpec(memory_space=pltpu.SMEM))
        args.insert(0, sinks.astype(F32) * LOG2E)
    return pl.pallas_call(
        functools.partial(_window_kernel_t, r_heads=r_heads, window=window, has_sink=sinks is not None),
        grid=(b, groups, t // TQ),
        in_specs=in_specs,
        out_specs=pl.BlockSpec((1, TQ, r_heads * HEAD_DIM), lambda bi, g, i: (bi, i, g)),
        out_shape=jax.ShapeDtypeStruct((b, t, groups * r_heads * HEAD_DIM), BF16),
        compiler_params=_cparams(("parallel", "parallel", "parallel"), 12),
        name=name,
    )(*args)


def _flash_bias(stream_slopes, tq, tk):
    ratio = tk // tq
    j = np.arange(tk)[:, None]
    q = np.arange(tq)[None, :]
    out = np.empty((len(stream_slopes), 1 + ratio, tk, len(stream_slopes[0]) * tq), np.float32)
    for s, slopes in enumerate(stream_slopes):
        for h, slope in enumerate(slopes):
            cols = slice(h * tq, (h + 1) * tq)
            out[s, 0, :, cols] = slope * j
            for r in range(ratio):
                out[s, 1 + r, :, cols] = np.where(j <= r * tq + q, slope * j, NEG_INF)
    return jnp.asarray(out)


def _flash_t(streams, kv_ref, kvt_ref, i, *, tq, tk, sel_t=None, interior=None):
    ratio = tk // tq
    n_full = lax.div(i, jnp.int32(ratio))
    rows = streams[0][0].shape[0]

    def tiles(kts, carry, edge):
        starts = [pl.multiple_of(kt * tk, tk) for kt in kts]
        scores = [[_dot_nt(kv_ref[0, pl.ds(k0, tk), s[1]:s[1] + LANES], s[0]) for s in streams]
                  for k0 in starts]
        state = [(m, l) for m, l, _ in carry]
        updates = []
        for n, kt in enumerate(kts):
            variant = 1 + i - kt * ratio if (edge and n == len(kts) - 1) else 0
            keep = None
            if sel_t is not None:
                blk = jnp.right_shift(lax.broadcasted_iota(jnp.int32, (tk, LANES), 0) + kt * tk, SLC_SHIFT)
                expand = jnp.where(_lane((tk, LANES)) == blk, 1.0, 0.0).astype(BF16)
                keep = _dot(expand, sel_t) > 0.5
            k0f = (kt * tk).astype(F32)
            row = []
            for s, (_, _, slope_lane, bias) in enumerate(streams):
                m, l = state[s]
                u = scores[n][s] + bias(variant)
                if keep is not None:
                    u = jnp.where(keep, u, NEG_INF)
                shift = slope_lane * k0f
                m_new = jnp.maximum(m, jnp.max(u, axis=0, keepdims=True) + shift)
                p = jnp.exp2(u - (m_new - shift))
                alpha = jnp.exp2(m - m_new)
                state[s] = (m_new, alpha * l + jnp.sum(p, axis=0, keepdims=True))
                row.append((alpha, p.astype(BF16)))
            updates.append(row)
        accs = [acc for _, _, acc in carry]
        for k0, row in zip(starts, updates):
            for s, (alpha, p) in enumerate(row):
                col = streams[s][1]
                kv_f = kvt_ref[col + HEAD_DIM:col + LANES, pl.ds(k0, tk)]
                accs[s] = alpha * accs[s] + _dot(kv_f, p)
        return tuple((m, l, acc) for (m, l), acc in zip(state, accs))

    init = tuple((jnp.full((1, rows), NEG_INF, F32), jnp.zeros((1, rows), F32), jnp.zeros((HEAD_DIM, rows), F32))
                 for _ in streams)
    if interior is None:
        count, tile_id = n_full, lambda j: j
    else:
        count, tile_id = interior
    carry = lax.fori_loop(0, jnp.right_shift(count, 1),
                          lambda j, c: tiles([tile_id(2 * j), tile_id(2 * j + 1)], c, False), init)
    carry = lax.cond(jnp.bitwise_and(count, 1) == 1,
                     lambda c: tiles([tile_id(count - 1), n_full], c, True),
                     lambda c: tiles([n_full], c, True), carry)
    return [(l, acc) for _, l, acc in carry]


def _selected_kernel_t(count_ref, ids_ref, q_ref, kv_ref, kvt_ref, sel_ref, bias_ref, o_ref, *, slopes):
    tq = q_ref.shape[1]
    i = pl.program_id(1)
    step = pl.program_id(0) * pl.num_programs(1) + i
    n_kt = kv_ref.shape[1] // TK_DENSE
    interior = (count_ref[step], lambda j: ids_ref[step * n_kt + j])
    sel_t = jnp.transpose(sel_ref[0].astype(F32)).astype(BF16)
    sel_t = jnp.concatenate([sel_t, sel_t], axis=1)
    streams = []
    for pair in range(A_HEADS // 2):
        qs = jnp.concatenate([q_ref[0, :, h * LANES:(h + 1) * LANES] for h in (2 * pair, 2 * pair + 1)], axis=0)
        slope_lane = jnp.concatenate([jnp.full((1, tq), slopes[h], F32) for h in (2 * pair, 2 * pair + 1)], axis=1)
        streams.append((qs, 0, slope_lane, functools.partial(lambda s, v: bias_ref[s, v], pair)))
    outs = []
    for l, acc in _flash_t(streams, kv_ref, kvt_ref, i, tq=tq, tk=TK_DENSE, sel_t=sel_t, interior=interior):
        o_t = acc / l
        outs += [o_t[:, :tq], o_t[:, tq:]]
    o_ref[0] = _heads_to_rows(outs).astype(BF16)


def _selected_attn_t(proj3, kv_t, sel, slopes):
    b, t, _ = proj3.shape
    kv_blk = KVA_BLK + 1
    bias = _flash_bias([slopes[0:2], slopes[2:4]], TQ_DENSE, TK_DENSE)
    nq, nk, per = t // TQ_DENSE, t // TK_DENSE, TK_DENSE // SLC_BLOCK
    picked = jnp.max(sel.reshape(b, nq, TQ_DENSE, LANES)[..., :nk * per], axis=2) > 0
    active = jnp.any(picked.reshape(b, nq, nk, per), axis=-1)
    past = jnp.arange(nk)[None, :] < (jnp.arange(nq) * TQ_DENSE // TK_DENSE)[:, None]
    active = active & past[None]
    ids = jnp.argsort(~active, axis=-1, stable=True).astype(jnp.int32).reshape(-1)
    counts = jnp.sum(active, axis=-1).astype(jnp.int32).reshape(-1)
    grid_spec = pltpu.PrefetchScalarGridSpec(
        num_scalar_prefetch=2,
        grid=(b, nq),
        in_specs=[pl.BlockSpec((1, TQ_DENSE, A_HEADS * LANES), lambda bi, i, *_: (bi, i, QA_BLK // A_HEADS)),
                  pl.BlockSpec((1, t, LANES), lambda bi, i, *_: (bi, 0, kv_blk)),
                  pl.BlockSpec((LANES, t), lambda bi, i, *_: (kv_blk - KVB_BLK, bi)),
                  pl.BlockSpec((1, TQ_DENSE, LANES), lambda bi, i, *_: (bi, i, 0)),
                  pl.BlockSpec(bias.shape, lambda bi, i, *_: (0, 0, 0, 0), pipeline_mode=pl.Buffered(1))],
        out_specs=pl.BlockSpec((1, TQ_DENSE, A_HEADS * HEAD_DIM), lambda bi, i, *_: (bi, i, 0)),
    )
    return pl.pallas_call(
        functools.partial(_selected_kernel_t, slopes=slopes),
        grid_spec=grid_spec,
        out_shape=jax.ShapeDtypeStruct((b, t, A_HEADS * HEAD_DIM), BF16),
        compiler_params=_cparams(("parallel", "parallel"), 32),
        name="nsa_selected",
    )(counts, ids, proj3, proj3, kv_t, sel, bias)


def _diff_kernel_t(slope_ref, dl_ref, g_ref, q_ref, kv_ref, kvt_ref, bias_ref, o_ref, *, lam_init):
    tq = q_ref.shape[1]
    hp = pl.program_id(1)
    i = pl.program_id(2)
    dl = dl_ref[...]
    lam = (jnp.exp(jnp.sum(dl[0:1] * dl[1:2], axis=-1, keepdims=True))
           - jnp.exp(jnp.sum(dl[2:3] * dl[3:4], axis=-1, keepdims=True)) + lam_init)
    lane = _lane((tq, LANES))
    streams = []
    for hh in range(2):
        q = q_ref[0, :, hh * LANES:(hh + 1) * LANES]
        zero = jnp.zeros_like(q)
        qs = jnp.concatenate([jnp.where(lane < DIFF_DK, q, zero),
                              jnp.where(lane >= DIFF_DK, q, zero)], axis=0)
        slope_lane = jnp.full((1, 2 * tq), slope_ref[hp * 2 + hh], F32)
        streams.append((qs, hh * LANES, slope_lane, functools.partial(lambda s, v: bias_ref[s, v], hh)))
    outs = []
    for l, acc in _flash_t(streams, kv_ref, kvt_ref, i, tq=tq, tk=TK_DENSE):
        o = acc / l
        w = o[:, :tq] - lam * o[:, tq:]
        ms = jnp.sum(w * w, axis=0, keepdims=True) * (1.0 / HEAD_DIM)
        outs.append(w * lax.rsqrt(ms + LN_EPS) * (1.0 - lam_init))
    o_ref[0] = (_heads_to_rows(outs) * g_ref[...]).astype(BF16)


def _diff_attn_t(proj3, kv_t, slopes, diff_lambda, subln, lam_init):
    b, t, _ = proj3.shape
    g_ext = jnp.tile(subln.reshape(1, HEAD_DIM).astype(F32), (1, 2))
    bias = _flash_bias([[s, s] for s in slopes], TQ_DENSE, TK_DENSE)
    return pl.pallas_call(
        functools.partial(_diff_kernel_t, lam_init=lam_init),
        grid=(b, B_HEADS // 2, t // TQ_DENSE),
        in_specs=[pl.BlockSpec(memory_space=pltpu.SMEM),
                  pl.BlockSpec((4, DIFF_DK), lambda bi, h, i: (0, 0)),
                  pl.BlockSpec((1, LANES), lambda bi, h, i: (0, 0)),
                  pl.BlockSpec((1, TQ_DENSE, 2 * LANES), lambda bi, h, i: (bi, i, QB_BLK // 2 + h)),
                  pl.BlockSpec((1, t, 2 * LANES), lambda bi, h, i: (bi, 0, KVB_BLK // 2 + h)),
                  pl.BlockSpec((2 * LANES, t), lambda bi, h, i: (h, bi)),
                  pl.BlockSpec((2,) + bias.shape[1:], lambda bi, h, i: (h, 0, 0, 0))],
        out_specs=pl.BlockSpec((1, TQ_DENSE, 2 * HEAD_DIM), lambda bi, h, i: (bi, i, h)),
        out_shape=jax.ShapeDtypeStruct((b, t, B_HEADS * HEAD_DIM), BF16),
        compiler_params=_cparams(("parallel", "parallel", "parallel"), 40),
        name="diff_attn",
    )(jnp.asarray(slopes, F32), diff_lambda.astype(F32), g_ext, proj3, proj3, kv_t, bias)


def _outproj_kernel(oc_ref, os_ref, ow_ref, gl_ref, ob_ref, osw_ref, x_ref, w_ref, ex_ref, g_ref, b_ref, o_ref):
    hi, lo = _split2(_sigmoid(gl_ref[...]))
    oa = None
    for j, ref in enumerate((oc_ref, os_ref, ow_ref)):
        gate = _dot(hi, ex_ref[j]) + _dot(lo, ex_ref[j])
        term = gate * ref[...].astype(F32)
        oa = term if oa is None else oa + term
    cat = jnp.concatenate([oa.astype(BF16), ob_ref[...], osw_ref[...]], axis=1)
    y = DN_ALPHA * x_ref[...] + _dot(cat, w_ref[...])
    o_ref[...] = _layer_norm(y, g_ref[...], b_ref[...])


def _outproj(o_c, o_s, o_w, gl, o_b, o_sw, x2d, w_out, ln_g, ln_b):
    n = x2d.shape[0]
    ex = np.zeros((3, LANES, A_HEADS * HEAD_DIM), np.float32)
    for j in range(3):
        for h in range(A_HEADS):
            ex[j, h * 3 + j, h * HEAD_DIM:(h + 1) * HEAD_DIM] = 1.0
    row = lambda w: pl.BlockSpec((TM, w), lambda i: (i, 0))
    full = lambda s: pl.BlockSpec(s, lambda i: (0,) * len(s))
    return pl.pallas_call(
        _outproj_kernel,
        grid=(n // TM,),
        in_specs=[row(256), row(256), row(256), row(LANES), row(256), row(512), row(D_MODEL),
                  full((D_MODEL, D_MODEL)), full(ex.shape), full((1, D_MODEL)), full((1, D_MODEL))],
        out_specs=row(D_MODEL),
        out_shape=jax.ShapeDtypeStruct((n, D_MODEL), F32),
        compiler_params=_cparams(("parallel",), 16),
        name="out_proj_ln",
    )(o_c, o_s, o_w, gl, o_b, o_sw, x2d, w_out.astype(BF16), jnp.asarray(ex, BF16),
      ln_g.reshape(1, -1), ln_b.reshape(1, -1))


def _ffn_kernel(x_ref, wg_ref, wu_ref, wd_ref, o_ref):
    xb = x_ref[...].astype(BF16)
    acc = None
    for c in range(D_FF_DENSE // FFN_CHUNK):
        cols = slice(c * FFN_CHUNK, (c + 1) * FFN_CHUNK)
        g = _dot(xb, wg_ref[:, cols])
        u = _dot(xb, wu_ref[:, cols])
        y = _dot((g * _sigmoid(g) * u).astype(BF16), wd_ref[cols, :])
        acc = y if acc is None else acc + y
    o_ref[...] = acc


def _ffn(x2d, wg, wu, wd):
    n = x2d.shape[0]
    resident = lambda shape: pl.BlockSpec(shape, lambda i: (0, 0), pipeline_mode=pl.Buffered(1))
    return pl.pallas_call(
        _ffn_kernel,
        grid=(n // TM,),
        in_specs=[pl.BlockSpec((TM, D_MODEL), lambda i: (i, 0)),
                  resident((D_MODEL, D_FF_DENSE)), resident((D_MODEL, D_FF_DENSE)), resident((D_FF_DENSE, D_MODEL))],
        out_specs=pl.BlockSpec((TM, D_MODEL), lambda i: (i, 0)),
        out_shape=jax.ShapeDtypeStruct((n, D_MODEL), F32),
        compiler_params=_cparams(("parallel",), 24),
        name="dense_swiglu",
    )(x2d, wg.astype(BF16), wu.astype(BF16), wd.astype(BF16))


def _router_kernel(x_ref, w_ref, tri_ref, xb_ref, gd_ref, scol_ref, srow_ref, base_ref, cnt_ref):
    tt = x_ref.shape[0]
    x = x_ref[...]
    xb_ref[...] = x.astype(BF16)
    xh, xl = _split2(x)
    wh, wl = _split2(w_ref[...])
    logits = _dot(xh, wh) + _dot(xh, wl) + _dot(xl, wh)
    lane = _lane((tt, LANES))
    lf = lane.astype(F32)
    logits = jnp.where(lane < N_EXPERTS, logits, -jnp.inf)
    m1 = jnp.max(logits, axis=-1, keepdims=True)
    i1 = jnp.min(jnp.where(logits == m1, lf, float(LANES)), axis=-1, keepdims=True)
    rest = jnp.where(lf == i1, -jnp.inf, logits)
    m2 = jnp.max(rest, axis=-1, keepdims=True)
    i2 = jnp.min(jnp.where(rest == m2, lf, float(LANES)), axis=-1, keepdims=True)
    e2 = jnp.exp(m2 - m1)
    g1 = 1.0 / (1.0 + e2)
    g2 = e2 / (1.0 + e2)
    first, second = lf == i1, lf == i2
    gd_ref[...] = jnp.where(first, g1, 0.0) + jnp.where(second, g2, 0.0)
    ind = jnp.where(first, 1.0, jnp.where(second, 1.0, 0.0))
    tri = tri_ref[...]
    run = jnp.zeros((1, LANES), F32)
    slots, bases = [], []
    for tb in range(tt // MOE_TB):
        blk = ind[tb * MOE_TB:(tb + 1) * MOE_TB]
        rank = _dot(tri, blk.astype(BF16)) + run
        slots.append(jnp.where(blk > 0.5, rank, -1.0))
        bases.append(run)
        run = run + jnp.sum(blk, axis=0, keepdims=True)
    slot = jnp.concatenate(slots, axis=0)
    scol_ref[...] = slot
    srow_ref[...] = jnp.transpose(slot)[0:N_EXPERTS]
    base_ref[0] = jnp.concatenate(bases, axis=0)
    cnt_ref[0] = jnp.broadcast_to(run, (8, LANES))


def _route(x2d, w_router):
    n = x2d.shape[0]
    nt = n // MOE_TILE
    ntb = MOE_TILE // MOE_TB
    w = jnp.zeros((D_MODEL, LANES), F32).at[:, :N_EXPERTS].set(w_router)
    tri = jnp.asarray(np.tril(np.ones((MOE_TB, MOE_TB), np.float32), -1), BF16)
    return pl.pallas_call(
        _router_kernel,
        grid=(nt,),
        in_specs=[pl.BlockSpec((MOE_TILE, D_MODEL), lambda i: (i, 0)),
                  pl.BlockSpec((D_MODEL, LANES), lambda i: (0, 0)),
                  pl.BlockSpec((MOE_TB, MOE_TB), lambda i: (0, 0))],
        out_specs=[pl.BlockSpec((MOE_TILE, D_MODEL), lambda i: (i, 0)),
                   pl.BlockSpec((MOE_TILE, LANES), lambda i: (i, 0)),
                   pl.BlockSpec((MOE_TILE, LANES), lambda i: (i, 0)),
                   pl.BlockSpec((N_EXPERTS, MOE_TILE), lambda i: (0, i)),
                   pl.BlockSpec((1, ntb, LANES), lambda i: (i, 0, 0)),
                   pl.BlockSpec((1, 8, LANES), lambda i: (i, 0, 0))],
        out_shape=[jax.ShapeDtypeStruct((n, D_MODEL), BF16),
                   jax.ShapeDtypeStruct((n, LANES), F32),
                   jax.ShapeDtypeStruct((n, LANES), F32),
                   jax.ShapeDtypeStruct((N_EXPERTS, n), F32),
                   jax.ShapeDtypeStruct((nt, ntb, LANES), F32),
                   jax.ShapeDtypeStruct((nt, 8, LANES), F32)],
        compiler_params=_cparams(("parallel",), 40),
        name="moe_router",
    )(x2d, w, tri)


MOE_GATHER_W = MOE_TB + 16
MOE_SCATTER_W = MOE_TB + 8
MOE_CAP = MOE_TILE + 512


def _moe_kernel(cnt_ref, base_ref, xb_ref, srow_ref, scol_ref, gd_ref, wg_ref, wu_ref, wd_ref,
                o_ref, xc_ref, yc_ref):
    i, e, c = pl.program_id(0), pl.program_id(1), pl.program_id(2)
    nc = pl.num_programs(2)
    ntb = MOE_TILE // MOE_TB
    count = cnt_ref[i * N_EXPERTS + e]
    n_full = jnp.right_shift(count * MOE_ROWS_RECIP[0], MOE_ROWS_RECIP[1])
    rem = count - n_full * MOE_ROWS

    @pl.when((i == 0) & (e == 0) & (c == 0))
    def _():
        yc_ref[...] = jnp.zeros_like(yc_ref)

    @pl.when((e == 0) & (c == 0))
    def _():
        o_ref[...] = jnp.zeros_like(o_ref)

    @pl.when(c == 0)
    def _():
        xc_ref[...] = jnp.zeros_like(xc_ref)
        srow = srow_ref[pl.ds(e, 1), :]
        rid = lax.broadcasted_iota(jnp.int32, (MOE_GATHER_W, MOE_TB), 0).astype(F32)
        for tb in range(ntb):
            b0 = base_ref[(i * ntb + tb) * N_EXPERTS + e]
            b_al = pl.multiple_of(jnp.left_shift(jnp.right_shift(b0, 4), 4), 16)
            rel = srow[:, tb * MOE_TB:(tb + 1) * MOE_TB] - b_al.astype(F32)
            onehot = jnp.where(rel == rid, 1.0, 0.0).astype(BF16)
            rows = _dot(onehot, xb_ref[tb * MOE_TB:(tb + 1) * MOE_TB, :])
            cur = xc_ref[pl.ds(b_al, MOE_GATHER_W), :].astype(F32)
            xc_ref[pl.ds(b_al, MOE_GATHER_W), :] = (cur + rows).astype(BF16)

    def expert_rows(r0, n_rows):
        xr = xc_ref[pl.ds(r0, n_rows), :]
        g = _dot(xr, wg_ref[0])
        u = _dot(xr, wu_ref[0])
        y = _dot((g * _sigmoid(g) * u).astype(BF16), wd_ref[0])
        prev = yc_ref[pl.ds(r0, n_rows), :]
        yc_ref[pl.ds(r0, n_rows), :] = jnp.where(c == 0, y, prev + y)

    def rows_body(rb, carry):
        expert_rows(pl.multiple_of(rb * MOE_ROWS, MOE_ROW_ALIGN), MOE_ROWS)
        return carry

    lax.fori_loop(0, n_full + jnp.where(rem > MOE_ROWS // 2, 1, 0), rows_body, 0)

    @pl.when((rem > 0) & (rem <= MOE_ROWS // 2))
    def _():
        expert_rows(pl.multiple_of(n_full * MOE_ROWS, MOE_ROW_ALIGN), MOE_ROWS // 2)

    @pl.when(c == nc - 1)
    def _():
        lane = _lane((MOE_TB, LANES))
        cid = lax.broadcasted_iota(jnp.int32, (MOE_TB, MOE_SCATTER_W), 1).astype(F32)
        for tb in range(ntb):
            b0 = base_ref[(i * ntb + tb) * N_EXPERTS + e]
            b_al = pl.multiple_of(jnp.left_shift(jnp.right_shift(b0, 3), 3), 8)
            rows = slice(tb * MOE_TB, (tb + 1) * MOE_TB)
            slot = jnp.sum(jnp.where(lane == e, scol_ref[rows, :], 0.0), axis=-1, keepdims=True)
            gate = jnp.sum(jnp.where(lane == e, gd_ref[rows, :], 0.0), axis=-1, keepdims=True)
            onehot = jnp.where(slot - b_al.astype(F32) == cid, 1.0, 0.0).astype(BF16)
            y_win = yc_ref[pl.ds(b_al, MOE_SCATTER_W), :].astype(BF16)
            o_ref[rows, :] += gate * _dot(onehot, y_win)


def _moe(xb, srow, scol, gd, cnt, base, wg, wu, wd):
    n = xb.shape[0]
    nt = n // MOE_TILE
    nc = D_FF_EXPERT // MOE_CHUNK
    grid_spec = pltpu.PrefetchScalarGridSpec(
        num_scalar_prefetch=2,
        grid=(nt, N_EXPERTS, nc),
        in_specs=[pl.BlockSpec((MOE_TILE, D_MODEL), lambda i, e, c, *_: (i, 0), pipeline_mode=pl.Buffered(1)),
                  pl.BlockSpec((N_EXPERTS, MOE_TILE), lambda i, e, c, *_: (0, i)),
                  pl.BlockSpec((MOE_TILE, LANES), lambda i, e, c, *_: (i, 0)),
                  pl.BlockSpec((MOE_TILE, LANES), lambda i, e, c, *_: (i, 0)),
                  pl.BlockSpec((1, D_MODEL, MOE_CHUNK), lambda i, e, c, *_: (e, 0, c)),
                  pl.BlockSpec((1, D_MODEL, MOE_CHUNK), lambda i, e, c, *_: (e, 0, c)),
                  pl.BlockSpec((1, MOE_CHUNK, D_MODEL), lambda i, e, c, *_: (e, c, 0))],
        out_specs=pl.BlockSpec((MOE_TILE, D_MODEL), lambda i, e, c, *_: (i, 0), pipeline_mode=pl.Buffered(1)),
        scratch_shapes=[pltpu.VMEM((MOE_CAP, D_MODEL), BF16), pltpu.VMEM((MOE_CAP, D_MODEL), F32)],
    )
    return pl.pallas_call(
        _moe_kernel,
        grid_spec=grid_spec,
        out_shape=jax.ShapeDtypeStruct((n, D_MODEL), F32),
        compiler_params=_cparams(("arbitrary", "arbitrary", "arbitrary"), 48),
        name="moe_experts",
    )(cnt, base, xb, srow, scol, gd, wg.astype(BF16), wu.astype(BF16), wd.astype(BF16))


def _ple_kernel(x_ref, f_ref, p_ref, wg_ref, wp_ref, g2_ref, b2_ref, g3_ref, b3_ref, o_ref):
    x2 = _layer_norm(DN_ALPHA * x_ref[...] + f_ref[...], g2_ref[...], b2_ref[...])
    e = _sigmoid(_dot(x2.astype(BF16), wg_ref[...])) * _dot(p_ref[...].astype(BF16), wp_ref[...])
    o_ref[...] = _layer_norm(DN_ALPHA * x2 + e, g3_ref[...], b3_ref[...])


def _ple(x2d, f2d, p2d, w_gate, w_proj, g2, b2, g3, b3):
    n = x2d.shape[0]
    row = lambda w: pl.BlockSpec((TM, w), lambda i: (i, 0))
    full = lambda s: pl.BlockSpec(s, lambda i: (0,) * len(s))
    vec = full((1, D_MODEL))
    return pl.pallas_call(
        _ple_kernel,
        grid=(n // TM,),
        in_specs=[row(D_MODEL), row(D_MODEL), row(PLE_DIM), full((D_MODEL, D_MODEL)), full((PLE_DIM, D_MODEL)),
                  vec, vec, vec, vec],
        out_specs=row(D_MODEL),
        out_shape=jax.ShapeDtypeStruct((n, D_MODEL), F32),
        compiler_params=_cparams(("parallel",), 20),
        name="ln_ple_ln",
    )(x2d, f2d, p2d, w_gate.astype(BF16), w_proj.astype(BF16),
      g2.reshape(1, -1), b2.reshape(1, -1), g3.reshape(1, -1), b3.reshape(1, -1))


def _mixer(x2d, b, t, i, w_in, cmp_pos, cmp_w1, cmp_w2, diff_lambda, diff_subln, sinks):
    slopes_a, slopes_b, slopes_c = _alibi_slopes()
    proj, gl, kv_t = _project(x2d, _build_w_in(w_in))
    proj3 = proj.reshape(b, t, N_MAIN)
    kvc, kvc_t = _compress(proj3, cmp_pos, cmp_w1, cmp_w2)
    o_c, sel = _cmp_attn(proj3, kvc, kvc_t, slopes_a)
    o_s = _selected_attn_t(proj3, kv_t, sel, slopes_a)
    o_w = _window_attn_t(proj3, kv_t, [slopes_a], q_blk=QA_BLK, kv_blk=KVA_BLK + 2, window=NSA_WINDOW,
                         name="nsa_window")
    lam_init = 0.8 - 0.6 * math.exp(-0.3 * i)
    o_b = _diff_attn_t(proj3, kv_t, slopes_b, diff_lambda, diff_subln, lam_init)
    r_c = C_HEADS // C_KV_HEADS
    o_sw = _window_attn_t(proj3, kv_t, [slopes_c[g * r_c:(g + 1) * r_c] for g in range(C_KV_HEADS)],
                          q_blk=QC_BLK, kv_blk=KVC_BLK, window=SWA_WINDOW, sinks=sinks, name="swa_gqa")
    flat = lambda a: a.reshape(b * t, a.shape[-1])
    return flat(o_c), flat(o_s), flat(o_w), gl, flat(o_b), flat(o_sw)


def kernel(x, p, w_in, cmp_pos, cmp_w1, cmp_w2, diff_lambda, diff_subln, sinks, w_out, ln1_g, ln1_b,
           ffn_w_gate, ffn_w_up, ffn_w_down, moe_router, moe_w_gate, moe_w_up, moe_w_down, ln2_g, ln2_b,
           ple_gate, ple_proj, ln3_g, ln3_b):
    b, t, d = x.shape
    n = b * t
    x2d = x.reshape(n, d)
    for i in range(DEPTH):
        heads = _mixer(x2d, b, t, i, w_in[i], cmp_pos[i], cmp_w1[i], cmp_w2[i],
                       diff_lambda[i], diff_subln[i], sinks[i])
        x1 = _outproj(*heads, x2d, w_out[i], ln1_g[i], ln1_b[i])
        if i % 2 == 0:
            f = _ffn(x1, ffn_w_gate[i // 2], ffn_w_up[i // 2], ffn_w_down[i // 2])
        else:
            xb, gd, scol, srow, base, cnt = _route(x1, moe_router[i // 2])
            cnt_i = cnt[:, 0, :N_EXPERTS].astype(jnp.int32).reshape(-1)
            base_i = base[:, :, :N_EXPERTS].astype(jnp.int32).reshape(-1)
            f = _moe(xb, srow, scol, gd, cnt_i, base_i,
                     moe_w_gate[i // 2], moe_w_up[i // 2], moe_w_down[i // 2])
        x2d = _ple(x1, f, p[i].reshape(n, PLE_DIM), ple_gate[i], ple_proj[i],
                   ln2_g[i], ln2_b[i], ln3_g[i], ln3_b[i])
    return x2d.reshape(b, t, d)
```

```python
import functools
import math

import jax
import jax.numpy as jnp
import numpy as np
from jax import lax
from jax.experimental import pallas as pl
from jax.experimental.pallas import tpu as pltpu

F32 = jnp.float32
BF16 = jnp.bfloat16

D_MODEL = 1024
HEAD_DIM = 64
A_HEADS = 4
B_HEADS = 4
C_HEADS = 8
C_KV_HEADS = 2
DIFF_DK = 32
CMP_LEN = 32
CMP_STRIDE = 16
CMP_HIDDEN = 128
SLC_BLOCK = 64
SLC_SHIFT = 6
SLC_TOPN = 16
NSA_WINDOW = 512
SWA_WINDOW = 128
N_EXPERTS = 8
D_FF_EXPERT = 3584
D_FF_DENSE = 2816
PLE_DIM = 256
LN_EPS = 1e-5
NEG_INF = -1e30
FORCE_SCORE = 1e9
DEPTH = 2
DN_ALPHA = (2 * DEPTH) ** 0.25

LANES = 128
QA_BLK, QB_BLK, QC_BLK, KVB_BLK, KVC_BLK, KVA_BLK, N_MAIN_BLK = 0, 4, 8, 16, 20, 22, 25
N_MAIN = N_MAIN_BLK * LANES

LOG2E = 1.4426950408889634
TQ = 256
TQ_CMP = 512
TQ_DENSE = 512
TK_DENSE = 512
TM = 512
FFN_CHUNK = 1408
MOE_TILE = 2048
MOE_ROWS = 576
MOE_ROWS_RECIP = (1821, 20)
MOE_ROW_ALIGN = 16
MOE_TB = 256
MOE_CHUNK = 896


def _cparams(sem, vmem_mib):
    return pltpu.CompilerParams(dimension_semantics=sem, vmem_limit_bytes=vmem_mib * 1024 * 1024)


def _alibi_slopes():
    n = A_HEADS + B_HEADS + C_HEADS
    s = [LOG2E * 2.0 ** (-8.0 * i / n) for i in range(1, n + 1)]
    rest = s[C_HEADS:]
    return rest[0::2], rest[1::2], s[:C_HEADS]


def _dot(a, b):
    return jnp.dot(a, b, preferred_element_type=F32)


def _dot_nt(a, b):
    return lax.dot_general(a, b, (((1,), (1,)), ((), ())), preferred_element_type=F32)


def _split2(x):
    hi = x.astype(BF16)
    lo = (x - hi.astype(F32)).astype(BF16)
    return hi, lo


def _layer_norm(y, g, b):
    mu = jnp.mean(y, axis=-1, keepdims=True)
    d = y - mu
    var = jnp.mean(d * d, axis=-1, keepdims=True)
    return d * lax.rsqrt(var + LN_EPS) * g + b


def _sigmoid(x):
    return 1.0 / (1.0 + jnp.exp(-x))


def _lane(shape):
    return lax.broadcasted_iota(jnp.int32, shape, 1)


def _heads_to_rows(parts):
    out = [jnp.transpose(jnp.concatenate([a, b], axis=0)) for a, b in zip(parts[0::2], parts[1::2])]
    return out[0] if len(out) == 1 else jnp.concatenate(out, axis=1)


N_Q = (A_HEADS + B_HEADS + C_HEADS) * HEAD_DIM
N_KV = N_MAIN - KVB_BLK * LANES


def _proj_kernel(x_ref, w_ref, o_ref, g_ref, t_ref):
    acc = _dot(x_ref[...].astype(BF16), w_ref[...])
    lane = _lane((acc.shape[0], LANES))
    for pair in range(N_Q // LANES):
        a = acc[:, pair * LANES:(pair + 1) * LANES]
        o_ref[:, (2 * pair) * LANES:(2 * pair + 1) * LANES] = jnp.where(lane < HEAD_DIM, a, 0.0).astype(BF16)
        o_ref[:, (2 * pair + 1) * LANES:(2 * pair + 2) * LANES] = (
            jnp.where(lane < HEAD_DIM, pltpu.roll(a, HEAD_DIM, 1), 0.0).astype(BF16))
    kv = acc[:, N_Q:N_Q + N_KV]
    o_ref[:, KVB_BLK * LANES:] = kv.astype(BF16)
    g_ref[...] = acc[:, N_Q + N_KV:]
    t_ref[...] = jnp.transpose(kv).astype(BF16)


def _project(x2d, w_ext):
    n = x2d.shape[0]
    n_kv = N_KV
    return pl.pallas_call(
        _proj_kernel,
        grid=(n // TM,),
        in_specs=[pl.BlockSpec((TM, D_MODEL), lambda i: (i, 0)),
                  pl.BlockSpec((D_MODEL, N_Q + N_KV + LANES), lambda i: (0, 0))],
        out_specs=[pl.BlockSpec((TM, N_MAIN), lambda i: (i, 0)),
                   pl.BlockSpec((TM, LANES), lambda i: (i, 0)),
                   pl.BlockSpec((n_kv, TM), lambda i: (0, i))],
        out_shape=[jax.ShapeDtypeStruct((n, N_MAIN), BF16),
                   jax.ShapeDtypeStruct((n, LANES), F32),
                   jax.ShapeDtypeStruct((n_kv, n), BF16)],
        compiler_params=_cparams(("parallel",), 32),
        name="in_proj",
    )(x2d, w_ext)


def _build_w_in(w):
    d = w.shape[0]
    o = np.cumsum([0, 256, 384, 12, 256, 256, 256, 512, 128, 128])
    q_a, kv_a, g_a, q_b, k_b, v_b, q_c, k_c, v_c = [w[:, o[i]:o[i + 1]] for i in range(9)]
    hd = HEAD_DIM
    cols = [q_a * (LOG2E * hd ** -0.5), q_b * (LOG2E * DIFF_DK ** -0.5), q_c * (LOG2E * hd ** -0.5)]
    for h in range(B_HEADS):
        cols += [k_b[:, h * hd:(h + 1) * hd], v_b[:, h * hd:(h + 1) * hd]]
    for g in range(C_KV_HEADS):
        cols += [k_c[:, g * hd:(g + 1) * hd], v_c[:, g * hd:(g + 1) * hd]]
    cols.append(kv_a)
    cols += [g_a, jnp.zeros((d, LANES - 12), F32)]
    return jnp.concatenate(cols, axis=1).astype(BF16)


def _compress_kernel(ch_ref, wc_ref, pos_ref, w1_ref, w2_ref, o_ref, ot_ref):
    nck = ch_ref.shape[1]
    u = _dot(ch_ref[0], wc_ref[...])
    hs = []
    for j in range(2):
        pw = _dot(pos_ref[j].astype(BF16), w1_ref[j])[0:1]
        ua = u[:, (2 * j) * LANES:(2 * j + 1) * LANES]
        ub = u[:, (2 * j + 1) * LANES:(2 * j + 2) * LANES]
        pre = ua + pltpu.roll(ub, nck - 1, 0) + pw
        hs.append(0.5 * pre * (1.0 + jnp.tanh(0.7978845608028654 * (pre + 0.044715 * pre * pre * pre))))
    h = jnp.concatenate(hs, axis=1).astype(BF16)
    kvc = _dot(h, w2_ref[...])
    o_ref[0] = kvc.astype(BF16)
    ot_ref[0] = jnp.transpose(kvc).astype(BF16)


def _compress(proj3, cmp_pos, cmp_w1, cmp_w2):
    b, t, _ = proj3.shape
    nck = t // CMP_STRIDE
    kv = proj3[:, :, KVA_BLK * LANES:(KVA_BLK + 1) * LANES]
    chunks = kv.reshape(b, nck, CMP_STRIDE * LANES)
    w1 = cmp_w1.reshape(2, 2, CMP_STRIDE, HEAD_DIM, CMP_HIDDEN)
    z = jnp.zeros((CMP_STRIDE, HEAD_DIM, CMP_HIDDEN), F32)
    blocks = []
    for j in range(2):
        for half in range(2):
            pair = [w1[j, half], z] if j == 0 else [z, w1[j, half]]
            blocks.append(jnp.concatenate(pair, axis=1).reshape(CMP_STRIDE * LANES, CMP_HIDDEN))
    wc = jnp.concatenate(blocks, axis=1).astype(BF16)
    pos = jnp.zeros((2, 8, CMP_LEN * HEAD_DIM), F32).at[:, 0].set(cmp_pos.reshape(2, -1))
    z2 = jnp.zeros((CMP_HIDDEN, HEAD_DIM), F32)
    w2 = jnp.concatenate([jnp.concatenate([cmp_w2[0], z2], axis=1),
                          jnp.concatenate([z2, cmp_w2[1]], axis=1)], axis=0).astype(BF16)
    return pl.pallas_call(
        _compress_kernel,
        grid=(b,),
        in_specs=[pl.BlockSpec((1, nck, CMP_STRIDE * LANES), lambda i: (i, 0, 0)),
                  pl.BlockSpec(wc.shape, lambda i: (0, 0)),
                  pl.BlockSpec(pos.shape, lambda i: (0, 0, 0)),
                  pl.BlockSpec((2, CMP_LEN * HEAD_DIM, CMP_HIDDEN), lambda i: (0, 0, 0)),
                  pl.BlockSpec(w2.shape, lambda i: (0, 0))],
        out_specs=[pl.BlockSpec((1, nck, LANES), lambda i: (i, 0, 0)),
                   pl.BlockSpec((1, LANES, nck), lambda i: (i, 0, 0))],
        out_shape=[jax.ShapeDtypeStruct((b, nck, LANES), BF16),
                   jax.ShapeDtypeStruct((b, LANES, nck), BF16)],
        compiler_params=_cparams(("parallel",), 16),
        name="nsa_compress",
    )(chunks, wc, pos, cmp_w1.astype(BF16), w2)


def _cmp_attn_kernel_t(q_ref, kvc_ref, kvct_ref, ovt_ref, bias_ref, o_ref, sel_ref):
    tq = q_ref.shape[1]
    ncp = kvc_ref.shape[1]
    rows = A_HEADS * tq
    t0 = pl.program_id(1) * tq
    qs = _stack_heads(q_ref[0], A_HEADS)
    u = _dot_nt(kvc_ref[0], qs) + bias_ref[...]
    n = lax.broadcasted_iota(jnp.int32, (ncp, rows), 0)
    block_end = jnp.where(n == ncp - 1, jnp.int32(1 << 30), n * CMP_STRIDE + (CMP_LEN - 1))
    keep = block_end <= jnp.bitwise_and(_lane((ncp, rows)), tq - 1) + t0
    sh = jnp.where(keep, u, NEG_INF)
    m = jnp.max(sh, axis=0, keepdims=True)
    e = jnp.where(keep, jnp.exp2(sh - m), 0.0)
    p = e / jnp.maximum(jnp.sum(e, axis=0, keepdims=True), 1e-30)
    o_t = _dot(kvct_ref[0, HEAD_DIM:, :], p.astype(BF16))
    o_ref[0] = _heads_to_rows([o_t[:, h * tq:(h + 1) * tq] for h in range(A_HEADS)]).astype(BF16)

    psum = p[:, 0:tq] + p[:, tq:2 * tq] + p[:, 2 * tq:3 * tq] + p[:, 3 * tq:4 * tq]
    hi = psum.astype(BF16)
    r1 = psum - hi.astype(F32)
    mid = r1.astype(BF16)
    lo = (r1 - mid.astype(F32)).astype(BF16)
    ovt = ovt_ref[...]
    imp = _dot(ovt, hi) + _dot(ovt, mid) + _dot(ovt, lo)
    nsl = ncp * CMP_STRIDE // SLC_BLOCK
    j = lax.broadcasted_iota(jnp.int32, (LANES, tq), 0)
    t_blk = jnp.right_shift(_lane((LANES, tq)) + t0, SLC_SHIFT)
    forced = (j == 0) | (j == t_blk) | (j == t_blk - 1)
    imp = jnp.where(forced, FORCE_SCORE, jnp.where(j > t_blk, -1.0, imp))
    work = jnp.where(j < nsl, imp, -3.0)
    jf = j.astype(F32)
    sel = jnp.zeros((LANES, tq), F32)
    for _ in range(SLC_TOPN):
        mx = jnp.max(work, axis=0, keepdims=True)
        first = jnp.min(jnp.where(work == mx, jf, float(LANES)), axis=0, keepdims=True)
        pick = jf == first
        sel = jnp.where(pick, 1.0, sel)
        work = jnp.where(pick, -4.0, work)
    sel_ref[0] = jnp.transpose(sel).astype(BF16)


def _cmp_attn(proj3, kvc, kvc_t, slopes):
    b, t, _ = proj3.shape
    ncp = kvc.shape[1]
    nsl = t // SLC_BLOCK
    c0 = np.arange(ncp)[None, :] * CMP_STRIDE
    s0 = np.arange(LANES)[:, None] * SLC_BLOCK
    ovt = np.clip(np.minimum(c0 + CMP_LEN, s0 + SLC_BLOCK) - np.maximum(c0, s0), 0, None) / CMP_LEN
    ovt[:, ncp - 1] = 0.0
    ovt[nsl:, :] = 0.0
    rel = np.arange(TQ_CMP)[None, :] - (np.arange(ncp)[:, None] * CMP_STRIDE + (CMP_LEN - 1))
    bias = np.concatenate([-s * rel for s in slopes], axis=1).astype(np.float32)
    return pl.pallas_call(
        _cmp_attn_kernel_t,
        grid=(b, t // TQ_CMP),
        in_specs=[pl.BlockSpec((1, TQ_CMP, A_HEADS * LANES), lambda bi, i: (bi, i, QA_BLK // A_HEADS)),
                  pl.BlockSpec((1, ncp, LANES), lambda bi, i: (bi, 0, 0)),
                  pl.BlockSpec((1, LANES, ncp), lambda bi, i: (bi, 0, 0)),
                  pl.BlockSpec((LANES, ncp), lambda bi, i: (0, 0)),
                  pl.BlockSpec((ncp, A_HEADS * TQ_CMP), lambda bi, i: (0, 0))],
        out_specs=[pl.BlockSpec((1, TQ_CMP, A_HEADS * HEAD_DIM), lambda bi, i: (bi, i, 0)),
                   pl.BlockSpec((1, TQ_CMP, LANES), lambda bi, i: (bi, i, 0))],
        out_shape=[jax.ShapeDtypeStruct((b, t, A_HEADS * HEAD_DIM), BF16),
                   jax.ShapeDtypeStruct((b, t, LANES), BF16)],
        compiler_params=_cparams(("parallel", "parallel"), 16),
        name="nsa_cmp_attn",
    )(proj3, kvc, kvc_t, jnp.asarray(ovt, BF16), jnp.asarray(bias))


def _stack_heads(q, n):
    return jnp.concatenate([q[:, h * LANES:(h + 1) * LANES] for h in range(n)], axis=0)


def _window_bias(slopes, tq, window):
    wk = window + tq
    n_var = -(-window // tq) + 1
    j = np.arange(wk)[:, None]
    q = np.arange(tq)[None, :]
    out = np.empty((len(slopes), n_var, wk, len(slopes[0]) * tq), np.float32)
    for v in range(n_var):
        d = v * tq - max(v * tq - window, 0) + q - j
        keep = (d >= 0) & (d < window)
        for g, row in enumerate(slopes):
            for r, slope in enumerate(row):
                out[g, v, :, r * tq:(r + 1) * tq] = np.where(keep, -slope * d, NEG_INF)
    return jnp.asarray(out)


def _window_kernel_t(*refs, r_heads, window, has_sink):
    if has_sink:
        sink_ref, q_ref, kv_ref, kvt_ref, bias_ref, o_ref = refs
    else:
        q_ref, kv_ref, kvt_ref, bias_ref, o_ref = refs
    g, i = pl.program_id(1), pl.program_id(2)
    tq = q_ref.shape[1]
    wk = window + tq
    start = pl.multiple_of(jnp.maximum(i * tq - window, 0), LANES)
    qs = _stack_heads(q_ref[0], r_heads)
    u = _dot_nt(kv_ref[0, pl.ds(start, wk), :], qs) + bias_ref[0, 0]
    m = jnp.max(u, axis=0, keepdims=True)
    if has_sink:
        sink = jnp.concatenate([jnp.full((1, tq), sink_ref[g * r_heads + r], F32) for r in range(r_heads)], axis=1)
        m = jnp.maximum(m, sink)
    p = jnp.exp2(u - m)
    den = jnp.sum(p, axis=0, keepdims=True)
    if has_sink:
        den = den + jnp.exp2(sink - m)
    o_t = _dot(kvt_ref[HEAD_DIM:, pl.ds(start, wk)], p.astype(BF16)) / den
    o_ref[0] = _heads_to_rows([o_t[:, r * tq:(r + 1) * tq] for r in range(r_heads)]).astype(BF16)


def _window_attn_t(proj3, kv_t, slopes, *, q_blk, kv_blk, window, sinks=None, name):
    b, t, _ = proj3.shape
    groups, r_heads = len(slopes), len(slopes[0])
    bias = _window_bias(slopes, TQ, window)
    n_var, wk = bias.shape[1], bias.shape[2]
    in_specs = [pl.BlockSpec((1, TQ, r_heads * LANES), lambda bi, g, i: (bi, i, q_blk // r_heads + g)),
                pl.BlockSpec((1, t, LANES), lambda bi, g, i: (bi, 0, kv_blk + g)),
                pl.BlockSpec((LANES, t), lambda bi, g, i: (kv_blk - KVB_BLK + g, bi)),
                pl.BlockSpec((1, 1, wk, r_heads * TQ), lambda bi, g, i: (g, jnp.minimum(i, n_var - 1), 0, 0))]
    args = [proj3, proj3, kv_t, bias]
    if sinks is not None:
        in_specs.insert(0, pl.BlockSpec(memory_space=pltpu.SMEM))
        args.insert(0, sinks.astype(F32) * LOG2E)
    return pl.pallas_call(
        functools.partial(_window_kernel_t, r_heads=r_heads, window=window, has_sink=sinks is not None),
        grid=(b, groups, t // TQ),
        in_specs=in_specs,
        out_specs=pl.BlockSpec((1, TQ, r_heads * HEAD_DIM), lambda bi, g, i: (bi, i, g)),
        out_shape=jax.ShapeDtypeStruct((b, t, groups * r_heads * HEAD_DIM), BF16),
        compiler_params=_cparams(("parallel", "parallel", "parallel"), 24),
        name=name,
    )(*args)


def _flash_bias(stream_slopes, tq, tk):
    ratio = tk // tq
    j = np.arange(tk)[:, None]
    q = np.arange(tq)[None, :]
    out = np.empty((len(stream_slopes), 1 + ratio, tk, len(stream_slopes[0]) * tq), np.float32)
    for s, slopes in enumerate(stream_slopes):
        for h, slope in enumerate(slopes):
            cols = slice(h * tq, (h + 1) * tq)
            out[s, 0, :, cols] = slope * j
            for r in range(ratio):
                out[s, 1 + r, :, cols] = np.where(j <= r * tq + q, slope * j, NEG_INF)
    return jnp.asarray(out)


def _flash_t(streams, kv_ref, kvt_ref, i, *, tq, tk, sel_t=None, interior=None):
    ratio = tk // tq
    n_full = lax.div(i, jnp.int32(ratio))
    rows = streams[0][0].shape[0]

    def tiles(kts, carry, edge):
        starts = [pl.multiple_of(kt * tk, tk) for kt in kts]
        scores = [[_dot_nt(kv_ref[0, pl.ds(k0, tk), s[1]:s[1] + LANES], s[0]) for s in streams]
                  for k0 in starts]
        state = [(m, l) for m, l, _ in carry]
        updates = []
        for n, kt in enumerate(kts):
            variant = 1 + i - kt * ratio if (edge and n == len(kts) - 1) else 0
            keep = None
            if sel_t is not None:
                blk = jnp.right_shift(lax.broadcasted_iota(jnp.int32, (tk, LANES), 0) + kt * tk, SLC_SHIFT)
                expand = jnp.where(_lane((tk, LANES)) == blk, 1.0, 0.0).astype(BF16)
                keep = _dot(expand, sel_t) > 0.5
            k0f = (kt * tk).astype(F32)
            row = []
            for s, (_, _, slope_lane, bias) in enumerate(streams):
                m, l = state[s]
                u = scores[n][s] + bias(variant)
                if keep is not None:
                    u = jnp.where(keep, u, NEG_INF)
                shift = slope_lane * k0f
                m_new = jnp.maximum(m, jnp.max(u, axis=0, keepdims=True) + shift)
                p = jnp.exp2(u - (m_new - shift))
                alpha = jnp.exp2(m - m_new)
                state[s] = (m_new, alpha * l + jnp.sum(p, axis=0, keepdims=True))
                row.append((alpha, p.astype(BF16)))
            updates.append(row)
        accs = [acc for _, _, acc in carry]
        for k0, row in zip(starts, updates):
            for s, (alpha, p) in enumerate(row):
                col = streams[s][1]
                kv_f = kvt_ref[col + HEAD_DIM:col + LANES, pl.ds(k0, tk)]
                accs[s] = alpha * accs[s] + _dot(kv_f, p)
        return tuple((m, l, acc) for (m, l), acc in zip(state, accs))

    init = tuple((jnp.full((1, rows), NEG_INF, F32), jnp.zeros((1, rows), F32), jnp.zeros((HEAD_DIM, rows), F32))
                 for _ in streams)
    if interior is None:
        count, tile_id = n_full, lambda j: j
    else:
        count, tile_id = interior
    carry = lax.fori_loop(0, jnp.right_shift(count, 1),
                          lambda j, c: tiles([tile_id(2 * j), tile_id(2 * j + 1)], c, False), init)
    carry = lax.cond(jnp.bitwise_and(count, 1) == 1,
                     lambda c: tiles([tile_id(count - 1), n_full], c, True),
                     lambda c: tiles([n_full], c, True), carry)
    return [(l, acc) for _, l, acc in carry]


def _selected_kernel_t(count_ref, ids_ref, q_ref, kv_ref, kvt_ref, sel_ref, bias_ref, o_ref, *, slopes):
    tq = q_ref.shape[1]
    i = pl.program_id(1)
    step = pl.program_id(0) * pl.num_programs(1) + i
    n_kt = kv_ref.shape[1] // TK_DENSE
    interior = (count_ref[step], lambda j: ids_ref[step * n_kt + j])
    sel_t = jnp.transpose(sel_ref[0].astype(F32)).astype(BF16)
    sel_t = jnp.concatenate([sel_t, sel_t], axis=1)
    streams = []
    for pair in range(A_HEADS // 2):
        qs = jnp.concatenate([q_ref[0, :, h * LANES:(h + 1) * LANES] for h in (2 * pair, 2 * pair + 1)], axis=0)
        slope_lane = jnp.concatenate([jnp.full((1, tq), slopes[h], F32) for h in (2 * pair, 2 * pair + 1)], axis=1)
        streams.append((qs, 0, slope_lane, functools.partial(lambda s, v: bias_ref[s, v], pair)))
    outs = []
    for l, acc in _flash_t(streams, kv_ref, kvt_ref, i, tq=tq, tk=TK_DENSE, sel_t=sel_t, interior=interior):
        o_t = acc / l
        outs += [o_t[:, :tq], o_t[:, tq:]]
    o_ref[0] = _heads_to_rows(outs).astype(BF16)


def _selected_attn_t(proj3, kv_t, sel, slopes):
    b, t, _ = proj3.shape
    kv_blk = KVA_BLK + 1
    bias = _flash_bias([slopes[0:2], slopes[2:4]], TQ_DENSE, TK_DENSE)
    nq, nk, per = t // TQ_DENSE, t // TK_DENSE, TK_DENSE // SLC_BLOCK
    picked = jnp.max(sel.reshape(b, nq, TQ_DENSE, LANES)[..., :nk * per], axis=2) > 0
    active = jnp.any(picked.reshape(b, nq, nk, per), axis=-1)
    past = jnp.arange(nk)[None, :] < (jnp.arange(nq) * TQ_DENSE // TK_DENSE)[:, None]
    active = active & past[None]
    ids = jnp.argsort(~active, axis=-1, stable=True).astype(jnp.int32).reshape(-1)
    counts = jnp.sum(active, axis=-1).astype(jnp.int32).reshape(-1)
    grid_spec = pltpu.PrefetchScalarGridSpec(
        num_scalar_prefetch=2,
        grid=(b, nq),
        in_specs=[pl.BlockSpec((1, TQ_DENSE, A_HEADS * LANES), lambda bi, i, *_: (bi, i, QA_BLK // A_HEADS)),
                  pl.BlockSpec((1, t, LANES), lambda bi, i, *_: (bi, 0, kv_blk)),
                  pl.BlockSpec((LANES, t), lambda bi, i, *_: (kv_blk - KVB_BLK, bi)),
                  pl.BlockSpec((1, TQ_DENSE, LANES), lambda bi, i, *_: (bi, i, 0)),
                  pl.BlockSpec(bias.shape, lambda bi, i, *_: (0, 0, 0, 0), pipeline_mode=pl.Buffered(1))],
        out_specs=pl.BlockSpec((1, TQ_DENSE, A_HEADS * HEAD_DIM), lambda bi, i, *_: (bi, i, 0)),
    )
    return pl.pallas_call(
        functools.partial(_selected_kernel_t, slopes=slopes),
        grid_spec=grid_spec,
        out_shape=jax.ShapeDtypeStruct((b, t, A_HEADS * HEAD_DIM), BF16),
        compiler_params=_cparams(("parallel", "parallel"), 40),
        name="nsa_selected",
    )(counts, ids, proj3, proj3, kv_t, sel, bias)


def _diff_kernel_t(slope_ref, dl_ref, g_ref, q_ref, kv_ref, kvt_ref, bias_ref, o_ref, *, lam_init):
    tq = q_ref.shape[1]
    hp = pl.program_id(1)
    i = pl.program_id(2)
    dl = dl_ref[...]
    lam = (jnp.exp(jnp.sum(dl[0:1] * dl[1:2], axis=-1, keepdims=True))
           - jnp.exp(jnp.sum(dl[2:3] * dl[3:4], axis=-1, keepdims=True)) + lam_init)
    lane = _lane((tq, LANES))
    streams = []
    for hh in range(2):
        q = q_ref[0, :, hh * LANES:(hh + 1) * LANES]
        zero = jnp.zeros_like(q)
        qs = jnp.concatenate([jnp.where(lane < DIFF_DK, q, zero),
                              jnp.where(lane >= DIFF_DK, q, zero)], axis=0)
        slope_lane = jnp.full((1, 2 * tq), slope_ref[hp * 2 + hh], F32)
        streams.append((qs, hh * LANES, slope_lane, functools.partial(lambda s, v: bias_ref[s, v], hh)))
    outs = []
    for l, acc in _flash_t(streams, kv_ref, kvt_ref, i, tq=tq, tk=TK_DENSE):
        o = acc / l
        w = o[:, :tq] - lam * o[:, tq:]
        ms = jnp.sum(w * w, axis=0, keepdims=True) * (1.0 / HEAD_DIM)
        outs.append(w * lax.rsqrt(ms + LN_EPS) * (1.0 - lam_init))
    o_ref[0] = (_heads_to_rows(outs) * g_ref[...]).astype(BF16)


def _diff_attn_t(proj3, kv_t, slopes, diff_lambda, subln, lam_init):
    b, t, _ = proj3.shape
    g_ext = jnp.tile(subln.reshape(1, HEAD_DIM).astype(F32), (1, 2))
    bias = _flash_bias([[s, s] for s in slopes], TQ_DENSE, TK_DENSE)
    return pl.pallas_call(
        functools.partial(_diff_kernel_t, lam_init=lam_init),
        grid=(b, B_HEADS // 2, t // TQ_DENSE),
        in_specs=[pl.BlockSpec(memory_space=pltpu.SMEM),
                  pl.BlockSpec((4, DIFF_DK), lambda bi, h, i: (0, 0)),
                  pl.BlockSpec((1, LANES), lambda bi, h, i: (0, 0)),
                  pl.BlockSpec((1, TQ_DENSE, 2 * LANES), lambda bi, h, i: (bi, i, QB_BLK // 2 + h)),
                  pl.BlockSpec((1, t, 2 * LANES), lambda bi, h, i: (bi, 0, KVB_BLK // 2 + h)),
                  pl.BlockSpec((2 * LANES, t), lambda bi, h, i: (h, bi)),
                  pl.BlockSpec((2,) + bias.shape[1:], lambda bi, h, i: (h, 0, 0, 0))],
        out_specs=pl.BlockSpec((1, TQ_DENSE, 2 * HEAD_DIM), lambda bi, h, i: (bi, i, h)),
        out_shape=jax.ShapeDtypeStruct((b, t, B_HEADS * HEAD_DIM), BF16),
        compiler_params=_cparams(("parallel", "parallel", "parallel"), 48),
        name="diff_attn",
    )(jnp.asarray(slopes, F32), diff_lambda.astype(F32), g_ext, proj3, proj3, kv_t, bias)


def _outproj_kernel(oc_ref, os_ref, ow_ref, gl_ref, ob_ref, osw_ref, x_ref, w_ref, ex_ref, g_ref, b_ref, o_ref):
    hi, lo = _split2(_sigmoid(gl_ref[...]))
    oa = None
    for j, ref in enumerate((oc_ref, os_ref, ow_ref)):
        gate = _dot(hi, ex_ref[j]) + _dot(lo, ex_ref[j])
        term = gate * ref[...].astype(F32)
        oa = term if oa is None else oa + term
    cat = jnp.concatenate([oa.astype(BF16), ob_ref[...], osw_ref[...]], axis=1)
    y = DN_ALPHA * x_ref[...] + _dot(cat, w_ref[...])
    o_ref[...] = _layer_norm(y, g_ref[...], b_ref[...])


def _outproj(o_c, o_s, o_w, gl, o_b, o_sw, x2d, w_out, ln_g, ln_b):
    n = x2d.shape[0]
    ex = np.zeros((3, LANES, A_HEADS * HEAD_DIM), np.float32)
    for j in range(3):
        for h in range(A_HEADS):
            ex[j, h * 3 + j, h * HEAD_DIM:(h + 1) * HEAD_DIM] = 1.0
    row = lambda w: pl.BlockSpec((TM, w), lambda i: (i, 0))
    full = lambda s: pl.BlockSpec(s, lambda i: (0,) * len(s))
    return pl.pallas_call(
        _outproj_kernel,
        grid=(n // TM,),
        in_specs=[row(256), row(256), row(256), row(LANES), row(256), row(512), row(D_MODEL),
                  full((D_MODEL, D_MODEL)), full(ex.shape), full((1, D_MODEL)), full((1, D_MODEL))],
        out_specs=row(D_MODEL),
        out_shape=jax.ShapeDtypeStruct((n, D_MODEL), F32),
        compiler_params=_cparams(("parallel",), 24),
        name="out_proj_ln",
    )(o_c, o_s, o_w, gl, o_b, o_sw, x2d, w_out.astype(BF16), jnp.asarray(ex, BF16),
      ln_g.reshape(1, -1), ln_b.reshape(1, -1))


def _ffn_kernel(x_ref, wg_ref, wu_ref, wd_ref, o_ref):
    xb = x_ref[...].astype(BF16)
    acc = None
    for c in range(D_FF_DENSE // FFN_CHUNK):
        cols = slice(c * FFN_CHUNK, (c + 1) * FFN_CHUNK)
        g = _dot(xb, wg_ref[:, cols])
        u = _dot(xb, wu_ref[:, cols])
        y = _dot((g * _sigmoid(g) * u).astype(BF16), wd_ref[cols, :])
        acc = y if acc is None else acc + y
    o_ref[...] = acc


def _ffn(x2d, wg, wu, wd):
    n = x2d.shape[0]
    resident = lambda shape: pl.BlockSpec(shape, lambda i: (0, 0), pipeline_mode=pl.Buffered(1))
    return pl.pallas_call(
        _ffn_kernel,
        grid=(n // TM,),
        in_specs=[pl.BlockSpec((TM, D_MODEL), lambda i: (i, 0)),
                  resident((D_MODEL, D_FF_DENSE)), resident((D_MODEL, D_FF_DENSE)), resident((D_FF_DENSE, D_MODEL))],
        out_specs=pl.BlockSpec((TM, D_MODEL), lambda i: (i, 0)),
        out_shape=jax.ShapeDtypeStruct((n, D_MODEL), F32),
        compiler_params=_cparams(("parallel",), 40),
        name="dense_swiglu",
    )(x2d, wg.astype(BF16), wu.astype(BF16), wd.astype(BF16))


def _router_kernel(x_ref, w_ref, tri_ref, xb_ref, gd_ref, scol_ref, srow_ref, base_ref, cnt_ref):
    tt = x_ref.shape[0]
    x = x_ref[...]
    xb_ref[...] = x.astype(BF16)
    xh, xl = _split2(x)
    wh, wl = _split2(w_ref[...])
    logits = _dot(xh, wh) + _dot(xh, wl) + _dot(xl, wh)
    lane = _lane((tt, LANES))
    lf = lane.astype(F32)
    logits = jnp.where(lane < N_EXPERTS, logits, -jnp.inf)
    m1 = jnp.max(logits, axis=-1, keepdims=True)
    i1 = jnp.min(jnp.where(logits == m1, lf, float(LANES)), axis=-1, keepdims=True)
    rest = jnp.where(lf == i1, -jnp.inf, logits)
    m2 = jnp.max(rest, axis=-1, keepdims=True)
    i2 = jnp.min(jnp.where(rest == m2, lf, float(LANES)), axis=-1, keepdims=True)
    e2 = jnp.exp(m2 - m1)
    g1 = 1.0 / (1.0 + e2)
    g2 = e2 / (1.0 + e2)
    first, second = lf == i1, lf == i2
    gd_ref[...] = jnp.where(first, g1, 0.0) + jnp.where(second, g2, 0.0)
    ind = jnp.where(first, 1.0, jnp.where(second, 1.0, 0.0))
    tri = tri_ref[...]
    run = jnp.zeros((1, LANES), F32)
    slots, bases = [], []
    for tb in range(tt // MOE_TB):
        blk = ind[tb * MOE_TB:(tb + 1) * MOE_TB]
        rank = _dot(tri, blk.astype(BF16)) + run
        slots.append(jnp.where(blk > 0.5, rank, -1.0))
        bases.append(run)
        run = run + jnp.sum(blk, axis=0, keepdims=True)
    slot = jnp.concatenate(slots, axis=0)
    scol_ref[...] = slot
    srow_ref[...] = jnp.transpose(slot)[0:N_EXPERTS]
    base_ref[0] = jnp.concatenate(bases, axis=0)
    cnt_ref[0] = jnp.broadcast_to(run, (8, LANES))


def _route(x2d, w_router):
    n = x2d.shape[0]
    nt = n // MOE_TILE
    ntb = MOE_TILE // MOE_TB
    w = jnp.zeros((D_MODEL, LANES), F32).at[:, :N_EXPERTS].set(w_router)
    tri = jnp.asarray(np.tril(np.ones((MOE_TB, MOE_TB), np.float32), -1), BF16)
    return pl.pallas_call(
        _router_kernel,
        grid=(nt,),
        in_specs=[pl.BlockSpec((MOE_TILE, D_MODEL), lambda i: (i, 0)),
                  pl.BlockSpec((D_MODEL, LANES), lambda i: (0, 0)),
                  pl.BlockSpec((MOE_TB, MOE_TB), lambda i: (0, 0))],
        out_specs=[pl.BlockSpec((MOE_TILE, D_MODEL), lambda i: (i, 0)),
                   pl.BlockSpec((MOE_TILE, LANES), lambda i: (i, 0)),
                   pl.BlockSpec((MOE_TILE, LANES), lambda i: (i, 0)),
                   pl.BlockSpec((N_EXPERTS, MOE_TILE), lambda i: (0, i)),
                   pl.BlockSpec((1, ntb, LANES), lambda i: (i, 0, 0)),
                   pl.BlockSpec((1, 8, LANES), lambda i: (i, 0, 0))],
        out_shape=[jax.ShapeDtypeStruct((n, D_MODEL), BF16),
                   jax.ShapeDtypeStruct((n, LANES), F32),
                   jax.ShapeDtypeStruct((n, LANES), F32),
                   jax.ShapeDtypeStruct((N_EXPERTS, n), F32),
                   jax.ShapeDtypeStruct((nt, ntb, LANES), F32),
                   jax.ShapeDtypeStruct((nt, 8, LANES), F32)],
        compiler_params=_cparams(("parallel",), 48),
        name="moe_router",
    )(x2d, w, tri)


MOE_GATHER_W = MOE_TB + 16
MOE_SCATTER_W = MOE_TB + 8
MOE_CAP = MOE_TILE + 512


def _moe_kernel(cnt_ref, base_ref, xb_ref, srow_ref, scol_ref, gd_ref, wg_ref, wu_ref, wd_ref,
                o_ref, xc_ref, yc_ref):
    i, e, c = pl.program_id(0), pl.program_id(1), pl.program_id(2)
    nc = pl.num_programs(2)
    ntb = MOE_TILE // MOE_TB
    count = cnt_ref[i * N_EXPERTS + e]
    n_full = jnp.right_shift(count * MOE_ROWS_RECIP[0], MOE_ROWS_RECIP[1])
    rem = count - n_full * MOE_ROWS

    @pl.when((i == 0) & (e == 0) & (c == 0))
    def _():
        yc_ref[...] = jnp.zeros_like(yc_ref)

    @pl.when((e == 0) & (c == 0))
    def _():
        o_ref[...] = jnp.zeros_like(o_ref)

    @pl.when(c == 0)
    def _():
        xc_ref[...] = jnp.zeros_like(xc_ref)
        srow = srow_ref[pl.ds(e, 1), :]
        rid = lax.broadcasted_iota(jnp.int32, (MOE_GATHER_W, MOE_TB), 0).astype(F32)
        for tb in range(ntb):
            b0 = base_ref[(i * ntb + tb) * N_EXPERTS + e]
            b_al = pl.multiple_of(jnp.left_shift(jnp.right_shift(b0, 4), 4), 16)
            rel = srow[:, tb * MOE_TB:(tb + 1) * MOE_TB] - b_al.astype(F32)
            onehot = jnp.where(rel == rid, 1.0, 0.0).astype(BF16)
            rows = _dot(onehot, xb_ref[tb * MOE_TB:(tb + 1) * MOE_TB, :])
            cur = xc_ref[pl.ds(b_al, MOE_GATHER_W), :].astype(F32)
            xc_ref[pl.ds(b_al, MOE_GATHER_W), :] = (cur + rows).astype(BF16)

    def expert_rows(r0, n_rows):
        xr = xc_ref[pl.ds(r0, n_rows), :]
        g = _dot(xr, wg_ref[0])
        u = _dot(xr, wu_ref[0])
        y = _dot((g * _sigmoid(g) * u).astype(BF16), wd_ref[0])
        prev = yc_ref[pl.ds(r0, n_rows), :]
        yc_ref[pl.ds(r0, n_rows), :] = jnp.where(c == 0, y, prev + y)

    def rows_body(rb, carry):
        expert_rows(pl.multiple_of(rb * MOE_ROWS, MOE_ROW_ALIGN), MOE_ROWS)
        return carry

    lax.fori_loop(0, n_full + jnp.where(rem > MOE_ROWS // 2, 1, 0), rows_body, 0)

    @pl.when((rem > 0) & (rem <= MOE_ROWS // 2))
    def _():
        expert_rows(pl.multiple_of(n_full * MOE_ROWS, MOE_ROW_ALIGN), MOE_ROWS // 2)

    @pl.when(c == nc - 1)
    def _():
        lane = _lane((MOE_TB, LANES))
        cid = lax.broadcasted_iota(jnp.int32, (MOE_TB, MOE_SCATTER_W), 1).astype(F32)
        for tb in range(ntb):
            b0 = base_ref[(i * ntb + tb) * N_EXPERTS + e]
            b_al = pl.multiple_of(jnp.left_shift(jnp.right_shift(b0, 3), 3), 8)
            rows = slice(tb * MOE_TB, (tb + 1) * MOE_TB)
            slot = jnp.sum(jnp.where(lane == e, scol_ref[rows, :], 0.0), axis=-1, keepdims=True)
            gate = jnp.sum(jnp.where(lane == e, gd_ref[rows, :], 0.0), axis=-1, keepdims=True)
            onehot = jnp.where(slot - b_al.astype(F32) == cid, 1.0, 0.0).astype(BF16)
            y_win = yc_ref[pl.ds(b_al, MOE_SCATTER_W), :].astype(BF16)
            o_ref[rows, :] += gate * _dot(onehot, y_win)


def _moe(xb, srow, scol, gd, cnt, base, wg, wu, wd):
    n = xb.shape[0]
    nt = n // MOE_TILE
    nc = D_FF_EXPERT // MOE_CHUNK
    grid_spec = pltpu.PrefetchScalarGridSpec(
        num_scalar_prefetch=2,
        grid=(nt, N_EXPERTS, nc),
        in_specs=[pl.BlockSpec((MOE_TILE, D_MODEL), lambda i, e, c, *_: (i, 0), pipeline_mode=pl.Buffered(1)),
                  pl.BlockSpec((N_EXPERTS, MOE_TILE), lambda i, e, c, *_: (0, i)),
                  pl.BlockSpec((MOE_TILE, LANES), lambda i, e, c, *_: (i, 0)),
                  pl.BlockSpec((MOE_TILE, LANES), lambda i, e, c, *_: (i, 0)),
                  pl.BlockSpec((1, D_MODEL, MOE_CHUNK), lambda i, e, c, *_: (e, 0, c)),
                  pl.BlockSpec((1, D_MODEL, MOE_CHUNK), lambda i, e, c, *_: (e, 0, c)),
                  pl.BlockSpec((1, MOE_CHUNK, D_MODEL), lambda i, e, c, *_: (e, c, 0))],
        out_specs=pl.BlockSpec((MOE_TILE, D_MODEL), lambda i, e, c, *_: (i, 0), pipeline_mode=pl.Buffered(1)),
        scratch_shapes=[pltpu.VMEM((MOE_CAP, D_MODEL), BF16), pltpu.VMEM((MOE_CAP, D_MODEL), F32)],
    )
    return pl.pallas_call(
        _moe_kernel,
        grid_spec=grid_spec,
        out_shape=jax.ShapeDtypeStruct((n, D_MODEL), F32),
        compiler_params=_cparams(("arbitrary", "arbitrary", "arbitrary"), 56),
        name="moe_experts",
    )(cnt, base, xb, srow, scol, gd, wg.astype(BF16), wu.astype(BF16), wd.astype(BF16))


def _ple_kernel(x_ref, f_ref, p_ref, wg_ref, wp_ref, g2_ref, b2_ref, g3_ref, b3_ref, o_ref):
    x2 = _layer_norm(DN_ALPHA * x_ref[...] + f_ref[...], g2_ref[...], b2_ref[...])
    e = _sigmoid(_dot(x2.astype(BF16), wg_ref[...])) * _dot(p_ref[...].astype(BF16), wp_ref[...])
    o_ref[...] = _layer_norm(DN_ALPHA * x2 + e, g3_ref[...], b3_ref[...])


def _ple(x2d, f2d, p2d, w_gate, w_proj, g2, b2, g3, b3):
    n = x2d.shape[0]
    row = lambda w: pl.BlockSpec((TM, w), lambda i: (i, 0))
    full = lambda s: pl.BlockSpec(s, lambda i: (0,) * len(s))
    vec = full((1, D_MODEL))
    return pl.pallas_call(
        _ple_kernel,
        grid=(n // TM,),
        in_specs=[row(D_MODEL), row(D_MODEL), row(PLE_DIM), full((D_MODEL, D_MODEL)), full((PLE_DIM, D_MODEL)),
                  vec, vec, vec, vec],
        out_specs=row(D_MODEL),
        out_shape=jax.ShapeDtypeStruct((n, D_MODEL), F32),
        compiler_params=_cparams(("parallel",), 32),
        name="ln_ple_ln",
    )(x2d, f2d, p2d, w_gate.astype(BF16), w_proj.astype(BF16),
      g2.reshape(1, -1), b2.reshape(1, -1), g3.reshape(1, -1), b3.reshape(1, -1))


def _mixer(x2d, b, t, i, w_in, cmp_pos, cmp_w1, cmp_w2, diff_lambda, diff_subln, sinks):
    slopes_a, slopes_b, slopes_c = _alibi_slopes()
    proj, gl, kv_t = _project(x2d, _build_w_in(w_in))
    proj3 = proj.reshape(b, t, N_MAIN)
    kvc, kvc_t = _compress(proj3, cmp_pos, cmp_w1, cmp_w2)
    o_c, sel = _cmp_attn(proj3, kvc, kvc_t, slopes_a)
    o_s = _selected_attn_t(proj3, kv_t, sel, slopes_a)
    o_w = _window_attn_t(proj3, kv_t, [slopes_a], q_blk=QA_BLK, kv_blk=KVA_BLK + 2, window=NSA_WINDOW,
                         name="nsa_window")
    lam_init = 0.8 - 0.6 * math.exp(-0.3 * i)
    o_b = _diff_attn_t(proj3, kv_t, slopes_b, diff_lambda, diff_subln, lam_init)
    r_c = C_HEADS // C_KV_HEADS
    o_sw = _window_attn_t(proj3, kv_t, [slopes_c[g * r_c:(g + 1) * r_c] for g in range(C_KV_HEADS)],
                          q_blk=QC_BLK, kv_blk=KVC_BLK, window=SWA_WINDOW, sinks=sinks, name="swa_gqa")
    flat = lambda a: a.reshape(b * t, a.shape[-1])
    return flat(o_c), flat(o_s), flat(o_w), gl, flat(o_b), flat(o_sw)


def kernel(x, p, w_in, cmp_pos, cmp_w1, cmp_w2, diff_lambda, diff_subln, sinks, w_out, ln1_g, ln1_b,
           ffn_w_gate, ffn_w_up, ffn_w_down, moe_router, moe_w_gate, moe_w_up, moe_w_down, ln2_g, ln2_b,
           ple_gate, ple_proj, ln3_g, ln3_b):
    b, t, d = x.shape
    n = b * t
    x2d = x.reshape(n, d)
    for i in range(DEPTH):
        heads = _mixer(x2d, b, t, i, w_in[i], cmp_pos[i], cmp_w1[i], cmp_w2[i],
                       diff_lambda[i], diff_subln[i], sinks[i])
        x1 = _outproj(*heads, x2d, w_out[i], ln1_g[i], ln1_b[i])
        if i % 2 == 0:
            f = _ffn(x1, ffn_w_gate[i // 2], ffn_w_up[i // 2], ffn_w_down[i // 2])
        else:
            xb, gd, scol, srow, base, cnt = _route(x1, moe_router[i // 2])
            cnt_i = cnt[:, 0, :N_EXPERTS].astype(jnp.int32).reshape(-1)
            base_i = base[:, :, :N_EXPERTS].astype(jnp.int32).reshape(-1)
            f = _moe(xb, srow, scol, gd, cnt_i, base_i,
                     moe_w_gate[i // 2], moe_w_up[i // 2], moe_w_down[i // 2])
        x2d = _ple(x1, f, p[i].reshape(n, PLE_DIM), ple_gate[i], ple_proj[i],
                   ln2_g[i], ln2_b[i], ln3_g[i], ln3_b[i])
    return x2d.reshape(b, t, d)
```

```python
import functools
import math

import jax
import jax.numpy as jnp
import numpy as np
from jax import lax
from jax.experimental import pallas as pl
from jax.experimental.pallas import tpu as pltpu

F32 = jnp.float32
BF16 = jnp.bfloat16

D_MODEL = 1024
HEAD_DIM = 64
A_HEADS = 4
B_HEADS = 4
C_HEADS = 8
C_KV_HEADS = 2
DIFF_DK = 32
CMP_LEN = 32
CMP_STRIDE = 16
CMP_HIDDEN = 128
SLC_BLOCK = 64
SLC_SHIFT = 6
SLC_TOPN = 16
NSA_WINDOW = 512
SWA_WINDOW = 128
N_EXPERTS = 8
D_FF_EXPERT = 3584
D_FF_DENSE = 2816
PLE_DIM = 256
LN_EPS = 1e-5
NEG_INF = -1e30
FORCE_SCORE = 1e9
DEPTH = 2
DN_ALPHA = (2 * DEPTH) ** 0.25

LANES = 128
QA_BLK, QB_BLK, QC_BLK, KVB_BLK, KVC_BLK, KVA_BLK, N_MAIN_BLK = 0, 4, 8, 16, 20, 22, 25
N_MAIN = N_MAIN_BLK * LANES

LOG2E = 1.4426950408889634
TQ = 256
TQ_CMP = 512
TQ_DENSE = 512
TK_DENSE = 512
TM = 512
FFN_CHUNK = 1408
MOE_TILE = 2048
MOE_ROWS = 576
MOE_ROWS_RECIP = (1821, 20)
MOE_ROW_ALIGN = 16
MOE_TB = 256
MOE_CHUNK = 896


def _cparams(sem, vmem_mib):
    return pltpu.CompilerParams(dimension_semantics=sem, vmem_limit_bytes=vmem_mib * 1024 * 1024)


def _alibi_slopes():
    n = A_HEADS + B_HEADS + C_HEADS
    s = [LOG2E * 2.0 ** (-8.0 * i / n) for i in range(1, n + 1)]
    rest = s[C_HEADS:]
    return rest[0::2], rest[1::2], s[:C_HEADS]


def _dot(a, b):
    return jnp.dot(a, b, preferred_element_type=F32)


def _dot_nt(a, b):
    return lax.dot_general(a, b, (((1,), (1,)), ((), ())), preferred_element_type=F32)


def _split2(x):
    hi = x.astype(BF16)
    lo = (x - hi.astype(F32)).astype(BF16)
    return hi, lo


def _layer_norm(y, g, b):
    mu = jnp.mean(y, axis=-1, keepdims=True)
    d = y - mu
    var = jnp.mean(d * d, axis=-1, keepdims=True)
    return d * lax.rsqrt(var + LN_EPS) * g + b


def _sigmoid(x):
    return 1.0 / (1.0 + jnp.exp(-x))


def _lane(shape):
    return lax.broadcasted_iota(jnp.int32, shape, 1)


def _heads_to_rows(parts):
    out = [jnp.transpose(jnp.concatenate([a, b], axis=0)) for a, b in zip(parts[0::2], parts[1::2])]
    return out[0] if len(out) == 1 else jnp.concatenate(out, axis=1)


N_Q = (A_HEADS + B_HEADS + C_HEADS) * HEAD_DIM
N_KV = N_MAIN - KVB_BLK * LANES


def _proj_kernel(x_ref, w_ref, o_ref, g_ref, t_ref):
    acc = _dot(x_ref[...].astype(BF16), w_ref[...])
    lane = _lane((acc.shape[0], LANES))
    for pair in range(N_Q // LANES):
        a = acc[:, pair * LANES:(pair + 1) * LANES]
        o_ref[:, (2 * pair) * LANES:(2 * pair + 1) * LANES] = jnp.where(lane < HEAD_DIM, a, 0.0).astype(BF16)
        o_ref[:, (2 * pair + 1) * LANES:(2 * pair + 2) * LANES] = (
            jnp.where(lane < HEAD_DIM, pltpu.roll(a, HEAD_DIM, 1), 0.0).astype(BF16))
    kv = acc[:, N_Q:N_Q + N_KV]
    o_ref[:, KVB_BLK * LANES:] = kv.astype(BF16)
    g_ref[...] = acc[:, N_Q + N_KV:]
    t_ref[...] = jnp.transpose(kv).astype(BF16)


def _project(x2d, w_ext):
    n = x2d.shape[0]
    n_kv = N_KV
    return pl.pallas_call(
        _proj_kernel,
        grid=(n // TM,),
        in_specs=[pl.BlockSpec((TM, D_MODEL), lambda i: (i, 0)),
                  pl.BlockSpec((D_MODEL, N_Q + N_KV + LANES), lambda i: (0, 0))],
        out_specs=[pl.BlockSpec((TM, N_MAIN), lambda i: (i, 0)),
                   pl.BlockSpec((TM, LANES), lambda i: (i, 0)),
                   pl.BlockSpec((n_kv, TM), lambda i: (0, i))],
        out_shape=[jax.ShapeDtypeStruct((n, N_MAIN), BF16),
                   jax.ShapeDtypeStruct((n, LANES), F32),
                   jax.ShapeDtypeStruct((n_kv, n), BF16)],
        compiler_params=_cparams(("parallel",), 32),
        name="in_proj",
    )(x2d, w_ext)


def _build_w_in(w):
    d = w.shape[0]
    o = np.cumsum([0, 256, 384, 12, 256, 256, 256, 512, 128, 128])
    q_a, kv_a, g_a, q_b, k_b, v_b, q_c, k_c, v_c = [w[:, o[i]:o[i + 1]] for i in range(9)]
    hd = HEAD_DIM
    cols = [q_a * (LOG2E * hd ** -0.5), q_b * (LOG2E * DIFF_DK ** -0.5), q_c * (LOG2E * hd ** -0.5)]
    for h in range(B_HEADS):
        cols += [k_b[:, h * hd:(h + 1) * hd], v_b[:, h * hd:(h + 1) * hd]]
    for g in range(C_KV_HEADS):
        cols += [k_c[:, g * hd:(g + 1) * hd], v_c[:, g * hd:(g + 1) * hd]]
    cols.append(kv_a)
    cols += [g_a, jnp.zeros((d, LANES - 12), F32)]
    return jnp.concatenate(cols, axis=1).astype(BF16)


def _compress_kernel(ch_ref, wc_ref, pos_ref, w1_ref, w2_ref, o_ref, ot_ref):
    nck = ch_ref.shape[1]
    u = _dot(ch_ref[0], wc_ref[...])
    hs = []
    for j in range(2):
        pw = _dot(pos_ref[j].astype(BF16), w1_ref[j])[0:1]
        ua = u[:, (2 * j) * LANES:(2 * j + 1) * LANES]
        ub = u[:, (2 * j + 1) * LANES:(2 * j + 2) * LANES]
        pre = ua + pltpu.roll(ub, nck - 1, 0) + pw
        hs.append(0.5 * pre * (1.0 + jnp.tanh(0.7978845608028654 * (pre + 0.044715 * pre * pre * pre))))
    h = jnp.concatenate(hs, axis=1).astype(BF16)
    kvc = _dot(h, w2_ref[...])
    o_ref[0] = kvc.astype(BF16)
    ot_ref[0] = jnp.transpose(kvc).astype(BF16)


def _compress(proj3, cmp_pos, cmp_w1, cmp_w2):
    b, t, _ = proj3.shape
    nck = t // CMP_STRIDE
    kv = proj3[:, :, KVA_BLK * LANES:(KVA_BLK + 1) * LANES]
    chunks = kv.reshape(b, nck, CMP_STRIDE * LANES)
    w1 = cmp_w1.reshape(2, 2, CMP_STRIDE, HEAD_DIM, CMP_HIDDEN)
    z = jnp.zeros((CMP_STRIDE, HEAD_DIM, CMP_HIDDEN), F32)
    blocks = []
    for j in range(2):
        for half in range(2):
            pair = [w1[j, half], z] if j == 0 else [z, w1[j, half]]
            blocks.append(jnp.concatenate(pair, axis=1).reshape(CMP_STRIDE * LANES, CMP_HIDDEN))
    wc = jnp.concatenate(blocks, axis=1).astype(BF16)
    pos = jnp.zeros((2, 8, CMP_LEN * HEAD_DIM), F32).at[:, 0].set(cmp_pos.reshape(2, -1))
    z2 = jnp.zeros((CMP_HIDDEN, HEAD_DIM), F32)
    w2 = jnp.concatenate([jnp.concatenate([cmp_w2[0], z2], axis=1),
                          jnp.concatenate([z2, cmp_w2[1]], axis=1)], axis=0).astype(BF16)
    return pl.pallas_call(
        _compress_kernel,
        grid=(b,),
        in_specs=[pl.BlockSpec((1, nck, CMP_STRIDE * LANES), lambda i: (i, 0, 0)),
                  pl.BlockSpec(wc.shape, lambda i: (0, 0)),
                  pl.BlockSpec(pos.shape, lambda i: (0, 0, 0)),
                  pl.BlockSpec((2, CMP_LEN * HEAD_DIM, CMP_HIDDEN), lambda i: (0, 0, 0)),
                  pl.BlockSpec(w2.shape, lambda i: (0, 0))],
        out_specs=[pl.BlockSpec((1, nck, LANES), lambda i: (i, 0, 0)),
                   pl.BlockSpec((1, LANES, nck), lambda i: (i, 0, 0))],
        out_shape=[jax.ShapeDtypeStruct((b, nck, LANES), BF16),
                   jax.ShapeDtypeStruct((b, LANES, nck), BF16)],
        compiler_params=_cparams(("parallel",), 16),
        name="nsa_compress",
    )(chunks, wc, pos, cmp_w1.astype(BF16), w2)


def _cmp_attn_kernel_t(q_ref, kvc_ref, kvct_ref, ovt_ref, bias_ref, o_ref, sel_ref):
    tq = q_ref.shape[1]
    ncp = kvc_ref.shape[1]
    rows = A_HEADS * tq
    t0 = pl.program_id(1) * tq
    qs = _stack_heads(q_ref[0], A_HEADS)
    u = _dot_nt(kvc_ref[0], qs) + bias_ref[...]
    n = lax.broadcasted_iota(jnp.int32, (ncp, rows), 0)
    block_end = jnp.where(n == ncp - 1, jnp.int32(1 << 30), n * CMP_STRIDE + (CMP_LEN - 1))
    keep = block_end <= jnp.bitwise_and(_lane((ncp, rows)), tq - 1) + t0
    sh = jnp.where(keep, u, NEG_INF)
    m = jnp.max(sh, axis=0, keepdims=True)
    e = jnp.where(keep, jnp.exp2(sh - m), 0.0)
    p = e / jnp.maximum(jnp.sum(e, axis=0, keepdims=True), 1e-30)
    o_t = _dot(kvct_ref[0, HEAD_DIM:, :], p.astype(BF16))
    o_ref[0] = _heads_to_rows([o_t[:, h * tq:(h + 1) * tq] for h in range(A_HEADS)]).astype(BF16)

    psum = p[:, 0:tq] + p[:, tq:2 * tq] + p[:, 2 * tq:3 * tq] + p[:, 3 * tq:4 * tq]
    hi = psum.astype(BF16)
    r1 = psum - hi.astype(F32)
    mid = r1.astype(BF16)
    lo = (r1 - mid.astype(F32)).astype(BF16)
    ovt = ovt_ref[...]
    imp = _dot(ovt, hi) + _dot(ovt, mid) + _dot(ovt, lo)
    nsl = ncp * CMP_STRIDE // SLC_BLOCK
    j = lax.broadcasted_iota(jnp.int32, (LANES, tq), 0)
    t_blk = jnp.right_shift(_lane((LANES, tq)) + t0, SLC_SHIFT)
    forced = (j == 0) | (j == t_blk) | (j == t_blk - 1)
    imp = jnp.where(forced, FORCE_SCORE, jnp.where(j > t_blk, -1.0, imp))
    work = jnp.where(j < nsl, imp, -3.0)
    jf = j.astype(F32)
    sel = jnp.zeros((LANES, tq), F32)
    for _ in range(SLC_TOPN):
        mx = jnp.max(work, axis=0, keepdims=True)
        first = jnp.min(jnp.where(work == mx, jf, float(LANES)), axis=0, keepdims=True)
        pick = jf == first
        sel = jnp.where(pick, 1.0, sel)
        work = jnp.where(pick, -4.0, work)
    sel_ref[0] = jnp.transpose(sel).astype(BF16)


def _cmp_attn(proj3, kvc, kvc_t, slopes):
    b, t, _ = proj3.shape
    ncp = kvc.shape[1]
    nsl = t // SLC_BLOCK
    c0 = np.arange(ncp)[None, :] * CMP_STRIDE
    s0 = np.arange(LANES)[:, None] * SLC_BLOCK
    ovt = np.clip(np.minimum(c0 + CMP_LEN, s0 + SLC_BLOCK) - np.maximum(c0, s0), 0, None) / CMP_LEN
    ovt[:, ncp - 1] = 0.0
    ovt[nsl:, :] = 0.0
    rel = np.arange(TQ_CMP)[None, :] - (np.arange(ncp)[:, None] * CMP_STRIDE + (CMP_LEN - 1))
    bias = np.concatenate([-s * rel for s in slopes], axis=1).astype(np.float32)
    return pl.pallas_call(
        _cmp_attn_kernel_t,
        grid=(b, t // TQ_CMP),
        in_specs=[pl.BlockSpec((1, TQ_CMP, A_HEADS * LANES), lambda bi, i: (bi, i, QA_BLK // A_HEADS)),
                  pl.BlockSpec((1, ncp, LANES), lambda bi, i: (bi, 0, 0)),
                  pl.BlockSpec((1, LANES, ncp), lambda bi, i: (bi, 0, 0)),
                  pl.BlockSpec((LANES, ncp), lambda bi, i: (0, 0)),
                  pl.BlockSpec((ncp, A_HEADS * TQ_CMP), lambda bi, i: (0, 0))],
        out_specs=[pl.BlockSpec((1, TQ_CMP, A_HEADS * HEAD_DIM), lambda bi, i: (bi, i, 0)),
                   pl.BlockSpec((1, TQ_CMP, LANES), lambda bi, i: (bi, i, 0))],
        out_shape=[jax.ShapeDtypeStruct((b, t, A_HEADS * HEAD_DIM), BF16),
                   jax.ShapeDtypeStruct((b, t, LANES), BF16)],
        compiler_params=_cparams(("parallel", "parallel"), 16),
        name="nsa_cmp_attn",
    )(proj3, kvc, kvc_t, jnp.asarray(ovt, BF16), jnp.asarray(bias))


def _stack_heads(q, n):
    return jnp.concatenate([q[:, h * LANES:(h + 1) * LANES] for h in range(n)], axis=0)


def _window_bias(slopes, tq, window):
    wk = window + tq
    n_var = -(-window // tq) + 1
    j = np.arange(wk)[:, None]
    q = np.arange(tq)[None, :]
    out = np.empty((len(slopes), n_var, wk, len(slopes[0]) * tq), np.float32)
    for v in range(n_var):
        d = v * tq - max(v * tq - window, 0) + q - j
        keep = (d >= 0) & (d < window)
        for g, row in enumerate(slopes):
            for r, slope in enumerate(row):
                out[g, v, :, r * tq:(r + 1) * tq] = np.where(keep, -slope * d, NEG_INF)
    return jnp.asarray(out)


def _window_kernel_t(*refs, r_heads, window, has_sink):
    if has_sink:
        sink_ref, q_ref, kv_ref, kvt_ref, bias_ref, o_ref = refs
    else:
        q_ref, kv_ref, kvt_ref, bias_ref, o_ref = refs
    g, i = pl.program_id(1), pl.program_id(2)
    tq = q_ref.shape[1]
    wk = window + tq
    start = pl.multiple_of(jnp.maximum(i * tq - window, 0), LANES)
    qs = _stack_heads(q_ref[0], r_heads)
    u = _dot_nt(kv_ref[0, pl.ds(start, wk), :], qs) + bias_ref[0, 0]
    m = jnp.max(u, axis=0, keepdims=True)
    if has_sink:
        sink = jnp.concatenate([jnp.full((1, tq), sink_ref[g * r_heads + r], F32) for r in range(r_heads)], axis=1)
        m = jnp.maximum(m, sink)
    p = jnp.exp2(u - m)
    den = jnp.sum(p, axis=0, keepdims=True)
    if has_sink:
        den = den + jnp.exp2(sink - m)
    o_t = _dot(kvt_ref[HEAD_DIM:, pl.ds(start, wk)], p.astype(BF16)) / den
    o_ref[0] = _heads_to_rows([o_t[:, r * tq:(r + 1) * tq] for r in range(r_heads)]).astype(BF16)


def _window_attn_t(proj3, kv_t, slopes, *, q_blk, kv_blk, window, sinks=None, name):
    b, t, _ = proj3.shape
    groups, r_heads = len(slopes), len(slopes[0])
    bias = _window_bias(slopes, TQ, window)
    n_var, wk = bias.shape[1], bias.shape[2]
    in_specs = [pl.BlockSpec((1, TQ, r_heads * LANES), lambda bi, g, i: (bi, i, q_blk // r_heads + g)),
                pl.BlockSpec((1, t, LANES), lambda bi, g, i: (bi, 0, kv_blk + g)),
                pl.BlockSpec((LANES, t), lambda bi, g, i: (kv_blk - KVB_BLK + g, bi)),
                pl.BlockSpec((1, 1, wk, r_heads * TQ), lambda bi, g, i: (g, jnp.minimum(i, n_var - 1), 0, 0))]
    args = [proj3, proj3, kv_t, bias]
    if sinks is not None:
        in_specs.insert(0, pl.BlockSpec(memory_space=pltpu.SMEM))
        args.insert(0, sinks.astype(F32) * LOG2E)
    return pl.pallas_call(
        functools.partial(_window_kernel_t, r_heads=r_heads, window=window, has_sink=sinks is not None),
        grid=(b, groups, t // TQ),
        in_specs=in_specs,
        out_specs=pl.BlockSpec((1, TQ, r_heads * HEAD_DIM), lambda bi, g, i: (bi, i, g)),
        out_shape=jax.ShapeDtypeStruct((b, t, groups * r_heads * HEAD_DIM), BF16),
        compiler_params=_cparams(("parallel", "parallel", "parallel"), 24),
        name=name,
    )(*args)


def _flash_bias(stream_slopes, tq, tk):
    ratio = tk // tq
    j = np.arange(tk)[:, None]
    q = np.arange(tq)[None, :]
    out = np.empty((len(stream_slopes), 1 + ratio, tk, len(stream_slopes[0]) * tq), np.float32)
    for s, slopes in enumerate(stream_slopes):
        for h, slope in enumerate(slopes):
            cols = slice(h * tq, (h + 1) * tq)
            out[s, 0, :, cols] = slope * j
            for r in range(ratio):
                out[s, 1 + r, :, cols] = np.where(j <= r * tq + q, slope * j, NEG_INF)
    return jnp.asarray(out)


def _flash_t(streams, kv_ref, kvt_ref, i, *, tq, tk, sel_t=None, interior=None):
    ratio = tk // tq
    n_full = lax.div(i, jnp.int32(ratio))
    rows = streams[0][0].shape[0]

    def tiles(kts, carry, edge):
        starts = [pl.multiple_of(kt * tk, tk) for kt in kts]
        scores = [[_dot_nt(kv_ref[0, pl.ds(k0, tk), s[1]:s[1] + LANES], s[0]) for s in streams]
                  for k0 in starts]
        state = [(m, l) for m, l, _ in carry]
        updates = []
        for n, kt in enumerate(kts):
            variant = 1 + i - kt * ratio if (edge and n == len(kts) - 1) else 0
            keep = None
            if sel_t is not None:
                blk = jnp.right_shift(lax.broadcasted_iota(jnp.int32, (tk, LANES), 0) + kt * tk, SLC_SHIFT)
                expand = jnp.where(_lane((tk, LANES)) == blk, 1.0, 0.0).astype(BF16)
                keep = _dot(expand, sel_t) > 0.5
            k0f = (kt * tk).astype(F32)
            row = []
            for s, (_, _, slope_lane, bias) in enumerate(streams):
                m, l = state[s]
                u = scores[n][s] + bias(variant)
                if keep is not None:
                    u = jnp.where(keep, u, NEG_INF)
                shift = slope_lane * k0f
                m_new = jnp.maximum(m, jnp.max(u, axis=0, keepdims=True) + shift)
                p = jnp.exp2(u - (m_new - shift))
                alpha = jnp.exp2(m - m_new)
                state[s] = (m_new, alpha * l + jnp.sum(p, axis=0, keepdims=True))
                row.append((alpha, p.astype(BF16)))
            updates.append(row)
        accs = [acc for _, _, acc in carry]
        for k0, row in zip(starts, updates):
            for s, (alpha, p) in enumerate(row):
                col = streams[s][1]
                kv_f = kvt_ref[col + HEAD_DIM:col + LANES, pl.ds(k0, tk)]
                accs[s] = alpha * accs[s] + _dot(kv_f, p)
        return tuple((m, l, acc) for (m, l), acc in zip(state, accs))

    init = tuple((jnp.full((1, rows), NEG_INF, F32), jnp.zeros((1, rows), F32), jnp.zeros((HEAD_DIM, rows), F32))
                 for _ in streams)
    if interior is None:
        count, tile_id = n_full, lambda j: j
    else:
        count, tile_id = interior
    carry = lax.fori_loop(0, jnp.right_shift(count, 1),
                          lambda j, c: tiles([tile_id(2 * j), tile_id(2 * j + 1)], c, False), init)
    carry = lax.cond(jnp.bitwise_and(count, 1) == 1,
                     lambda c: tiles([tile_id(count - 1), n_full], c, True),
                     lambda c: tiles([n_full], c, True), carry)
    return [(l, acc) for _, l, acc in carry]


def _selected_kernel_t(count_ref, ids_ref, q_ref, kv_ref, kvt_ref, sel_ref, bias_ref, o_ref, *, slopes):
    tq = q_ref.shape[1]
    i = pl.program_id(1)
    step = pl.program_id(0) * pl.num_programs(1) + i
    n_kt = kv_ref.shape[1] // TK_DENSE
    interior = (count_ref[step], lambda j: ids_ref[step * n_kt + j])
    sel_t = jnp.transpose(sel_ref[0].astype(F32)).astype(BF16)
    sel_t = jnp.concatenate([sel_t, sel_t], axis=1)
    streams = []
    for pair in range(A_HEADS // 2):
        qs = jnp.concatenate([q_ref[0, :, h * LANES:(h + 1) * LANES] for h in (2 * pair, 2 * pair + 1)], axis=0)
        slope_lane = jnp.concatenate([jnp.full((1, tq), slopes[h], F32) for h in (2 * pair, 2 * pair + 1)], axis=1)
        streams.append((qs, 0, slope_lane, functools.partial(lambda s, v: bias_ref[s, v], pair)))
    outs = []
    for l, acc in _flash_t(streams, kv_ref, kvt_ref, i, tq=tq, tk=TK_DENSE, sel_t=sel_t, interior=interior):
        o_t = acc / l
        outs += [o_t[:, :tq], o_t[:, tq:]]
    o_ref[0] = _heads_to_rows(outs).astype(BF16)


def _selected_attn_t(proj3, kv_t, sel, slopes):
    b, t, _ = proj3.shape
    kv_blk = KVA_BLK + 1
    bias = _flash_bias([slopes[0:2], slopes[2:4]], TQ_DENSE, TK_DENSE)
    nq, nk, per = t // TQ_DENSE, t // TK_DENSE, TK_DENSE // SLC_BLOCK
    picked = jnp.max(sel.reshape(b, nq, TQ_DENSE, LANES)[..., :nk * per], axis=2) > 0
    active = jnp.any(picked.reshape(b, nq, nk, per), axis=-1)
    past = jnp.arange(nk)[None, :] < (jnp.arange(nq) * TQ_DENSE // TK_DENSE)[:, None]
    active = active & past[None]
    ids = jnp.argsort(~active, axis=-1, stable=True).astype(jnp.int32).reshape(-1)
    counts = jnp.sum(active, axis=-1).astype(jnp.int32).reshape(-1)
    grid_spec = pltpu.PrefetchScalarGridSpec(
        num_scalar_prefetch=2,
        grid=(b, nq),
        in_specs=[pl.BlockSpec((1, TQ_DENSE, A_HEADS * LANES), lambda bi, i, *_: (bi, i, QA_BLK // A_HEADS)),
                  pl.BlockSpec((1, t, LANES), lambda bi, i, *_: (bi, 0, kv_blk)),
                  pl.BlockSpec((LANES, t), lambda bi, i, *_: (kv_blk - KVB_BLK, bi)),
                  pl.BlockSpec((1, TQ_DENSE, LANES), lambda bi, i, *_: (bi, i, 0)),
                  pl.BlockSpec(bias.shape, lambda bi, i, *_: (0, 0, 0, 0), pipeline_mode=pl.Buffered(1))],
        out_specs=pl.BlockSpec((1, TQ_DENSE, A_HEADS * HEAD_DIM), lambda bi, i, *_: (bi, i, 0)),
    )
    return pl.pallas_call(
        functools.partial(_selected_kernel_t, slopes=slopes),
        grid_spec=grid_spec,
        out_shape=jax.ShapeDtypeStruct((b, t, A_HEADS * HEAD_DIM), BF16),
        compiler_params=_cparams(("parallel", "parallel"), 40),
        name="nsa_selected",
    )(counts, ids, proj3, proj3, kv_t, sel, bias)


def _diff_kernel_t(slope_ref, dl_ref, g_ref, q_ref, kv_ref, kvt_ref, bias_ref, o_ref, *, lam_init):
    tq = q_ref.shape[1]
    hp = pl.program_id(1)
    i = pl.program_id(2)
    dl = dl_ref[...]
    lam = (jnp.exp(jnp.sum(dl[0:1] * dl[1:2], axis=-1, keepdims=True))
           - jnp.exp(jnp.sum(dl[2:3] * dl[3:4], axis=-1, keepdims=True)) + lam_init)
    lane = _lane((tq, LANES))
    streams = []
    for hh in range(2):
        q = q_ref[0, :, hh * LANES:(hh + 1) * LANES]
        zero = jnp.zeros_like(q)
        qs = jnp.concatenate([jnp.where(lane < DIFF_DK, q, zero),
                              jnp.where(lane >= DIFF_DK, q, zero)], axis=0)
        slope_lane = jnp.full((1, 2 * tq), slope_ref[hp * 2 + hh], F32)
        streams.append((qs, hh * LANES, slope_lane, functools.partial(lambda s, v: bias_ref[s, v], hh)))
    outs = []
    for l, acc in _flash_t(streams, kv_ref, kvt_ref, i, tq=tq, tk=TK_DENSE):
        o = acc / l
        w = o[:, :tq] - lam * o[:, tq:]
        ms = jnp.sum(w * w, axis=0, keepdims=True) * (1.0 / HEAD_DIM)
        outs.append(w * lax.rsqrt(ms + LN_EPS) * (1.0 - lam_init))
    o_ref[0] = (_heads_to_rows(outs) * g_ref[...]).astype(BF16)


def _diff_attn_t(proj3, kv_t, slopes, diff_lambda, subln, lam_init):
    b, t, _ = proj3.shape
    g_ext = jnp.tile(subln.reshape(1, HEAD_DIM).astype(F32), (1, 2))
    bias = _flash_bias([[s, s] for s in slopes], TQ_DENSE, TK_DENSE)
    return pl.pallas_call(
        functools.partial(_diff_kernel_t, lam_init=lam_init),
        grid=(b, B_HEADS // 2, t // TQ_DENSE),
        in_specs=[pl.BlockSpec(memory_space=pltpu.SMEM),
                  pl.BlockSpec((4, DIFF_DK), lambda bi, h, i: (0, 0)),
                  pl.BlockSpec((1, LANES), lambda bi, h, i: (0, 0)),
                  pl.BlockSpec((1, TQ_DENSE, 2 * LANES), lambda bi, h, i: (bi, i, QB_BLK // 2 + h)),
                  pl.BlockSpec((1, t, 2 * LANES), lambda bi, h, i: (bi, 0, KVB_BLK // 2 + h)),
                  pl.BlockSpec((2 * LANES, t), lambda bi, h, i: (h, bi)),
                  pl.BlockSpec((2,) + bias.shape[1:], lambda bi, h, i: (h, 0, 0, 0))],
        out_specs=pl.BlockSpec((1, TQ_DENSE, 2 * HEAD_DIM), lambda bi, h, i: (bi, i, h)),
        out_shape=jax.ShapeDtypeStruct((b, t, B_HEADS * HEAD_DIM), BF16),
        compiler_params=_cparams(("parallel", "parallel", "parallel"), 48),
        name="diff_attn",
    )(jnp.asarray(slopes, F32), diff_lambda.astype(F32), g_ext, proj3, proj3, kv_t, bias)


def _outproj_kernel(oc_ref, os_ref, ow_ref, gl_ref, ob_ref, osw_ref, x_ref, w_ref, ex_ref, g_ref, b_ref, o_ref):
    hi, lo = _split2(_sigmoid(gl_ref[...]))
    oa = None
    for j, ref in enumerate((oc_ref, os_ref, ow_ref)):
        gate = _dot(hi, ex_ref[j]) + _dot(lo, ex_ref[j])
        term = gate * ref[...].astype(F32)
        oa = term if oa is None else oa + term
    cat = jnp.concatenate([oa.astype(BF16), ob_ref[...], osw_ref[...]], axis=1)
    y = DN_ALPHA * x_ref[...] + _dot(cat, w_ref[...])
    o_ref[...] = _layer_norm(y, g_ref[...], b_ref[...])


def _outproj(o_c, o_s, o_w, gl, o_b, o_sw, x2d, w_out, ln_g, ln_b):
    n = x2d.shape[0]
    ex = np.zeros((3, LANES, A_HEADS * HEAD_DIM), np.float32)
    for j in range(3):
        for h in range(A_HEADS):
            ex[j, h * 3 + j, h * HEAD_DIM:(h + 1) * HEAD_DIM] = 1.0
    row = lambda w: pl.BlockSpec((TM, w), lambda i: (i, 0))
    full = lambda s: pl.BlockSpec(s, lambda i: (0,) * len(s))
    return pl.pallas_call(
        _outproj_kernel,
        grid=(n // TM,),
        in_specs=[row(256), row(256), row(256), row(LANES), row(256), row(512), row(D_MODEL),
                  full((D_MODEL, D_MODEL)), full(ex.shape), full((1, D_MODEL)), full((1, D_MODEL))],
        out_specs=row(D_MODEL),
        out_shape=jax.ShapeDtypeStruct((n, D_MODEL), F32),
        compiler_params=_cparams(("parallel",), 24),
        name="out_proj_ln",
    )(o_c, o_s, o_w, gl, o_b, o_sw, x2d, w_out.astype(BF16), jnp.asarray(ex, BF16),
      ln_g.reshape(1, -1), ln_b.reshape(1, -1))


def _ffn_kernel(x_ref, wg_ref, wu_ref, wd_ref, o_ref):
    xb = x_ref[...].astype(BF16)
    acc = None
    for c in range(D_FF_DENSE // FFN_CHUNK):
        cols = slice(c * FFN_CHUNK, (c + 1) * FFN_CHUNK)
        g = _dot(xb, wg_ref[:, cols])
        u = _dot(xb, wu_ref[:, cols])
        y = _dot((g * _sigmoid(g) * u).astype(BF16), wd_ref[cols, :])
        acc = y if acc is None else acc + y
    o_ref[...] = acc


def _ffn(x2d, wg, wu, wd):
    n = x2d.shape[0]
    resident = lambda shape: pl.BlockSpec(shape, lambda i: (0, 0), pipeline_mode=pl.Buffered(1))
    return pl.pallas_call(
        _ffn_kernel,
        grid=(n // TM,),
        in_specs=[pl.BlockSpec((TM, D_MODEL), lambda i: (i, 0)),
                  resident((D_MODEL, D_FF_DENSE)), resident((D_MODEL, D_FF_DENSE)), resident((D_FF_DENSE, D_MODEL))],
        out_specs=pl.BlockSpec((TM, D_MODEL), lambda i: (i, 0)),
        out_shape=jax.ShapeDtypeStruct((n, D_MODEL), F32),
        compiler_params=_cparams(("parallel",), 40),
        name="dense_swiglu",
    )(x2d, wg.astype(BF16), wu.astype(BF16), wd.astype(BF16))


def _router_kernel(x_ref, w_ref, tri_ref, xb_ref, gd_ref, scol_ref, srow_ref, base_ref, cnt_ref):
    tt = x_ref.shape[0]
    x = x_ref[...]
    xb_ref[...] = x.astype(BF16)
    xh, xl = _split2(x)
    wh, wl = _split2(w_ref[...])
    logits = _dot(xh, wh) + _dot(xh, wl) + _dot(xl, wh)
    lane = _lane((tt, LANES))
    lf = lane.astype(F32)
    logits = jnp.where(lane < N_EXPERTS, logits, -jnp.inf)
    m1 = jnp.max(logits, axis=-1, keepdims=True)
    i1 = jnp.min(jnp.where(logits == m1, lf, float(LANES)), axis=-1, keepdims=True)
    rest = jnp.where(lf == i1, -jnp.inf, logits)
    m2 = jnp.max(rest, axis=-1, keepdims=True)
    i2 = jnp.min(jnp.where(rest == m2, lf, float(LANES)), axis=-1, keepdims=True)
    e2 = jnp.exp(m2 - m1)
    g1 = 1.0 / (1.0 + e2)
    g2 = e2 / (1.0 + e2)
    first, second = lf == i1, lf == i2
    gd_ref[...] = jnp.where(first, g1, 0.0) + jnp.where(second, g2, 0.0)
    ind = jnp.where(first, 1.0, jnp.where(second, 1.0, 0.0))
    tri = tri_ref[...]
    run = jnp.zeros((1, LANES), F32)
    slots, bases = [], []
    for tb in range(tt // MOE_TB):
        blk = ind[tb * MOE_TB:(tb + 1) * MOE_TB]
        rank = _dot(tri, blk.astype(BF16)) + run
        slots.append(jnp.where(blk > 0.5, rank, -1.0))
        bases.append(run)
        run = run + jnp.sum(blk, axis=0, keepdims=True)
    slot = jnp.concatenate(slots, axis=0)
    scol_ref[...] = slot
    srow_ref[...] = jnp.transpose(slot)[0:N_EXPERTS]
    base_ref[0] = jnp.concatenate(bases, axis=0)
    cnt_ref[0] = jnp.broadcast_to(run, (8, LANES))


def _route(x2d, w_router):
    n = x2d.shape[0]
    nt = n // MOE_TILE
    ntb = MOE_TILE // MOE_TB
    w = jnp.zeros((D_MODEL, LANES), F32).at[:, :N_EXPERTS].set(w_router)
    tri = jnp.asarray(np.tril(np.ones((MOE_TB, MOE_TB), np.float32), -1), BF16)
    return pl.pallas_call(
        _router_kernel,
        grid=(nt,),
        in_specs=[pl.BlockSpec((MOE_TILE, D_MODEL), lambda i: (i, 0)),
                  pl.BlockSpec((D_MODEL, LANES), lambda i: (0, 0)),
                  pl.BlockSpec((MOE_TB, MOE_TB), lambda i: (0, 0))],
        out_specs=[pl.BlockSpec((MOE_TILE, D_MODEL), lambda i: (i, 0)),
                   pl.BlockSpec((MOE_TILE, LANES), lambda i: (i, 0)),
                   pl.BlockSpec((MOE_TILE, LANES), lambda i: (i, 0)),
                   pl.BlockSpec((N_EXPERTS, MOE_TILE), lambda i: (0, i)),
                   pl.BlockSpec((1, ntb, LANES), lambda i: (i, 0, 0)),
                   pl.BlockSpec((1, 8, LANES), lambda i: (i, 0, 0))],
        out_shape=[jax.ShapeDtypeStruct((n, D_MODEL), BF16),
                   jax.ShapeDtypeStruct((n, LANES), F32),
                   jax.ShapeDtypeStruct((n, LANES), F32),
                   jax.ShapeDtypeStruct((N_EXPERTS, n), F32),
                   jax.ShapeDtypeStruct((nt, ntb, LANES), F32),
                   jax.ShapeDtypeStruct((nt, 8, LANES), F32)],
        compiler_params=_cparams(("parallel",), 48),
        name="moe_router",
    )(x2d, w, tri)


MOE_GATHER_W = MOE_TB + 16
MOE_SCATTER_W = MOE_TB + 8
MOE_CAP = MOE_TILE + 512


def _moe_kernel(cnt_ref, base_ref, xb_ref, srow_ref, scol_ref, gd_ref, wg_ref, wu_ref, wd_ref,
                o_ref, xc_ref, yc_ref):
    i, e, c = pl.program_id(0), pl.program_id(1), pl.program_id(2)
    nc = pl.num_programs(2)
    ntb = MOE_TILE // MOE_TB
    count = cnt_ref[i * N_EXPERTS + e]
    n_full = jnp.right_shift(count * MOE_ROWS_RECIP[0], MOE_ROWS_RECIP[1])
    rem = count - n_full * MOE_ROWS

    @pl.when((i == 0) & (e == 0) & (c == 0))
    def _():
        yc_ref[...] = jnp.zeros_like(yc_ref)

    @pl.when((e == 0) & (c == 0))
    def _():
        o_ref[...] = jnp.zeros_like(o_ref)

    @pl.when(c == 0)
    def _():
        xc_ref[...] = jnp.zeros_like(xc_ref)
        srow = srow_ref[pl.ds(e, 1), :]
        rid = lax.broadcasted_iota(jnp.int32, (MOE_GATHER_W, MOE_TB), 0).astype(F32)
        for tb in range(ntb):
            b0 = base_ref[(i * ntb + tb) * N_EXPERTS + e]
            b_al = pl.multiple_of(jnp.left_shift(jnp.right_shift(b0, 4), 4), 16)
            rel = srow[:, tb * MOE_TB:(tb + 1) * MOE_TB] - b_al.astype(F32)
            onehot = jnp.where(rel == rid, 1.0, 0.0).astype(BF16)
            rows = _dot(onehot, xb_ref[tb * MOE_TB:(tb + 1) * MOE_TB, :])
            cur = xc_ref[pl.ds(b_al, MOE_GATHER_W), :].astype(F32)
            xc_ref[pl.ds(b_al, MOE_GATHER_W), :] = (cur + rows).astype(BF16)

    def expert_rows(r0, n_rows):
        xr = xc_ref[pl.ds(r0, n_rows), :]
        g = _dot(xr, wg_ref[0])
        u = _dot(xr, wu_ref[0])
        y = _dot((g * _sigmoid(g) * u).astype(BF16), wd_ref[0])
        prev = yc_ref[pl.ds(r0, n_rows), :]
        yc_ref[pl.ds(r0, n_rows), :] = jnp.where(c == 0, y, prev + y)

    def rows_body(rb, carry):
        expert_rows(pl.multiple_of(rb * MOE_ROWS, MOE_ROW_ALIGN), MOE_ROWS)
        return carry

    lax.fori_loop(0, n_full + jnp.where(rem > MOE_ROWS // 2, 1, 0), rows_body, 0)

    @pl.when((rem > 0) & (rem <= MOE_ROWS // 2))
    def _():
        expert_rows(pl.multiple_of(n_full * MOE_ROWS, MOE_ROW_ALIGN), MOE_ROWS // 2)

    @pl.when(c == nc - 1)
    def _():
        lane = _lane((MOE_TB, LANES))
        cid = lax.broadcasted_iota(jnp.int32, (MOE_TB, MOE_SCATTER_W), 1).astype(F32)
        for tb in range(ntb):
            b0 = base_ref[(i * ntb + tb) * N_EXPERTS + e]
            b_al = pl.multiple_of(jnp.left_shift(jnp.right_shift(b0, 3), 3), 8)
            rows = slice(tb * MOE_TB, (tb + 1) * MOE_TB)
            slot = jnp.sum(jnp.where(lane == e, scol_ref[rows, :], 0.0), axis=-1, keepdims=True)
            gate = jnp.sum(jnp.where(lane == e, gd_ref[rows, :], 0.0), axis=-1, keepdims=True)
            onehot = jnp.where(slot - b_al.astype(F32) == cid, 1.0, 0.0).astype(BF16)
            y_win = yc_ref[pl.ds(b_al, MOE_SCATTER_W), :].astype(BF16)
            o_ref[rows, :] += gate * _dot(onehot, y_win)


def _moe(xb, srow, scol, gd, cnt, base, wg, wu, wd):
    n = xb.shape[0]
    nt = n // MOE_TILE
    nc = D_FF_EXPERT // MOE_CHUNK
    grid_spec = pltpu.PrefetchScalarGridSpec(
        num_scalar_prefetch=2,
        grid=(nt, N_EXPERTS, nc),
        in_specs=[pl.BlockSpec((MOE_TILE, D_MODEL), lambda i, e, c, *_: (i, 0), pipeline_mode=pl.Buffered(1)),
                  pl.BlockSpec((N_EXPERTS, MOE_TILE), lambda i, e, c, *_: (0, i)),
                  pl.BlockSpec((MOE_TILE, LANES), lambda i, e, c, *_: (i, 0)),
                  pl.BlockSpec((MOE_TILE, LANES), lambda i, e, c, *_: (i, 0)),
                  pl.BlockSpec((1, D_MODEL, MOE_CHUNK), lambda i, e, c, *_: (e, 0, c)),
                  pl.BlockSpec((1, D_MODEL, MOE_CHUNK), lambda i, e, c, *_: (e, 0, c)),
                  pl.BlockSpec((1, MOE_CHUNK, D_MODEL), lambda i, e, c, *_: (e, c, 0))],
        out_specs=pl.BlockSpec((MOE_TILE, D_MODEL), lambda i, e, c, *_: (i, 0), pipeline_mode=pl.Buffered(1)),
        scratch_shapes=[pltpu.VMEM((MOE_CAP, D_MODEL), BF16), pltpu.VMEM((MOE_CAP, D_MODEL), F32)],
    )
    return pl.pallas_call(
        _moe_kernel,
        grid_spec=grid_spec,
        out_shape=jax.ShapeDtypeStruct((n, D_MODEL), F32),
        compiler_params=_cparams(("arbitrary", "arbitrary", "arbitrary"), 56),
        name="moe_experts",
    )(cnt, base, xb, srow, scol, gd, wg.astype(BF16), wu.astype(BF16), wd.astype(BF16))


def _ple_kernel(x_ref, f_ref, p_ref, wg_ref, wp_ref, g2_ref, b2_ref, g3_ref, b3_ref, o_ref):
    x2 = _layer_norm(DN_ALPHA * x_ref[...] + f_ref[...], g2_ref[...], b2_ref[...])
    e = _sigmoid(_dot(x2.astype(BF16), wg_ref[...])) * _dot(p_ref[...].astype(BF16), wp_ref[...])
    o_ref[...] = _layer_norm(DN_ALPHA * x2 + e, g3_ref[...], b3_ref[...])


def _ple(x2d, f2d, p_all, layer, w_gate, w_proj, g2, b2, g3, b3):
    n = x2d.shape[0]
    row = lambda w: pl.BlockSpec((TM, w), lambda i: (i, 0))
    full = lambda s: pl.BlockSpec(s, lambda i: (0,) * len(s))
    vec = full((1, D_MODEL))
    return pl.pallas_call(
        _ple_kernel,
        grid=(n // TM,),
        in_specs=[row(D_MODEL), row(D_MODEL), pl.BlockSpec((TM, PLE_DIM), lambda i: (layer * (n // TM) + i, 0)),
                  full((D_MODEL, D_MODEL)), full((PLE_DIM, D_MODEL)), vec, vec, vec, vec],
        out_specs=row(D_MODEL),
        out_shape=jax.ShapeDtypeStruct((n, D_MODEL), F32),
        compiler_params=_cparams(("parallel",), 32),
        name="ln_ple_ln",
    )(x2d, f2d, p_all, w_gate.astype(BF16), w_proj.astype(BF16),
      g2.reshape(1, -1), b2.reshape(1, -1), g3.reshape(1, -1), b3.reshape(1, -1))


def _mixer(x2d, b, t, i, w_in, cmp_pos, cmp_w1, cmp_w2, diff_lambda, diff_subln, sinks):
    slopes_a, slopes_b, slopes_c = _alibi_slopes()
    proj, gl, kv_t = _project(x2d, _build_w_in(w_in))
    proj3 = proj.reshape(b, t, N_MAIN)
    kvc, kvc_t = _compress(proj3, cmp_pos, cmp_w1, cmp_w2)
    o_c, sel = _cmp_attn(proj3, kvc, kvc_t, slopes_a)
    o_s = _selected_attn_t(proj3, kv_t, sel, slopes_a)
    o_w = _window_attn_t(proj3, kv_t, [slopes_a], q_blk=QA_BLK, kv_blk=KVA_BLK + 2, window=NSA_WINDOW,
                         name="nsa_window")
    lam_init = 0.8 - 0.6 * math.exp(-0.3 * i)
    o_b = _diff_attn_t(proj3, kv_t, slopes_b, diff_lambda, diff_subln, lam_init)
    r_c = C_HEADS // C_KV_HEADS
    o_sw = _window_attn_t(proj3, kv_t, [slopes_c[g * r_c:(g + 1) * r_c] for g in range(C_KV_HEADS)],
                          q_blk=QC_BLK, kv_blk=KVC_BLK, window=SWA_WINDOW, sinks=sinks, name="swa_gqa")
    flat = lambda a: a.reshape(b * t, a.shape[-1])
    return flat(o_c), flat(o_s), flat(o_w), gl, flat(o_b), flat(o_sw)


def kernel(x, p, w_in, cmp_pos, cmp_w1, cmp_w2, diff_lambda, diff_subln, sinks, w_out, ln1_g, ln1_b,
           ffn_w_gate, ffn_w_up, ffn_w_down, moe_router, moe_w_gate, moe_w_up, moe_w_down, ln2_g, ln2_b,
           ple_gate, ple_proj, ln3_g, ln3_b):
    b, t, d = x.shape
    n = b * t
    x2d = x.reshape(n, d)
    for i in range(DEPTH):
        heads = _mixer(x2d, b, t, i, w_in[i], cmp_pos[i], cmp_w1[i], cmp_w2[i],
                       diff_lambda[i], diff_subln[i], sinks[i])
        x1 = _outproj(*heads, x2d, w_out[i], ln1_g[i], ln1_b[i])
        if i % 2 == 0:
            f = _ffn(x1, ffn_w_gate[i // 2], ffn_w_up[i // 2], ffn_w_down[i // 2])
        else:
            xb, gd, scol, srow, base, cnt = _route(x1, moe_router[i // 2])
            cnt_i = cnt[:, 0, :N_EXPERTS].astype(jnp.int32).reshape(-1)
            base_i = base[:, :, :N_EXPERTS].astype(jnp.int32).reshape(-1)
            f = _moe(xb, srow, scol, gd, cnt_i, base_i,
                     moe_w_gate[i // 2], moe_w_up[i // 2], moe_w_down[i // 2])
        x2d = _ple(x1, f, p.reshape(DEPTH * n, PLE_DIM), i, ple_gate[i], ple_proj[i],
                   ln2_g[i], ln2_b[i], ln3_g[i], ln3_b[i])
    return x2d.reshape(b, t, d)
```
